```python
import math
import jax, jax.numpy as jnp
from jax import lax
import numpy as np

D_MODEL = 1024
BATCH = 8
SEQ = 2048
DEPTH = 1
DEC_BATCH = 128
DEC_SEQ = 8
PAST_LEN = 16384
PAGE_SIZE = 128

D_SSM = D_MODEL // 2
SSM_GROUP = 16
N_GROUPS = D_SSM // SSM_GROUP
SSM_STATE = 64
ML_HEADS = 4
ML_DK = 128
ML_DV = 128
D_ML = ML_HEADS * ML_DV
ML_CHUNK = 64
N_EXPERTS = 32
TOP_K = 4
D_EXPERT = D_MODEL
SWIGLU_LIMIT = 7.0
SWIGLU_ALPHA = 1.702
MOE_BLOCK = 128
D_PLE = 256
RMS_EPS = 1e-6
D_IN = D_SSM + 4 * D_ML + 2 * ML_HEADS + 2 * D_MODEL

kernel_name = 'hybrid_s5_mlstm_moe_step'


def rmsnorm(x, g):
    xf = x.astype(jnp.float32)
    y = xf * lax.rsqrt(jnp.mean(xf * xf, axis=-1, keepdims=True) + RMS_EPS)
    return (y * g.astype(jnp.float32)).astype(x.dtype)


def _cplx_affine(e1, e2):
    a1r, a1i, b1r, b1i = e1
    a2r, a2i, b2r, b2i = e2
    return (a2r * a1r - a2i * a1i, a2r * a1i + a2i * a1r,
            a2r * b1r - a2i * b1i + b2r, a2r * b1i + a2i * b1r + b2i)


def s5_mixer(u, h0_re, h0_im, a_re, a_im, log_dt, b_re, b_im, c_re, c_im, d_skip, w_glu, b_glu):
    f32 = jnp.float32
    a_re, a_im, log_dt = a_re.astype(f32), a_im.astype(f32), log_dt.astype(f32)
    b_re, b_im, c_re, c_im = b_re.astype(f32), b_im.astype(f32), c_re.astype(f32), c_im.astype(f32)
    bsz, L, _ = u.shape
    ug = u.reshape(bsz, L, N_GROUPS, SSM_GROUP)
    dt = jnp.exp(log_dt)[:, None]
    mag = jnp.exp(a_re * dt)
    abar_r, abar_i = mag * jnp.cos(a_im * dt), mag * jnp.sin(a_im * dt)
    den = a_re * a_re + a_im * a_im
    nr, ni = abar_r - 1.0, abar_i
    coef_r = (nr * a_re + ni * a_im) / den
    coef_i = (ni * a_re - nr * a_im) / den
    bbar_r = coef_r[..., None] * b_re - coef_i[..., None] * b_im
    bbar_i = coef_r[..., None] * b_im + coef_i[..., None] * b_re
    bu_r = jnp.einsum('blgh,gph->blgp', ug, bbar_r)
    bu_i = jnp.einsum('blgh,gph->blgp', ug, bbar_i)
    h0_re, h0_im = h0_re.astype(f32), h0_im.astype(f32)
    bu_r = bu_r.at[:, 0].add(abar_r * h0_re - abar_i * h0_im)
    bu_i = bu_i.at[:, 0].add(abar_r * h0_im + abar_i * h0_re)
    ar = jnp.broadcast_to(abar_r, bu_r.shape)
    ai = jnp.broadcast_to(abar_i, bu_i.shape)
    _, _, xr, xi = lax.associative_scan(_cplx_affine, (ar, ai, bu_r, bu_i), axis=1)
    y = (jnp.einsum('blgp,ghp->blgh', xr, c_re) - jnp.einsum('blgp,ghp->blgh', xi, c_im)
         + d_skip.astype(f32) * ug)
    y = jax.nn.gelu(y.reshape(bsz, L, D_SSM))
    y = y * jax.nn.sigmoid(y @ w_glu.astype(f32) + b_glu.astype(f32))
    return y, xr[:, -1], xi[:, -1]


def mlstm_mixer(q, k, v, ig, fg, o, c0, n0, m0):
    bsz, L = q.shape[0], q.shape[1]
    lc = math.gcd(L, ML_CHUNK)
    nc = L // lc
    logf = jax.nn.log_sigmoid(fg)
    k = k * (ML_DK ** -0.5)

    def to_chunks(t):
        t = t.reshape((bsz, nc, lc) + t.shape[2:])
        return jnp.moveaxis(t, (1, 3), (0, 2))

    causal = jnp.tril(jnp.ones((lc, lc), dtype=bool))

    def step(carry, xs):
        C, n, m = carry
        qc, kc, vc, ic, fc = xs
        b = jnp.cumsum(fc, axis=-1)
        dmat = b[..., :, None] - b[..., None, :] + ic[..., None, :]
        dmat = jnp.where(causal, dmat, -jnp.inf)
        inter = b + m[..., None]
        mt = jnp.maximum(inter, jnp.max(dmat, axis=-1))
        w_in = jnp.exp(dmat - mt[..., None])
        w_out = jnp.exp(inter - mt)
        s = jnp.einsum('bhtd,bhsd->bhts', qc, kc) * w_in
        num = (w_out[..., None] * jnp.einsum('bhtd,bhde->bhte', qc, C)
               + jnp.einsum('bhts,bhse->bhte', s, vc))
        qn = w_out * jnp.einsum('bhtd,bhd->bht', qc, n) + jnp.sum(s, axis=-1)
        h = num / jnp.maximum(jnp.abs(qn), jnp.exp(-mt))[..., None]
        m_end = mt[..., -1]
        decay = jnp.exp(b[..., -1] + m - m_end)
        ws = jnp.exp(b[..., -1:] - b + ic - m_end[..., None])
        C_new = decay[..., None, None] * C + jnp.einsum('bhs,bhsd,bhse->bhde', ws, kc, vc)
        n_new = decay[..., None] * n + jnp.einsum('bhs,bhsd->bhd', ws, kc)
        return (C_new, n_new, m_end), h

    xs = (to_chunks(q), to_chunks(k), to_chunks(v), to_chunks(ig), to_chunks(logf))
    carry0 = (c0.astype(jnp.float32), n0.astype(jnp.float32), m0.astype(jnp.float32))
    (C, n, m), hs = lax.scan(step, carry0, xs)
    hs = jnp.moveaxis(hs, (0, 2), (1, 3)).reshape(bsz, L, D_ML)
    return jax.nn.sigmoid(o) * hs, C, n, m


def moe_ffn(x, w_router, b_router, w_gate, b_gate, w_up, b_up, w_down, b_down):
    shape = x.shape
    xt = x.reshape(-1, shape[-1])
    n_tok = xt.shape[0]
    logits = (xt @ w_router + b_router).astype(jnp.float32)
    top_val, top_idx = lax.top_k(logits, TOP_K)
    top_w = jax.nn.softmax(top_val, axis=-1)
    n_assign = n_tok * TOP_K
    flat_e = top_idx.reshape(-1)
    flat_tok = jnp.repeat(jnp.arange(n_tok, dtype=jnp.int32), TOP_K)
    flat_w = top_w.reshape(-1)
    order = jnp.argsort(flat_e)
    sorted_e = flat_e[order]
    counts = jnp.zeros((N_EXPERTS,), jnp.int32).at[flat_e].add(1)
    padded = (counts + MOE_BLOCK - 1) // MOE_BLOCK * MOE_BLOCK
    pad_end = jnp.cumsum(padded)
    pad_start = pad_end - padded
    start = jnp.cumsum(counts) - counts
    rank = jnp.arange(n_assign, dtype=jnp.int32) - start[sorted_e]
    dest = pad_start[sorted_e] + rank
    n_blocks = -(-(n_assign + N_EXPERTS * (MOE_BLOCK - 1)) // MOE_BLOCK)
    n_rows = n_blocks * MOE_BLOCK
    row_tok = jnp.zeros((n_rows,), jnp.int32).at[dest].set(flat_tok[order])
    row_w = jnp.zeros((n_rows,), jnp.float32).at[dest].set(flat_w[order])
    block_e = jnp.minimum(
        jnp.searchsorted(pad_end, jnp.arange(n_blocks, dtype=jnp.int32) * MOE_BLOCK, side='right'),
        N_EXPERTS - 1)

    def expert_block(args):
        tok, e = args
        xb = xt[tok]
        g = xb @ w_gate[e] + b_gate[e]
        u = xb @ w_up[e] + b_up[e]
        g = jnp.minimum(g, SWIGLU_LIMIT)
        u = jnp.clip(u, -SWIGLU_LIMIT, SWIGLU_LIMIT)
        return (g * jax.nn.sigmoid(SWIGLU_ALPHA * g) * (u + 1.0)) @ w_down[e] + b_down[e]

    rows = lax.map(expert_block, (row_tok.reshape(n_blocks, MOE_BLOCK), block_e))
    rows = rows.reshape(n_rows, -1) * row_w[:, None].astype(rows.dtype)
    out = jnp.zeros_like(xt).at[row_tok].add(rows.astype(xt.dtype))
    return out.reshape(shape)


def trunk(x, p, s_re, s_im, s_c, s_n, s_m,
          g_mix, w_in, ssm_a_re, ssm_a_im, ssm_log_dt, ssm_b_re, ssm_b_im, ssm_c_re, ssm_c_im,
          ssm_d, ssm_w_glu, ssm_b_glu, ml_b_ig, ml_b_fg, w_ssm_up, w_ml_up, w_out,
          g_ffn, w_router, b_router, w_gate, b_gate, w_up, b_up, w_down, b_down,
          g_ple, w_ple_gate, w_ple_proj, g_final):
    h = x
    bsz, L = x.shape[0], x.shape[1]
    splits = list(np.cumsum([D_SSM, D_ML, D_ML, D_ML, D_ML, ML_HEADS, ML_HEADS, D_MODEL]))
    o_re, o_im, o_c, o_n, o_m = [], [], [], [], []
    for i in range(DEPTH):
        hn = rmsnorm(h, g_mix[i])
        z = (hn @ w_in[i]).astype(jnp.float32)
        u, q, k, v, o, ig, fg, gs, gm = jnp.split(z, splits, axis=-1)
        ys, nre, nim = s5_mixer(u, s_re[i], s_im[i], ssm_a_re[i], ssm_a_im[i], ssm_log_dt[i],
                                ssm_b_re[i], ssm_b_im[i], ssm_c_re[i], ssm_c_im[i], ssm_d[i],
                                ssm_w_glu[i], ssm_b_glu[i])
        heads = lambda t: t.reshape(bsz, L, ML_HEADS, -1)
        ym, nC, nN, nM = mlstm_mixer(heads(q), heads(k), heads(v),
                                     ig + ml_b_ig[i].astype(jnp.float32),
                                     fg + ml_b_fg[i].astype(jnp.float32),
                                     o, s_c[i], s_n[i], s_m[i])
        merged = (jax.nn.sigmoid(gs) * (ys @ w_ssm_up[i].astype(jnp.float32))
                  + jax.nn.sigmoid(gm) * (ym @ w_ml_up[i].astype(jnp.float32)))
        h = h + (merged @ w_out[i].astype(jnp.float32)).astype(h.dtype)
        h = h + moe_ffn(rmsnorm(h, g_ffn[i]), w_router[i], b_router[i], w_gate[i], b_gate[i],
                        w_up[i], b_up[i], w_down[i], b_down[i])
        gate = jax.nn.sigmoid(rmsnorm(h, g_ple[i]) @ w_ple_gate[i])
        h = h + gate * (p[i] @ w_ple_proj[i])
        o_re.append(nre); o_im.append(nim); o_c.append(nC); o_n.append(nN); o_m.append(nM)
    y = rmsnorm(h, g_final)
    return (y, jnp.stack(o_re), jnp.stack(o_im), jnp.stack(o_c), jnp.stack(o_n), jnp.stack(o_m))


def setup_inputs(seed: int = 0) -> dict:
    key = jax.random.key(seed)
    kit = iter(jax.random.split(key, 64))
    f32 = jnp.float32
    nrm = lambda shape, scale: scale * jax.random.normal(next(kit), shape, f32)
    P, G, H = SSM_STATE, N_GROUPS, ML_HEADS
    inp = {}
    inp['x_prompt'] = nrm((BATCH, SEQ, D_MODEL), 1.0)
    inp['x_sample'] = nrm((DEC_BATCH, DEC_SEQ, D_MODEL), 1.0)
    inp['p_prompt'] = nrm((DEPTH, BATCH, SEQ, D_PLE), 1.0)
    inp['p_sample'] = nrm((DEPTH, DEC_BATCH, DEC_SEQ, D_PLE), 1.0)
    inp['state_ssm_re'] = nrm((DEPTH, DEC_BATCH, G, P), 0.3)
    inp['state_ssm_im'] = nrm((DEPTH, DEC_BATCH, G, P), 0.3)
    inp['state_ml_c'] = nrm((DEPTH, DEC_BATCH, H, ML_DK, ML_DV), 0.1)
    inp['state_ml_n'] = nrm((DEPTH, DEC_BATCH, H, ML_DK), 0.1)
    inp['state_ml_m'] = nrm((DEPTH, DEC_BATCH, H), 1.0)
    inp['g_mix'] = 1.0 + nrm((DEPTH, D_MODEL), 0.02)
    inp['w_in'] = nrm((DEPTH, D_MODEL, D_IN), D_MODEL ** -0.5)
    inp['ssm_a_re'] = -0.5 + nrm((DEPTH, G, P), 0.01)
    inp['ssm_a_im'] = jnp.pi * jnp.arange(P, dtype=f32) + nrm((DEPTH, G, P), 0.01)
    inp['ssm_log_dt'] = jax.random.uniform(next(kit), (DEPTH, G), f32, math.log(1e-3), math.log(1e-1))
    inp['ssm_b_re'] = nrm((DEPTH, G, P, SSM_GROUP), (2 * SSM_GROUP) ** -0.5)
    inp['ssm_b_im'] = nrm((DEPTH, G, P, SSM_GROUP), (2 * SSM_GROUP) ** -0.5)
    inp['ssm_c_re'] = nrm((DEPTH, G, SSM_GROUP, P), (2 * P) ** -0.5)
    inp['ssm_c_im'] = nrm((DEPTH, G, SSM_GROUP, P), (2 * P) ** -0.5)
    inp['ssm_d'] = nrm((DEPTH, G, SSM_GROUP), 1.0)
    inp['ssm_w_glu'] = nrm((DEPTH, D_SSM, D_SSM), D_SSM ** -0.5)
    inp['ssm_b_glu'] = nrm((DEPTH, D_SSM), 0.01)
    inp['ml_b_ig'] = -1.0 + nrm((DEPTH, H), 0.1)
    inp['ml_b_fg'] = jnp.linspace(3.0, 6.0, H, dtype=f32) + nrm((DEPTH, H), 0.1)
    inp['w_ssm_up'] = nrm((DEPTH, D_SSM, D_MODEL), D_SSM ** -0.5)
    inp['w_ml_up'] = nrm((DEPTH, D_ML, D_MODEL), D_ML ** -0.5)
    inp['w_out'] = nrm((DEPTH, D_MODEL, D_MODEL), D_MODEL ** -0.5)
    inp['g_ffn'] = 1.0 + nrm((DEPTH, D_MODEL), 0.02)
    inp['w_router'] = nrm((DEPTH, D_MODEL, N_EXPERTS), D_MODEL ** -0.5)
    inp['b_router'] = nrm((DEPTH, N_EXPERTS), 0.01)
    inp['w_gate'] = nrm((DEPTH, N_EXPERTS, D_MODEL, D_EXPERT), D_MODEL ** -0.5)
    inp['b_gate'] = nrm((DEPTH, N_EXPERTS, D_EXPERT), 0.01)
    inp['w_up'] = nrm((DEPTH, N_EXPERTS, D_MODEL, D_EXPERT), D_MODEL ** -0.5)
    inp['b_up'] = nrm((DEPTH, N_EXPERTS, D_EXPERT), 0.01)
    inp['w_down'] = nrm((DEPTH, N_EXPERTS, D_EXPERT, D_MODEL), D_EXPERT ** -0.5)
    inp['b_down'] = nrm((DEPTH, N_EXPERTS, D_MODEL), 0.01)
    inp['g_ple'] = 1.0 + nrm((DEPTH, D_MODEL), 0.02)
    inp['w_ple_gate'] = nrm((DEPTH, D_MODEL, D_MODEL), D_MODEL ** -0.5)
    inp['w_ple_proj'] = nrm((DEPTH, D_PLE, D_MODEL), D_PLE ** -0.5)
    inp['g_final'] = 1.0 + nrm((D_MODEL,), 0.02)
    return inp


def reference(x_prompt, x_sample, p_prompt, p_sample, state_ssm_re, state_ssm_im, state_ml_c,
              state_ml_n, state_ml_m, g_mix, w_in, ssm_a_re, ssm_a_im, ssm_log_dt, ssm_b_re,
              ssm_b_im, ssm_c_re, ssm_c_im, ssm_d, ssm_w_glu, ssm_b_glu, ml_b_ig, ml_b_fg,
              w_ssm_up, w_ml_up, w_out, g_ffn, w_router, b_router, w_gate, b_gate, w_up, b_up,
              w_down, b_down, g_ple, w_ple_gate, w_ple_proj, g_final):
    f32 = jnp.float32
    weights = (g_mix, w_in, ssm_a_re, ssm_a_im, ssm_log_dt, ssm_b_re, ssm_b_im, ssm_c_re, ssm_c_im,
               ssm_d, ssm_w_glu, ssm_b_glu, ml_b_ig, ml_b_fg, w_ssm_up, w_ml_up, w_out,
               g_ffn, w_router, b_router, w_gate, b_gate, w_up, b_up, w_down, b_down,
               g_ple, w_ple_gate, w_ple_proj, g_final)
    bp = x_prompt.shape[0]
    z_re = jnp.zeros((DEPTH, bp, N_GROUPS, SSM_STATE), f32)
    z_c = jnp.zeros((DEPTH, bp, ML_HEADS, ML_DK, ML_DV), f32)
    z_n = jnp.zeros((DEPTH, bp, ML_HEADS, ML_DK), f32)
    z_m = jnp.zeros((DEPTH, bp, ML_HEADS), f32)
    y_prompt, re_p, im_p, c_p, n_p, m_p = trunk(x_prompt, p_prompt, z_re, z_re, z_c, z_n, z_m, *weights)
    y_sample, re_s, im_s, c_s, n_s, m_s = trunk(x_sample, p_sample, state_ssm_re, state_ssm_im,
                                                state_ml_c, state_ml_n, state_ml_m, *weights)
    return (y_prompt, y_sample, re_p, im_p, c_p, n_p, m_p, re_s, im_s, c_s, n_s, m_s)
```

```python
import functools
import math

import jax
import jax.numpy as jnp
from jax import lax
from jax.experimental import pallas as pl
from jax.experimental.pallas import tpu as pltpu

F32 = jnp.float32
BF16 = jnp.bfloat16
HIGHEST = lax.Precision.HIGHEST

D_MODEL = 1024
D_SSM = 512
SSM_GROUP = 16
N_GROUPS = 32
SSM_STATE = 64
ML_HEADS = 4
ML_DK = 128
D_ML = 512
N_EXPERTS = 32
TOP_K = 4
SWIGLU_LIMIT = 7.0
SWIGLU_ALPHA = 1.702
D_PLE = 256
RMS_EPS = 1e-6

LANES = 128
GATE_PAD = LANES
S5_CHUNK = 16
ML_CHUNK_PROMPT = 256
TM_TOKENS = 256
TM_MOE = 256
VMEM_LIMIT = 56 * 1024 * 1024


def _cparams(*sem):
    return pltpu.CompilerParams(dimension_semantics=sem, vmem_limit_bytes=VMEM_LIMIT)


def _rms(x, g):
    return x * lax.rsqrt(jnp.mean(x * x, axis=-1, keepdims=True) + RMS_EPS) * g


def _dot(a, b):
    return jnp.dot(a, b, preferred_element_type=F32)


def _dot_hi(a, b):
    return jnp.dot(a, b, preferred_element_type=F32, precision=HIGHEST)


def _full(shape):
    n = len(shape)
    return pl.BlockSpec(shape, lambda *_: (0,) * n)


def _inproj_kernel(xp_ref, xs_ref, g_ref, wu_ref, wqkv_ref, wo_ref, wgt_ref, wgsm_ref,
                   u_ref, qkv_ref, o_ref, gt_ref, gsm_ref, *, n_prompt_blocks):
    i = pl.program_id(0)
    x = jnp.where(i < n_prompt_blocks, xp_ref[...], xs_ref[...])
    hn = _rms(x, g_ref[...]).astype(BF16)
    u_ref[...] = _dot(hn, wu_ref[...])
    qkv = _dot(hn, wqkv_ref[...])
    col = lax.broadcasted_iota(jnp.int32, (1, 3 * D_ML), 1)
    k_scale = jnp.where((col >= D_ML) & (col < 2 * D_ML), ML_DK ** -0.5, 1.0).astype(F32)
    qkv_ref[...] = (qkv * k_scale).astype(BF16)
    o_ref[...] = _dot(hn, wo_ref[...])
    gt_ref[...] = _dot(hn, wgt_ref[...])
    gsm_ref[...] = _dot(hn, wgsm_ref[...])


def _two_source_specs(tm, width, n_prompt_blocks):
    last = n_prompt_blocks - 1
    return (pl.BlockSpec((tm, width), lambda i: (jnp.minimum(i, last), 0)),
            pl.BlockSpec((tm, width), lambda i: (jnp.maximum(i - n_prompt_blocks, 0), 0)))


def _inproj(xp, xs, g_mix, w_in):
    tm = TM_TOKENS
    tp, ts = xp.shape[0], xs.shape[0]
    t = tp + ts
    npb = tp // tm
    w = w_in.astype(BF16)
    o0 = D_SSM
    wu = w[:, :o0]
    wqkv = w[:, o0:o0 + 3 * D_ML]
    wo = w[:, o0 + 3 * D_ML:o0 + 4 * D_ML]
    g0 = o0 + 4 * D_ML
    wgt = jnp.pad(w[:, g0:g0 + 2 * ML_HEADS], ((0, 0), (0, GATE_PAD - 2 * ML_HEADS)))
    wgsm = w[:, g0 + 2 * ML_HEADS:]
    xp_spec, xs_spec = _two_source_specs(tm, D_MODEL, npb)
    outs = (jax.ShapeDtypeStruct((t, D_SSM), F32), jax.ShapeDtypeStruct((t, 3 * D_ML), BF16),
            jax.ShapeDtypeStruct((t, D_ML), F32), jax.ShapeDtypeStruct((t, GATE_PAD), F32),
            jax.ShapeDtypeStruct((t, 2 * D_MODEL), F32))
    row = lambda width: pl.BlockSpec((tm, width), lambda i: (i, 0))
    return pl.pallas_call(
        functools.partial(_inproj_kernel, n_prompt_blocks=npb),
        grid=(t // tm,),
        in_specs=[xp_spec, xs_spec, _full((1, D_MODEL)), _full(wu.shape), _full(wqkv.shape),
                  _full(wo.shape), _full(wgt.shape), _full(wgsm.shape)],
        out_specs=[row(D_SSM), row(3 * D_ML), row(D_ML), row(GATE_PAD), row(2 * D_MODEL)],
        out_shape=outs,
        compiler_params=_cparams("arbitrary"),
        name="inproj",
    )(xp, xs, g_mix.reshape(1, D_MODEL), wu, wqkv, wo, wgt, wgsm)


def _s5_tables(a_re, a_im, log_dt, b_re, b_im, c_re, c_im, d_skip, tc):
    ein = functools.partial(jnp.einsum, precision=HIGHEST)
    g, p, h = N_GROUPS, SSM_STATE, SSM_GROUP
    dt = jnp.exp(log_dt)[:, None]
    mag = jnp.exp(a_re * dt)
    abar_r, abar_i = mag * jnp.cos(a_im * dt), mag * jnp.sin(a_im * dt)
    den = a_re * a_re + a_im * a_im
    nr, ni = abar_r - 1.0, abar_i
    coef_r = (nr * a_re + ni * a_im) / den
    coef_i = (ni * a_re - nr * a_im) / den
    bbar_r = coef_r[..., None] * b_re - coef_i[..., None] * b_im
    bbar_i = coef_r[..., None] * b_im + coef_i[..., None] * b_re

    def step(carry, _):
        pr, pi = carry
        return (pr * abar_r - pi * abar_i, pr * abar_i + pi * abar_r), (pr, pi)

    (at_r, at_i), (pow_r, pow_i) = lax.scan(step, (jnp.ones_like(abar_r), jnp.zeros_like(abar_r)),
                                            None, length=tc)
    ab_r = pow_r[..., None] * bbar_r - pow_i[..., None] * bbar_i
    ab_i = pow_r[..., None] * bbar_i + pow_i[..., None] * bbar_r
    kern = ein('ghp,jgpk->jghk', c_re, ab_r) - ein('ghp,jgpk->jghk', c_im, ab_i)
    lag = jnp.arange(tc)[None, :] - jnp.arange(tc)[:, None]
    toe = jnp.where((lag >= 0)[..., None, None, None], kern[jnp.clip(lag, 0, tc - 1)], 0.0)
    m = jnp.transpose(toe, (2, 0, 4, 1, 3)).reshape(g, tc * h, tc * h)
    rev_r, rev_i = pow_r[::-1], pow_i[::-1]
    s_r = jnp.transpose(rev_r[..., None] * bbar_r - rev_i[..., None] * bbar_i, (1, 0, 3, 2)).reshape(g, tc * h, p)
    s_i = jnp.transpose(rev_r[..., None] * bbar_i + rev_i[..., None] * bbar_r, (1, 0, 3, 2)).reshape(g, tc * h, p)
    a1_r = jnp.concatenate([pow_r[1:], at_r[None]], axis=0)
    a1_i = jnp.concatenate([pow_i[1:], at_i[None]], axis=0)
    p_r = (c_re[None] * a1_r[:, :, None, :] - c_im[None] * a1_i[:, :, None, :])
    p_i = (-c_re[None] * a1_i[:, :, None, :] - c_im[None] * a1_r[:, :, None, :])
    p_r = jnp.transpose(p_r, (1, 3, 0, 2)).reshape(g, p, tc * h)
    p_i = jnp.transpose(p_i, (1, 3, 0, 2)).reshape(g, p, tc * h)

    w = tc * h
    z_s = jnp.zeros((g // 2, w, p), F32)
    z_p = jnp.zeros((g // 2, p, w), F32)

    def pair_s(s):
        return jnp.concatenate([jnp.concatenate([s[0::2], z_s], axis=2),
                                jnp.concatenate([z_s, s[1::2]], axis=2)], axis=1)

    def pair_p(q):
        return jnp.concatenate([jnp.concatenate([q[0::2], z_p], axis=2),
                                jnp.concatenate([z_p, q[1::2]], axis=2)], axis=1)

    a_pair = jnp.stack([at_r.reshape(g // 2, 2 * p), at_i.reshape(g // 2, 2 * p)], axis=1)
    d_row = jnp.tile(d_skip, (1, tc)).reshape(g // 2, 2, 1, w)
    return (m.astype(BF16).reshape(g // 2, 2, w, w), pair_s(s_r), pair_s(s_i),
            pair_p(p_r).astype(BF16), pair_p(p_i).astype(BF16), a_pair, d_row)


def _s5_kernel(u_ref, m_ref, sr_ref, si_ref, pr_ref, pi_ref, a_ref, d_ref, h0r_ref, h0i_ref,
               y_ref, fr_ref, fi_ref, locr, loci, xpr, xpi, *, n_chunks, nb, w):
    u0, u1 = u_ref[0], u_ref[1]
    locr[...] = _dot_hi(u0, sr_ref[:w, :]) + _dot_hi(u1, sr_ref[w:, :])
    loci[...] = _dot_hi(u0, si_ref[:w, :]) + _dot_hi(u1, si_ref[w:, :])
    ar, ai = a_ref[0:1, :], a_ref[1:2, :]

    def body(c, carry):
        xr, xi = carry
        rows = pl.ds(pl.multiple_of(c * nb, nb), nb)
        xpr[rows, :] = xr
        xpi[rows, :] = xi
        return (ar * xr - ai * xi + locr[rows, :], ar * xi + ai * xr + loci[rows, :])

    xr, xi = lax.fori_loop(0, n_chunks, body, (h0r_ref[...], h0i_ref[...]))
    fr_ref[...] = xr
    fi_ref[...] = xi
    xr_prev = xpr[...].astype(BF16)
    xi_prev = xpi[...].astype(BF16)
    for j in range(2):
        uj = u_ref[j]
        y_ref[j] = (_dot(uj.astype(BF16), m_ref[j])
                    + _dot(xr_prev, pr_ref[:, j * w:(j + 1) * w])
                    + _dot(xi_prev, pi_ref[:, j * w:(j + 1) * w])
                    + uj * d_ref[j])


def _s5(u_grouped, tables, h0_r, h0_i, *, n_chunks, nb):
    m, s_r, s_i, p_r, p_i, a_pair, d_row = tables
    g, rows, w = u_grouped.shape
    gp = g // 2
    u_pair = u_grouped.reshape(gp, 2, rows, w)
    lead = lambda *tail: pl.BlockSpec((None,) + tail, lambda i: (i,) + (0,) * len(tail))
    st = 2 * SSM_STATE
    return pl.pallas_call(
        functools.partial(_s5_kernel, n_chunks=n_chunks, nb=nb, w=w),
        grid=(gp,),
        in_specs=[lead(2, rows, w), lead(2, w, w), lead(2 * w, st), lead(2 * w, st), lead(st, 2 * w),
                  lead(st, 2 * w), lead(2, st), lead(2, 1, w), lead(nb, st), lead(nb, st)],
        out_specs=[lead(2, rows, w), lead(nb, st), lead(nb, st)],
        out_shape=(jax.ShapeDtypeStruct((gp, 2, rows, w), F32), jax.ShapeDtypeStruct((gp, nb, st), F32),
                   jax.ShapeDtypeStruct((gp, nb, st), F32)),
        scratch_shapes=[pltpu.VMEM((rows, st), F32)] * 4,
        compiler_params=_cparams("arbitrary"),
        name=f"s5_c{n_chunks}",
    )(u_pair, m, s_r, s_i, p_r, p_i, a_pair, d_row, h0_r, h0_i)


def _log_sigmoid(x):
    return jnp.minimum(x, 0.0) - jnp.log1p(jnp.exp(-jnp.abs(x)))


def _mlstm_gates(gates, bias_row, lc):
    g = gates + bias_row
    col = lax.broadcasted_iota(jnp.int32, (1, GATE_PAD), 1)
    gl = jnp.where(col >= ML_HEADS, _log_sigmoid(g), g)
    r = lax.broadcasted_iota(jnp.int32, (lc, lc), 0)
    c = lax.broadcasted_iota(jnp.int32, (lc, lc), 1)
    tril = (r >= c).astype(F32)
    bcols = _dot_hi(tril, gl)
    sel = (lax.broadcasted_iota(jnp.int32, (8, GATE_PAD), 0)
           == lax.broadcasted_iota(jnp.int32, (8, GATE_PAD), 1)).astype(F32)
    nt = (((1,), (1,)), ((), ()))
    grows = lax.dot_general(sel, gl, nt, precision=HIGHEST, preferred_element_type=F32)
    brows = lax.dot_general(sel, bcols, nt, precision=HIGHEST, preferred_element_type=F32)
    return gl, bcols, grows, brows, (r >= c)


def _mlstm_head(q, k, v, ic, bc, ir, br, causal, c_state, n_state, m_state, lc):
    dmat = jnp.where(causal, bc - br + ir, -jnp.inf)
    inter = bc + m_state
    mt = jnp.maximum(inter, jnp.max(dmat, axis=1, keepdims=True))
    w_in = jnp.exp(dmat - mt)
    w_out = jnp.exp(inter - mt)
    nt = (((1,), (1,)), ((), ()))
    s = lax.dot_general(q, k, nt, preferred_element_type=F32) * w_in
    num = w_out * _dot(q, c_state.astype(BF16)) + _dot(s.astype(BF16), v)
    qf = q.astype(F32)
    qn = w_out * jnp.sum(qf * n_state, axis=1, keepdims=True) + jnp.sum(s, axis=1, keepdims=True)
    h = num / jnp.maximum(jnp.abs(qn), jnp.exp(-mt))
    m_end = mt[lc - 1:lc, :]
    b_last = bc[lc - 1:lc, :]
    decay = jnp.exp(b_last + m_state - m_end)
    ws = jnp.exp(b_last - bc + ic - m_end)
    kw = k.astype(F32) * ws
    tn = (((0,), (0,)), ((), ()))
    c_new = decay * c_state + lax.dot_general(kw.astype(BF16), v, tn, preferred_element_type=F32)
    n_new = decay * n_state + jnp.sum(kw, axis=0, keepdims=True)
    return h, c_new, n_new, m_end


def _mlstm_chunk(qkv, o, gates, bias_row, get_state, put_state, lc):
    gl, bcols, grows, brows, causal = _mlstm_gates(gates, bias_row, lc)
    outs = []
    for hd in range(ML_HEADS):
        q = qkv[:, hd * ML_DK:(hd + 1) * ML_DK]
        k = qkv[:, D_ML + hd * ML_DK:D_ML + (hd + 1) * ML_DK]
        v = qkv[:, 2 * D_ML + hd * ML_DK:2 * D_ML + (hd + 1) * ML_DK]
        f = ML_HEADS + hd
        c_state, n_state, m_state = get_state(hd)
        h, c_new, n_new, m_new = _mlstm_head(
            q, k, v, gl[:, hd:hd + 1], bcols[:, f:f + 1], grows[hd:hd + 1, :], brows[f:f + 1, :],
            causal, c_state, n_state, m_state, lc)
        put_state(hd, c_new, n_new, m_new)
        outs.append(jax.nn.sigmoid(o[:, hd * ML_DK:(hd + 1) * ML_DK]) * h)
    return jnp.concatenate(outs, axis=1)


def _mlstm_prompt_kernel(qkv_ref, o_ref, gt_ref, bias_ref, h_ref, c_ref, n_ref, m_ref, *, lc):
    @pl.when(pl.program_id(1) == 0)
    def _():
        c_ref[...] = jnp.zeros_like(c_ref)
        n_ref[...] = jnp.zeros_like(n_ref)
        m_ref[...] = jnp.zeros_like(m_ref)

    def get_state(hd):
        return c_ref[hd], n_ref[hd], m_ref[hd][:, 0:1]

    def put_state(hd, c_new, n_new, m_new):
        c_ref[hd] = c_new
        n_ref[hd] = n_new
        m_ref[hd] = jnp.broadcast_to(m_new, (1, LANES))

    h = _mlstm_chunk(qkv_ref[...], o_ref[...], gt_ref[...], bias_ref[...], get_state, put_state, lc)
    h_ref[...] = h.astype(BF16)


def _mlstm_prompt(qkv, o, gates, bias_row, bsz, seq):
    lc = ML_CHUNK_PROMPT
    nc = seq // lc
    row = lambda width: pl.BlockSpec((lc, width), lambda b, c: (b * nc + c, 0))
    st = lambda *tail: pl.BlockSpec((None, ML_HEADS) + tail, lambda b, c: (b, 0) + (0,) * len(tail))
    return pl.pallas_call(
        functools.partial(_mlstm_prompt_kernel, lc=lc),
        grid=(bsz, nc),
        in_specs=[row(3 * D_ML), row(D_ML), row(GATE_PAD), pl.BlockSpec((1, GATE_PAD), lambda b, c: (0, 0))],
        out_specs=[row(D_ML), st(ML_DK, ML_DK), st(1, ML_DK), st(1, LANES)],
        out_shape=(jax.ShapeDtypeStruct((bsz * seq, D_ML), BF16),
                   jax.ShapeDtypeStruct((bsz, ML_HEADS, ML_DK, ML_DK), F32),
                   jax.ShapeDtypeStruct((bsz, ML_HEADS, 1, ML_DK), F32),
                   jax.ShapeDtypeStruct((bsz, ML_HEADS, 1, LANES), F32)),
        compiler_params=_cparams("arbitrary", "arbitrary"),
        name="mlstm_prompt",
    )(qkv, o, gates, bias_row)


def _mlstm_sample_kernel(qkv_ref, o_ref, gt_ref, bias_ref, c0_ref, n0_ref, m0_ref,
                         h_ref, c_ref, n_ref, m_ref, qkv_s, *, lc, nb):
    qkv_s[...] = qkv_ref[...].astype(F32)

    def body(b, carry):
        rows = pl.ds(pl.multiple_of(b * lc, lc), lc)

        def get_state(hd):
            return c0_ref[b, hd], n0_ref[b, hd], m0_ref[b, hd][:, 0:1]

        def put_state(hd, c_new, n_new, m_new):
            c_ref[b, hd] = c_new
            n_ref[b, hd] = n_new
            m_ref[b, hd] = jnp.broadcast_to(m_new, (1, LANES))

        h = _mlstm_chunk(qkv_s[rows, :].astype(BF16), o_ref[rows, :], gt_ref[rows, :], bias_ref[...],
                         get_state, put_state, lc)
        h_ref[rows, :] = h
        return carry

    lax.fori_loop(0, nb, body, 0)


def _mlstm_sample(qkv, o, gates, bias_row, c0, n0, m0, row_block_offset, bsz, seq):
    nb = 8
    lc = seq
    rows = nb * lc
    off = row_block_offset // rows
    row = lambda width: pl.BlockSpec((rows, width), lambda i: (off + i, 0))
    st = lambda *tail: pl.BlockSpec((nb, ML_HEADS) + tail, lambda i: (i, 0) + (0,) * len(tail))
    return pl.pallas_call(
        functools.partial(_mlstm_sample_kernel, lc=lc, nb=nb),
        grid=(bsz // nb,),
        in_specs=[row(3 * D_ML), row(D_ML), row(GATE_PAD), pl.BlockSpec((1, GATE_PAD), lambda i: (0, 0)),
                  st(ML_DK, ML_DK), st(1, ML_DK), st(1, LANES)],
        out_specs=[pl.BlockSpec((rows, D_ML), lambda i: (i, 0)), st(ML_DK, ML_DK), st(1, ML_DK), st(1, LANES)],
        out_shape=(jax.ShapeDtypeStruct((bsz * seq, D_ML), F32),
                   jax.ShapeDtypeStruct((bsz, ML_HEADS, ML_DK, ML_DK), F32),
                   jax.ShapeDtypeStruct((bsz, ML_HEADS, 1, ML_DK), F32),
                   jax.ShapeDtypeStruct((bsz, ML_HEADS, 1, LANES), F32)),
        scratch_shapes=[pltpu.VMEM((rows, 3 * D_ML), F32)],
        compiler_params=_cparams("arbitrary"),
        name="mlstm_sample",
    )(qkv, o, gates, bias_row, c0, n0, m0)


def _merge_kernel(xp_ref, xs_ref, ypre_ref, ym_ref, gsm_ref, wglu_ref, bglu_ref, wsu_ref, wmu_ref,
                  wout_ref, gffn_ref, wr_ref, br_ref, h1_ref, xn_ref, lg_ref, *, n_prompt_blocks):
    i = pl.program_id(0)
    x = jnp.where(i < n_prompt_blocks, xp_ref[...], xs_ref[...])
    y = jax.nn.gelu(ypre_ref[...])
    ys = y * jax.nn.sigmoid(_dot(y.astype(BF16), wglu_ref[...]) + bglu_ref[...])
    gsm = gsm_ref[...]
    merged = (jax.nn.sigmoid(gsm[:, :D_MODEL]) * _dot(ys.astype(BF16), wsu_ref[...])
              + jax.nn.sigmoid(gsm[:, D_MODEL:]) * _dot(ym_ref[...], wmu_ref[...]))
    h1 = x + _dot(merged.astype(BF16), wout_ref[...])
    h1_ref[...] = h1
    xn = _rms(h1, gffn_ref[...])
    xn_ref[...] = xn.astype(BF16)
    lg_ref[...] = _dot_hi(xn, wr_ref[...]) + br_ref[...]


def _merge(xp, xs, y_pre, ym, gsm, w_glu, b_glu, w_ssm_up, w_ml_up, w_out, g_ffn, w_router, b_router):
    tm = TM_TOKENS
    t = y_pre.shape[0]
    npb = xp.shape[0] // tm
    xp_spec, xs_spec = _two_source_specs(tm, D_MODEL, npb)
    row = lambda width: pl.BlockSpec((tm, width), lambda i: (i, 0))
    wr = jnp.pad(w_router, ((0, 0), (0, LANES - N_EXPERTS)))
    br = jnp.pad(b_router, (0, LANES - N_EXPERTS)).reshape(1, LANES)
    return pl.pallas_call(
        functools.partial(_merge_kernel, n_prompt_blocks=npb),
        grid=(t // tm,),
        in_specs=[xp_spec, xs_spec, row(D_SSM), row(D_ML), row(2 * D_MODEL), _full((D_SSM, D_SSM)),
                  _full((1, D_SSM)), _full((D_SSM, D_MODEL)), _full((D_ML, D_MODEL)),
                  _full((D_MODEL, D_MODEL)), _full((1, D_MODEL)), _full((D_MODEL, LANES)), _full((1, LANES))],
        out_specs=[row(D_MODEL), row(D_MODEL), row(LANES)],
        out_shape=(jax.ShapeDtypeStruct((t, D_MODEL), F32), jax.ShapeDtypeStruct((t, D_MODEL), BF16),
                   jax.ShapeDtypeStruct((t, LANES), F32)),
        compiler_params=_cparams("arbitrary"),
        name="merge",
    )(xp, xs, y_pre, ym, gsm, w_glu.astype(BF16), b_glu.reshape(1, D_SSM), w_ssm_up.astype(BF16),
      w_ml_up.astype(BF16), w_out.astype(BF16), g_ffn.reshape(1, D_MODEL), wr, br)


def _route(logits, tm):
    t = logits.shape[0]
    top_val, top_idx = lax.top_k(logits, TOP_K)
    top_w = jax.nn.softmax(top_val, axis=-1)
    flat_e = top_idx.reshape(-1)
    n_assign = t * TOP_K
    onehot = (flat_e[:, None] == jnp.arange(N_EXPERTS, dtype=jnp.int32)[None, :]).astype(jnp.int32)
    csum = jnp.cumsum(onehot, axis=0)
    rank = jnp.sum(csum * onehot, axis=1) - 1
    counts = csum[-1]
    padded = (counts + tm - 1) // tm * tm
    pad_end = jnp.cumsum(padded)
    pad_start = pad_end - padded
    pos = pad_start[flat_e] + rank
    n_blocks = -(-(n_assign + N_EXPERTS * (tm - 1)) // tm)
    row_tok = jnp.zeros((n_blocks * tm,), jnp.int32).at[pos].set(
        jnp.arange(n_assign, dtype=jnp.int32) // TOP_K)
    block_e = jnp.minimum(
        jnp.searchsorted(pad_end, jnp.arange(n_blocks, dtype=jnp.int32) * tm, side='right'),
        N_EXPERTS - 1).astype(jnp.int32)
    n_valid = (pad_end[-1] // tm).astype(jnp.int32).reshape(1)
    return top_w, pos.reshape(t, TOP_K), row_tok, block_e, n_valid


def _moe_kernel(be_ref, nv_ref, x_ref, wg_ref, bg_ref, wu_ref, bu_ref, wd_ref, bd_ref, y_ref,
                wg_s, wu_s, wd_s):
    i = pl.program_id(0)
    e = be_ref[i]
    prev = be_ref[jnp.maximum(i - 1, 0)]
    valid = i < nv_ref[0]
    first = jnp.logical_or(i == 0, e != prev)

    @pl.when(jnp.logical_and(valid, first))
    def _():
        wg_s[...] = wg_ref[...].astype(BF16)
        wu_s[...] = wu_ref[...].astype(BF16)
        wd_s[...] = wd_ref[...].astype(BF16)

    @pl.when(valid)
    def _():
        x = x_ref[...]
        g = jnp.minimum(_dot(x, wg_s[...]) + bg_ref[...], SWIGLU_LIMIT)
        u = jnp.clip(_dot(x, wu_s[...]) + bu_ref[...], -SWIGLU_LIMIT, SWIGLU_LIMIT)
        a = g * jax.nn.sigmoid(SWIGLU_ALPHA * g) * (u + 1.0)
        y_ref[...] = _dot(a.astype(BF16), wd_s[...]) + bd_ref[...]

    @pl.when(jnp.logical_not(valid))
    def _():
        y_ref[...] = jnp.zeros_like(y_ref)


def _moe_experts(x_rows, block_e, n_valid, w_gate, b_gate, w_up, b_up, w_down, b_down):
    tm = TM_MOE
    n_rows = x_rows.shape[0]
    wspec = pl.BlockSpec((None, D_MODEL, D_MODEL), lambda i, be, nv: (be[i], 0, 0))
    bspec = pl.BlockSpec((None, 1, D_MODEL), lambda i, be, nv: (be[i], 0, 0))
    rows = pl.BlockSpec((tm, D_MODEL), lambda i, be, nv: (i, 0))
    grid_spec = pltpu.PrefetchScalarGridSpec(
        num_scalar_prefetch=2,
        grid=(n_rows // tm,),
        in_specs=[rows, wspec, bspec, wspec, bspec, wspec, bspec],
        out_specs=rows,
        scratch_shapes=[pltpu.VMEM((D_MODEL, D_MODEL), BF16)] * 3,
    )
    b3 = lambda b: b.reshape(N_EXPERTS, 1, D_MODEL)
    return pl.pallas_call(
        _moe_kernel,
        grid_spec=grid_spec,
        out_shape=jax.ShapeDtypeStruct((n_rows, D_MODEL), F32),
        compiler_params=_cparams("arbitrary"),
        name="moe_experts",
    )(block_e, n_valid, x_rows, w_gate, b3(b_gate), w_up, b3(b_up), w_down, b3(b_down))


def _ple_kernel(h1_ref, moe_ref, p_ref, gple_ref, wg_ref, wp_ref, gfin_ref, y_ref):
    h2 = h1_ref[...] + moe_ref[...]
    gate = jax.nn.sigmoid(_dot(_rms(h2, gple_ref[...]).astype(BF16), wg_ref[...]))
    h3 = h2 + gate * _dot(p_ref[...].astype(BF16), wp_ref[...])
    y_ref[...] = _rms(h3, gfin_ref[...])


def _ple_final(h1, moe, p, row_block_offset, g_ple, w_ple_gate, w_ple_proj, g_final):
    tm = TM_TOKENS
    n = p.shape[0]
    off = row_block_offset // tm
    src = lambda width: pl.BlockSpec((tm, width), lambda i: (off + i, 0))
    loc = lambda width: pl.BlockSpec((tm, width), lambda i: (i, 0))
    return pl.pallas_call(
        _ple_kernel,
        grid=(n // tm,),
        in_specs=[src(D_MODEL), src(D_MODEL), loc(D_PLE), _full((1, D_MODEL)), _full((D_MODEL, D_MODEL)),
                  _full((D_PLE, D_MODEL)), _full((1, D_MODEL))],
        out_specs=loc(D_MODEL),
        out_shape=jax.ShapeDtypeStruct((n, D_MODEL), F32),
        compiler_params=_cparams("arbitrary"),
        name="ple_final",
    )(h1, moe, p, g_ple.reshape(1, D_MODEL), w_ple_gate.astype(BF16), w_ple_proj.astype(BF16),
      g_final.reshape(1, D_MODEL))


def kernel(x_prompt, x_sample, p_prompt, p_sample, state_ssm_re, state_ssm_im, state_ml_c, state_ml_n, state_ml_m, g_mix, w_in, ssm_a_re, ssm_a_im, ssm_log_dt, ssm_b_re, ssm_b_im, ssm_c_re, ssm_c_im, ssm_d, ssm_w_glu, ssm_b_glu, ml_b_ig, ml_b_fg, w_ssm_up, w_ml_up, w_out, g_ffn, w_router, b_router, w_gate, b_gate, w_up, b_up, w_down, b_down, g_ple, w_ple_gate, w_ple_proj, g_final):
    assert g_mix.shape[0] == 1, "single-layer trunk"
    bp, lp, _ = x_prompt.shape
    bs, ls, _ = x_sample.shape
    tp, ts = bp * lp, bs * ls
    g, p, h = N_GROUPS, SSM_STATE, SSM_GROUP
    xp = x_prompt.reshape(tp, D_MODEL)
    xs = x_sample.reshape(ts, D_MODEL)

    u, qkv, o, gates, gsm = _inproj(xp, xs, g_mix[0], w_in[0])

    s5_args = (ssm_a_re[0], ssm_a_im[0], ssm_log_dt[0], ssm_b_re[0], ssm_b_im[0], ssm_c_re[0],
               ssm_c_im[0], ssm_d[0])
    tc = S5_CHUNK
    ncp = lp // tc
    u_p = u[:tp].reshape(bp, ncp, tc, g, h).transpose(3, 1, 0, 2, 4).reshape(g, ncp * bp, tc * h)
    zero_state = jnp.zeros((g // 2, bp, 2 * p), F32)
    y_p, fr_p, fi_p = _s5(u_p, _s5_tables(*s5_args, tc), zero_state, zero_state, n_chunks=ncp, nb=bp)
    y_p = y_p.reshape(g, ncp, bp, tc, h).transpose(2, 1, 3, 0, 4).reshape(tp, D_SSM)
    u_s = u[tp:].reshape(bs, ls, g, h).transpose(2, 0, 1, 3).reshape(g, bs, ls * h)
    pair_state = lambda s: s.reshape(bs, g // 2, 2 * p).transpose(1, 0, 2)
    y_s, fr_s, fi_s = _s5(u_s, _s5_tables(*s5_args, ls), pair_state(state_ssm_re[0]),
                          pair_state(state_ssm_im[0]), n_chunks=1, nb=bs)
    y_s = y_s.reshape(g, bs, ls, h).transpose(1, 2, 0, 3).reshape(ts, D_SSM)
    y_pre = jnp.concatenate([y_p, y_s], axis=0)
    unpair = lambda f, b: f.transpose(1, 0, 2).reshape(1, b, g, p)

    bias_row = jnp.pad(jnp.concatenate([ml_b_ig[0], ml_b_fg[0]]), (0, GATE_PAD - 2 * ML_HEADS)).reshape(1, GATE_PAD)
    hm_p, c_p, n_p, m_p = _mlstm_prompt(qkv, o, gates, bias_row, bp, lp)
    m0 = jnp.broadcast_to(state_ml_m[0][:, :, None, None], (bs, ML_HEADS, 1, LANES))
    hm_s, c_s, n_s, m_s = _mlstm_sample(qkv, o, gates, bias_row, state_ml_c[0],
                                        state_ml_n[0].reshape(bs, ML_HEADS, 1, ML_DK), m0, tp, bs, ls)
    ym = jnp.concatenate([hm_p, hm_s.astype(BF16)], axis=0)

    h1, xn, logits = _merge(xp, xs, y_pre, ym, gsm, ssm_w_glu[0], ssm_b_glu[0], w_ssm_up[0], w_ml_up[0],
                            w_out[0], g_ffn[0], w_router[0], b_router[0])

    top_w, pos, row_tok, block_e, n_valid = _route(logits[:, :N_EXPERTS], TM_MOE)
    y_rows = _moe_experts(xn[row_tok], block_e, n_valid, w_gate[0], b_gate[0], w_up[0], b_up[0],
                          w_down[0], b_down[0])
    moe = jnp.sum(y_rows[pos] * top_w[..., None], axis=1)

    y_prompt = _ple_final(h1, moe, p_prompt[0].reshape(tp, D_PLE), 0, g_ple[0], w_ple_gate[0],
                          w_ple_proj[0], g_final)
    y_sample = _ple_final(h1, moe, p_sample[0].reshape(ts, D_PLE), tp, g_ple[0], w_ple_gate[0],
                          w_ple_proj[0], g_final)

    return (y_prompt.reshape(bp, lp, D_MODEL), y_sample.reshape(bs, ls, D_MODEL),
            unpair(fr_p, bp), unpair(fi_p, bp), c_p[None], n_p.reshape(1, bp, ML_HEADS, ML_DK),
            m_p[:, :, 0, 0][None],
            unpair(fr_s, bs), unpair(fi_s, bs), c_s[None], n_s.reshape(1, bs, ML_HEADS, ML_DK),
            m_s[:, :, 0, 0][None])
```

```python
import functools

import jax
import jax.numpy as jnp
from jax import lax
from jax.experimental import pallas as pl
from jax.experimental.pallas import tpu as pltpu

F32 = jnp.float32
BF16 = jnp.bfloat16
HIGHEST = lax.Precision.HIGHEST

D_MODEL = 1024
D_SSM = 512
SSM_GROUP = 16
N_GROUPS = 32
SSM_STATE = 64
ML_HEADS = 4
ML_DK = 128
D_ML = 512
N_EXPERTS = 32
TOP_K = 4
SWIGLU_LIMIT = 7.0
SWIGLU_ALPHA = 1.702
D_PLE = 256
RMS_EPS = 1e-6

LANES = 128
GATE_PAD = LANES
S5_CHUNK = 16
ML_CHUNK_PROMPT = 256
TM_TOKENS = 256
TM_MOE = 256
RANK_BLOCK = 256
VMEM_LIMIT = 56 * 1024 * 1024


def _cparams(*sem):
    return pltpu.CompilerParams(dimension_semantics=sem, vmem_limit_bytes=VMEM_LIMIT)


def _rms(x, g):
    return x * lax.rsqrt(jnp.mean(x * x, axis=-1, keepdims=True) + RMS_EPS) * g


def _dot(a, b):
    return jnp.dot(a, b, preferred_element_type=F32)


def _dot_hi(a, b):
    return jnp.dot(a, b, preferred_element_type=F32, precision=HIGHEST)


def _full(shape):
    n = len(shape)
    return pl.BlockSpec(shape, lambda *_: (0,) * n)


def _inproj_kernel(xp_ref, xs_ref, g_ref, wu_ref, wqkv_ref, wo_ref, wgt_ref, wgsm_ref,
                   u_ref, qkv_ref, o_ref, gt_ref, gsm_ref, *, n_prompt_blocks):
    i = pl.program_id(0)
    x = jnp.where(i < n_prompt_blocks, xp_ref[...], xs_ref[...])
    hn = _rms(x, g_ref[...]).astype(BF16)
    u_ref[...] = _dot(hn, wu_ref[...])
    qkv = _dot(hn, wqkv_ref[...])
    col = lax.broadcasted_iota(jnp.int32, (1, 3 * D_ML), 1)
    k_scale = jnp.where((col >= D_ML) & (col < 2 * D_ML), ML_DK ** -0.5, 1.0).astype(F32)
    qkv_ref[...] = (qkv * k_scale).astype(BF16)
    o_ref[...] = _dot(hn, wo_ref[...])
    gt_ref[...] = _dot(hn, wgt_ref[...])
    gsm_ref[...] = _dot(hn, wgsm_ref[...])


def _two_source_specs(tm, width, n_prompt_blocks):
    last = n_prompt_blocks - 1
    return (pl.BlockSpec((tm, width), lambda i: (jnp.minimum(i, last), 0)),
            pl.BlockSpec((tm, width), lambda i: (jnp.maximum(i - n_prompt_blocks, 0), 0)))


def _inproj(xp, xs, g_mix, w_in):
    tm = TM_TOKENS
    tp, ts = xp.shape[0], xs.shape[0]
    t = tp + ts
    npb = tp // tm
    w = w_in.astype(BF16)
    o0 = D_SSM
    wu = w[:, :o0]
    wqkv = w[:, o0:o0 + 3 * D_ML]
    wo = w[:, o0 + 3 * D_ML:o0 + 4 * D_ML]
    g0 = o0 + 4 * D_ML
    wgt = jnp.pad(w[:, g0:g0 + 2 * ML_HEADS], ((0, 0), (0, GATE_PAD - 2 * ML_HEADS)))
    wgsm = w[:, g0 + 2 * ML_HEADS:]
    xp_spec, xs_spec = _two_source_specs(tm, D_MODEL, npb)
    outs = (jax.ShapeDtypeStruct((t, D_SSM), F32), jax.ShapeDtypeStruct((t, 3 * D_ML), BF16),
            jax.ShapeDtypeStruct((t, D_ML), F32), jax.ShapeDtypeStruct((t, GATE_PAD), F32),
            jax.ShapeDtypeStruct((t, 2 * D_MODEL), F32))
    row = lambda width: pl.BlockSpec((tm, width), lambda i: (i, 0))
    return pl.pallas_call(
        functools.partial(_inproj_kernel, n_prompt_blocks=npb),
        grid=(t // tm,),
        in_specs=[xp_spec, xs_spec, _full((1, D_MODEL)), _full(wu.shape), _full(wqkv.shape),
                  _full(wo.shape), _full(wgt.shape), _full(wgsm.shape)],
        out_specs=[row(D_SSM), row(3 * D_ML), row(D_ML), row(GATE_PAD), row(2 * D_MODEL)],
        out_shape=outs,
        compiler_params=_cparams("arbitrary"),
        name="inproj",
    )(xp, xs, g_mix.reshape(1, D_MODEL), wu, wqkv, wo, wgt, wgsm)


S5_TILES = D_SSM // LANES
S5_TILE_GROUPS = LANES // SSM_GROUP
S5_TILE_STATE = S5_TILE_GROUPS * SSM_STATE


def _block_diag_tiles(x):
    gt = S5_TILE_GROUPS
    _, r, c = x.shape
    x4 = x.reshape(S5_TILES, gt, r, c)
    eye = jnp.eye(gt, dtype=x.dtype)
    return (x4[:, :, :, None, :] * eye[None, :, None, :, None]).reshape(S5_TILES, gt * r, gt * c)


def _s5_tables(a_re, a_im, log_dt, b_re, b_im, c_re, c_im, d_skip, tc):
    ein = functools.partial(jnp.einsum, precision=HIGHEST)
    dt = jnp.exp(log_dt)[:, None]
    mag = jnp.exp(a_re * dt)
    abar_r, abar_i = mag * jnp.cos(a_im * dt), mag * jnp.sin(a_im * dt)
    den = a_re * a_re + a_im * a_im
    nr, ni = abar_r - 1.0, abar_i
    coef_r = (nr * a_re + ni * a_im) / den
    coef_i = (ni * a_re - nr * a_im) / den
    bbar_r = coef_r[..., None] * b_re - coef_i[..., None] * b_im
    bbar_i = coef_r[..., None] * b_im + coef_i[..., None] * b_re

    def step(carry, _):
        pr, pi = carry
        return (pr * abar_r - pi * abar_i, pr * abar_i + pi * abar_r), (pr, pi)

    (at_r, at_i), (pow_r, pow_i) = lax.scan(step, (jnp.ones_like(abar_r), jnp.zeros_like(abar_r)),
                                            None, length=tc)
    ab_r = pow_r[..., None] * bbar_r - pow_i[..., None] * bbar_i
    ab_i = pow_r[..., None] * bbar_i + pow_i[..., None] * bbar_r
    kern = ein('ghp,jgpk->jgkh', c_re, ab_r) - ein('ghp,jgpk->jgkh', c_im, ab_i)
    bd_lag = [_block_diag_tiles(kern[j]) for j in range(tc)]
    zero = jnp.zeros_like(bd_lag[0])
    lag = lambda j: bd_lag[j] if j >= 0 else zero
    half = tc // 2
    toe = jnp.stack([jnp.concatenate([jnp.concatenate([lag(2 * d), lag(2 * d + 1)], axis=2),
                                      jnp.concatenate([lag(2 * d - 1), lag(2 * d)], axis=2)], axis=1)
                     for d in range(half)], axis=1)
    rev_r, rev_i = pow_r[::-1], pow_i[::-1]
    s_r = jnp.transpose(rev_r[..., None] * bbar_r - rev_i[..., None] * bbar_i, (0, 1, 3, 2))
    s_i = jnp.transpose(rev_r[..., None] * bbar_i + rev_i[..., None] * bbar_r, (0, 1, 3, 2))
    s_step = [jnp.concatenate([_block_diag_tiles(s_r[s]), _block_diag_tiles(s_i[s])], axis=2)
              for s in range(tc)]
    s_tab = jnp.stack([jnp.concatenate([s_step[2 * a], s_step[2 * a + 1]], axis=1) for a in range(half)],
                      axis=1)
    a1_r = jnp.concatenate([pow_r[1:], at_r[None]], axis=0)
    a1_i = jnp.concatenate([pow_i[1:], at_i[None]], axis=0)
    c_re_t, c_im_t = jnp.transpose(c_re, (0, 2, 1)), jnp.transpose(c_im, (0, 2, 1))
    p_r = c_re_t[None] * a1_r[..., None] - c_im_t[None] * a1_i[..., None]
    p_i = -c_re_t[None] * a1_i[..., None] - c_im_t[None] * a1_r[..., None]
    p_step = [jnp.concatenate([_block_diag_tiles(p_r[t]), _block_diag_tiles(p_i[t])], axis=1)
              for t in range(tc)]
    p_tab = jnp.stack([jnp.concatenate([p_step[2 * a], p_step[2 * a + 1]], axis=2) for a in range(half)],
                      axis=1)
    a_tab = jnp.stack([at_r.reshape(S5_TILES, S5_TILE_STATE), at_i.reshape(S5_TILES, S5_TILE_STATE)], axis=1)
    d_tab = d_skip.reshape(S5_TILES, 1, LANES)
    return toe.astype(BF16), s_tab.astype(BF16), p_tab.astype(BF16), a_tab, d_tab


def _s5_kernel(u_ref, t_ref, s_ref, p_ref, a_ref, d_ref, h0_ref, y_ref, f_ref, loc, xprev,
               *, tc, n_chunks, nb):
    r = n_chunks * nb
    half = tc // 2
    ns = S5_TILE_STATE
    step_rows = lambda t: pl.ds(t, r, stride=tc)
    v = [u_ref[step_rows(t), :] for t in range(tc)]
    vp = [jnp.concatenate([v[2 * a].astype(BF16), v[2 * a + 1].astype(BF16)], axis=1) for a in range(half)]
    acc = _dot(vp[0], s_ref[0])
    for a in range(1, half):
        acc = acc + _dot(vp[a], s_ref[a])
    nt = ns // LANES
    lane_tile = lambda k: slice(k * LANES, (k + 1) * LANES)
    for k in range(2 * nt):
        loc[k] = acc[:, lane_tile(k)]
    abar = a_ref[...]

    def body(c, carry):
        rows = pl.ds(c, nb, stride=n_chunks) if n_chunks > 1 else pl.ds(0, nb)
        new = []
        for k in range(nt):
            xr, xi = carry[k], carry[nt + k]
            xprev[k, rows, :] = xr
            xprev[nt + k, rows, :] = xi
            ar, ai = abar[0:1, lane_tile(k)], abar[1:2, lane_tile(k)]
            new.append((ar * xr - ai * xi + loc[k, rows, :], ar * xi + ai * xr + loc[nt + k, rows, :]))
        return tuple(n[0] for n in new) + tuple(n[1] for n in new)

    x_end = lax.fori_loop(0, n_chunks, body, tuple(h0_ref[:, lane_tile(k)] for k in range(2 * nt)))
    for k in range(2 * nt):
        f_ref[:, lane_tile(k)] = x_end[k]
    xp = jnp.concatenate([xprev[k] for k in range(2 * nt)], axis=1).astype(BF16)
    d = d_ref[...]
    for a2 in range(half):
        acc = _dot(xp, p_ref[a2])
        for a in range(a2 + 1):
            acc = acc + _dot(vp[a], t_ref[a2 - a])
        for k in range(2):
            t = 2 * a2 + k
            y_ref[step_rows(t), :] = acc[:, k * LANES:(k + 1) * LANES] + v[t] * d


def _s5(u, row_block_offset, tables, h0, *, tc, n_chunks, nb, n_splits):
    toe, s_tab, p_tab, a_tab, d_tab = tables
    rows = nb * n_chunks * tc
    off = row_block_offset // rows
    half = tc // 2
    ns2 = 2 * S5_TILE_STATE
    tile = lambda *tail: pl.BlockSpec((None,) + tail, lambda j, s: (j,) + (0,) * len(tail))
    state = pl.BlockSpec((None, None, nb, ns2), lambda j, s: (j, s, 0, 0))
    y, f = pl.pallas_call(
        functools.partial(_s5_kernel, tc=tc, n_chunks=n_chunks, nb=nb),
        grid=(S5_TILES, n_splits),
        in_specs=[pl.BlockSpec((rows, LANES), lambda j, s: (off + s, j)),
                  tile(half, 2 * LANES, 2 * LANES), tile(half, 2 * LANES, ns2), tile(half, ns2, 2 * LANES),
                  tile(2, S5_TILE_STATE), tile(1, LANES), state],
        out_specs=[pl.BlockSpec((rows, LANES), lambda j, s: (s, j)), state],
        out_shape=(jax.ShapeDtypeStruct((rows * n_splits, D_SSM), F32),
                   jax.ShapeDtypeStruct((S5_TILES, n_splits, nb, ns2), F32)),
        scratch_shapes=[pltpu.VMEM((ns2 // LANES, nb * n_chunks, LANES), F32)] * 2,
        compiler_params=_cparams("arbitrary", "arbitrary"),
        name=f"s5_c{n_chunks}",
    )(u, toe, s_tab, p_tab, a_tab, d_tab, h0.reshape(S5_TILES, n_splits, nb, ns2))
    return y, f.reshape(S5_TILES, n_splits * nb, ns2)


def _s5_state_to_tiles(s_re, s_im):
    b = s_re.shape[0]
    f = lambda s: s.reshape(b, S5_TILES, S5_TILE_STATE).transpose(1, 0, 2)
    return jnp.concatenate([f(s_re), f(s_im)], axis=2)


def _s5_state_from_tiles(f):
    b = f.shape[1]
    g = lambda s: s.transpose(1, 0, 2).reshape(1, b, N_GROUPS, SSM_STATE)
    return g(f[:, :, :S5_TILE_STATE]), g(f[:, :, S5_TILE_STATE:])


def _log_sigmoid(x):
    return jnp.minimum(x, 0.0) - jnp.log1p(jnp.exp(-jnp.abs(x)))


def _mlstm_gates(gates, bias_row, lc):
    g = gates + bias_row
    col = lax.broadcasted_iota(jnp.int32, (1, GATE_PAD), 1)
    gl = jnp.where(col >= ML_HEADS, _log_sigmoid(g), g)
    r = lax.broadcasted_iota(jnp.int32, (lc, lc), 0)
    c = lax.broadcasted_iota(jnp.int32, (lc, lc), 1)
    tril = (r >= c).astype(F32)
    bcols = _dot_hi(tril, gl)
    sel = (lax.broadcasted_iota(jnp.int32, (8, GATE_PAD), 0)
           == lax.broadcasted_iota(jnp.int32, (8, GATE_PAD), 1)).astype(F32)
    nt = (((1,), (1,)), ((), ()))
    grows = lax.dot_general(sel, gl, nt, precision=HIGHEST, preferred_element_type=F32)
    brows = lax.dot_general(sel, bcols, nt, precision=HIGHEST, preferred_element_type=F32)
    return gl, bcols, grows, brows, (r >= c)


def _mlstm_head(q, k, v, ic, bc, ir, br, causal, c_state, n_state, m_state, lc):
    dmat = jnp.where(causal, bc - br + ir, -jnp.inf)
    inter = bc + m_state
    mt = jnp.maximum(inter, jnp.max(dmat, axis=1, keepdims=True))
    w_in = jnp.exp(dmat - mt)
    w_out = jnp.exp(inter - mt)
    nt = (((1,), (1,)), ((), ()))
    s = lax.dot_general(q, k, nt, preferred_element_type=F32) * w_in
    num = w_out * _dot(q, c_state.astype(BF16)) + _dot(s.astype(BF16), v)
    qf = q.astype(F32)
    qn = w_out * jnp.sum(qf * n_state, axis=1, keepdims=True) + jnp.sum(s, axis=1, keepdims=True)
    h = num / jnp.maximum(jnp.abs(qn), jnp.exp(-mt))
    m_end = mt[lc - 1:lc, :]
    b_last = bc[lc - 1:lc, :]
    decay = jnp.exp(b_last + m_state - m_end)
    ws = jnp.exp(b_last - bc + ic - m_end)
    kw = k.astype(F32) * ws
    tn = (((0,), (0,)), ((), ()))
    c_new = decay * c_state + lax.dot_general(kw.astype(BF16), v, tn, preferred_element_type=F32)
    n_new = decay * n_state + jnp.sum(kw, axis=0, keepdims=True)
    return h, c_new, n_new, m_end


def _mlstm_chunk(qkv, o, gates, bias_row, get_state, put_state, lc):
    gl, bcols, grows, brows, causal = _mlstm_gates(gates, bias_row, lc)
    outs = []
    for hd in range(ML_HEADS):
        q = qkv[:, hd * ML_DK:(hd + 1) * ML_DK]
        k = qkv[:, D_ML + hd * ML_DK:D_ML + (hd + 1) * ML_DK]
        v = qkv[:, 2 * D_ML + hd * ML_DK:2 * D_ML + (hd + 1) * ML_DK]
        f = ML_HEADS + hd
        c_state, n_state, m_state = get_state(hd)
        h, c_new, n_new, m_new = _mlstm_head(
            q, k, v, gl[:, hd:hd + 1], bcols[:, f:f + 1], grows[hd:hd + 1, :], brows[f:f + 1, :],
            causal, c_state, n_state, m_state, lc)
        put_state(hd, c_new, n_new, m_new)
        outs.append(jax.nn.sigmoid(o[:, hd * ML_DK:(hd + 1) * ML_DK]) * h)
    return jnp.concatenate(outs, axis=1)


def _mlstm_prompt_kernel(qkv_ref, o_ref, gt_ref, bias_ref, h_ref, c_ref, n_ref, m_ref, *, lc):
    @pl.when(pl.program_id(1) == 0)
    def _():
        c_ref[...] = jnp.zeros_like(c_ref)
        n_ref[...] = jnp.zeros_like(n_ref)
        m_ref[...] = jnp.zeros_like(m_ref)

    def get_state(hd):
        return c_ref[hd], n_ref[hd], m_ref[hd][:, 0:1]

    def put_state(hd, c_new, n_new, m_new):
        c_ref[hd] = c_new
        n_ref[hd] = n_new
        m_ref[hd] = jnp.broadcast_to(m_new, (1, LANES))

    h = _mlstm_chunk(qkv_ref[...], o_ref[...], gt_ref[...], bias_ref[...], get_state, put_state, lc)
    h_ref[...] = h.astype(BF16)


def _mlstm_prompt(qkv, o, gates, bias_row, bsz, seq):
    lc = ML_CHUNK_PROMPT
    nc = seq // lc
    row = lambda width: pl.BlockSpec((lc, width), lambda b, c: (b * nc + c, 0))
    st = lambda *tail: pl.BlockSpec((None, ML_HEADS) + tail, lambda b, c: (b, 0) + (0,) * len(tail))
    return pl.pallas_call(
        functools.partial(_mlstm_prompt_kernel, lc=lc),
        grid=(bsz, nc),
        in_specs=[row(3 * D_ML), row(D_ML), row(GATE_PAD), pl.BlockSpec((1, GATE_PAD), lambda b, c: (0, 0))],
        out_specs=[row(D_ML), st(ML_DK, ML_DK), st(1, ML_DK), st(1, LANES)],
        out_shape=(jax.ShapeDtypeStruct((bsz * seq, D_ML), BF16),
                   jax.ShapeDtypeStruct((bsz, ML_HEADS, ML_DK, ML_DK), F32),
                   jax.ShapeDtypeStruct((bsz, ML_HEADS, 1, ML_DK), F32),
                   jax.ShapeDtypeStruct((bsz, ML_HEADS, 1, LANES), F32)),
        compiler_params=_cparams("arbitrary", "arbitrary"),
        name="mlstm_prompt",
    )(qkv, o, gates, bias_row)


def _mlstm_sample_kernel(qkv_ref, o_ref, gt_ref, bias_ref, c0_ref, n0_ref, m0_ref,
                         h_ref, c_ref, n_ref, m_ref, qkv_s, *, lc, nb):
    qkv_s[...] = qkv_ref[...].astype(F32)

    def body(b, carry):
        rows = pl.ds(pl.multiple_of(b * lc, lc), lc)

        def get_state(hd):
            return c0_ref[b, hd], n0_ref[b, hd], m0_ref[b, hd][:, 0:1]

        def put_state(hd, c_new, n_new, m_new):
            c_ref[b, hd] = c_new
            n_ref[b, hd] = n_new
            m_ref[b, hd] = jnp.broadcast_to(m_new, (1, LANES))

        h = _mlstm_chunk(qkv_s[rows, :].astype(BF16), o_ref[rows, :], gt_ref[rows, :], bias_ref[...],
                         get_state, put_state, lc)
        h_ref[rows, :] = h
        return carry

    lax.fori_loop(0, nb, body, 0)


def _mlstm_sample(qkv, o, gates, bias_row, c0, n0, m0, row_block_offset, bsz, seq):
    nb = 8
    lc = seq
    rows = nb * lc
    off = row_block_offset // rows
    row = lambda width: pl.BlockSpec((rows, width), lambda i: (off + i, 0))
    st = lambda *tail: pl.BlockSpec((nb, ML_HEADS) + tail, lambda i: (i, 0) + (0,) * len(tail))
    return pl.pallas_call(
        functools.partial(_mlstm_sample_kernel, lc=lc, nb=nb),
        grid=(bsz // nb,),
        in_specs=[row(3 * D_ML), row(D_ML), row(GATE_PAD), pl.BlockSpec((1, GATE_PAD), lambda i: (0, 0)),
                  st(ML_DK, ML_DK), st(1, ML_DK), st(1, LANES)],
        out_specs=[pl.BlockSpec((rows, D_ML), lambda i: (i, 0)), st(ML_DK, ML_DK), st(1, ML_DK), st(1, LANES)],
        out_shape=(jax.ShapeDtypeStruct((bsz * seq, D_ML), F32),
                   jax.ShapeDtypeStruct((bsz, ML_HEADS, ML_DK, ML_DK), F32),
                   jax.ShapeDtypeStruct((bsz, ML_HEADS, 1, ML_DK), F32),
                   jax.ShapeDtypeStruct((bsz, ML_HEADS, 1, LANES), F32)),
        scratch_shapes=[pltpu.VMEM((rows, 3 * D_ML), F32)],
        compiler_params=_cparams("arbitrary"),
        name="mlstm_sample",
    )(qkv, o, gates, bias_row, c0, n0, m0)


def _merge_kernel(xp_ref, xs_ref, ypre_ref, ym_ref, gsm_ref, wglu_ref, bglu_ref, wsu_ref, wmu_ref,
                  wout_ref, gffn_ref, wr_ref, br_ref, h1_ref, xn_ref, lg_ref, *, n_prompt_blocks):
    i = pl.program_id(0)
    x = jnp.where(i < n_prompt_blocks, xp_ref[...], xs_ref[...])
    y = jax.nn.gelu(ypre_ref[...])
    ys = y * jax.nn.sigmoid(_dot(y.astype(BF16), wglu_ref[...]) + bglu_ref[...])
    gsm = gsm_ref[...]
    merged = (jax.nn.sigmoid(gsm[:, :D_MODEL]) * _dot(ys.astype(BF16), wsu_ref[...])
              + jax.nn.sigmoid(gsm[:, D_MODEL:]) * _dot(ym_ref[...], wmu_ref[...]))
    h1 = x + _dot(merged.astype(BF16), wout_ref[...])
    h1_ref[...] = h1
    xn = _rms(h1, gffn_ref[...])
    xn_ref[...] = xn
    lg_ref[...] = _dot_hi(xn, wr_ref[...]) + br_ref[...]


def _merge(xp, xs, y_pre, ym, gsm, w_glu, b_glu, w_ssm_up, w_ml_up, w_out, g_ffn, w_router, b_router):
    tm = TM_TOKENS
    t = y_pre.shape[0]
    npb = xp.shape[0] // tm
    xp_spec, xs_spec = _two_source_specs(tm, D_MODEL, npb)
    row = lambda width: pl.BlockSpec((tm, width), lambda i: (i, 0))
    wr = jnp.pad(w_router, ((0, 0), (0, LANES - N_EXPERTS)))
    br = jnp.pad(b_router, (0, LANES - N_EXPERTS)).reshape(1, LANES)
    return pl.pallas_call(
        functools.partial(_merge_kernel, n_prompt_blocks=npb),
        grid=(t // tm,),
        in_specs=[xp_spec, xs_spec, row(D_SSM), row(D_ML), row(2 * D_MODEL), _full((D_SSM, D_SSM)),
                  _full((1, D_SSM)), _full((D_SSM, D_MODEL)), _full((D_ML, D_MODEL)),
                  _full((D_MODEL, D_MODEL)), _full((1, D_MODEL)), _full((D_MODEL, LANES)), _full((1, LANES))],
        out_specs=[row(D_MODEL), row(D_MODEL), row(LANES)],
        out_shape=(jax.ShapeDtypeStruct((t, D_MODEL), F32), jax.ShapeDtypeStruct((t, D_MODEL), F32),
                   jax.ShapeDtypeStruct((t, LANES), F32)),
        compiler_params=_cparams("arbitrary"),
        name="merge",
    )(xp, xs, y_pre, ym, gsm, w_glu.astype(BF16), b_glu.reshape(1, D_SSM), w_ssm_up.astype(BF16),
      w_ml_up.astype(BF16), w_out.astype(BF16), g_ffn.reshape(1, D_MODEL), wr, br)


def _route(logits, tm):
    t = logits.shape[0]
    top_val, top_idx = lax.top_k(logits, TOP_K)
    top_w = jax.nn.softmax(top_val, axis=-1)
    flat_e = top_idx.reshape(-1)
    n_assign = t * TOP_K
    onehot = (flat_e[:, None] == jnp.arange(N_EXPERTS, dtype=jnp.int32)[None, :])
    rb = RANK_BLOCK
    oh3 = onehot.astype(F32).reshape(n_assign // rb, rb, N_EXPERTS)
    tril = jnp.tril(jnp.ones((rb, rb), F32))
    within = jnp.einsum('ij,bjk->bik', tril, oh3).astype(jnp.int32)
    totals = within[:, -1, :]
    before = jnp.cumsum(totals, axis=0) - totals
    csum = (within + before[:, None, :]).reshape(n_assign, N_EXPERTS)
    rank = jnp.sum(jnp.where(onehot, csum, 0), axis=1) - 1
    counts = before[-1] + totals[-1]
    padded = (counts + tm - 1) // tm * tm
    pad_end = jnp.cumsum(padded)
    pad_start = pad_end - padded
    pos = jnp.sum(jnp.where(onehot, pad_start[None, :], 0), axis=1) + rank
    n_blocks = -(-(n_assign + N_EXPERTS * (tm - 1)) // tm)
    row_tok = jnp.zeros((n_blocks * tm,), jnp.int32).at[pos].set(
        jnp.arange(n_assign, dtype=jnp.int32) // TOP_K)
    block_row0 = jnp.arange(n_blocks, dtype=jnp.int32) * tm
    block_e = jnp.minimum(jnp.sum((pad_end[None, :] <= block_row0[:, None]).astype(jnp.int32), axis=1),
                          N_EXPERTS - 1)
    n_valid = (pad_end[-1] // tm).astype(jnp.int32).reshape(1)
    return top_w, pos.reshape(t, TOP_K), row_tok, block_e, n_valid


def _moe_kernel(be_ref, nv_ref, x_ref, wg_ref, bg_ref, wu_ref, bu_ref, wd_ref, bd_ref, y_ref,
                wg_s, wu_s, wd_s):
    i = pl.program_id(0)
    e = be_ref[i]
    prev = be_ref[jnp.maximum(i - 1, 0)]
    valid = i < nv_ref[0]
    first = jnp.logical_or(i == 0, e != prev)

    @pl.when(jnp.logical_and(valid, first))
    def _():
        wg_s[...] = wg_ref[...].astype(BF16)
        wu_s[...] = wu_ref[...].astype(BF16)
        wd_s[...] = wd_ref[...].astype(BF16)

    @pl.when(valid)
    def _():
        x = x_ref[...].astype(BF16)
        g = jnp.minimum(_dot(x, wg_s[...]) + bg_ref[...], SWIGLU_LIMIT)
        u = jnp.clip(_dot(x, wu_s[...]) + bu_ref[...], -SWIGLU_LIMIT, SWIGLU_LIMIT)
        a = g * jax.nn.sigmoid(SWIGLU_ALPHA * g) * (u + 1.0)
        y_ref[...] = _dot(a.astype(BF16), wd_s[...]) + bd_ref[...]

    @pl.when(jnp.logical_not(valid))
    def _():
        y_ref[...] = jnp.zeros_like(y_ref)


def _moe_experts(x_rows, block_e, n_valid, w_gate, b_gate, w_up, b_up, w_down, b_down):
    tm = TM_MOE
    n_rows = x_rows.shape[0]
    wspec = pl.BlockSpec((None, D_MODEL, D_MODEL), lambda i, be, nv: (be[i], 0, 0))
    bspec = pl.BlockSpec((None, 1, D_MODEL), lambda i, be, nv: (be[i], 0, 0))
    rows = pl.BlockSpec((tm, D_MODEL), lambda i, be, nv: (i, 0))
    grid_spec = pltpu.PrefetchScalarGridSpec(
        num_scalar_prefetch=2,
        grid=(n_rows // tm,),
        in_specs=[rows, wspec, bspec, wspec, bspec, wspec, bspec],
        out_specs=rows,
        scratch_shapes=[pltpu.VMEM((D_MODEL, D_MODEL), BF16)] * 3,
    )
    b3 = lambda b: b.reshape(N_EXPERTS, 1, D_MODEL)
    return pl.pallas_call(
        _moe_kernel,
        grid_spec=grid_spec,
        out_shape=jax.ShapeDtypeStruct((n_rows, D_MODEL), F32),
        compiler_params=_cparams("arbitrary"),
        name="moe_experts",
    )(block_e, n_valid, x_rows, w_gate, b3(b_gate), w_up, b3(b_up), w_down, b3(b_down))


def _ple_kernel(h1_ref, rows_ref, tw_ref, p_ref, gple_ref, wg_ref, wp_ref, gfin_ref, y_ref):
    tw = tw_ref[...]
    h2 = h1_ref[...]
    for k in range(TOP_K):
        h2 = h2 + rows_ref[k] * tw[:, k:k + 1]
    gate = jax.nn.sigmoid(_dot(_rms(h2, gple_ref[...]).astype(BF16), wg_ref[...]))
    h3 = h2 + gate * _dot(p_ref[...].astype(BF16), wp_ref[...])
    y_ref[...] = _rms(h3, gfin_ref[...])


def _ple_final(h1, expert_rows, top_w, p, row_block_offset, g_ple, w_ple_gate, w_ple_proj, g_final):
    tm = TM_TOKENS
    n = p.shape[0]
    off = row_block_offset // tm
    src = lambda width: pl.BlockSpec((tm, width), lambda i: (off + i, 0))
    loc = lambda width: pl.BlockSpec((tm, width), lambda i: (i, 0))
    return pl.pallas_call(
        _ple_kernel,
        grid=(n // tm,),
        in_specs=[src(D_MODEL), pl.BlockSpec((TOP_K, tm, D_MODEL), lambda i: (0, off + i, 0)), src(LANES),
                  loc(D_PLE), _full((1, D_MODEL)), _full((D_MODEL, D_MODEL)), _full((D_PLE, D_MODEL)),
                  _full((1, D_MODEL))],
        out_specs=loc(D_MODEL),
        out_shape=jax.ShapeDtypeStruct((n, D_MODEL), F32),
        compiler_params=_cparams("arbitrary"),
        name="ple_final",
    )(h1, expert_rows, top_w, p, g_ple.reshape(1, D_MODEL), w_ple_gate.astype(BF16),
      w_ple_proj.astype(BF16), g_final.reshape(1, D_MODEL))


def kernel(x_prompt, x_sample, p_prompt, p_sample, state_ssm_re, state_ssm_im, state_ml_c, state_ml_n, state_ml_m, g_mix, w_in, ssm_a_re, ssm_a_im, ssm_log_dt, ssm_b_re, ssm_b_im, ssm_c_re, ssm_c_im, ssm_d, ssm_w_glu, ssm_b_glu, ml_b_ig, ml_b_fg, w_ssm_up, w_ml_up, w_out, g_ffn, w_router, b_router, w_gate, b_gate, w_up, b_up, w_down, b_down, g_ple, w_ple_gate, w_ple_proj, g_final):
    assert g_mix.shape[0] == 1, "single-layer trunk"
    bp, lp, _ = x_prompt.shape
    bs, ls, _ = x_sample.shape
    tp, ts = bp * lp, bs * ls
    t = tp + ts
    xp = x_prompt.reshape(tp, D_MODEL)
    xs = x_sample.reshape(ts, D_MODEL)

    u, qkv, o, gates, gsm = _inproj(xp, xs, g_mix[0], w_in[0])

    s5_args = (ssm_a_re[0], ssm_a_im[0], ssm_log_dt[0], ssm_b_re[0], ssm_b_im[0], ssm_c_re[0],
               ssm_c_im[0], ssm_d[0])
    zero_state = jnp.zeros((S5_TILES, bp, 2 * S5_TILE_STATE), F32)
    y_p, f_p = _s5(u, 0, _s5_tables(*s5_args, S5_CHUNK), zero_state,
                   tc=S5_CHUNK, n_chunks=lp // S5_CHUNK, nb=bp // 2, n_splits=2)
    y_s, f_s = _s5(u, tp, _s5_tables(*s5_args, ls), _s5_state_to_tiles(state_ssm_re[0], state_ssm_im[0]),
                   tc=ls, n_chunks=1, nb=bs, n_splits=1)
    y_pre = jnp.concatenate([y_p, y_s], axis=0)
    re_p, im_p = _s5_state_from_tiles(f_p)
    re_s, im_s = _s5_state_from_tiles(f_s)

    bias_row = jnp.pad(jnp.concatenate([ml_b_ig[0], ml_b_fg[0]]), (0, GATE_PAD - 2 * ML_HEADS)).reshape(1, GATE_PAD)
    hm_p, c_p, n_p, m_p = _mlstm_prompt(qkv, o, gates, bias_row, bp, lp)
    m0 = jnp.broadcast_to(state_ml_m[0][:, :, None, None], (bs, ML_HEADS, 1, LANES))
    hm_s, c_s, n_s, m_s = _mlstm_sample(qkv, o, gates, bias_row, state_ml_c[0],
                                        state_ml_n[0].reshape(bs, ML_HEADS, 1, ML_DK), m0, tp, bs, ls)
    ym = jnp.concatenate([hm_p, hm_s.astype(BF16)], axis=0)

    h1, xn, logits = _merge(xp, xs, y_pre, ym, gsm, ssm_w_glu[0], ssm_b_glu[0], w_ssm_up[0], w_ml_up[0],
                            w_out[0], g_ffn[0], w_router[0], b_router[0])

    top_w, pos, row_tok, block_e, n_valid = _route(logits[:, :N_EXPERTS], TM_MOE)
    y_rows = _moe_experts(xn[row_tok], block_e, n_valid, w_gate[0], b_gate[0], w_up[0], b_up[0],
                          w_down[0], b_down[0])
    expert_rows = y_rows[pos.T.reshape(-1)].reshape(TOP_K, t, D_MODEL)
    top_w_pad = jnp.pad(top_w, ((0, 0), (0, LANES - TOP_K)))

    ple_w = (g_ple[0], w_ple_gate[0], w_ple_proj[0], g_final)
    y_prompt = _ple_final(h1, expert_rows, top_w_pad, p_prompt[0].reshape(tp, D_PLE), 0, *ple_w)
    y_sample = _ple_final(h1, expert_rows, top_w_pad, p_sample[0].reshape(ts, D_PLE), tp, *ple_w)

    return (y_prompt.reshape(bp, lp, D_MODEL), y_sample.reshape(bs, ls, D_MODEL),
            re_p, im_p, c_p[None], n_p.reshape(1, bp, ML_HEADS, ML_DK), m_p[:, :, 0, 0][None],
            re_s, im_s, c_s[None], n_s.reshape(1, bs, ML_HEADS, ML_DK), m_s[:, :, 0, 0][None])
```

```python
import functools

import jax
import jax.numpy as jnp
from jax import lax
from jax.experimental import pallas as pl
from jax.experimental.pallas import tpu as pltpu

F32 = jnp.float32
BF16 = jnp.bfloat16
HIGHEST = lax.Precision.HIGHEST

D_MODEL = 1024
D_SSM = 512
SSM_GROUP = 16
N_GROUPS = 32
SSM_STATE = 64
ML_HEADS = 4
ML_DK = 128
D_ML = 512
N_EXPERTS = 32
TOP_K = 4
SWIGLU_LIMIT = 7.0
SWIGLU_ALPHA = 1.702
D_PLE = 256
RMS_EPS = 1e-6

LANES = 128
GATE_PAD = LANES
S5_CHUNK = 16
ML_CHUNK_PROMPT = 256
TM_TOKENS = 256
TM_MOE = 256
RANK_BLOCK = 256
VMEM_LIMIT = 56 * 1024 * 1024


def _cparams(*sem):
    return pltpu.CompilerParams(dimension_semantics=sem, vmem_limit_bytes=VMEM_LIMIT)


def _rms(x, g):
    return x * lax.rsqrt(jnp.mean(x * x, axis=-1, keepdims=True) + RMS_EPS) * g


def _dot(a, b):
    return jnp.dot(a, b, preferred_element_type=F32)


def _dot_hi(a, b):
    return jnp.dot(a, b, preferred_element_type=F32, precision=HIGHEST)


def _full(shape):
    n = len(shape)
    return pl.BlockSpec(shape, lambda *_: (0,) * n)


def _inproj_kernel(xp_ref, xs_ref, g_ref, wu_ref, wqkv_ref, wo_ref, wgt_ref, wgsm_ref,
                   u_ref, qkv_ref, o_ref, gt_ref, gsm_ref, *, n_prompt_blocks):
    i = pl.program_id(0)
    x = jnp.where(i < n_prompt_blocks, xp_ref[...], xs_ref[...])
    hn = _rms(x, g_ref[...]).astype(BF16)
    u_ref[...] = _dot(hn, wu_ref[...])
    qkv = _dot(hn, wqkv_ref[...])
    col = lax.broadcasted_iota(jnp.int32, (1, 3 * D_ML), 1)
    k_scale = jnp.where((col >= D_ML) & (col < 2 * D_ML), ML_DK ** -0.5, 1.0).astype(F32)
    qkv_ref[...] = (qkv * k_scale).astype(BF16)
    o_ref[...] = _dot(hn, wo_ref[...])
    gt_ref[...] = _dot(hn, wgt_ref[...])
    gsm_ref[...] = _dot(hn, wgsm_ref[...])


def _two_source_specs(tm, width, n_prompt_blocks):
    last = n_prompt_blocks - 1
    return (pl.BlockSpec((tm, width), lambda i: (jnp.minimum(i, last), 0)),
            pl.BlockSpec((tm, width), lambda i: (jnp.maximum(i - n_prompt_blocks, 0), 0)))


def _inproj(xp, xs, g_mix, w_in):
    tm = TM_TOKENS
    tp, ts = xp.shape[0], xs.shape[0]
    t = tp + ts
    npb = tp // tm
    w = w_in.astype(BF16)
    o0 = D_SSM
    wu = w[:, :o0]
    wqkv = w[:, o0:o0 + 3 * D_ML]
    wo = w[:, o0 + 3 * D_ML:o0 + 4 * D_ML]
    g0 = o0 + 4 * D_ML
    wgt = jnp.pad(w[:, g0:g0 + 2 * ML_HEADS], ((0, 0), (0, GATE_PAD - 2 * ML_HEADS)))
    wgsm = w[:, g0 + 2 * ML_HEADS:]
    xp_spec, xs_spec = _two_source_specs(tm, D_MODEL, npb)
    outs = (jax.ShapeDtypeStruct((t, D_SSM), F32), jax.ShapeDtypeStruct((t, 3 * D_ML), BF16),
            jax.ShapeDtypeStruct((t, D_ML), F32), jax.ShapeDtypeStruct((t, GATE_PAD), F32),
            jax.ShapeDtypeStruct((t, 2 * D_MODEL), F32))
    row = lambda width: pl.BlockSpec((tm, width), lambda i: (i, 0))
    return pl.pallas_call(
        functools.partial(_inproj_kernel, n_prompt_blocks=npb),
        grid=(t // tm,),
        in_specs=[xp_spec, xs_spec, _full((1, D_MODEL)), _full(wu.shape), _full(wqkv.shape),
                  _full(wo.shape), _full(wgt.shape), _full(wgsm.shape)],
        out_specs=[row(D_SSM), row(3 * D_ML), row(D_ML), row(GATE_PAD), row(2 * D_MODEL)],
        out_shape=outs,
        compiler_params=_cparams("arbitrary"),
        name="inproj",
    )(xp, xs, g_mix.reshape(1, D_MODEL), wu, wqkv, wo, wgt, wgsm)


S5_TILES = D_SSM // LANES
S5_TILE_GROUPS = LANES // SSM_GROUP
S5_TILE_STATE = S5_TILE_GROUPS * SSM_STATE


def _block_diag_tiles(x):
    gt = S5_TILE_GROUPS
    n, _, r, c = x.shape
    x5 = x.reshape(n, S5_TILES, gt, r, c)
    eye = jnp.eye(gt, dtype=x.dtype)
    return (x5[:, :, :, :, None, :] * eye[None, None, :, None, :, None]).reshape(n, S5_TILES, gt * r, gt * c)


def _s5_tables(a_re, a_im, log_dt, b_re, b_im, c_re, c_im, d_skip, tc):
    ein = functools.partial(jnp.einsum, precision=HIGHEST)
    dt = jnp.exp(log_dt)[:, None]
    mag = jnp.exp(a_re * dt)
    abar_r, abar_i = mag * jnp.cos(a_im * dt), mag * jnp.sin(a_im * dt)
    den = a_re * a_re + a_im * a_im
    nr, ni = abar_r - 1.0, abar_i
    coef_r = (nr * a_re + ni * a_im) / den
    coef_i = (ni * a_re - nr * a_im) / den
    bbar_r = coef_r[..., None] * b_re - coef_i[..., None] * b_im
    bbar_i = coef_r[..., None] * b_im + coef_i[..., None] * b_re

    def abar_pow(j):
        jj = j[..., None, None]
        mag_j = jnp.where(jj >= 0, jnp.exp(jj * (a_re * dt)), 0.0)
        return mag_j * jnp.cos(jj * (a_im * dt)), mag_j * jnp.sin(jj * (a_im * dt))

    half = tc // 2
    steps = jnp.arange(tc, dtype=F32)
    at_r, at_i = abar_pow(jnp.full((), tc, F32))
    lags = (2.0 * jnp.arange(half, dtype=F32)[:, None, None]
            + jnp.array([[0.0, 1.0], [-1.0, 0.0]], F32)[None])
    lag_r, lag_i = abar_pow(lags.reshape(-1))
    ab_r = lag_r[..., None] * bbar_r - lag_i[..., None] * bbar_i
    ab_i = lag_r[..., None] * bbar_i + lag_i[..., None] * bbar_r
    kern = ein('ghp,jgpk->jgkh', c_re, ab_r) - ein('ghp,jgpk->jgkh', c_im, ab_i)
    bd_lag = _block_diag_tiles(kern.astype(BF16)).reshape(half, 2, 2, S5_TILES, LANES, LANES)
    toe = jnp.transpose(bd_lag, (3, 0, 1, 4, 2, 5)).reshape(S5_TILES, half, 2 * LANES, 2 * LANES)
    rev_r, rev_i = abar_pow(tc - 1.0 - steps)
    s_r = jnp.transpose(rev_r[..., None] * bbar_r - rev_i[..., None] * bbar_i, (0, 1, 3, 2))
    s_i = jnp.transpose(rev_r[..., None] * bbar_i + rev_i[..., None] * bbar_r, (0, 1, 3, 2))
    s_step = jnp.concatenate([_block_diag_tiles(s_r.astype(BF16)), _block_diag_tiles(s_i.astype(BF16))],
                             axis=3)
    s_tab = jnp.transpose(s_step.reshape(half, 2, S5_TILES, LANES, 2 * S5_TILE_STATE),
                          (2, 0, 1, 3, 4)).reshape(S5_TILES, half, 2 * LANES, 2 * S5_TILE_STATE)
    a1_r, a1_i = abar_pow(steps + 1.0)
    c_re_t, c_im_t = jnp.transpose(c_re, (0, 2, 1)), jnp.transpose(c_im, (0, 2, 1))
    p_r = c_re_t[None] * a1_r[..., None] - c_im_t[None] * a1_i[..., None]
    p_i = -c_re_t[None] * a1_i[..., None] - c_im_t[None] * a1_r[..., None]
    p_step = jnp.concatenate([_block_diag_tiles(p_r.astype(BF16)), _block_diag_tiles(p_i.astype(BF16))],
                             axis=2)
    p_tab = jnp.transpose(p_step.reshape(half, 2, S5_TILES, 2 * S5_TILE_STATE, LANES),
                          (2, 0, 3, 1, 4)).reshape(S5_TILES, half, 2 * S5_TILE_STATE, 2 * LANES)
    a_tab = jnp.stack([at_r.reshape(S5_TILES, S5_TILE_STATE), at_i.reshape(S5_TILES, S5_TILE_STATE)], axis=1)
    d_tab = d_skip.reshape(S5_TILES, 1, LANES)
    return toe.astype(BF16), s_tab.astype(BF16), p_tab.astype(BF16), a_tab, d_tab


def _s5_kernel(u_ref, t_ref, s_ref, p_ref, a_ref, d_ref, h0_ref, y_ref, f_ref, loc, xprev,
               *, tc, n_chunks, nb):
    r = n_chunks * nb
    half = tc // 2
    ns = S5_TILE_STATE
    step_rows = lambda t: pl.ds(t, r, stride=tc)
    v = [u_ref[step_rows(t), :] for t in range(tc)]
    vp = [jnp.concatenate([v[2 * a].astype(BF16), v[2 * a + 1].astype(BF16)], axis=1) for a in range(half)]
    acc = _dot(vp[0], s_ref[0])
    for a in range(1, half):
        acc = acc + _dot(vp[a], s_ref[a])
    nt = ns // LANES
    lane_tile = lambda k: slice(k * LANES, (k + 1) * LANES)
    for k in range(2 * nt):
        loc[k] = acc[:, lane_tile(k)]
    abar = a_ref[...]

    def body(c, carry):
        rows = pl.ds(c, nb, stride=n_chunks) if n_chunks > 1 else pl.ds(0, nb)
        new = []
        for k in range(nt):
            xr, xi = carry[k], carry[nt + k]
            xprev[k, rows, :] = xr
            xprev[nt + k, rows, :] = xi
            ar, ai = abar[0:1, lane_tile(k)], abar[1:2, lane_tile(k)]
            new.append((ar * xr - ai * xi + loc[k, rows, :], ar * xi + ai * xr + loc[nt + k, rows, :]))
        return tuple(n[0] for n in new) + tuple(n[1] for n in new)

    x_end = lax.fori_loop(0, n_chunks, body, tuple(h0_ref[:, lane_tile(k)] for k in range(2 * nt)))
    for k in range(2 * nt):
        f_ref[:, lane_tile(k)] = x_end[k]
    xp = jnp.concatenate([xprev[k] for k in range(2 * nt)], axis=1).astype(BF16)
    d = d_ref[...]
    for a2 in range(half):
        acc = _dot(xp, p_ref[a2])
        for a in range(a2 + 1):
            acc = acc + _dot(vp[a], t_ref[a2 - a])
        for k in range(2):
            t = 2 * a2 + k
            y_ref[step_rows(t), :] = acc[:, k * LANES:(k + 1) * LANES] + v[t] * d


def _s5(u, row_block_offset, tables, h0, *, tc, n_chunks, nb, n_splits):
    toe, s_tab, p_tab, a_tab, d_tab = tables
    rows = nb * n_chunks * tc
    off = row_block_offset // rows
    half = tc // 2
    ns2 = 2 * S5_TILE_STATE
    tile = lambda *tail: pl.BlockSpec((None,) + tail, lambda j, s: (j,) + (0,) * len(tail))
    state = pl.BlockSpec((None, None, nb, ns2), lambda j, s: (j, s, 0, 0))
    y, f = pl.pallas_call(
        functools.partial(_s5_kernel, tc=tc, n_chunks=n_chunks, nb=nb),
        grid=(S5_TILES, n_splits),
        in_specs=[pl.BlockSpec((rows, LANES), lambda j, s: (off + s, j)),
                  tile(half, 2 * LANES, 2 * LANES), tile(half, 2 * LANES, ns2), tile(half, ns2, 2 * LANES),
                  tile(2, S5_TILE_STATE), tile(1, LANES), state],
        out_specs=[pl.BlockSpec((rows, LANES), lambda j, s: (s, j)), state],
        out_shape=(jax.ShapeDtypeStruct((rows * n_splits, D_SSM), F32),
                   jax.ShapeDtypeStruct((S5_TILES, n_splits, nb, ns2), F32)),
        scratch_shapes=[pltpu.VMEM((ns2 // LANES, nb * n_chunks, LANES), F32)] * 2,
        compiler_params=_cparams("arbitrary", "arbitrary"),
        name=f"s5_c{n_chunks}",
    )(u, toe, s_tab, p_tab, a_tab, d_tab, h0.reshape(S5_TILES, n_splits, nb, ns2))
    return y, f.reshape(S5_TILES, n_splits * nb, ns2)


def _s5_state_to_tiles(s_re, s_im):
    b = s_re.shape[0]
    f = lambda s: s.reshape(b, S5_TILES, S5_TILE_STATE).transpose(1, 0, 2)
    return jnp.concatenate([f(s_re), f(s_im)], axis=2)


def _s5_state_from_tiles(f):
    b = f.shape[1]
    g = lambda s: s.transpose(1, 0, 2).reshape(1, b, N_GROUPS, SSM_STATE)
    return g(f[:, :, :S5_TILE_STATE]), g(f[:, :, S5_TILE_STATE:])


def _log_sigmoid(x):
    return jnp.minimum(x, 0.0) - jnp.log1p(jnp.exp(-jnp.abs(x)))


def _mlstm_gates(gates, bias_row, lc):
    g = gates + bias_row
    col = lax.broadcasted_iota(jnp.int32, (1, GATE_PAD), 1)
    gl = jnp.where(col >= ML_HEADS, _log_sigmoid(g), g)
    r = lax.broadcasted_iota(jnp.int32, (lc, lc), 0)
    c = lax.broadcasted_iota(jnp.int32, (lc, lc), 1)
    tril = (r >= c).astype(F32)
    bcols = _dot_hi(tril, gl)
    sel = (lax.broadcasted_iota(jnp.int32, (8, GATE_PAD), 0)
           == lax.broadcasted_iota(jnp.int32, (8, GATE_PAD), 1)).astype(F32)
    nt = (((1,), (1,)), ((), ()))
    grows = lax.dot_general(sel, gl, nt, precision=HIGHEST, preferred_element_type=F32)
    brows = lax.dot_general(sel, bcols, nt, precision=HIGHEST, preferred_element_type=F32)
    return gl, bcols, grows, brows, (r >= c)


def _mlstm_head(q, k, v, ic, bc, ir, br, causal, c_state, n_state, m_state, lc):
    dmat = jnp.where(causal, bc - br + ir, -jnp.inf)
    inter = bc + m_state
    mt = jnp.maximum(inter, jnp.max(dmat, axis=1, keepdims=True))
    w_in = jnp.exp(dmat - mt)
    w_out = jnp.exp(inter - mt)
    nt = (((1,), (1,)), ((), ()))
    s = lax.dot_general(q, k, nt, preferred_element_type=F32) * w_in
    num = w_out * _dot(q, c_state.astype(BF16)) + _dot(s.astype(BF16), v)
    qf = q.astype(F32)
    qn = w_out * jnp.sum(qf * n_state, axis=1, keepdims=True) + jnp.sum(s, axis=1, keepdims=True)
    h = num / jnp.maximum(jnp.abs(qn), jnp.exp(-mt))
    m_end = mt[lc - 1:lc, :]
    b_last = bc[lc - 1:lc, :]
    decay = jnp.exp(b_last + m_state - m_end)
    ws = jnp.exp(b_last - bc + ic - m_end)
    kw = k.astype(F32) * ws
    tn = (((0,), (0,)), ((), ()))
    c_new = decay * c_state + lax.dot_general(kw.astype(BF16), v, tn, preferred_element_type=F32)
    n_new = decay * n_state + jnp.sum(kw, axis=0, keepdims=True)
    return h, c_new, n_new, m_end


def _mlstm_chunk(qkv, o, gates, bias_row, get_state, put_state, lc):
    gl, bcols, grows, brows, causal = _mlstm_gates(gates, bias_row, lc)
    outs = []
    for hd in range(ML_HEADS):
        q = qkv[:, hd * ML_DK:(hd + 1) * ML_DK]
        k = qkv[:, D_ML + hd * ML_DK:D_ML + (hd + 1) * ML_DK]
        v = qkv[:, 2 * D_ML + hd * ML_DK:2 * D_ML + (hd + 1) * ML_DK]
        f = ML_HEADS + hd
        c_state, n_state, m_state = get_state(hd)
        h, c_new, n_new, m_new = _mlstm_head(
            q, k, v, gl[:, hd:hd + 1], bcols[:, f:f + 1], grows[hd:hd + 1, :], brows[f:f + 1, :],
            causal, c_state, n_state, m_state, lc)
        put_state(hd, c_new, n_new, m_new)
        outs.append(jax.nn.sigmoid(o[:, hd * ML_DK:(hd + 1) * ML_DK]) * h)
    return jnp.concatenate(outs, axis=1)


def _mlstm_prompt_kernel(qkv_ref, o_ref, gt_ref, bias_ref, h_ref, c_ref, n_ref, m_ref, *, lc):
    @pl.when(pl.program_id(1) == 0)
    def _():
        c_ref[...] = jnp.zeros_like(c_ref)
        n_ref[...] = jnp.zeros_like(n_ref)
        m_ref[...] = jnp.zeros_like(m_ref)

    def get_state(hd):
        return c_ref[hd], n_ref[hd], m_ref[hd][:, 0:1]

    def put_state(hd, c_new, n_new, m_new):
        c_ref[hd] = c_new
        n_ref[hd] = n_new
        m_ref[hd] = jnp.broadcast_to(m_new, (1, LANES))

    h = _mlstm_chunk(qkv_ref[...], o_ref[...], gt_ref[...], bias_ref[...], get_state, put_state, lc)
    h_ref[...] = h.astype(BF16)


def _mlstm_prompt(qkv, o, gates, bias_row, bsz, seq):
    lc = ML_CHUNK_PROMPT
    nc = seq // lc
    row = lambda width: pl.BlockSpec((lc, width), lambda b, c: (b * nc + c, 0))
    st = lambda *tail: pl.BlockSpec((None, ML_HEADS) + tail, lambda b, c: (b, 0) + (0,) * len(tail))
    return pl.pallas_call(
        functools.partial(_mlstm_prompt_kernel, lc=lc),
        grid=(bsz, nc),
        in_specs=[row(3 * D_ML), row(D_ML), row(GATE_PAD), pl.BlockSpec((1, GATE_PAD), lambda b, c: (0, 0))],
        out_specs=[row(D_ML), st(ML_DK, ML_DK), st(1, ML_DK), st(1, LANES)],
        out_shape=(jax.ShapeDtypeStruct((bsz * seq, D_ML), BF16),
                   jax.ShapeDtypeStruct((bsz, ML_HEADS, ML_DK, ML_DK), F32),
                   jax.ShapeDtypeStruct((bsz, ML_HEADS, 1, ML_DK), F32),
                   jax.ShapeDtypeStruct((bsz, ML_HEADS, 1, LANES), F32)),
        compiler_params=_cparams("arbitrary", "arbitrary"),
        name="mlstm_prompt",
    )(qkv, o, gates, bias_row)


def _mlstm_sample_kernel(qkv_ref, o_ref, gt_ref, bias_ref, c0_ref, n0_ref, m0_ref,
                         h_ref, c_ref, n_ref, m_ref, qkv_s, *, lc, nb):
    qkv_s[...] = qkv_ref[...].astype(F32)

    def body(b, carry):
        rows = pl.ds(pl.multiple_of(b * lc, lc), lc)

        def get_state(hd):
            return c0_ref[b, hd], n0_ref[b, hd], m0_ref[b, hd][:, 0:1]

        def put_state(hd, c_new, n_new, m_new):
            c_ref[b, hd] = c_new
            n_ref[b, hd] = n_new
            m_ref[b, hd] = jnp.broadcast_to(m_new, (1, LANES))

        h = _mlstm_chunk(qkv_s[rows, :].astype(BF16), o_ref[rows, :], gt_ref[rows, :], bias_ref[...],
                         get_state, put_state, lc)
        h_ref[rows, :] = h
        return carry

    lax.fori_loop(0, nb, body, 0)


def _mlstm_sample(qkv, o, gates, bias_row, c0, n0, m0, row_block_offset, bsz, seq):
    nb = 8
    lc = seq
    rows = nb * lc
    off = row_block_offset // rows
    row = lambda width: pl.BlockSpec((rows, width), lambda i: (off + i, 0))
    st = lambda *tail: pl.BlockSpec((nb, ML_HEADS) + tail, lambda i: (i, 0) + (0,) * len(tail))
    return pl.pallas_call(
        functools.partial(_mlstm_sample_kernel, lc=lc, nb=nb),
        grid=(bsz // nb,),
        in_specs=[row(3 * D_ML), row(D_ML), row(GATE_PAD), pl.BlockSpec((1, GATE_PAD), lambda i: (0, 0)),
                  st(ML_DK, ML_DK), st(1, ML_DK), st(1, LANES)],
        out_specs=[pl.BlockSpec((rows, D_ML), lambda i: (i, 0)), st(ML_DK, ML_DK), st(1, ML_DK), st(1, LANES)],
        out_shape=(jax.ShapeDtypeStruct((bsz * seq, D_ML), F32),
                   jax.ShapeDtypeStruct((bsz, ML_HEADS, ML_DK, ML_DK), F32),
                   jax.ShapeDtypeStruct((bsz, ML_HEADS, 1, ML_DK), F32),
                   jax.ShapeDtypeStruct((bsz, ML_HEADS, 1, LANES), F32)),
        scratch_shapes=[pltpu.VMEM((rows, 3 * D_ML), F32)],
        compiler_params=_cparams("arbitrary"),
        name="mlstm_sample",
    )(qkv, o, gates, bias_row, c0, n0, m0)


def _merge_kernel(xp_ref, xs_ref, ypre_ref, ym_ref, gsm_ref, wglu_ref, bglu_ref, wsu_ref, wmu_ref,
                  wout_ref, gffn_ref, wr_ref, br_ref, h1_ref, xn_ref, lg_ref, *, n_prompt_blocks):
    i = pl.program_id(0)
    x = jnp.where(i < n_prompt_blocks, xp_ref[...], xs_ref[...])
    y = jax.nn.gelu(ypre_ref[...])
    ys = y * jax.nn.sigmoid(_dot(y.astype(BF16), wglu_ref[...]) + bglu_ref[...])
    gsm = gsm_ref[...]
    merged = (jax.nn.sigmoid(gsm[:, :D_MODEL]) * _dot(ys.astype(BF16), wsu_ref[...])
              + jax.nn.sigmoid(gsm[:, D_MODEL:]) * _dot(ym_ref[...], wmu_ref[...]))
    h1 = x + _dot(merged.astype(BF16), wout_ref[...])
    h1_ref[...] = h1
    xn = _rms(h1, gffn_ref[...])
    xn_ref[...] = xn
    lg_ref[...] = _dot_hi(xn, wr_ref[...]) + br_ref[...]


def _merge(xp, xs, y_pre, ym, gsm, w_glu, b_glu, w_ssm_up, w_ml_up, w_out, g_ffn, w_router, b_router):
    tm = TM_TOKENS
    t = y_pre.shape[0]
    npb = xp.shape[0] // tm
    xp_spec, xs_spec = _two_source_specs(tm, D_MODEL, npb)
    row = lambda width: pl.BlockSpec((tm, width), lambda i: (i, 0))
    wr = jnp.pad(w_router, ((0, 0), (0, LANES - N_EXPERTS)))
    br = jnp.pad(b_router, (0, LANES - N_EXPERTS)).reshape(1, LANES)
    return pl.pallas_call(
        functools.partial(_merge_kernel, n_prompt_blocks=npb),
        grid=(t // tm,),
        in_specs=[xp_spec, xs_spec, row(D_SSM), row(D_ML), row(2 * D_MODEL), _full((D_SSM, D_SSM)),
                  _full((1, D_SSM)), _full((D_SSM, D_MODEL)), _full((D_ML, D_MODEL)),
                  _full((D_MODEL, D_MODEL)), _full((1, D_MODEL)), _full((D_MODEL, LANES)), _full((1, LANES))],
        out_specs=[row(D_MODEL), row(D_MODEL), row(LANES)],
        out_shape=(jax.ShapeDtypeStruct((t, D_MODEL), F32), jax.ShapeDtypeStruct((t, D_MODEL), F32),
                   jax.ShapeDtypeStruct((t, LANES), F32)),
        compiler_params=_cparams("arbitrary"),
        name="merge",
    )(xp, xs, y_pre, ym, gsm, w_glu.astype(BF16), b_glu.reshape(1, D_SSM), w_ssm_up.astype(BF16),
      w_ml_up.astype(BF16), w_out.astype(BF16), g_ffn.reshape(1, D_MODEL), wr, br)


def _route(logits, tm):
    t = logits.shape[0]
    top_val, top_idx = lax.top_k(logits, TOP_K)
    top_w = jax.nn.softmax(top_val, axis=-1)
    flat_e = top_idx.reshape(-1)
    n_assign = t * TOP_K
    onehot = (flat_e[:, None] == jnp.arange(N_EXPERTS, dtype=jnp.int32)[None, :])
    rb = RANK_BLOCK
    oh3 = onehot.astype(F32).reshape(n_assign // rb, rb, N_EXPERTS)
    tril = jnp.tril(jnp.ones((rb, rb), F32))
    within = jnp.einsum('ij,bjk->bik', tril, oh3).astype(jnp.int32)
    totals = within[:, -1, :]
    before = jnp.cumsum(totals, axis=0) - totals
    csum = (within + before[:, None, :]).reshape(n_assign, N_EXPERTS)
    rank = jnp.sum(jnp.where(onehot, csum, 0), axis=1) - 1
    counts = before[-1] + totals[-1]
    padded = (counts + tm - 1) // tm * tm
    pad_end = jnp.cumsum(padded)
    pad_start = pad_end - padded
    pos = jnp.sum(jnp.where(onehot, pad_start[None, :], 0), axis=1) + rank
    n_blocks = -(-(n_assign + N_EXPERTS * (tm - 1)) // tm)
    row_tok = jnp.zeros((n_blocks * tm,), jnp.int32).at[pos].set(
        jnp.arange(n_assign, dtype=jnp.int32) // TOP_K)
    block_row0 = jnp.arange(n_blocks, dtype=jnp.int32) * tm
    block_e = jnp.minimum(jnp.sum((pad_end[None, :] <= block_row0[:, None]).astype(jnp.int32), axis=1),
                          N_EXPERTS - 1)
    n_valid = (pad_end[-1] // tm).astype(jnp.int32).reshape(1)
    ids = jnp.arange(N_EXPERTS, dtype=jnp.int32)
    later = (ids[None, :] > ids[:, None]) & (counts[None, :] > 0)
    next_e = jnp.min(jnp.where(later, ids[None, :], N_EXPERTS), axis=1)
    next_e = jnp.where(next_e == N_EXPERTS, -1, next_e)[block_e]
    return top_w, pos.reshape(t, TOP_K), row_tok, block_e, next_e, n_valid


def _moe_weight_copies(e, w_hbm, wbuf, sems):
    return [pltpu.make_async_copy(w.at[e], wbuf.at[k], sems.at[k]) for k, w in enumerate(w_hbm)]


def _moe_kernel(be_ref, ne_ref, nv_ref, x_ref, wg_hbm, bg_ref, wu_hbm, bu_ref, wd_hbm, bd_ref, y_ref,
                w_bf, wbuf, sems):
    i = pl.program_id(0)
    e = be_ref[i]
    prev = be_ref[jnp.maximum(i - 1, 0)]
    valid = i < nv_ref[0]
    first = jnp.logical_or(i == 0, e != prev)
    copies = functools.partial(_moe_weight_copies, w_hbm=(wg_hbm, wu_hbm, wd_hbm), wbuf=wbuf, sems=sems)

    @pl.when(i == 0)
    def _():
        for c in copies(e):
            c.start()

    @pl.when(jnp.logical_and(valid, first))
    def _():
        nxt = ne_ref[i]
        for k, c in enumerate(copies(e)):
            c.wait()
            w_bf[k] = wbuf[k].astype(BF16)

        @pl.when(nxt >= 0)
        def _():
            for c in copies(nxt):
                c.start()

    @pl.when(valid)
    def _():
        x = x_ref[...].astype(BF16)
        g =jnp.minimum(_dot(x, w_bf[0]) + bg_ref[...], SWIGLU_LIMIT)
        u = jnp.clip(_dot(x, w_bf[1]) + bu_ref[...], -SWIGLU_LIMIT, SWIGLU_LIMIT)
        a = g * jax.nn.sigmoid(SWIGLU_ALPHA * g) * (u + 1.0)
        y_ref[...] = _dot(a.astype(BF16), w_bf[2]) + bd_ref[...]

    @pl.when(jnp.logical_not(valid))
    def _():
        y_ref[...] = jnp.zeros_like(y_ref)


def _moe_experts(x_rows, block_e, next_e, n_valid, w_gate, b_gate, w_up, b_up, w_down, b_down):
    tm = TM_MOE
    n_rows = x_rows.shape[0]
    wspec = pl.BlockSpec(memory_space=pl.ANY)
    bspec = pl.BlockSpec((None, 1, D_MODEL), lambda i, be, ne, nv: (be[i], 0, 0))
    rows = pl.BlockSpec((tm, D_MODEL), lambda i, be, ne, nv: (i, 0))
    grid_spec = pltpu.PrefetchScalarGridSpec(
        num_scalar_prefetch=3,
        grid=(n_rows // tm,),
        in_specs=[rows, wspec, bspec, wspec, bspec, wspec, bspec],
        out_specs=rows,
        scratch_shapes=[pltpu.VMEM((3, D_MODEL, D_MODEL), BF16), pltpu.VMEM((3, D_MODEL, D_MODEL), F32),
                        pltpu.SemaphoreType.DMA((3,))],
    )
    b3 = lambda b: b.reshape(N_EXPERTS, 1, D_MODEL)
    return pl.pallas_call(
        _moe_kernel,
        grid_spec=grid_spec,
        out_shape=jax.ShapeDtypeStruct((n_rows, D_MODEL), F32),
        compiler_params=_cparams("arbitrary"),
        name="moe_experts",
    )(block_e, next_e, n_valid, x_rows, w_gate, b3(b_gate), w_up, b3(b_up), w_down, b3(b_down))


def _ple_kernel(h1_ref, rows_ref, tw_ref, p_ref, gple_ref, wg_ref, wp_ref, gfin_ref, y_ref):
    tw = tw_ref[...]
    h2 = h1_ref[...]
    for k in range(TOP_K):
        h2 = h2 + rows_ref[k] * tw[:, k:k + 1]
    gate =jax.nn.sigmoid(_dot(_rms(h2, gple_ref[...]).astype(BF16), wg_ref[...]))
    h3 = h2 + gate * _dot(p_ref[...].astype(BF16), wp_ref[...])
    y_ref[...] = _rms(h3, gfin_ref[...])


def _ple_final(h1, expert_rows, top_w, p, row_block_offset, g_ple, w_ple_gate, w_ple_proj, g_final):
    tm = TM_TOKENS
    n = p.shape[0]
    off = row_block_offset // tm
    src = lambda width: pl.BlockSpec((tm, width), lambda i: (off + i, 0))
    loc = lambda width: pl.BlockSpec((tm, width), lambda i: (i, 0))
    return pl.pallas_call(
        _ple_kernel,
        grid=(n // tm,),
        in_specs=[src(D_MODEL), pl.BlockSpec((TOP_K, tm, D_MODEL), lambda i: (0, off + i, 0)), src(LANES),
                  loc(D_PLE), _full((1, D_MODEL)), _full((D_MODEL, D_MODEL)), _full((D_PLE, D_MODEL)),
                  _full((1, D_MODEL))],
        out_specs=loc(D_MODEL),
        out_shape=jax.ShapeDtypeStruct((n, D_MODEL), F32),
        compiler_params=_cparams("arbitrary"),
        name="ple_final",
    )(h1, expert_rows, top_w, p, g_ple.reshape(1, D_MODEL), w_ple_gate.astype(BF16),
      w_ple_proj.astype(BF16), g_final.reshape(1, D_MODEL))


def kernel(x_prompt, x_sample, p_prompt, p_sample, state_ssm_re, state_ssm_im, state_ml_c, state_ml_n, state_ml_m, g_mix, w_in, ssm_a_re, ssm_a_im, ssm_log_dt, ssm_b_re, ssm_b_im, ssm_c_re, ssm_c_im, ssm_d, ssm_w_glu, ssm_b_glu, ml_b_ig, ml_b_fg, w_ssm_up, w_ml_up, w_out, g_ffn, w_router, b_router, w_gate, b_gate, w_up, b_up, w_down, b_down, g_ple, w_ple_gate, w_ple_proj, g_final):
    assert g_mix.shape[0] == 1, "single-layer trunk"
    bp, lp, _ = x_prompt.shape
    bs, ls, _ = x_sample.shape
    tp, ts = bp * lp, bs * ls
    t = tp + ts
    xp = x_prompt.reshape(tp, D_MODEL)
    xs = x_sample.reshape(ts, D_MODEL)

    u, qkv, o, gates, gsm = _inproj(xp, xs, g_mix[0], w_in[0])

    s5_args = (ssm_a_re[0], ssm_a_im[0], ssm_log_dt[0], ssm_b_re[0], ssm_b_im[0], ssm_c_re[0],
               ssm_c_im[0], ssm_d[0])
    zero_state = jnp.zeros((S5_TILES, bp, 2 * S5_TILE_STATE), F32)
    y_p, f_p = _s5(u, 0, _s5_tables(*s5_args, S5_CHUNK), zero_state,
                   tc=S5_CHUNK, n_chunks=lp // S5_CHUNK, nb=bp // 2, n_splits=2)
    y_s, f_s = _s5(u, tp, _s5_tables(*s5_args, ls), _s5_state_to_tiles(state_ssm_re[0], state_ssm_im[0]),
                   tc=ls, n_chunks=1, nb=bs, n_splits=1)
    y_pre = jnp.concatenate([y_p, y_s], axis=0)
    re_p, im_p = _s5_state_from_tiles(f_p)
    re_s, im_s = _s5_state_from_tiles(f_s)

    bias_row = jnp.pad(jnp.concatenate([ml_b_ig[0], ml_b_fg[0]]), (0, GATE_PAD - 2 * ML_HEADS)).reshape(1, GATE_PAD)
    hm_p, c_p, n_p, m_p = _mlstm_prompt(qkv, o, gates, bias_row, bp, lp)
    m0 = jnp.broadcast_to(state_ml_m[0][:, :, None, None], (bs, ML_HEADS, 1, LANES))
    hm_s, c_s, n_s, m_s = _mlstm_sample(qkv, o, gates, bias_row, state_ml_c[0],
                                        state_ml_n[0].reshape(bs, ML_HEADS, 1, ML_DK), m0, tp, bs, ls)
    ym = jnp.concatenate([hm_p, hm_s.astype(BF16)], axis=0)

    h1, xn, logits = _merge(xp, xs, y_pre, ym, gsm, ssm_w_glu[0], ssm_b_glu[0], w_ssm_up[0], w_ml_up[0],
                            w_out[0], g_ffn[0], w_router[0], b_router[0])

    top_w, pos, row_tok, block_e, next_e, n_valid = _route(logits[:, :N_EXPERTS], TM_MOE)
    expert_w = lambda w: w.reshape(N_EXPERTS, D_MODEL, D_MODEL)
    y_rows = _moe_experts(xn[row_tok], block_e, next_e, n_valid, expert_w(w_gate), b_gate[0], expert_w(w_up),
                          b_up[0], expert_w(w_down), b_down[0])
    expert_rows = y_rows[pos.T.reshape(-1)].reshape(TOP_K, t, D_MODEL)
    top_w_pad = jnp.pad(top_w, ((0, 0), (0, LANES - TOP_K)))

    ple_w = (g_ple[0], w_ple_gate[0], w_ple_proj[0], g_final)
    y_prompt = _ple_final(h1, expert_rows, top_w_pad, p_prompt[0].reshape(tp, D_PLE), 0, *ple_w)
    y_sample = _ple_final(h1, expert_rows, top_w_pad, p_sample[0].reshape(ts, D_PLE), tp, *ple_w)

    return (y_prompt.reshape(bp, lp, D_MODEL), y_sample.reshape(bs, ls, D_MODEL),
            re_p, im_p, c_p[None], n_p.reshape(1, bp, ML_HEADS, ML_DK), m_p[:, :, 0, 0][None],
            re_s, im_s, c_s[None], n_s.reshape(1, bs, ML_HEADS, ML_DK), m_s[:, :, 0, 0][None])
```

```python
import functools

import jax
import jax.numpy as jnp
from jax import lax
from jax.experimental import pallas as pl
from jax.experimental.pallas import tpu as pltpu

F32 = jnp.float32
BF16 = jnp.bfloat16
HIGHEST = lax.Precision.HIGHEST

D_MODEL = 1024
D_SSM = 512
SSM_GROUP = 16
N_GROUPS = 32
SSM_STATE = 64
ML_HEADS = 4
ML_DK = 128
D_ML = 512
N_EXPERTS = 32
TOP_K = 4
SWIGLU_LIMIT = 7.0
SWIGLU_ALPHA = 1.702
D_PLE = 256
RMS_EPS = 1e-6

LANES = 128
GATE_PAD = LANES
S5_CHUNK = 16
ML_CHUNK_PROMPT = 256
TM_TOKENS = 256
TM_MOE = 256
RANK_BLOCK = 256
VMEM_LIMIT = 56 * 1024 * 1024


def _cparams(*sem):
    return pltpu.CompilerParams(dimension_semantics=sem, vmem_limit_bytes=VMEM_LIMIT)


def _rms(x, g):
    return x * lax.rsqrt(jnp.mean(x * x, axis=-1, keepdims=True) + RMS_EPS) * g


def _dot(a, b):
    return jnp.dot(a, b, preferred_element_type=F32)


def _dot_hi(a, b):
    return jnp.dot(a, b, preferred_element_type=F32, precision=HIGHEST)


def _full(shape):
    n = len(shape)
    return pl.BlockSpec(shape, lambda *_: (0,) * n)


def _inproj_kernel(xp_ref, xs_ref, g_ref, wu_ref, wqkv_ref, wo_ref, wgt_ref, wgsm_ref,
                   u_ref, qkv_ref, o_ref, gt_ref, gsm_ref, *, n_prompt_blocks):
    i = pl.program_id(0)
    x = jnp.where(i < n_prompt_blocks, xp_ref[...], xs_ref[...])
    hn = _rms(x, g_ref[...]).astype(BF16)
    u_ref[...] = _dot(hn, wu_ref[...])
    qkv = _dot(hn, wqkv_ref[...])
    col = lax.broadcasted_iota(jnp.int32, (1, 3 * D_ML), 1)
    k_scale = jnp.where((col >= D_ML) & (col < 2 * D_ML), ML_DK ** -0.5, 1.0).astype(F32)
    qkv_ref[...] = (qkv * k_scale).astype(BF16)
    o_ref[...] = _dot(hn, wo_ref[...])
    gt_ref[...] = _dot(hn, wgt_ref[...])
    gsm_ref[...] = _dot(hn, wgsm_ref[...])


def _two_source_specs(tm, width, n_prompt_blocks):
    last = n_prompt_blocks - 1
    return (pl.BlockSpec((tm, width), lambda i: (jnp.minimum(i, last), 0)),
            pl.BlockSpec((tm, width), lambda i: (jnp.maximum(i - n_prompt_blocks, 0), 0)))


def _inproj(xp, xs, g_mix, w_in):
    tm = TM_TOKENS
    tp, ts = xp.shape[0], xs.shape[0]
    t = tp + ts
    npb = tp // tm
    w = w_in.astype(BF16)
    o0 = D_SSM
    wu = w[:, :o0]
    wqkv = w[:, o0:o0 + 3 * D_ML]
    wo = w[:, o0 + 3 * D_ML:o0 + 4 * D_ML]
    g0 = o0 + 4 * D_ML
    wgt = jnp.pad(w[:, g0:g0 + 2 * ML_HEADS], ((0, 0), (0, GATE_PAD - 2 * ML_HEADS)))
    wgsm = w[:, g0 + 2 * ML_HEADS:]
    xp_spec, xs_spec = _two_source_specs(tm, D_MODEL, npb)
    outs = (jax.ShapeDtypeStruct((t, D_SSM), F32), jax.ShapeDtypeStruct((t, 3 * D_ML), BF16),
            jax.ShapeDtypeStruct((t, D_ML), F32), jax.ShapeDtypeStruct((t, GATE_PAD), F32),
            jax.ShapeDtypeStruct((t, 2 * D_MODEL), F32))
    row = lambda width: pl.BlockSpec((tm, width), lambda i: (i, 0))
    return pl.pallas_call(
        functools.partial(_inproj_kernel, n_prompt_blocks=npb),
        grid=(t // tm,),
        in_specs=[xp_spec, xs_spec, _full((1, D_MODEL)), _full(wu.shape), _full(wqkv.shape),
                  _full(wo.shape), _full(wgt.shape), _full(wgsm.shape)],
        out_specs=[row(D_SSM), row(3 * D_ML), row(D_ML), row(GATE_PAD), row(2 * D_MODEL)],
        out_shape=outs,
        compiler_params=_cparams("arbitrary"),
        name="inproj",
    )(xp, xs, g_mix.reshape(1, D_MODEL), wu, wqkv, wo, wgt, wgsm)


S5_TILES = D_SSM // LANES
S5_TILE_GROUPS = LANES // SSM_GROUP
S5_TILE_STATE = S5_TILE_GROUPS * SSM_STATE


def _block_diag_tiles(x):
    gt = S5_TILE_GROUPS
    n, _, r, c = x.shape
    x5 = x.reshape(n, S5_TILES, gt, r, c)
    eye = jnp.eye(gt, dtype=x.dtype)
    return (x5[:, :, :, :, None, :] * eye[None, None, :, None, :, None]).reshape(n, S5_TILES, gt * r, gt * c)


def _s5_tables(a_re, a_im, log_dt, b_re, b_im, c_re, c_im, d_skip, tc):
    ein = functools.partial(jnp.einsum, precision=HIGHEST)
    dt = jnp.exp(log_dt)[:, None]
    mag = jnp.exp(a_re * dt)
    abar_r, abar_i = mag * jnp.cos(a_im * dt), mag * jnp.sin(a_im * dt)
    den = a_re * a_re + a_im * a_im
    nr, ni = abar_r - 1.0, abar_i
    coef_r = (nr * a_re + ni * a_im) / den
    coef_i = (ni * a_re - nr * a_im) / den
    bbar_r = coef_r[..., None] * b_re - coef_i[..., None] * b_im
    bbar_i = coef_r[..., None] * b_im + coef_i[..., None] * b_re

    def abar_pow(j):
        jj = j[..., None, None]
        mag_j = jnp.where(jj >= 0, jnp.exp(jj * (a_re * dt)), 0.0)
        return mag_j * jnp.cos(jj * (a_im * dt)), mag_j * jnp.sin(jj * (a_im * dt))

    half = tc // 2
    steps = jnp.arange(tc, dtype=F32)
    at_r, at_i = abar_pow(jnp.full((), tc, F32))
    lags = (2.0 * jnp.arange(half, dtype=F32)[:, None, None]
            + jnp.array([[0.0, 1.0], [-1.0, 0.0]], F32)[None])
    lag_r, lag_i = abar_pow(lags.reshape(-1))
    ab_r = lag_r[..., None] * bbar_r - lag_i[..., None] * bbar_i
    ab_i = lag_r[..., None] * bbar_i + lag_i[..., None] * bbar_r
    kern = ein('ghp,jgpk->jgkh', c_re, ab_r) - ein('ghp,jgpk->jgkh', c_im, ab_i)
    bd_lag = _block_diag_tiles(kern.astype(BF16)).reshape(half, 2, 2, S5_TILES, LANES, LANES)
    toe = jnp.transpose(bd_lag, (3, 0, 1, 4, 2, 5)).reshape(S5_TILES, half, 2 * LANES, 2 * LANES)
    rev_r, rev_i = abar_pow(tc - 1.0 - steps)
    s_r = jnp.transpose(rev_r[..., None] * bbar_r - rev_i[..., None] * bbar_i, (0, 1, 3, 2))
    s_i = jnp.transpose(rev_r[..., None] * bbar_i + rev_i[..., None] * bbar_r, (0, 1, 3, 2))
    s_step = jnp.concatenate([_block_diag_tiles(s_r.astype(BF16)), _block_diag_tiles(s_i.astype(BF16))],
                             axis=3)
    s_tab = jnp.transpose(s_step.reshape(half, 2, S5_TILES, LANES, 2 * S5_TILE_STATE),
                          (2, 0, 1, 3, 4)).reshape(S5_TILES, half, 2 * LANES, 2 * S5_TILE_STATE)
    a1_r, a1_i = abar_pow(steps + 1.0)
    c_re_t, c_im_t = jnp.transpose(c_re, (0, 2, 1)), jnp.transpose(c_im, (0, 2, 1))
    p_r = c_re_t[None] * a1_r[..., None] - c_im_t[None] * a1_i[..., None]
    p_i = -c_re_t[None] * a1_i[..., None] - c_im_t[None] * a1_r[..., None]
    p_step = jnp.concatenate([_block_diag_tiles(p_r.astype(BF16)), _block_diag_tiles(p_i.astype(BF16))],
                             axis=2)
    p_tab = jnp.transpose(p_step.reshape(half, 2, S5_TILES, 2 * S5_TILE_STATE, LANES),
                          (2, 0, 3, 1, 4)).reshape(S5_TILES, half, 2 * S5_TILE_STATE, 2 * LANES)
    a_tab = jnp.stack([at_r.reshape(S5_TILES, S5_TILE_STATE), at_i.reshape(S5_TILES, S5_TILE_STATE)], axis=1)
    d_tab = d_skip.reshape(S5_TILES, 1, LANES)
    return toe.astype(BF16), s_tab.astype(BF16), p_tab.astype(BF16), a_tab, d_tab


def _s5_kernel(u_ref, t_ref, s_ref, p_ref, a_ref, d_ref, h0_ref, y_ref, f_ref, loc, xprev,
               *, tc, n_chunks, nb):
    r = n_chunks * nb
    half = tc // 2
    ns = S5_TILE_STATE
    step_rows = lambda t: pl.ds(t, r, stride=tc)
    v = [u_ref[step_rows(t), :] for t in range(tc)]
    vp = [jnp.concatenate([v[2 * a].astype(BF16), v[2 * a + 1].astype(BF16)], axis=1) for a in range(half)]
    acc = _dot(vp[0], s_ref[0])
    for a in range(1, half):
        acc = acc + _dot(vp[a], s_ref[a])
    nt = ns // LANES
    lane_tile = lambda k: slice(k * LANES, (k + 1) * LANES)
    for k in range(2 * nt):
        loc[k] = acc[:, lane_tile(k)]
    abar = a_ref[...]

    def body(c, carry):
        rows = pl.ds(c, nb, stride=n_chunks) if n_chunks > 1 else pl.ds(0, nb)
        new = []
        for k in range(nt):
            xr, xi = carry[k], carry[nt + k]
            xprev[k, rows, :] = xr
            xprev[nt + k, rows, :] = xi
            ar, ai = abar[0:1, lane_tile(k)], abar[1:2, lane_tile(k)]
            new.append((ar * xr - ai * xi + loc[k, rows, :], ar * xi + ai * xr + loc[nt + k, rows, :]))
        return tuple(n[0] for n in new) + tuple(n[1] for n in new)

    x_end = lax.fori_loop(0, n_chunks, body, tuple(h0_ref[:, lane_tile(k)] for k in range(2 * nt)))
    for k in range(2 * nt):
        f_ref[:, lane_tile(k)] = x_end[k]
    xp = jnp.concatenate([xprev[k] for k in range(2 * nt)], axis=1).astype(BF16)
    d = d_ref[...]
    for a2 in range(half):
        acc = _dot(xp, p_ref[a2])
        for a in range(a2 + 1):
            acc = acc + _dot(vp[a], t_ref[a2 - a])
        for k in range(2):
            t = 2 * a2 + k
            y_ref[step_rows(t), :] = acc[:, k * LANES:(k + 1) * LANES] + v[t] * d


def _s5(u, row_block_offset, tables, h0, *, tc, n_chunks, nb, n_splits):
    toe, s_tab, p_tab, a_tab, d_tab = tables
    rows = nb * n_chunks * tc
    off = row_block_offset // rows
    half = tc // 2
    ns2 = 2 * S5_TILE_STATE
    tile = lambda *tail: pl.BlockSpec((None,) + tail, lambda j, s: (j,) + (0,) * len(tail))
    state = pl.BlockSpec((None, None, nb, ns2), lambda j, s: (j, s, 0, 0))
    y, f = pl.pallas_call(
        functools.partial(_s5_kernel, tc=tc, n_chunks=n_chunks, nb=nb),
        grid=(S5_TILES, n_splits),
        in_specs=[pl.BlockSpec((rows, LANES), lambda j, s: (off + s, j)),
                  tile(half, 2 * LANES, 2 * LANES), tile(half, 2 * LANES, ns2), tile(half, ns2, 2 * LANES),
                  tile(2, S5_TILE_STATE), tile(1, LANES), state],
        out_specs=[pl.BlockSpec((rows, LANES), lambda j, s: (s, j)), state],
        out_shape=(jax.ShapeDtypeStruct((rows * n_splits, D_SSM), F32),
                   jax.ShapeDtypeStruct((S5_TILES, n_splits, nb, ns2), F32)),
        scratch_shapes=[pltpu.VMEM((ns2 // LANES, nb * n_chunks, LANES), F32)] * 2,
        compiler_params=_cparams("arbitrary", "arbitrary"),
        name=f"s5_c{n_chunks}",
    )(u, toe, s_tab, p_tab, a_tab, d_tab, h0.reshape(S5_TILES, n_splits, nb, ns2))
    return y, f.reshape(S5_TILES, n_splits * nb, ns2)


def _s5_state_to_tiles(s_re, s_im):
    b = s_re.shape[0]
    f = lambda s: s.reshape(b, S5_TILES, S5_TILE_STATE).transpose(1, 0, 2)
    return jnp.concatenate([f(s_re), f(s_im)], axis=2)


def _s5_state_from_tiles(f):
    b = f.shape[1]
    g = lambda s: s.transpose(1, 0, 2).reshape(1, b, N_GROUPS, SSM_STATE)
    return g(f[:, :, :S5_TILE_STATE]), g(f[:, :, S5_TILE_STATE:])


def _log_sigmoid(x):
    return jnp.minimum(x, 0.0) - jnp.log1p(jnp.exp(-jnp.abs(x)))


def _mlstm_gates(gates, bias_row, lc):
    g = gates + bias_row
    col = lax.broadcasted_iota(jnp.int32, (1, GATE_PAD), 1)
    gl = jnp.where(col >= ML_HEADS, _log_sigmoid(g), g)
    r = lax.broadcasted_iota(jnp.int32, (lc, lc), 0)
    c = lax.broadcasted_iota(jnp.int32, (lc, lc), 1)
    tril = (r >= c).astype(F32)
    bcols = _dot_hi(tril, gl)
    sel = (lax.broadcasted_iota(jnp.int32, (8, GATE_PAD), 0)
           == lax.broadcasted_iota(jnp.int32, (8, GATE_PAD), 1)).astype(F32)
    nt = (((1,), (1,)), ((), ()))
    grows = lax.dot_general(sel, gl, nt, precision=HIGHEST, preferred_element_type=F32)
    brows = lax.dot_general(sel, bcols, nt, precision=HIGHEST, preferred_element_type=F32)
    return gl, bcols, grows, brows, (r >= c)


def _mlstm_head(q, k, v, ic, bc, ir, br, causal, c_state, n_state, m_state, lc):
    dmat = jnp.where(causal, bc - br + ir, -jnp.inf)
    inter = bc + m_state
    mt = jnp.maximum(inter, jnp.max(dmat, axis=1, keepdims=True))
    w_in = jnp.exp(dmat - mt)
    w_out = jnp.exp(inter - mt)
    nt = (((1,), (1,)), ((), ()))
    s = lax.dot_general(q, k, nt, preferred_element_type=F32) * w_in
    num = w_out * _dot(q, c_state.astype(BF16)) + _dot(s.astype(BF16), v)
    qf = q.astype(F32)
    qn = w_out * jnp.sum(qf * n_state, axis=1, keepdims=True) + jnp.sum(s, axis=1, keepdims=True)
    h = num / jnp.maximum(jnp.abs(qn), jnp.exp(-mt))
    m_end = mt[lc - 1:lc, :]
    b_last = bc[lc - 1:lc, :]
    decay = jnp.exp(b_last + m_state - m_end)
    ws = jnp.exp(b_last - bc + ic - m_end)
    kw = k.astype(F32) * ws
    tn = (((0,), (0,)), ((), ()))
    c_new = decay * c_state + lax.dot_general(kw.astype(BF16), v, tn, preferred_element_type=F32)
    n_new = decay * n_state + jnp.sum(kw, axis=0, keepdims=True)
    return h, c_new, n_new, m_end


def _mlstm_chunk(qkv, o, gates, bias_row, get_state, put_state, lc):
    gl, bcols, grows, brows, causal = _mlstm_gates(gates, bias_row, lc)
    outs = []
    for hd in range(ML_HEADS):
        q = qkv[:, hd * ML_DK:(hd + 1) * ML_DK]
        k = qkv[:, D_ML + hd * ML_DK:D_ML + (hd + 1) * ML_DK]
        v = qkv[:, 2 * D_ML + hd * ML_DK:2 * D_ML + (hd + 1) * ML_DK]
        f = ML_HEADS + hd
        c_state, n_state, m_state = get_state(hd)
        h, c_new, n_new, m_new = _mlstm_head(
            q, k, v, gl[:, hd:hd + 1], bcols[:, f:f + 1], grows[hd:hd + 1, :], brows[f:f + 1, :],
            causal, c_state, n_state, m_state, lc)
        put_state(hd, c_new, n_new, m_new)
        outs.append(jax.nn.sigmoid(o[:, hd * ML_DK:(hd + 1) * ML_DK]) * h)
    return jnp.concatenate(outs, axis=1)


def _mlstm_prompt_kernel(qkv_ref, o_ref, gt_ref, bias_ref, h_ref, c_ref, n_ref, m_ref, *, lc):
    @pl.when(pl.program_id(1) == 0)
    def _():
        c_ref[...] = jnp.zeros_like(c_ref)
        n_ref[...] = jnp.zeros_like(n_ref)
        m_ref[...] = jnp.zeros_like(m_ref)

    def get_state(hd):
        return c_ref[hd], n_ref[hd], m_ref[hd][:, 0:1]

    def put_state(hd, c_new, n_new, m_new):
        c_ref[hd] = c_new
        n_ref[hd] = n_new
        m_ref[hd] = jnp.broadcast_to(m_new, (1, LANES))

    h = _mlstm_chunk(qkv_ref[...], o_ref[...], gt_ref[...], bias_ref[...], get_state, put_state, lc)
    h_ref[...] = h.astype(BF16)


def _mlstm_prompt(qkv, o, gates, bias_row, bsz, seq):
    lc = ML_CHUNK_PROMPT
    nc = seq // lc
    row = lambda width: pl.BlockSpec((lc, width), lambda b, c: (b * nc + c, 0))
    st = lambda *tail: pl.BlockSpec((None, ML_HEADS) + tail, lambda b, c: (b, 0) + (0,) * len(tail))
    return pl.pallas_call(
        functools.partial(_mlstm_prompt_kernel, lc=lc),
        grid=(bsz, nc),
        in_specs=[row(3 * D_ML), row(D_ML), row(GATE_PAD), pl.BlockSpec((1, GATE_PAD), lambda b, c: (0, 0))],
        out_specs=[row(D_ML), st(ML_DK, ML_DK), st(1, ML_DK), st(1, LANES)],
        out_shape=(jax.ShapeDtypeStruct((bsz * seq, D_ML), BF16),
                   jax.ShapeDtypeStruct((bsz, ML_HEADS, ML_DK, ML_DK), F32),
                   jax.ShapeDtypeStruct((bsz, ML_HEADS, 1, ML_DK), F32),
                   jax.ShapeDtypeStruct((bsz, ML_HEADS, 1, LANES), F32)),
        compiler_params=_cparams("arbitrary", "arbitrary"),
        name="mlstm_prompt",
    )(qkv, o, gates, bias_row)


def _mlstm_sample_kernel(qkv_ref, o_ref, gt_ref, bias_ref, c0_ref, n0_ref, m0_ref,
                         h_ref, c_ref, n_ref, m_ref, qkv_s, *, lc, nb):
    qkv_s[...] = qkv_ref[...].astype(F32)

    def body(b, carry):
        rows = pl.ds(pl.multiple_of(b * lc, lc), lc)

        def get_state(hd):
            return c0_ref[b, hd], n0_ref[b, hd], m0_ref[b, hd][:, 0:1]

        def put_state(hd, c_new, n_new, m_new):
            c_ref[b, hd] = c_new
            n_ref[b, hd] = n_new
            m_ref[b, hd] = jnp.broadcast_to(m_new, (1, LANES))

        h = _mlstm_chunk(qkv_s[rows, :].astype(BF16), o_ref[rows, :], gt_ref[rows, :], bias_ref[...],
                         get_state, put_state, lc)
        h_ref[rows, :] = h
        return carry

    lax.fori_loop(0, nb, body, 0, unroll=True)


def _mlstm_sample(qkv, o, gates, bias_row, c0, n0, m0, row_block_offset, bsz, seq):
    nb = 8
    lc = seq
    rows = nb * lc
    off = row_block_offset // rows
    row = lambda width: pl.BlockSpec((rows, width), lambda i: (off + i, 0))
    st = lambda *tail: pl.BlockSpec((nb, ML_HEADS) + tail, lambda i: (i, 0) + (0,) * len(tail))
    return pl.pallas_call(
        functools.partial(_mlstm_sample_kernel, lc=lc, nb=nb),
        grid=(bsz // nb,),
        in_specs=[row(3 * D_ML), row(D_ML), row(GATE_PAD), pl.BlockSpec((1, GATE_PAD), lambda i: (0, 0)),
                  st(ML_DK, ML_DK), st(1, ML_DK), st(1, LANES)],
        out_specs=[pl.BlockSpec((rows, D_ML), lambda i: (i, 0)), st(ML_DK, ML_DK), st(1, ML_DK), st(1, LANES)],
        out_shape=(jax.ShapeDtypeStruct((bsz * seq, D_ML), F32),
                   jax.ShapeDtypeStruct((bsz, ML_HEADS, ML_DK, ML_DK), F32),
                   jax.ShapeDtypeStruct((bsz, ML_HEADS, 1, ML_DK), F32),
                   jax.ShapeDtypeStruct((bsz, ML_HEADS, 1, LANES), F32)),
        scratch_shapes=[pltpu.VMEM((rows, 3 * D_ML), F32)],
        compiler_params=_cparams("arbitrary"),
        name="mlstm_sample",
    )(qkv, o, gates, bias_row, c0, n0, m0)


def _merge_kernel(xp_ref, xs_ref, ypre_ref, ym_ref, gsm_ref, wglu_ref, bglu_ref, wsu_ref, wmu_ref,
                  wout_ref, gffn_ref, wrh_ref, wrl_ref, br_ref, h1_ref, xn_ref, lg_ref, *, n_prompt_blocks):
    i = pl.program_id(0)
    x = jnp.where(i < n_prompt_blocks, xp_ref[...], xs_ref[...])
    y = jax.nn.gelu(ypre_ref[...])
    ys = y * jax.nn.sigmoid(_dot(y.astype(BF16), wglu_ref[...]) + bglu_ref[...])
    gsm = gsm_ref[...]
    merged = (jax.nn.sigmoid(gsm[:, :D_MODEL]) * _dot(ys.astype(BF16), wsu_ref[...])
              + jax.nn.sigmoid(gsm[:, D_MODEL:]) * _dot(ym_ref[...], wmu_ref[...]))
    h1 = x + _dot(merged.astype(BF16), wout_ref[...])
    h1_ref[...] = h1
    xn = _rms(h1, gffn_ref[...])
    xn_ref[...] = xn
    xn_hi = xn.astype(BF16)
    xn_lo = (xn - xn_hi.astype(F32)).astype(BF16)
    lg_ref[...] = (_dot(xn_hi, wrh_ref[...]) + _dot(xn_lo, wrh_ref[...]) + _dot(xn_hi, wrl_ref[...])
                   + br_ref[...])


def _merge(xp, xs, y_pre, ym, gsm, w_glu, b_glu, w_ssm_up, w_ml_up, w_out, g_ffn, w_router, b_router):
    tm = TM_TOKENS
    t = y_pre.shape[0]
    npb = xp.shape[0] // tm
    xp_spec, xs_spec = _two_source_specs(tm, D_MODEL, npb)
    row = lambda width: pl.BlockSpec((tm, width), lambda i: (i, 0))
    wr = jnp.pad(w_router, ((0, 0), (0, LANES - N_EXPERTS)))
    wr_hi = wr.astype(BF16)
    wr_lo = (wr - wr_hi.astype(F32)).astype(BF16)
    br =jnp.pad(b_router, (0, LANES - N_EXPERTS)).reshape(1, LANES)
    return pl.pallas_call(
        functools.partial(_merge_kernel, n_prompt_blocks=npb),
        grid=(t // tm,),
        in_specs=[xp_spec, xs_spec, row(D_SSM), row(D_ML), row(2 * D_MODEL), _full((D_SSM, D_SSM)),
                  _full((1, D_SSM)), _full((D_SSM, D_MODEL)), _full((D_ML, D_MODEL)),
                  _full((D_MODEL, D_MODEL)), _full((1, D_MODEL)), _full((D_MODEL, LANES)),
                  _full((D_MODEL, LANES)), _full((1, LANES))],
        out_specs=[row(D_MODEL), row(D_MODEL), row(LANES)],
        out_shape=(jax.ShapeDtypeStruct((t, D_MODEL), F32), jax.ShapeDtypeStruct((t, D_MODEL), F32),
                   jax.ShapeDtypeStruct((t, LANES), F32)),
        compiler_params=_cparams("arbitrary"),
        name="merge",
    )(xp, xs, y_pre, ym, gsm, w_glu.astype(BF16), b_glu.reshape(1, D_SSM), w_ssm_up.astype(BF16),
      w_ml_up.astype(BF16), w_out.astype(BF16), g_ffn.reshape(1, D_MODEL), wr_hi, wr_lo, br)


def _route(logits, tm):
    t = logits.shape[0]
    top_val, top_idx = lax.top_k(logits, TOP_K)
    top_w = jax.nn.softmax(top_val, axis=-1)
    flat_e = top_idx.reshape(-1)
    n_assign = t * TOP_K
    onehot = (flat_e[:, None] == jnp.arange(N_EXPERTS, dtype=jnp.int32)[None, :])
    rb = RANK_BLOCK
    oh3 = onehot.astype(F32).reshape(n_assign // rb, rb, N_EXPERTS)
    tril = jnp.tril(jnp.ones((rb, rb), F32))
    within = jnp.einsum('ij,bjk->bik', tril, oh3).astype(jnp.int32)
    totals = within[:, -1, :]
    before = jnp.cumsum(totals, axis=0) - totals
    csum = (within + before[:, None, :]).reshape(n_assign, N_EXPERTS)
    rank = jnp.sum(jnp.where(onehot, csum, 0), axis=1) - 1
    counts = before[-1] + totals[-1]
    padded = (counts + tm - 1) // tm * tm
    pad_end = jnp.cumsum(padded)
    pad_start = pad_end - padded
    pos = jnp.sum(jnp.where(onehot, pad_start[None, :], 0), axis=1) + rank
    n_blocks = -(-(n_assign + N_EXPERTS * (tm - 1)) // tm)
    row_tok = jnp.zeros((n_blocks * tm,), jnp.int32).at[pos].set(
        jnp.arange(n_assign, dtype=jnp.int32) // TOP_K, unique_indices=True, mode='promise_in_bounds')
    block_row0 = jnp.arange(n_blocks, dtype=jnp.int32) * tm
    block_e = jnp.minimum(jnp.sum((pad_end[None, :] <= block_row0[:, None]).astype(jnp.int32), axis=1),
                          N_EXPERTS - 1)
    n_valid = (pad_end[-1] // tm).astype(jnp.int32).reshape(1)
    ids = jnp.arange(N_EXPERTS, dtype=jnp.int32)
    later = (ids[None, :] > ids[:, None]) & (counts[None, :] > 0)
    next_e = jnp.min(jnp.where(later, ids[None, :], N_EXPERTS), axis=1)
    next_e = jnp.where(next_e == N_EXPERTS, -1, next_e)[block_e]
    return top_w, pos.reshape(t, TOP_K), row_tok, block_e, next_e, n_valid


def _moe_weight_copies(e, w_hbm, wbuf, sems):
    return [pltpu.make_async_copy(w.at[e], wbuf.at[k], sems.at[k]) for k, w in enumerate(w_hbm)]


def _moe_kernel(be_ref, ne_ref, nv_ref, x_ref, wg_hbm, bg_ref, wu_hbm, bu_ref, wd_hbm, bd_ref, y_ref,
                w_bf, wbuf, sems):
    i = pl.program_id(0)
    e = be_ref[i]
    prev = be_ref[jnp.maximum(i - 1, 0)]
    valid = i < nv_ref[0]
    first = jnp.logical_or(i == 0, e != prev)
    copies = functools.partial(_moe_weight_copies, w_hbm=(wg_hbm, wu_hbm, wd_hbm), wbuf=wbuf, sems=sems)

    @pl.when(i == 0)
    def _():
        for c in copies(e):
            c.start()

    @pl.when(jnp.logical_and(valid, first))
    def _():
        nxt = ne_ref[i]
        for k, c in enumerate(copies(e)):
            c.wait()
            w_bf[k] = wbuf[k].astype(BF16)

        @pl.when(nxt >= 0)
        def _():
            for c in copies(nxt):
                c.start()

    @pl.when(valid)
    def _():
        x = x_ref[...].astype(BF16)
        g =jnp.minimum(_dot(x, w_bf[0]) + bg_ref[...], SWIGLU_LIMIT)
        u = jnp.clip(_dot(x, w_bf[1]) + bu_ref[...], -SWIGLU_LIMIT, SWIGLU_LIMIT)
        a = g * jax.nn.sigmoid(SWIGLU_ALPHA * g) * (u + 1.0)
        y_ref[...] = _dot(a.astype(BF16), w_bf[2]) + bd_ref[...]

    @pl.when(jnp.logical_not(valid))
    def _():
        y_ref[...] = jnp.zeros_like(y_ref)


def _moe_experts(x_rows, block_e, next_e, n_valid, w_gate, b_gate, w_up, b_up, w_down, b_down):
    tm = TM_MOE
    n_rows = x_rows.shape[0]
    wspec = pl.BlockSpec(memory_space=pl.ANY)
    bspec = pl.BlockSpec((None, 1, D_MODEL), lambda i, be, ne, nv: (be[i], 0, 0))
    rows = pl.BlockSpec((tm, D_MODEL), lambda i, be, ne, nv: (i, 0))
    grid_spec = pltpu.PrefetchScalarGridSpec(
        num_scalar_prefetch=3,
        grid=(n_rows // tm,),
        in_specs=[rows, wspec, bspec, wspec, bspec, wspec, bspec],
        out_specs=rows,
        scratch_shapes=[pltpu.VMEM((3, D_MODEL, D_MODEL), BF16), pltpu.VMEM((3, D_MODEL, D_MODEL), F32),
                        pltpu.SemaphoreType.DMA((3,))],
    )
    b3 = lambda b: b.reshape(N_EXPERTS, 1, D_MODEL)
    return pl.pallas_call(
        _moe_kernel,
        grid_spec=grid_spec,
        out_shape=jax.ShapeDtypeStruct((n_rows, D_MODEL), F32),
        compiler_params=_cparams("arbitrary"),
        name="moe_experts",
    )(block_e, next_e, n_valid, x_rows, w_gate, b3(b_gate), w_up, b3(b_up), w_down, b3(b_down))


def _ple_kernel(h1_ref, rows_ref, tw_ref, p_ref, gple_ref, wg_ref, wp_ref, gfin_ref, y_ref):
    tw = tw_ref[...]
    h2 = h1_ref[...]
    for k in range(TOP_K):
        h2 = h2 + rows_ref[k] * tw[:, k:k + 1]
    gate =jax.nn.sigmoid(_dot(_rms(h2, gple_ref[...]).astype(BF16), wg_ref[...]))
    h3 = h2 + gate * _dot(p_ref[...].astype(BF16), wp_ref[...])
    y_ref[...] = _rms(h3, gfin_ref[...])


def _ple_final(h1, expert_rows, top_w, p, row_block_offset, g_ple, w_ple_gate, w_ple_proj, g_final):
    tm = TM_TOKENS
    n = p.shape[0]
    off = row_block_offset // tm
    src = lambda width: pl.BlockSpec((tm, width), lambda i: (off + i, 0))
    loc = lambda width: pl.BlockSpec((tm, width), lambda i: (i, 0))
    return pl.pallas_call(
        _ple_kernel,
        grid=(n // tm,),
        in_specs=[src(D_MODEL), pl.BlockSpec((TOP_K, tm, D_MODEL), lambda i: (0, off + i, 0)), src(LANES),
                  loc(D_PLE), _full((1, D_MODEL)), _full((D_MODEL, D_MODEL)), _full((D_PLE, D_MODEL)),
                  _full((1, D_MODEL))],
        out_specs=loc(D_MODEL),
        out_shape=jax.ShapeDtypeStruct((n, D_MODEL), F32),
        compiler_params=_cparams("arbitrary"),
        name="ple_final",
    )(h1, expert_rows, top_w, p, g_ple.reshape(1, D_MODEL), w_ple_gate.astype(BF16),
      w_ple_proj.astype(BF16), g_final.reshape(1, D_MODEL))


def kernel(x_prompt, x_sample, p_prompt, p_sample, state_ssm_re, state_ssm_im, state_ml_c, state_ml_n, state_ml_m, g_mix, w_in, ssm_a_re, ssm_a_im, ssm_log_dt, ssm_b_re, ssm_b_im, ssm_c_re, ssm_c_im, ssm_d, ssm_w_glu, ssm_b_glu, ml_b_ig, ml_b_fg, w_ssm_up, w_ml_up, w_out, g_ffn, w_router, b_router, w_gate, b_gate, w_up, b_up, w_down, b_down, g_ple, w_ple_gate, w_ple_proj, g_final):
    assert g_mix.shape[0] == 1, "single-layer trunk"
    bp, lp, _ = x_prompt.shape
    bs, ls, _ = x_sample.shape
    tp, ts = bp * lp, bs * ls
    t = tp + ts
    xp = x_prompt.reshape(tp, D_MODEL)
    xs = x_sample.reshape(ts, D_MODEL)

    u, qkv, o, gates, gsm = _inproj(xp, xs, g_mix[0], w_in[0])

    s5_args = (ssm_a_re[0], ssm_a_im[0], ssm_log_dt[0], ssm_b_re[0], ssm_b_im[0], ssm_c_re[0],
               ssm_c_im[0], ssm_d[0])
    zero_state = jnp.zeros((S5_TILES, bp, 2 * S5_TILE_STATE), F32)
    y_p, f_p = _s5(u, 0, _s5_tables(*s5_args, S5_CHUNK), zero_state,
                   tc=S5_CHUNK, n_chunks=lp // S5_CHUNK, nb=bp // 2, n_splits=2)
    y_s, f_s = _s5(u, tp, _s5_tables(*s5_args, ls), _s5_state_to_tiles(state_ssm_re[0], state_ssm_im[0]),
                   tc=ls, n_chunks=1, nb=bs, n_splits=1)
    y_pre = jnp.concatenate([y_p, y_s], axis=0)
    re_p, im_p = _s5_state_from_tiles(f_p)
    re_s, im_s = _s5_state_from_tiles(f_s)

    bias_row = jnp.pad(jnp.concatenate([ml_b_ig[0], ml_b_fg[0]]), (0, GATE_PAD - 2 * ML_HEADS)).reshape(1, GATE_PAD)
    hm_p, c_p, n_p, m_p = _mlstm_prompt(qkv, o, gates, bias_row, bp, lp)
    m0 = jnp.broadcast_to(state_ml_m[0][:, :, None, None], (bs, ML_HEADS, 1, LANES))
    hm_s, c_s, n_s, m_s = _mlstm_sample(qkv, o, gates, bias_row, state_ml_c[0],
                                        state_ml_n[0].reshape(bs, ML_HEADS, 1, ML_DK), m0, tp, bs, ls)
    ym = jnp.concatenate([hm_p, hm_s.astype(BF16)], axis=0)

    h1, xn, logits = _merge(xp, xs, y_pre, ym, gsm, ssm_w_glu[0], ssm_b_glu[0], w_ssm_up[0], w_ml_up[0],
                            w_out[0], g_ffn[0], w_router[0], b_router[0])

    top_w, pos, row_tok, block_e, next_e, n_valid = _route(logits[:, :N_EXPERTS], TM_MOE)
    expert_w = lambda w: w.reshape(N_EXPERTS, D_MODEL, D_MODEL)
    y_rows = _moe_experts(xn[row_tok], block_e, next_e, n_valid, expert_w(w_gate), b_gate[0], expert_w(w_up),
                          b_up[0], expert_w(w_down), b_down[0])
    expert_rows = y_rows[pos.T.reshape(-1)].reshape(TOP_K, t, D_MODEL)
    top_w_pad = jnp.pad(top_w, ((0, 0), (0, LANES - TOP_K)))

    ple_w = (g_ple[0], w_ple_gate[0], w_ple_proj[0], g_final)
    y_prompt = _ple_final(h1, expert_rows, top_w_pad, p_prompt[0].reshape(tp, D_PLE), 0, *ple_w)
    y_sample = _ple_final(h1, expert_rows, top_w_pad, p_sample[0].reshape(ts, D_PLE), tp, *ple_w)

    return (y_prompt.reshape(bp, lp, D_MODEL), y_sample.reshape(bs, ls, D_MODEL),
            re_p, im_p, c_p[None], n_p.reshape(1, bp, ML_HEADS, ML_DK), m_p[:, :, 0, 0][None],
            re_s, im_s, c_s[None], n_s.reshape(1, bs, ML_HEADS, ML_DK), m_s[:, :, 0, 0][None])
```

```python
import functools

import jax
import jax.numpy as jnp
from jax import lax
from jax.experimental import pallas as pl
from jax.experimental.pallas import tpu as pltpu

F32 = jnp.float32
BF16 = jnp.bfloat16
HIGHEST = lax.Precision.HIGHEST

D_MODEL = 1024
D_SSM = 512
SSM_GROUP = 16
N_GROUPS = 32
SSM_STATE = 64
ML_HEADS = 4
ML_DK = 128
D_ML = 512
N_EXPERTS = 32
TOP_K = 4
SWIGLU_LIMIT = 7.0
SWIGLU_ALPHA = 1.702
D_PLE = 256
RMS_EPS = 1e-6

LANES = 128
GATE_PAD = LANES
S5_CHUNK = 8
ML_CHUNK_PROMPT = 256
TM_TOKENS = 256
TM_MOE = 256
RANK_BLOCK = 256
VMEM_LIMIT = 56 * 1024 * 1024


def _cparams(*sem):
    return pltpu.CompilerParams(dimension_semantics=sem, vmem_limit_bytes=VMEM_LIMIT)


def _rms(x, g):
    return x * lax.rsqrt(jnp.mean(x * x, axis=-1, keepdims=True) + RMS_EPS) * g


def _dot(a, b):
    return jnp.dot(a, b, preferred_element_type=F32)


def _dot_hi(a, b):
    return jnp.dot(a, b, preferred_element_type=F32, precision=HIGHEST)


def _full(shape):
    n = len(shape)
    return pl.BlockSpec(shape, lambda *_: (0,) * n)


def _inproj_kernel(xp_ref, xs_ref, g_ref, wu_ref, wqkv_ref, wo_ref, wgt_ref, wgsm_ref,
                   u_ref, qkv_ref, o_ref, gt_ref, gsm_ref, *, n_prompt_blocks):
    i = pl.program_id(0)
    x = jnp.where(i < n_prompt_blocks, xp_ref[...], xs_ref[...])
    hn = _rms(x, g_ref[...]).astype(BF16)
    u_ref[...] = _dot(hn, wu_ref[...])
    qkv = _dot(hn, wqkv_ref[...])
    col = lax.broadcasted_iota(jnp.int32, (1, 3 * D_ML), 1)
    k_scale = jnp.where((col >= D_ML) & (col < 2 * D_ML), ML_DK ** -0.5, 1.0).astype(F32)
    qkv_ref[...] = (qkv * k_scale).astype(BF16)
    o_ref[...] = _dot(hn, wo_ref[...])
    gt_ref[...] = _dot(hn, wgt_ref[...])
    gsm_ref[...] = _dot(hn, wgsm_ref[...])


def _two_source_specs(tm, width, n_prompt_blocks):
    last = n_prompt_blocks - 1
    return (pl.BlockSpec((tm, width), lambda i: (jnp.minimum(i, last), 0)),
            pl.BlockSpec((tm, width), lambda i: (jnp.maximum(i - n_prompt_blocks, 0), 0)))


def _inproj(xp, xs, g_mix, w_in):
    tm = TM_TOKENS
    tp, ts = xp.shape[0], xs.shape[0]
    t = tp + ts
    npb = tp // tm
    w = w_in.astype(BF16)
    o0 = D_SSM
    wu = w[:, :o0]
    wqkv = w[:, o0:o0 + 3 * D_ML]
    wo = w[:, o0 + 3 * D_ML:o0 + 4 * D_ML]
    g0 = o0 + 4 * D_ML
    wgt = jnp.pad(w[:, g0:g0 + 2 * ML_HEADS], ((0, 0), (0, GATE_PAD - 2 * ML_HEADS)))
    wgsm = w[:, g0 + 2 * ML_HEADS:]
    xp_spec, xs_spec = _two_source_specs(tm, D_MODEL, npb)
    outs = (jax.ShapeDtypeStruct((t, D_SSM), F32), jax.ShapeDtypeStruct((t, 3 * D_ML), BF16),
            jax.ShapeDtypeStruct((t, D_ML), F32), jax.ShapeDtypeStruct((t, GATE_PAD), F32),
            jax.ShapeDtypeStruct((t, 2 * D_MODEL), F32))
    row = lambda width: pl.BlockSpec((tm, width), lambda i: (i, 0))
    return pl.pallas_call(
        functools.partial(_inproj_kernel, n_prompt_blocks=npb),
        grid=(t // tm,),
        in_specs=[xp_spec, xs_spec, _full((1, D_MODEL)), _full(wu.shape), _full(wqkv.shape),
                  _full(wo.shape), _full(wgt.shape), _full(wgsm.shape)],
        out_specs=[row(D_SSM), row(3 * D_ML), row(D_ML), row(GATE_PAD), row(2 * D_MODEL)],
        out_shape=outs,
        compiler_params=_cparams("arbitrary"),
        name="inproj",
    )(xp, xs, g_mix.reshape(1, D_MODEL), wu, wqkv, wo, wgt, wgsm)


S5_TILES = D_SSM // LANES
S5_TILE_GROUPS = LANES // SSM_GROUP
S5_TILE_STATE = S5_TILE_GROUPS * SSM_STATE


def _block_diag_tiles(x):
    gt = S5_TILE_GROUPS
    n, _, r, c = x.shape
    x5 = x.reshape(n, S5_TILES, gt, r, c)
    eye = jnp.eye(gt, dtype=x.dtype)
    return (x5[:, :, :, :, None, :] * eye[None, None, :, None, :, None]).reshape(n, S5_TILES, gt * r, gt * c)


def _s5_tables(a_re, a_im, log_dt, b_re, b_im, c_re, c_im, d_skip, tc):
    ein = functools.partial(jnp.einsum, precision=HIGHEST)
    dt = jnp.exp(log_dt)[:, None]
    mag = jnp.exp(a_re * dt)
    abar_r, abar_i = mag * jnp.cos(a_im * dt), mag * jnp.sin(a_im * dt)
    den = a_re * a_re + a_im * a_im
    nr, ni = abar_r - 1.0, abar_i
    coef_r = (nr * a_re + ni * a_im) / den
    coef_i = (ni * a_re - nr * a_im) / den
    bbar_r = coef_r[..., None] * b_re - coef_i[..., None] * b_im
    bbar_i = coef_r[..., None] * b_im + coef_i[..., None] * b_re

    def abar_pow(j):
        jj = j[..., None, None]
        mag_j = jnp.where(jj >= 0, jnp.exp(jj * (a_re * dt)), 0.0)
        return mag_j * jnp.cos(jj * (a_im * dt)), mag_j * jnp.sin(jj * (a_im * dt))

    half = tc // 2
    steps = jnp.arange(tc, dtype=F32)
    at_r, at_i = abar_pow(jnp.full((), tc, F32))
    lags = (2.0 * jnp.arange(half, dtype=F32)[:, None, None]
            + jnp.array([[0.0, 1.0], [-1.0, 0.0]], F32)[None])
    lag_r, lag_i = abar_pow(lags.reshape(-1))
    ab_r = lag_r[..., None] * bbar_r - lag_i[..., None] * bbar_i
    ab_i = lag_r[..., None] * bbar_i + lag_i[..., None] * bbar_r
    kern = ein('ghp,jgpk->jgkh', c_re, ab_r) - ein('ghp,jgpk->jgkh', c_im, ab_i)
    bd_lag = _block_diag_tiles(kern.astype(BF16)).reshape(half, 2, 2, S5_TILES, LANES, LANES)
    toe = jnp.transpose(bd_lag, (3, 0, 1, 4, 2, 5)).reshape(S5_TILES, half, 2 * LANES, 2 * LANES)
    rev_r, rev_i = abar_pow(tc - 1.0 - steps)
    s_r = jnp.transpose(rev_r[..., None] * bbar_r - rev_i[..., None] * bbar_i, (0, 1, 3, 2))
    s_i = jnp.transpose(rev_r[..., None] * bbar_i + rev_i[..., None] * bbar_r, (0, 1, 3, 2))
    s_step = jnp.concatenate([_block_diag_tiles(s_r.astype(BF16)), _block_diag_tiles(s_i.astype(BF16))],
                             axis=3)
    s_tab = jnp.transpose(s_step.reshape(half, 2, S5_TILES, LANES, 2 * S5_TILE_STATE),
                          (2, 0, 1, 3, 4)).reshape(S5_TILES, half, 2 * LANES, 2 * S5_TILE_STATE)
    a1_r, a1_i = abar_pow(steps + 1.0)
    c_re_t, c_im_t = jnp.transpose(c_re, (0, 2, 1)), jnp.transpose(c_im, (0, 2, 1))
    p_r = c_re_t[None] * a1_r[..., None] - c_im_t[None] * a1_i[..., None]
    p_i = -c_re_t[None] * a1_i[..., None] - c_im_t[None] * a1_r[..., None]
    p_step = jnp.concatenate([_block_diag_tiles(p_r.astype(BF16)), _block_diag_tiles(p_i.astype(BF16))],
                             axis=2)
    p_tab = jnp.transpose(p_step.reshape(half, 2, S5_TILES, 2 * S5_TILE_STATE, LANES),
                          (2, 0, 3, 1, 4)).reshape(S5_TILES, half, 2 * S5_TILE_STATE, 2 * LANES)
    a_tab = jnp.stack([at_r.reshape(S5_TILES, S5_TILE_STATE), at_i.reshape(S5_TILES, S5_TILE_STATE)], axis=1)
    d_tab = d_skip.reshape(S5_TILES, 1, LANES)
    return toe.astype(BF16), s_tab.astype(BF16), p_tab.astype(BF16), a_tab, d_tab


def _s5_kernel(u_ref, t_ref, s_ref, p_ref, a_ref, d_ref, h0_ref, y_ref, f_ref, loc, xprev,
               *, tc, n_chunks, nb):
    r = n_chunks * nb
    half = tc // 2
    ns = S5_TILE_STATE
    step_rows = lambda t: pl.ds(t, r, stride=tc)
    v = [u_ref[step_rows(t), :] for t in range(tc)]
    vp = [jnp.concatenate([v[2 * a].astype(BF16), v[2 * a + 1].astype(BF16)], axis=1) for a in range(half)]
    acc = _dot(vp[0], s_ref[0])
    for a in range(1, half):
        acc = acc + _dot(vp[a], s_ref[a])
    nt = ns // LANES
    lane_tile = lambda k: slice(k * LANES, (k + 1) * LANES)
    for k in range(2 * nt):
        loc[k] = acc[:, lane_tile(k)]
    abar = a_ref[...]

    def body(c, carry):
        rows = pl.ds(c, nb, stride=n_chunks) if n_chunks > 1 else pl.ds(0, nb)
        new = []
        for k in range(nt):
            xr, xi = carry[k], carry[nt + k]
            xprev[k, rows, :] = xr
            xprev[nt + k, rows, :] = xi
            ar, ai = abar[0:1, lane_tile(k)], abar[1:2, lane_tile(k)]
            new.append((ar * xr - ai * xi + loc[k, rows, :], ar * xi + ai * xr + loc[nt + k, rows, :]))
        return tuple(n[0] for n in new) + tuple(n[1] for n in new)

    x_end = lax.fori_loop(0, n_chunks, body, tuple(h0_ref[:, lane_tile(k)] for k in range(2 * nt)))
    for k in range(2 * nt):
        f_ref[:, lane_tile(k)] = x_end[k]
    xp = jnp.concatenate([xprev[k] for k in range(2 * nt)], axis=1).astype(BF16)
    d = d_ref[...]
    for a2 in range(half):
        acc = _dot(xp, p_ref[a2])
        for a in range(a2 + 1):
            acc = acc + _dot(vp[a], t_ref[a2 - a])
        for k in range(2):
            t = 2 * a2 + k
            y_ref[step_rows(t), :] = acc[:, k * LANES:(k + 1) * LANES] + v[t] * d


def _s5(u, row_block_offset, tables, h0, *, tc, n_chunks, nb, n_splits):
    toe, s_tab, p_tab, a_tab, d_tab = tables
    rows = nb * n_chunks * tc
    off = row_block_offset // rows
    half = tc // 2
    ns2 = 2 * S5_TILE_STATE
    tile = lambda *tail: pl.BlockSpec((None,) + tail, lambda j, s: (j,) + (0,) * len(tail))
    state = pl.BlockSpec((None, None, nb, ns2), lambda j, s: (j, s, 0, 0))
    y, f = pl.pallas_call(
        functools.partial(_s5_kernel, tc=tc, n_chunks=n_chunks, nb=nb),
        grid=(S5_TILES, n_splits),
        in_specs=[pl.BlockSpec((rows, LANES), lambda j, s: (off + s, j)),
                  tile(half, 2 * LANES, 2 * LANES), tile(half, 2 * LANES, ns2), tile(half, ns2, 2 * LANES),
                  tile(2, S5_TILE_STATE), tile(1, LANES), state],
        out_specs=[pl.BlockSpec((rows, LANES), lambda j, s: (s, j)), state],
        out_shape=(jax.ShapeDtypeStruct((rows * n_splits, D_SSM), F32),
                   jax.ShapeDtypeStruct((S5_TILES, n_splits, nb, ns2), F32)),
        scratch_shapes=[pltpu.VMEM((ns2 // LANES, nb * n_chunks, LANES), F32)] * 2,
        compiler_params=_cparams("arbitrary", "arbitrary"),
        name=f"s5_c{n_chunks}",
    )(u, toe, s_tab, p_tab, a_tab, d_tab, h0.reshape(S5_TILES, n_splits, nb, ns2))
    return y, f.reshape(S5_TILES, n_splits * nb, ns2)


def _s5_state_to_tiles(s_re, s_im):
    b = s_re.shape[0]
    f = lambda s: s.reshape(b, S5_TILES, S5_TILE_STATE).transpose(1, 0, 2)
    return jnp.concatenate([f(s_re), f(s_im)], axis=2)


def _s5_state_from_tiles(f):
    b = f.shape[1]
    g = lambda s: s.transpose(1, 0, 2).reshape(1, b, N_GROUPS, SSM_STATE)
    return g(f[:, :, :S5_TILE_STATE]), g(f[:, :, S5_TILE_STATE:])


def _log_sigmoid(x):
    return jnp.minimum(x, 0.0) - jnp.log1p(jnp.exp(-jnp.abs(x)))


def _mlstm_gates(gates, bias_row, lc):
    g = gates + bias_row
    col = lax.broadcasted_iota(jnp.int32, (1, GATE_PAD), 1)
    gl = jnp.where(col >= ML_HEADS, _log_sigmoid(g), g)
    r = lax.broadcasted_iota(jnp.int32, (lc, lc), 0)
    c = lax.broadcasted_iota(jnp.int32, (lc, lc), 1)
    tril = (r >= c).astype(F32)
    bcols = _dot_hi(tril, gl)
    sel = (lax.broadcasted_iota(jnp.int32, (8, GATE_PAD), 0)
           == lax.broadcasted_iota(jnp.int32, (8, GATE_PAD), 1)).astype(F32)
    nt = (((1,), (1,)), ((), ()))
    grows = lax.dot_general(sel, gl, nt, precision=HIGHEST, preferred_element_type=F32)
    brows = lax.dot_general(sel, bcols, nt, precision=HIGHEST, preferred_element_type=F32)
    return gl, bcols, grows, brows, (r >= c)


def _mlstm_head(q, k, v, ic, bc, ir, br, causal, c_state, n_state, m_state, lc):
    dmat = jnp.where(causal, bc - br + ir, -jnp.inf)
    inter = bc + m_state
    mt = jnp.maximum(inter, jnp.max(dmat, axis=1, keepdims=True))
    w_in = jnp.exp(dmat - mt)
    w_out = jnp.exp(inter - mt)
    nt = (((1,), (1,)), ((), ()))
    s = lax.dot_general(q, k, nt, preferred_element_type=F32) * w_in
    num = w_out * _dot(q, c_state.astype(BF16)) + _dot(s.astype(BF16), v)
    qf = q.astype(F32)
    qn = w_out * jnp.sum(qf * n_state, axis=1, keepdims=True) + jnp.sum(s, axis=1, keepdims=True)
    h = num / jnp.maximum(jnp.abs(qn), jnp.exp(-mt))
    m_end = mt[lc - 1:lc, :]
    b_last = bc[lc - 1:lc, :]
    decay = jnp.exp(b_last + m_state - m_end)
    ws = jnp.exp(b_last - bc + ic - m_end)
    kw = k.astype(F32) * ws
    tn = (((0,), (0,)), ((), ()))
    c_new = decay * c_state + lax.dot_general(kw.astype(BF16), v, tn, preferred_element_type=F32)
    n_new = decay * n_state + jnp.sum(kw, axis=0, keepdims=True)
    return h, c_new, n_new, m_end


def _mlstm_chunk(qkv, o, gates, bias_row, get_state, put_state, lc):
    gl, bcols, grows, brows, causal = _mlstm_gates(gates, bias_row, lc)
    outs = []
    for hd in range(ML_HEADS):
        q = qkv[:, hd * ML_DK:(hd + 1) * ML_DK]
        k = qkv[:, D_ML + hd * ML_DK:D_ML + (hd + 1) * ML_DK]
        v = qkv[:, 2 * D_ML + hd * ML_DK:2 * D_ML + (hd + 1) * ML_DK]
        f = ML_HEADS + hd
        c_state, n_state, m_state = get_state(hd)
        h, c_new, n_new, m_new = _mlstm_head(
            q, k, v, gl[:, hd:hd + 1], bcols[:, f:f + 1], grows[hd:hd + 1, :], brows[f:f + 1, :],
            causal, c_state, n_state, m_state, lc)
        put_state(hd, c_new, n_new, m_new)
        outs.append(jax.nn.sigmoid(o[:, hd * ML_DK:(hd + 1) * ML_DK]) * h)
    return jnp.concatenate(outs, axis=1)


def _mlstm_prompt_kernel(qkv_ref, o_ref, gt_ref, bias_ref, h_ref, c_ref, n_ref, m_ref, *, lc):
    @pl.when(pl.program_id(1) == 0)
    def _():
        c_ref[...] = jnp.zeros_like(c_ref)
        n_ref[...] = jnp.zeros_like(n_ref)
        m_ref[...] = jnp.zeros_like(m_ref)

    def get_state(hd):
        return c_ref[hd], n_ref[hd], m_ref[hd][:, 0:1]

    def put_state(hd, c_new, n_new, m_new):
        c_ref[hd] = c_new
        n_ref[hd] = n_new
        m_ref[hd] = jnp.broadcast_to(m_new, (1, LANES))

    h = _mlstm_chunk(qkv_ref[...], o_ref[...], gt_ref[...], bias_ref[...], get_state, put_state, lc)
    h_ref[...] = h.astype(BF16)


def _mlstm_prompt(qkv, o, gates, bias_row, bsz, seq):
    lc = ML_CHUNK_PROMPT
    nc = seq // lc
    row = lambda width: pl.BlockSpec((lc, width), lambda b, c: (b * nc + c, 0))
    st = lambda *tail: pl.BlockSpec((None, ML_HEADS) + tail, lambda b, c: (b, 0) + (0,) * len(tail))
    return pl.pallas_call(
        functools.partial(_mlstm_prompt_kernel, lc=lc),
        grid=(bsz, nc),
        in_specs=[row(3 * D_ML), row(D_ML), row(GATE_PAD), pl.BlockSpec((1, GATE_PAD), lambda b, c: (0, 0))],
        out_specs=[row(D_ML), st(ML_DK, ML_DK), st(1, ML_DK), st(1, LANES)],
        out_shape=(jax.ShapeDtypeStruct((bsz * seq, D_ML), BF16),
                   jax.ShapeDtypeStruct((bsz, ML_HEADS, ML_DK, ML_DK), F32),
                   jax.ShapeDtypeStruct((bsz, ML_HEADS, 1, ML_DK), F32),
                   jax.ShapeDtypeStruct((bsz, ML_HEADS, 1, LANES), F32)),
        compiler_params=_cparams("arbitrary", "arbitrary"),
        name="mlstm_prompt",
    )(qkv, o, gates, bias_row)


def _mlstm_sample_kernel(qkv_ref, o_ref, gt_ref, bias_ref, c0_ref, n0_ref, m0_ref,
                         h_ref, c_ref, n_ref, m_ref, qkv_s, *, lc, nb):
    qkv_s[...] = qkv_ref[...].astype(F32)

    def body(b, carry):
        rows = pl.ds(pl.multiple_of(b * lc, lc), lc)

        def get_state(hd):
            return c0_ref[b, hd], n0_ref[b, hd], m0_ref[b, hd][:, 0:1]

        def put_state(hd, c_new, n_new, m_new):
            c_ref[b, hd] = c_new
            n_ref[b, hd] = n_new
            m_ref[b, hd] = jnp.broadcast_to(m_new, (1, LANES))

        h = _mlstm_chunk(qkv_s[rows, :].astype(BF16), o_ref[rows, :], gt_ref[rows, :], bias_ref[...],
                         get_state, put_state, lc)
        h_ref[rows, :] = h
        return carry

    lax.fori_loop(0, nb, body, 0, unroll=True)


def _mlstm_sample(qkv, o, gates, bias_row, c0, n0, m0, row_block_offset, bsz, seq):
    nb = 8
    lc = seq
    rows = nb * lc
    off = row_block_offset // rows
    row = lambda width: pl.BlockSpec((rows, width), lambda i: (off + i, 0))
    st = lambda *tail: pl.BlockSpec((nb, ML_HEADS) + tail, lambda i: (i, 0) + (0,) * len(tail))
    return pl.pallas_call(
        functools.partial(_mlstm_sample_kernel, lc=lc, nb=nb),
        grid=(bsz // nb,),
        in_specs=[row(3 * D_ML), row(D_ML), row(GATE_PAD), pl.BlockSpec((1, GATE_PAD), lambda i: (0, 0)),
                  st(ML_DK, ML_DK), st(1, ML_DK), st(1, LANES)],
        out_specs=[pl.BlockSpec((rows, D_ML), lambda i: (i, 0)), st(ML_DK, ML_DK), st(1, ML_DK), st(1, LANES)],
        out_shape=(jax.ShapeDtypeStruct((bsz * seq, D_ML), F32),
                   jax.ShapeDtypeStruct((bsz, ML_HEADS, ML_DK, ML_DK), F32),
                   jax.ShapeDtypeStruct((bsz, ML_HEADS, 1, ML_DK), F32),
                   jax.ShapeDtypeStruct((bsz, ML_HEADS, 1, LANES), F32)),
        scratch_shapes=[pltpu.VMEM((rows, 3 * D_ML), F32)],
        compiler_params=_cparams("arbitrary"),
        name="mlstm_sample",
    )(qkv, o, gates, bias_row, c0, n0, m0)


def _merge_kernel(xp_ref, xs_ref, yp_ref, ys_ref, mp_ref, ms_ref, gsm_ref, wglu_ref, bglu_ref, wsu_ref, wmu_ref,
                  wout_ref, gffn_ref, wrh_ref, wrl_ref, br_ref, h1_ref, xn_ref, lg_ref, *, n_prompt_blocks):
    i = pl.program_id(0)
    is_prompt = i < n_prompt_blocks
    x = jnp.where(is_prompt, xp_ref[...], xs_ref[...])
    y = jax.nn.gelu(jnp.where(is_prompt, yp_ref[...], ys_ref[...]))
    ym = jnp.where(is_prompt, mp_ref[...], ms_ref[...].astype(BF16))
    ys = y * jax.nn.sigmoid(_dot(y.astype(BF16), wglu_ref[...]) + bglu_ref[...])
    gsm = gsm_ref[...]
    merged = (jax.nn.sigmoid(gsm[:, :D_MODEL]) * _dot(ys.astype(BF16), wsu_ref[...])
              + jax.nn.sigmoid(gsm[:, D_MODEL:]) * _dot(ym, wmu_ref[...]))
    h1 = x + _dot(merged.astype(BF16), wout_ref[...])
    h1_ref[...] = h1
    xn = _rms(h1, gffn_ref[...])
    xn_ref[...] = xn
    xn_hi = xn.astype(BF16)
    xn_lo = (xn - xn_hi.astype(F32)).astype(BF16)
    lg_ref[...] = (_dot(xn_hi, wrh_ref[...]) + _dot(xn_lo, wrh_ref[...]) + _dot(xn_hi, wrl_ref[...])
                   + br_ref[...])


def _merge(xp, xs, y_pre, ym, gsm, w_glu, b_glu, w_ssm_up, w_ml_up, w_out, g_ffn, w_router, b_router):
    tm = TM_TOKENS
    t = gsm.shape[0]
    npb = xp.shape[0] // tm
    xp_spec, xs_spec = _two_source_specs(tm, D_MODEL, npb)
    yp_spec, ys_spec = _two_source_specs(tm, D_SSM, npb)
    mp_spec, ms_spec = _two_source_specs(tm, D_ML, npb)
    row = lambda width: pl.BlockSpec((tm, width), lambda i: (i, 0))
    wr = jnp.pad(w_router, ((0, 0), (0, LANES - N_EXPERTS)))
    wr_hi = wr.astype(BF16)
    wr_lo = (wr - wr_hi.astype(F32)).astype(BF16)
    br =jnp.pad(b_router, (0, LANES - N_EXPERTS)).reshape(1, LANES)
    return pl.pallas_call(
        functools.partial(_merge_kernel, n_prompt_blocks=npb),
        grid=(t // tm,),
        in_specs=[xp_spec, xs_spec, yp_spec, ys_spec, mp_spec, ms_spec, row(2 * D_MODEL), _full((D_SSM, D_SSM)),
                  _full((1, D_SSM)), _full((D_SSM, D_MODEL)), _full((D_ML, D_MODEL)),
                  _full((D_MODEL, D_MODEL)), _full((1, D_MODEL)), _full((D_MODEL, LANES)),
                  _full((D_MODEL, LANES)), _full((1, LANES))],
        out_specs=[row(D_MODEL), row(D_MODEL), row(LANES)],
        out_shape=(jax.ShapeDtypeStruct((t, D_MODEL), F32), jax.ShapeDtypeStruct((t, D_MODEL), F32),
                   jax.ShapeDtypeStruct((t, LANES), F32)),
        compiler_params=_cparams("arbitrary"),
        name="merge",
    )(xp, xs, *y_pre, *ym, gsm, w_glu.astype(BF16), b_glu.reshape(1, D_SSM), w_ssm_up.astype(BF16),
      w_ml_up.astype(BF16), w_out.astype(BF16), g_ffn.reshape(1, D_MODEL), wr_hi, wr_lo, br)


def _route(logits, tm):
    t = logits.shape[0]
    top_val, top_idx = lax.top_k(logits, TOP_K)
    top_w = jax.nn.softmax(top_val, axis=-1)
    flat_e = top_idx.reshape(-1)
    n_assign = t * TOP_K
    onehot = (flat_e[:, None] == jnp.arange(N_EXPERTS, dtype=jnp.int32)[None, :])
    rb = RANK_BLOCK
    oh3 = onehot.astype(F32).reshape(n_assign // rb, rb, N_EXPERTS)
    tril = jnp.tril(jnp.ones((rb, rb), F32))
    within = jnp.einsum('ij,bjk->bik', tril, oh3).astype(jnp.int32)
    totals = within[:, -1, :]
    before = jnp.cumsum(totals, axis=0) - totals
    csum = (within + before[:, None, :]).reshape(n_assign, N_EXPERTS)
    rank = jnp.sum(jnp.where(onehot, csum, 0), axis=1) - 1
    counts = before[-1] + totals[-1]
    padded = (counts + tm - 1) // tm * tm
    pad_end = jnp.cumsum(padded)
    pad_start = pad_end - padded
    pos = jnp.sum(jnp.where(onehot, pad_start[None, :], 0), axis=1) + rank
    n_blocks = -(-(n_assign + N_EXPERTS * (tm - 1)) // tm)
    row_tok = jnp.zeros((n_blocks * tm,), jnp.int32).at[pos].set(
        jnp.arange(n_assign, dtype=jnp.int32) // TOP_K, unique_indices=True, mode='promise_in_bounds')
    block_row0 = jnp.arange(n_blocks, dtype=jnp.int32) * tm
    block_e = jnp.minimum(jnp.sum((pad_end[None, :] <= block_row0[:, None]).astype(jnp.int32), axis=1),
                          N_EXPERTS - 1)
    n_valid = (pad_end[-1] // tm).astype(jnp.int32).reshape(1)
    ids = jnp.arange(N_EXPERTS, dtype=jnp.int32)
    later = (ids[None, :] > ids[:, None]) & (counts[None, :] > 0)
    next_e = jnp.min(jnp.where(later, ids[None, :], N_EXPERTS), axis=1)
    next_e = jnp.where(next_e == N_EXPERTS, -1, next_e)[block_e]
    return top_w, pos.reshape(t, TOP_K), row_tok, block_e, next_e, n_valid


def _moe_weight_copies(e, w_hbm, wbuf, sems):
    return [pltpu.make_async_copy(w.at[e], wbuf.at[k], sems.at[k]) for k, w in enumerate(w_hbm)]


def _moe_kernel(be_ref, ne_ref, nv_ref, x_ref, wg_hbm, bg_ref, wu_hbm, bu_ref, wd_hbm, bd_ref, y_ref,
                w_bf, wbuf, sems):
    i = pl.program_id(0)
    e = be_ref[i]
    prev = be_ref[jnp.maximum(i - 1, 0)]
    valid = i < nv_ref[0]
    first = jnp.logical_or(i == 0, e != prev)
    copies = functools.partial(_moe_weight_copies, w_hbm=(wg_hbm, wu_hbm, wd_hbm), wbuf=wbuf, sems=sems)

    @pl.when(i == 0)
    def _():
        for c in copies(e):
            c.start()

    @pl.when(jnp.logical_and(valid, first))
    def _():
        nxt = ne_ref[i]
        for k, c in enumerate(copies(e)):
            c.wait()
            w_bf[k] = wbuf[k].astype(BF16)

        @pl.when(nxt >= 0)
        def _():
            for c in copies(nxt):
                c.start()

    @pl.when(valid)
    def _():
        x = x_ref[...].astype(BF16)
        g =jnp.minimum(_dot(x, w_bf[0]) + bg_ref[...], SWIGLU_LIMIT)
        u = jnp.clip(_dot(x, w_bf[1]) + bu_ref[...], -SWIGLU_LIMIT, SWIGLU_LIMIT)
        a = g * jax.nn.sigmoid(SWIGLU_ALPHA * g) * (u + 1.0)
        y_ref[...] = _dot(a.astype(BF16), w_bf[2]) + bd_ref[...]

    @pl.when(jnp.logical_not(valid))
    def _():
        y_ref[...] = jnp.zeros_like(y_ref)


def _moe_experts(x_rows, block_e, next_e, n_valid, w_gate, b_gate, w_up, b_up, w_down, b_down):
    tm = TM_MOE
    n_rows = x_rows.shape[0]
    wspec = pl.BlockSpec(memory_space=pl.ANY)
    bspec = pl.BlockSpec((None, 1, D_MODEL), lambda i, be, ne, nv: (be[i], 0, 0))
    rows = pl.BlockSpec((tm, D_MODEL), lambda i, be, ne, nv: (i, 0))
    grid_spec = pltpu.PrefetchScalarGridSpec(
        num_scalar_prefetch=3,
        grid=(n_rows // tm,),
        in_specs=[rows, wspec, bspec, wspec, bspec, wspec, bspec],
        out_specs=rows,
        scratch_shapes=[pltpu.VMEM((3, D_MODEL, D_MODEL), BF16), pltpu.VMEM((3, D_MODEL, D_MODEL), F32),
                        pltpu.SemaphoreType.DMA((3,))],
    )
    b3 = lambda b: b.reshape(N_EXPERTS, 1, D_MODEL)
    return pl.pallas_call(
        _moe_kernel,
        grid_spec=grid_spec,
        out_shape=jax.ShapeDtypeStruct((n_rows, D_MODEL), F32),
        compiler_params=_cparams("arbitrary"),
        name="moe_experts",
    )(block_e, next_e, n_valid, x_rows, w_gate, b3(b_gate), w_up, b3(b_up), w_down, b3(b_down))


def _ple_kernel(h1_ref, rows_ref, tw_ref, p_ref, gple_ref, wg_ref, wp_ref, gfin_ref, y_ref):
    tw = tw_ref[...]
    h2 = h1_ref[...]
    for k in range(TOP_K):
        h2 = h2 + rows_ref[k] * tw[:, k:k + 1]
    gate =jax.nn.sigmoid(_dot(_rms(h2, gple_ref[...]).astype(BF16), wg_ref[...]))
    h3 = h2 + gate * _dot(p_ref[...].astype(BF16), wp_ref[...])
    y_ref[...] = _rms(h3, gfin_ref[...])


def _ple_final(h1, expert_rows, top_w, p, row_block_offset, g_ple, w_ple_gate, w_ple_proj, g_final):
    tm = TM_TOKENS
    n = p.shape[0]
    off = row_block_offset // tm
    src = lambda width: pl.BlockSpec((tm, width), lambda i: (off + i, 0))
    loc = lambda width: pl.BlockSpec((tm, width), lambda i: (i, 0))
    return pl.pallas_call(
        _ple_kernel,
        grid=(n // tm,),
        in_specs=[src(D_MODEL), pl.BlockSpec((TOP_K, tm, D_MODEL), lambda i: (0, off + i, 0)), src(LANES),
                  loc(D_PLE), _full((1, D_MODEL)), _full((D_MODEL, D_MODEL)), _full((D_PLE, D_MODEL)),
                  _full((1, D_MODEL))],
        out_specs=loc(D_MODEL),
        out_shape=jax.ShapeDtypeStruct((n, D_MODEL), F32),
        compiler_params=_cparams("arbitrary"),
        name="ple_final",
    )(h1, expert_rows, top_w, p, g_ple.reshape(1, D_MODEL), w_ple_gate.astype(BF16),
      w_ple_proj.astype(BF16), g_final.reshape(1, D_MODEL))


def kernel(x_prompt, x_sample, p_prompt, p_sample, state_ssm_re, state_ssm_im, state_ml_c, state_ml_n, state_ml_m, g_mix, w_in, ssm_a_re, ssm_a_im, ssm_log_dt, ssm_b_re, ssm_b_im, ssm_c_re, ssm_c_im, ssm_d, ssm_w_glu, ssm_b_glu, ml_b_ig, ml_b_fg, w_ssm_up, w_ml_up, w_out, g_ffn, w_router, b_router, w_gate, b_gate, w_up, b_up, w_down, b_down, g_ple, w_ple_gate, w_ple_proj, g_final):
    assert g_mix.shape[0] == 1, "single-layer trunk"
    bp, lp, _ = x_prompt.shape
    bs, ls, _ = x_sample.shape
    tp, ts = bp * lp, bs * ls
    t = tp + ts
    xp = x_prompt.reshape(tp, D_MODEL)
    xs = x_sample.reshape(ts, D_MODEL)

    u, qkv, o, gates, gsm = _inproj(xp, xs, g_mix[0], w_in[0])

    s5_args = (ssm_a_re[0], ssm_a_im[0], ssm_log_dt[0], ssm_b_re[0], ssm_b_im[0], ssm_c_re[0],
               ssm_c_im[0], ssm_d[0])
    zero_state = jnp.zeros((S5_TILES, bp, 2 * S5_TILE_STATE), F32)
    tables = {tc: _s5_tables(*s5_args, tc) for tc in {S5_CHUNK, ls}}
    y_p, f_p = _s5(u, 0, tables[S5_CHUNK], zero_state,
                   tc=S5_CHUNK, n_chunks=lp // S5_CHUNK, nb=bp // 2, n_splits=2)
    y_s, f_s = _s5(u, tp, tables[ls], _s5_state_to_tiles(state_ssm_re[0], state_ssm_im[0]),
                   tc=ls, n_chunks=1, nb=bs, n_splits=1)
    re_p, im_p = _s5_state_from_tiles(f_p)
    re_s, im_s = _s5_state_from_tiles(f_s)

    bias_row = jnp.pad(jnp.concatenate([ml_b_ig[0], ml_b_fg[0]]), (0, GATE_PAD - 2 * ML_HEADS)).reshape(1, GATE_PAD)
    hm_p, c_p, n_p, m_p = _mlstm_prompt(qkv, o, gates, bias_row, bp, lp)
    m0 = jnp.broadcast_to(state_ml_m[0][:, :, None, None], (bs, ML_HEADS, 1, LANES))
    hm_s, c_s, n_s, m_s = _mlstm_sample(qkv, o, gates, bias_row, state_ml_c[0],
                                        state_ml_n[0].reshape(bs, ML_HEADS, 1, ML_DK), m0, tp, bs, ls)

    h1, xn, logits = _merge(xp, xs, (y_p, y_s), (hm_p, hm_s), gsm, ssm_w_glu[0], ssm_b_glu[0], w_ssm_up[0], w_ml_up[0],
                            w_out[0], g_ffn[0], w_router[0], b_router[0])

    top_w, pos, row_tok, block_e, next_e, n_valid = _route(logits[:, :N_EXPERTS], TM_MOE)
    expert_w = lambda w: w.reshape(N_EXPERTS, D_MODEL, D_MODEL)
    y_rows = _moe_experts(xn[row_tok], block_e, next_e, n_valid, expert_w(w_gate), b_gate[0], expert_w(w_up),
                          b_up[0], expert_w(w_down), b_down[0])
    expert_rows = y_rows[pos.T.reshape(-1)].reshape(TOP_K, t, D_MODEL)
    top_w_pad = jnp.pad(top_w, ((0, 0), (0, LANES - TOP_K)))

    ple_w = (g_ple[0], w_ple_gate[0], w_ple_proj[0], g_final)
    y_prompt = _ple_final(h1, expert_rows, top_w_pad, p_prompt[0].reshape(tp, D_PLE), 0, *ple_w)
    y_sample = _ple_final(h1, expert_rows, top_w_pad, p_sample[0].reshape(ts, D_PLE), tp, *ple_w)

    return (y_prompt.reshape(bp, lp, D_MODEL), y_sample.reshape(bs, ls, D_MODEL),
            re_p, im_p, c_p[None], n_p.reshape(1, bp, ML_HEADS, ML_DK), m_p[:, :, 0, 0][None],
            re_s, im_s, c_s[None], n_s.reshape(1, bs, ML_HEADS, ML_DK), m_s[:, :, 0, 0][None])
```

```python
import functools

import jax
import jax.numpy as jnp
from jax import lax
from jax.experimental import pallas as pl
from jax.experimental.pallas import tpu as pltpu

F32 = jnp.float32
BF16 = jnp.bfloat16
HIGHEST = lax.Precision.HIGHEST

D_MODEL = 1024
D_SSM = 512
SSM_GROUP = 16
N_GROUPS = 32
SSM_STATE = 64
ML_HEADS = 4
ML_DK = 128
D_ML = 512
N_EXPERTS = 32
TOP_K = 4
SWIGLU_LIMIT = 7.0
SWIGLU_ALPHA = 1.702
D_PLE = 256
RMS_EPS = 1e-6

LANES = 128
GATE_PAD = LANES
S5_CHUNK = 8
ML_CHUNK_PROMPT = 128
TM_TOKENS = 256
TM_MOE = 256
RANK_BLOCK = 256
VMEM_LIMIT = 56 * 1024 * 1024


def _cparams(*sem):
    return pltpu.CompilerParams(dimension_semantics=sem, vmem_limit_bytes=VMEM_LIMIT)


def _rms(x, g):
    return x * lax.rsqrt(jnp.mean(x * x, axis=-1, keepdims=True) + RMS_EPS) * g


def _dot(a, b):
    return jnp.dot(a, b, preferred_element_type=F32)


def _dot_hi(a, b):
    return jnp.dot(a, b, preferred_element_type=F32, precision=HIGHEST)


def _full(shape):
    n = len(shape)
    return pl.BlockSpec(shape, lambda *_: (0,) * n)


def _inproj_kernel(xp_ref, xs_ref, g_ref, wu_ref, wqkv_ref, wo_ref, wgt_ref, wgsm_ref,
                   u_ref, qkv_ref, o_ref, gt_ref, gsm_ref, *, n_prompt_blocks):
    i = pl.program_id(0)
    x = jnp.where(i < n_prompt_blocks, xp_ref[...], xs_ref[...])
    hn = _rms(x, g_ref[...]).astype(BF16)
    u_ref[...] = _dot(hn, wu_ref[...])
    qkv = _dot(hn, wqkv_ref[...])
    col = lax.broadcasted_iota(jnp.int32, (1, 3 * D_ML), 1)
    k_scale = jnp.where((col >= D_ML) & (col < 2 * D_ML), ML_DK ** -0.5, 1.0).astype(F32)
    qkv_ref[...] = (qkv * k_scale).astype(BF16)
    o_ref[...] = _dot(hn, wo_ref[...])
    gt_ref[...] = _dot(hn, wgt_ref[...])
    gsm_ref[...] = _dot(hn, wgsm_ref[...])


def _two_source_specs(tm, width, n_prompt_blocks):
    last = n_prompt_blocks - 1
    return (pl.BlockSpec((tm, width), lambda i: (jnp.minimum(i, last), 0)),
            pl.BlockSpec((tm, width), lambda i: (jnp.maximum(i - n_prompt_blocks, 0), 0)))


def _inproj(xp, xs, g_mix, w_in):
    tm = TM_TOKENS
    tp, ts = xp.shape[0], xs.shape[0]
    t = tp + ts
    npb = tp // tm
    w = w_in.astype(BF16)
    o0 = D_SSM
    wu = w[:, :o0]
    wqkv = w[:, o0:o0 + 3 * D_ML]
    wo = w[:, o0 + 3 * D_ML:o0 + 4 * D_ML]
    g0 = o0 + 4 * D_ML
    wgt = jnp.pad(w[:, g0:g0 + 2 * ML_HEADS], ((0, 0), (0, GATE_PAD - 2 * ML_HEADS)))
    wgsm = w[:, g0 + 2 * ML_HEADS:]
    xp_spec, xs_spec = _two_source_specs(tm, D_MODEL, npb)
    outs = (jax.ShapeDtypeStruct((t, D_SSM), F32), jax.ShapeDtypeStruct((t, 3 * D_ML), BF16),
            jax.ShapeDtypeStruct((t, D_ML), F32), jax.ShapeDtypeStruct((t, GATE_PAD), F32),
            jax.ShapeDtypeStruct((t, 2 * D_MODEL), F32))
    row = lambda width: pl.BlockSpec((tm, width), lambda i: (i, 0))
    return pl.pallas_call(
        functools.partial(_inproj_kernel, n_prompt_blocks=npb),
        grid=(t // tm,),
        in_specs=[xp_spec, xs_spec, _full((1, D_MODEL)), _full(wu.shape), _full(wqkv.shape),
                  _full(wo.shape), _full(wgt.shape), _full(wgsm.shape)],
        out_specs=[row(D_SSM), row(3 * D_ML), row(D_ML), row(GATE_PAD), row(2 * D_MODEL)],
        out_shape=outs,
        compiler_params=_cparams("arbitrary"),
        name="inproj",
    )(xp, xs, g_mix.reshape(1, D_MODEL), wu, wqkv, wo, wgt, wgsm)


S5_TILES = D_SSM // LANES
S5_TILE_GROUPS = LANES // SSM_GROUP
S5_TILE_STATE = S5_TILE_GROUPS * SSM_STATE


def _block_diag_tiles(x):
    gt = S5_TILE_GROUPS
    n, _, r, c = x.shape
    x5 = x.reshape(n, S5_TILES, gt, r, c)
    eye = jnp.eye(gt, dtype=x.dtype)
    return (x5[:, :, :, :, None, :] * eye[None, None, :, None, :, None]).reshape(n, S5_TILES, gt * r, gt * c)


def _s5_tables(a_re, a_im, log_dt, b_re, b_im, c_re, c_im, d_skip, tc):
    ein = functools.partial(jnp.einsum, precision=HIGHEST)
    dt = jnp.exp(log_dt)[:, None]
    mag = jnp.exp(a_re * dt)
    abar_r, abar_i = mag * jnp.cos(a_im * dt), mag * jnp.sin(a_im * dt)
    den = a_re * a_re + a_im * a_im
    nr, ni = abar_r - 1.0, abar_i
    coef_r = (nr * a_re + ni * a_im) / den
    coef_i = (ni * a_re - nr * a_im) / den
    bbar_r = coef_r[..., None] * b_re - coef_i[..., None] * b_im
    bbar_i = coef_r[..., None] * b_im + coef_i[..., None] * b_re

    def abar_pow(j):
        jj = j[..., None, None]
        mag_j = jnp.where(jj >= 0, jnp.exp(jj * (a_re * dt)), 0.0)
        return mag_j * jnp.cos(jj * (a_im * dt)), mag_j * jnp.sin(jj * (a_im * dt))

    half = tc // 2
    steps = jnp.arange(tc, dtype=F32)
    at_r, at_i = abar_pow(jnp.full((), tc, F32))
    lags = (2.0 * jnp.arange(half, dtype=F32)[:, None, None]
            + jnp.array([[0.0, 1.0], [-1.0, 0.0]], F32)[None])
    lag_r, lag_i = abar_pow(lags.reshape(-1))
    ab_r = lag_r[..., None] * bbar_r - lag_i[..., None] * bbar_i
    ab_i = lag_r[..., None] * bbar_i + lag_i[..., None] * bbar_r
    kern = ein('ghp,jgpk->jgkh', c_re, ab_r) - ein('ghp,jgpk->jgkh', c_im, ab_i)
    bd_lag = _block_diag_tiles(kern.astype(BF16)).reshape(half, 2, 2, S5_TILES, LANES, LANES)
    toe = jnp.transpose(bd_lag, (3, 0, 1, 4, 2, 5)).reshape(S5_TILES, half, 2 * LANES, 2 * LANES)
    rev_r, rev_i = abar_pow(tc - 1.0 - steps)
    s_r = jnp.transpose(rev_r[..., None] * bbar_r - rev_i[..., None] * bbar_i, (0, 1, 3, 2))
    s_i = jnp.transpose(rev_r[..., None] * bbar_i + rev_i[..., None] * bbar_r, (0, 1, 3, 2))
    s_step = jnp.concatenate([_block_diag_tiles(s_r.astype(BF16)), _block_diag_tiles(s_i.astype(BF16))],
                             axis=3)
    s_tab = jnp.transpose(s_step.reshape(half, 2, S5_TILES, LANES, 2 * S5_TILE_STATE),
                          (2, 0, 1, 3, 4)).reshape(S5_TILES, half, 2 * LANES, 2 * S5_TILE_STATE)
    a1_r, a1_i = abar_pow(steps + 1.0)
    c_re_t, c_im_t = jnp.transpose(c_re, (0, 2, 1)), jnp.transpose(c_im, (0, 2, 1))
    p_r = c_re_t[None] * a1_r[..., None] - c_im_t[None] * a1_i[..., None]
    p_i = -c_re_t[None] * a1_i[..., None] - c_im_t[None] * a1_r[..., None]
    p_step = jnp.concatenate([_block_diag_tiles(p_r.astype(BF16)), _block_diag_tiles(p_i.astype(BF16))],
                             axis=2)
    p_tab = jnp.transpose(p_step.reshape(half, 2, S5_TILES, 2 * S5_TILE_STATE, LANES),
                          (2, 0, 3, 1, 4)).reshape(S5_TILES, half, 2 * S5_TILE_STATE, 2 * LANES)
    a_tab = jnp.stack([at_r.reshape(S5_TILES, S5_TILE_STATE), at_i.reshape(S5_TILES, S5_TILE_STATE)], axis=1)
    d_tab = d_skip.reshape(S5_TILES, 1, LANES)
    return toe.astype(BF16), s_tab.astype(BF16), p_tab.astype(BF16), a_tab, d_tab


def _s5_kernel(u_ref, t_ref, s_ref, p_ref, a_ref, d_ref, h0_ref, y_ref, f_ref, loc, xprev,
               *, tc, n_chunks, nb):
    r = n_chunks * nb
    half = tc // 2
    ns = S5_TILE_STATE
    step_rows = lambda t: pl.ds(t, r, stride=tc)
    v = [u_ref[step_rows(t), :] for t in range(tc)]
    vp = [jnp.concatenate([v[2 * a].astype(BF16), v[2 * a + 1].astype(BF16)], axis=1) for a in range(half)]
    acc = _dot(vp[0], s_ref[0])
    for a in range(1, half):
        acc = acc + _dot(vp[a], s_ref[a])
    nt = ns // LANES
    lane_tile = lambda k: slice(k * LANES, (k + 1) * LANES)
    for k in range(2 * nt):
        loc[k] = acc[:, lane_tile(k)]
    abar = a_ref[...]

    def body(c, carry):
        rows = pl.ds(c, nb, stride=n_chunks) if n_chunks > 1 else pl.ds(0, nb)
        new = []
        for k in range(nt):
            xr, xi = carry[k], carry[nt + k]
            xprev[k, rows, :] = xr
            xprev[nt + k, rows, :] = xi
            ar, ai = abar[0:1, lane_tile(k)], abar[1:2, lane_tile(k)]
            new.append((ar * xr - ai * xi + loc[k, rows, :], ar * xi + ai * xr + loc[nt + k, rows, :]))
        return tuple(n[0] for n in new) + tuple(n[1] for n in new)

    x_end = lax.fori_loop(0, n_chunks, body, tuple(h0_ref[:, lane_tile(k)] for k in range(2 * nt)))
    for k in range(2 * nt):
        f_ref[:, lane_tile(k)] = x_end[k]
    xp = jnp.concatenate([xprev[k] for k in range(2 * nt)], axis=1).astype(BF16)
    d = d_ref[...]
    for a2 in range(half):
        acc = _dot(xp, p_ref[a2])
        for a in range(a2 + 1):
            acc = acc + _dot(vp[a], t_ref[a2 - a])
        for k in range(2):
            t = 2 * a2 + k
            y_ref[step_rows(t), :] = acc[:, k * LANES:(k + 1) * LANES] + v[t] * d


def _s5(u, row_block_offset, tables, h0, *, tc, n_chunks, nb, n_splits):
    toe, s_tab, p_tab, a_tab, d_tab = tables
    rows = nb * n_chunks * tc
    off = row_block_offset // rows
    half = tc // 2
    ns2 = 2 * S5_TILE_STATE
    tile = lambda *tail: pl.BlockSpec((None,) + tail, lambda j, s: (j,) + (0,) * len(tail))
    state = pl.BlockSpec((None, None, nb, ns2), lambda j, s: (j, s, 0, 0))
    y, f = pl.pallas_call(
        functools.partial(_s5_kernel, tc=tc, n_chunks=n_chunks, nb=nb),
        grid=(S5_TILES, n_splits),
        in_specs=[pl.BlockSpec((rows, LANES), lambda j, s: (off + s, j)),
                  tile(half, 2 * LANES, 2 * LANES), tile(half, 2 * LANES, ns2), tile(half, ns2, 2 * LANES),
                  tile(2, S5_TILE_STATE), tile(1, LANES), state],
        out_specs=[pl.BlockSpec((rows, LANES), lambda j, s: (s, j)), state],
        out_shape=(jax.ShapeDtypeStruct((rows * n_splits, D_SSM), F32),
                   jax.ShapeDtypeStruct((S5_TILES, n_splits, nb, ns2), F32)),
        scratch_shapes=[pltpu.VMEM((ns2 // LANES, nb * n_chunks, LANES), F32)] * 2,
        compiler_params=_cparams("arbitrary", "arbitrary"),
        name=f"s5_c{n_chunks}",
    )(u, toe, s_tab, p_tab, a_tab, d_tab, h0.reshape(S5_TILES, n_splits, nb, ns2))
    return y, f.reshape(S5_TILES, n_splits * nb, ns2)


def _s5_state_to_tiles(s_re, s_im):
    b = s_re.shape[0]
    f = lambda s: s.reshape(b, S5_TILES, S5_TILE_STATE).transpose(1, 0, 2)
    return jnp.concatenate([f(s_re), f(s_im)], axis=2)


def _s5_state_from_tiles(f):
    b = f.shape[1]
    g = lambda s: s.transpose(1, 0, 2).reshape(1, b, N_GROUPS, SSM_STATE)
    return g(f[:, :, :S5_TILE_STATE]), g(f[:, :, S5_TILE_STATE:])


def _log_sigmoid(x):
    return jnp.minimum(x, 0.0) - jnp.log1p(jnp.exp(-jnp.abs(x)))


def _mlstm_gates(gates, bias_row, lc):
    g = gates + bias_row
    col = lax.broadcasted_iota(jnp.int32, (1, GATE_PAD), 1)
    gl = jnp.where(col >= ML_HEADS, _log_sigmoid(g), g)
    r = lax.broadcasted_iota(jnp.int32, (lc, lc), 0)
    c = lax.broadcasted_iota(jnp.int32, (lc, lc), 1)
    tril = (r >= c).astype(F32)
    bcols = _dot_hi(tril, gl)
    sel = (lax.broadcasted_iota(jnp.int32, (8, GATE_PAD), 0)
           == lax.broadcasted_iota(jnp.int32, (8, GATE_PAD), 1)).astype(F32)
    nt = (((1,), (1,)), ((), ()))
    grows = lax.dot_general(sel, gl, nt, precision=HIGHEST, preferred_element_type=F32)
    brows = lax.dot_general(sel, bcols, nt, precision=HIGHEST, preferred_element_type=F32)
    return gl, bcols, grows, brows, (r >= c)


def _mlstm_head(q, k, v, ic, bc, ir, br, causal, c_state, n_state, m_state, lc):
    dmat = jnp.where(causal, bc - br + ir, -jnp.inf)
    inter = bc + m_state
    mt = jnp.maximum(inter, jnp.max(dmat, axis=1, keepdims=True))
    w_in = jnp.exp(dmat - mt)
    w_out = jnp.exp(inter - mt)
    nt = (((1,), (1,)), ((), ()))
    s = lax.dot_general(q, k, nt, preferred_element_type=F32) * w_in
    num = w_out * _dot(q, c_state.astype(BF16)) + _dot(s.astype(BF16), v)
    qf = q.astype(F32)
    qn = w_out * jnp.sum(qf * n_state, axis=1, keepdims=True) + jnp.sum(s, axis=1, keepdims=True)
    h = num / jnp.maximum(jnp.abs(qn), jnp.exp(-mt))
    m_end = mt[lc - 1:lc, :]
    b_last = bc[lc - 1:lc, :]
    decay = jnp.exp(b_last + m_state - m_end)
    ws = jnp.exp(b_last - bc + ic - m_end)
    kw = k.astype(F32) * ws
    tn = (((0,), (0,)), ((), ()))
    c_new = decay * c_state + lax.dot_general(kw.astype(BF16), v, tn, preferred_element_type=F32)
    n_new = decay * n_state + jnp.sum(kw, axis=0, keepdims=True)
    return h, c_new, n_new, m_end


def _mlstm_chunk(qkv, o, gates, bias_row, get_state, put_state, lc):
    gl, bcols, grows, brows, causal = _mlstm_gates(gates, bias_row, lc)
    outs = []
    for hd in range(ML_HEADS):
        q = qkv[:, hd * ML_DK:(hd + 1) * ML_DK]
        k = qkv[:, D_ML + hd * ML_DK:D_ML + (hd + 1) * ML_DK]
        v = qkv[:, 2 * D_ML + hd * ML_DK:2 * D_ML + (hd + 1) * ML_DK]
        f = ML_HEADS + hd
        c_state, n_state, m_state = get_state(hd)
        h, c_new, n_new, m_new = _mlstm_head(
            q, k, v, gl[:, hd:hd + 1], bcols[:, f:f + 1], grows[hd:hd + 1, :], brows[f:f + 1, :],
            causal, c_state, n_state, m_state, lc)
        put_state(hd, c_new, n_new, m_new)
        outs.append(jax.nn.sigmoid(o[:, hd * ML_DK:(hd + 1) * ML_DK]) * h)
    return jnp.concatenate(outs, axis=1)


def _mlstm_prompt_kernel(qkv_ref, o_ref, gt_ref, bias_ref, h_ref, c_ref, n_ref, m_ref, *, lc):
    @pl.when(pl.program_id(1) == 0)
    def _():
        c_ref[...] = jnp.zeros_like(c_ref)
        n_ref[...] = jnp.zeros_like(n_ref)
        m_ref[...] = jnp.zeros_like(m_ref)

    def get_state(hd):
        return c_ref[hd], n_ref[hd], m_ref[hd][:, 0:1]

    def put_state(hd, c_new, n_new, m_new):
        c_ref[hd] = c_new
        n_ref[hd] = n_new
        m_ref[hd] = jnp.broadcast_to(m_new, (1, LANES))

    h = _mlstm_chunk(qkv_ref[...], o_ref[...], gt_ref[...], bias_ref[...], get_state, put_state, lc)
    h_ref[...] = h.astype(BF16)


def _mlstm_prompt(qkv, o, gates, bias_row, bsz, seq):
    lc = ML_CHUNK_PROMPT
    nc = seq // lc
    row = lambda width: pl.BlockSpec((lc, width), lambda b, c: (b * nc + c, 0))
    st = lambda *tail: pl.BlockSpec((None, ML_HEADS) + tail, lambda b, c: (b, 0) + (0,) * len(tail))
    return pl.pallas_call(
        functools.partial(_mlstm_prompt_kernel, lc=lc),
        grid=(bsz, nc),
        in_specs=[row(3 * D_ML), row(D_ML), row(GATE_PAD), pl.BlockSpec((1, GATE_PAD), lambda b, c: (0, 0))],
        out_specs=[row(D_ML), st(ML_DK, ML_DK), st(1, ML_DK), st(1, LANES)],
        out_shape=(jax.ShapeDtypeStruct((bsz * seq, D_ML), BF16),
                   jax.ShapeDtypeStruct((bsz, ML_HEADS, ML_DK, ML_DK), F32),
                   jax.ShapeDtypeStruct((bsz, ML_HEADS, 1, ML_DK), F32),
                   jax.ShapeDtypeStruct((bsz, ML_HEADS, 1, LANES), F32)),
        compiler_params=_cparams("arbitrary", "arbitrary"),
        name="mlstm_prompt",
    )(qkv, o, gates, bias_row)


def _mlstm_sample_kernel(qkv_ref, o_ref, gt_ref, bias_ref, c0_ref, n0_ref, m0_ref,
                         h_ref, c_ref, n_ref, m_ref, qkv_s, *, lc, nb):
    qkv_s[...] = qkv_ref[...].astype(F32)

    def body(b, carry):
        rows = pl.ds(pl.multiple_of(b * lc, lc), lc)

        def get_state(hd):
            return c0_ref[b, hd], n0_ref[b, hd], m0_ref[b, hd][:, 0:1]

        def put_state(hd, c_new, n_new, m_new):
            c_ref[b, hd] = c_new
            n_ref[b, hd] = n_new
            m_ref[b, hd] = jnp.broadcast_to(m_new, (1, LANES))

        h = _mlstm_chunk(qkv_s[rows, :].astype(BF16), o_ref[rows, :], gt_ref[rows, :], bias_ref[...],
                         get_state, put_state, lc)
        h_ref[rows, :] = h
        return carry

    lax.fori_loop(0, nb, body, 0, unroll=True)


def _mlstm_sample(qkv, o, gates, bias_row, c0, n0, m0, row_block_offset, bsz, seq):
    nb = 8
    lc = seq
    rows = nb * lc
    off = row_block_offset // rows
    row = lambda width: pl.BlockSpec((rows, width), lambda i: (off + i, 0))
    st = lambda *tail: pl.BlockSpec((nb, ML_HEADS) + tail, lambda i: (i, 0) + (0,) * len(tail))
    return pl.pallas_call(
        functools.partial(_mlstm_sample_kernel, lc=lc, nb=nb),
        grid=(bsz // nb,),
        in_specs=[row(3 * D_ML), row(D_ML), row(GATE_PAD), pl.BlockSpec((1, GATE_PAD), lambda i: (0, 0)),
                  st(ML_DK, ML_DK), st(1, ML_DK), st(1, LANES)],
        out_specs=[pl.BlockSpec((rows, D_ML), lambda i: (i, 0)), st(ML_DK, ML_DK), st(1, ML_DK), st(1, LANES)],
        out_shape=(jax.ShapeDtypeStruct((bsz * seq, D_ML), F32),
                   jax.ShapeDtypeStruct((bsz, ML_HEADS, ML_DK, ML_DK), F32),
                   jax.ShapeDtypeStruct((bsz, ML_HEADS, 1, ML_DK), F32),
                   jax.ShapeDtypeStruct((bsz, ML_HEADS, 1, LANES), F32)),
        scratch_shapes=[pltpu.VMEM((rows, 3 * D_ML), F32)],
        compiler_params=_cparams("arbitrary"),
        name="mlstm_sample",
    )(qkv, o, gates, bias_row, c0, n0, m0)


def _merge_kernel(xp_ref, xs_ref, yp_ref, ys_ref, mp_ref, ms_ref, gsm_ref, wglu_ref, bglu_ref, wsu_ref, wmu_ref,
                  wout_ref, gffn_ref, wrh_ref, wrl_ref, br_ref, h1_ref, xn_ref, lg_ref, *, n_prompt_blocks):
    i = pl.program_id(0)
    is_prompt = i < n_prompt_blocks
    x = jnp.where(is_prompt, xp_ref[...], xs_ref[...])
    y = jax.nn.gelu(jnp.where(is_prompt, yp_ref[...], ys_ref[...]))
    ym = jnp.where(is_prompt, mp_ref[...], ms_ref[...].astype(BF16))
    ys = y * jax.nn.sigmoid(_dot(y.astype(BF16), wglu_ref[...]) + bglu_ref[...])
    gsm = gsm_ref[...]
    merged = (jax.nn.sigmoid(gsm[:, :D_MODEL]) * _dot(ys.astype(BF16), wsu_ref[...])
              + jax.nn.sigmoid(gsm[:, D_MODEL:]) * _dot(ym, wmu_ref[...]))
    h1 = x + _dot(merged.astype(BF16), wout_ref[...])
    h1_ref[...] = h1
    xn = _rms(h1, gffn_ref[...])
    xn_ref[...] = xn
    xn_hi = xn.astype(BF16)
    xn_lo = (xn - xn_hi.astype(F32)).astype(BF16)
    lg_ref[...] = (_dot(xn_hi, wrh_ref[...]) + _dot(xn_lo, wrh_ref[...]) + _dot(xn_hi, wrl_ref[...])
                   + br_ref[...])


def _merge(xp, xs, y_pre, ym, gsm, w_glu, b_glu, w_ssm_up, w_ml_up, w_out, g_ffn, w_router, b_router):
    tm = TM_TOKENS
    t = gsm.shape[0]
    npb = xp.shape[0] // tm
    xp_spec, xs_spec = _two_source_specs(tm, D_MODEL, npb)
    yp_spec, ys_spec = _two_source_specs(tm, D_SSM, npb)
    mp_spec, ms_spec = _two_source_specs(tm, D_ML, npb)
    row = lambda width: pl.BlockSpec((tm, width), lambda i: (i, 0))
    wr = jnp.pad(w_router, ((0, 0), (0, LANES - N_EXPERTS)))
    wr_hi = wr.astype(BF16)
    wr_lo = (wr - wr_hi.astype(F32)).astype(BF16)
    br =jnp.pad(b_router, (0, LANES - N_EXPERTS)).reshape(1, LANES)
    return pl.pallas_call(
        functools.partial(_merge_kernel, n_prompt_blocks=npb),
        grid=(t // tm,),
        in_specs=[xp_spec, xs_spec, yp_spec, ys_spec, mp_spec, ms_spec, row(2 * D_MODEL), _full((D_SSM, D_SSM)),
                  _full((1, D_SSM)), _full((D_SSM, D_MODEL)), _full((D_ML, D_MODEL)),
                  _full((D_MODEL, D_MODEL)), _full((1, D_MODEL)), _full((D_MODEL, LANES)),
                  _full((D_MODEL, LANES)), _full((1, LANES))],
        out_specs=[row(D_MODEL), row(D_MODEL), row(LANES)],
        out_shape=(jax.ShapeDtypeStruct((t, D_MODEL), F32), jax.ShapeDtypeStruct((t, D_MODEL), F32),
                   jax.ShapeDtypeStruct((t, LANES), F32)),
        compiler_params=_cparams("arbitrary"),
        name="merge",
    )(xp, xs, *y_pre, *ym, gsm, w_glu.astype(BF16), b_glu.reshape(1, D_SSM), w_ssm_up.astype(BF16),
      w_ml_up.astype(BF16), w_out.astype(BF16), g_ffn.reshape(1, D_MODEL), wr_hi, wr_lo, br)


def _route(logits, tm):
    t = logits.shape[0]
    top_val, top_idx = lax.top_k(logits, TOP_K)
    top_w = jax.nn.softmax(top_val, axis=-1)
    flat_e = top_idx.reshape(-1)
    n_assign = t * TOP_K
    onehot = (flat_e[:, None] == jnp.arange(N_EXPERTS, dtype=jnp.int32)[None, :])
    rb = RANK_BLOCK
    oh3 = onehot.astype(F32).reshape(n_assign // rb, rb, N_EXPERTS)
    tril = jnp.tril(jnp.ones((rb, rb), F32))
    within = jnp.einsum('ij,bjk->bik', tril, oh3).astype(jnp.int32)
    totals = within[:, -1, :]
    before = jnp.cumsum(totals, axis=0) - totals
    csum = (within + before[:, None, :]).reshape(n_assign, N_EXPERTS)
    rank = jnp.sum(jnp.where(onehot, csum, 0), axis=1) - 1
    counts = before[-1] + totals[-1]
    padded = (counts + tm - 1) // tm * tm
    pad_end = jnp.cumsum(padded)
    pad_start = pad_end - padded
    pos = jnp.sum(jnp.where(onehot, pad_start[None, :], 0), axis=1) + rank
    n_blocks = -(-(n_assign + N_EXPERTS * (tm - 1)) // tm)
    block_row0 = jnp.arange(n_blocks, dtype=jnp.int32) * tm
    block_e = jnp.minimum(jnp.sum((pad_end[None, :] <= block_row0[:, None]).astype(jnp.int32), axis=1),
                          N_EXPERTS - 1)
    n_valid = (pad_end[-1] // tm).astype(jnp.int32).reshape(1)
    ids = jnp.arange(N_EXPERTS, dtype=jnp.int32)
    later = (ids[None, :] > ids[:, None]) & (counts[None, :] > 0)
    next_e = jnp.min(jnp.where(later, ids[None, :], N_EXPERTS), axis=1)
    next_e = jnp.where(next_e == N_EXPERTS, -1, next_e)[block_e]
    fill_lo = pad_start + counts
    return top_w, pos.reshape(t, TOP_K), fill_lo, pad_end, n_blocks * tm, block_e, next_e, n_valid


def _dispatch_copy(x_ref, rows_hbm, sem, r, dst):
    return pltpu.make_async_copy(x_ref.at[pl.ds(r, 1), :], rows_hbm.at[pl.ds(dst, 1), :], sem)


def _dispatch_kernel(lo_ref, hi_ref, pos_ref, x_ref, rows_hbm, zero_s, sem, zero_sem):
    tm = x_ref.shape[0]

    @pl.when(pl.program_id(0) == 0)
    def _():
        zero_s[...] = jnp.zeros_like(zero_s)

        def tail(do):
            def body(b, c):
                rows = pl.ds(pl.multiple_of(b * tm, tm), tm)
                do(pltpu.make_async_copy(zero_s, rows_hbm.at[rows, :], zero_sem))
                return c
            lax.fori_loop(hi_ref[N_EXPERTS - 1] // tm, rows_hbm.shape[0] // tm, body, 0)

        tail(lambda c: c.start())
        tail(lambda c: c.wait())

        def fill(e, carry):
            def each(do):
                def body(r, c):
                    do(_dispatch_copy(zero_s, rows_hbm, zero_sem, 0, r))
                    return c
                lax.fori_loop(lo_ref[e], hi_ref[e], body, 0)
            each(lambda c: c.start())
            each(lambda c: c.wait())
            return carry

        lax.fori_loop(0, N_EXPERTS, fill, 0)

    def each(do):
        def body(r, c):
            for k in range(TOP_K):
                do(_dispatch_copy(x_ref, rows_hbm, sem, r, pos_ref[r * TOP_K + k]), k)
            return c
        lax.fori_loop(0, tm, body, 0, unroll=8)

    each(lambda c, k: c.start(priority=k % 2))
    each(lambda c, k: c.wait())


def _dispatch(xn, pos, fill_lo, pad_end, n_rows):
    tm = TM_TOKENS
    t = xn.shape[0]
    grid_spec = pltpu.PrefetchScalarGridSpec(
        num_scalar_prefetch=2,
        grid=(t // tm,),
        in_specs=[pl.BlockSpec((tm * TOP_K,), lambda i, lo, hi: (i,), memory_space=pltpu.SMEM),
                  pl.BlockSpec((tm, D_MODEL), lambda i, lo, hi: (i, 0))],
        out_specs=pl.BlockSpec(memory_space=pl.ANY),
        scratch_shapes=[pltpu.VMEM((tm, D_MODEL), F32), pltpu.SemaphoreType.DMA(()), pltpu.SemaphoreType.DMA(())],
    )
    return pl.pallas_call(
        _dispatch_kernel,
        grid_spec=grid_spec,
        out_shape=jax.ShapeDtypeStruct((n_rows, D_MODEL), F32),
        compiler_params=_cparams("arbitrary"),
        name="moe_dispatch",
    )(fill_lo, pad_end, pos.reshape(-1), xn)


def _moe_weight_copies(e, w_hbm, wbuf, sems):
    return [pltpu.make_async_copy(w.at[e], wbuf.at[k], sems.at[k]) for k, w in enumerate(w_hbm)]


def _moe_kernel(be_ref, ne_ref, nv_ref, x_ref, wg_hbm, bg_ref, wu_hbm, bu_ref, wd_hbm, bd_ref, y_ref,
                w_bf, wbuf, sems):
    i = pl.program_id(0)
    e = be_ref[i]
    prev = be_ref[jnp.maximum(i - 1, 0)]
    valid = i < nv_ref[0]
    first = jnp.logical_or(i == 0, e != prev)
    copies = functools.partial(_moe_weight_copies, w_hbm=(wg_hbm, wu_hbm, wd_hbm), wbuf=wbuf, sems=sems)

    @pl.when(i == 0)
    def _():
        for c in copies(e):
            c.start()

    @pl.when(jnp.logical_and(valid, first))
    def _():
        nxt = ne_ref[i]
        for k, c in enumerate(copies(e)):
            c.wait()
            w_bf[k] = wbuf[k].astype(BF16)

        @pl.when(nxt >= 0)
        def _():
            for c in copies(nxt):
                c.start()

    @pl.when(valid)
    def _():
        x = x_ref[...].astype(BF16)
        g =jnp.minimum(_dot(x, w_bf[0]) + bg_ref[...], SWIGLU_LIMIT)
        u = jnp.clip(_dot(x, w_bf[1]) + bu_ref[...], -SWIGLU_LIMIT, SWIGLU_LIMIT)
        a = g * jax.nn.sigmoid(SWIGLU_ALPHA * g) * (u + 1.0)
        y_ref[...] = _dot(a.astype(BF16), w_bf[2]) + bd_ref[...]

    @pl.when(jnp.logical_not(valid))
    def _():
        y_ref[...] = jnp.zeros_like(y_ref)


def _moe_experts(x_rows, block_e, next_e, n_valid, w_gate, b_gate, w_up, b_up, w_down, b_down):
    tm = TM_MOE
    n_rows = x_rows.shape[0]
    wspec = pl.BlockSpec(memory_space=pl.ANY)
    bspec = pl.BlockSpec((None, 1, D_MODEL), lambda i, be, ne, nv: (be[i], 0, 0))
    rows = pl.BlockSpec((tm, D_MODEL), lambda i, be, ne, nv: (i, 0))
    grid_spec = pltpu.PrefetchScalarGridSpec(
        num_scalar_prefetch=3,
        grid=(n_rows // tm,),
        in_specs=[rows, wspec, bspec, wspec, bspec, wspec, bspec],
        out_specs=rows,
        scratch_shapes=[pltpu.VMEM((3, D_MODEL, D_MODEL), BF16), pltpu.VMEM((3, D_MODEL, D_MODEL), F32),
                        pltpu.SemaphoreType.DMA((3,))],
    )
    b3 = lambda b: b.reshape(N_EXPERTS, 1, D_MODEL)
    return pl.pallas_call(
        _moe_kernel,
        grid_spec=grid_spec,
        out_shape=jax.ShapeDtypeStruct((n_rows, D_MODEL), F32),
        compiler_params=_cparams("arbitrary"),
        name="moe_experts",
    )(block_e, next_e, n_valid, x_rows, w_gate, b3(b_gate), w_up, b3(b_up), w_down, b3(b_down))


def _ple_kernel(h1_ref, rows_ref, tw_ref, p_ref, gple_ref, wg_ref, wp_ref, gfin_ref, y_ref):
    tw = tw_ref[...]
    h2 = h1_ref[...]
    for k in range(TOP_K):
        h2 = h2 + rows_ref[k] * tw[:, k:k + 1]
    gate =jax.nn.sigmoid(_dot(_rms(h2, gple_ref[...]).astype(BF16), wg_ref[...]))
    h3 = h2 + gate * _dot(p_ref[...].astype(BF16), wp_ref[...])
    y_ref[...] = _rms(h3, gfin_ref[...])


def _ple_final(h1, expert_rows, top_w, p, row_block_offset, g_ple, w_ple_gate, w_ple_proj, g_final):
    tm = TM_TOKENS
    n = p.shape[0]
    off = row_block_offset // tm
    src = lambda width: pl.BlockSpec((tm, width), lambda i: (off + i, 0))
    loc = lambda width: pl.BlockSpec((tm, width), lambda i: (i, 0))
    return pl.pallas_call(
        _ple_kernel,
        grid=(n // tm,),
        in_specs=[src(D_MODEL), pl.BlockSpec((TOP_K, tm, D_MODEL), lambda i: (0, off + i, 0)), src(LANES),
                  loc(D_PLE), _full((1, D_MODEL)), _full((D_MODEL, D_MODEL)), _full((D_PLE, D_MODEL)),
                  _full((1, D_MODEL))],
        out_specs=loc(D_MODEL),
        out_shape=jax.ShapeDtypeStruct((n, D_MODEL), F32),
        compiler_params=_cparams("arbitrary"),
        name="ple_final",
    )(h1, expert_rows, top_w, p, g_ple.reshape(1, D_MODEL), w_ple_gate.astype(BF16),
      w_ple_proj.astype(BF16), g_final.reshape(1, D_MODEL))


def kernel(x_prompt, x_sample, p_prompt, p_sample, state_ssm_re, state_ssm_im, state_ml_c, state_ml_n, state_ml_m, g_mix, w_in, ssm_a_re, ssm_a_im, ssm_log_dt, ssm_b_re, ssm_b_im, ssm_c_re, ssm_c_im, ssm_d, ssm_w_glu, ssm_b_glu, ml_b_ig, ml_b_fg, w_ssm_up, w_ml_up, w_out, g_ffn, w_router, b_router, w_gate, b_gate, w_up, b_up, w_down, b_down, g_ple, w_ple_gate, w_ple_proj, g_final):
    assert g_mix.shape[0] == 1, "single-layer trunk"
    bp, lp, _ = x_prompt.shape
    bs, ls, _ = x_sample.shape
    tp, ts = bp * lp, bs * ls
    t = tp + ts
    xp = x_prompt.reshape(tp, D_MODEL)
    xs = x_sample.reshape(ts, D_MODEL)

    u, qkv, o, gates, gsm = _inproj(xp, xs, g_mix[0], w_in[0])

    s5_args = (ssm_a_re[0], ssm_a_im[0], ssm_log_dt[0], ssm_b_re[0], ssm_b_im[0], ssm_c_re[0],
               ssm_c_im[0], ssm_d[0])
    zero_state = jnp.zeros((S5_TILES, bp, 2 * S5_TILE_STATE), F32)
    tables = {tc: _s5_tables(*s5_args, tc) for tc in {S5_CHUNK, ls}}
    y_p, f_p = _s5(u, 0, tables[S5_CHUNK], zero_state,
                   tc=S5_CHUNK, n_chunks=lp // S5_CHUNK, nb=bp // 2, n_splits=2)
    y_s, f_s = _s5(u, tp, tables[ls], _s5_state_to_tiles(state_ssm_re[0], state_ssm_im[0]),
                   tc=ls, n_chunks=1, nb=bs, n_splits=1)
    re_p, im_p = _s5_state_from_tiles(f_p)
    re_s, im_s = _s5_state_from_tiles(f_s)

    bias_row = jnp.pad(jnp.concatenate([ml_b_ig[0], ml_b_fg[0]]), (0, GATE_PAD - 2 * ML_HEADS)).reshape(1, GATE_PAD)
    hm_p, c_p, n_p, m_p = _mlstm_prompt(qkv, o, gates, bias_row, bp, lp)
    m0 = jnp.broadcast_to(state_ml_m[0][:, :, None, None], (bs, ML_HEADS, 1, LANES))
    hm_s, c_s, n_s, m_s = _mlstm_sample(qkv, o, gates, bias_row, state_ml_c[0],
                                        state_ml_n[0].reshape(bs, ML_HEADS, 1, ML_DK), m0, tp, bs, ls)

    h1, xn, logits = _merge(xp, xs, (y_p, y_s), (hm_p, hm_s), gsm, ssm_w_glu[0], ssm_b_glu[0], w_ssm_up[0], w_ml_up[0],
                            w_out[0], g_ffn[0], w_router[0], b_router[0])

    top_w, pos, fill_lo, pad_end, n_rows, block_e, next_e, n_valid = _route(logits[:, :N_EXPERTS], TM_MOE)
    x_rows = _dispatch(xn, pos, fill_lo, pad_end, n_rows)
    expert_w = lambda w: w.reshape(N_EXPERTS, D_MODEL, D_MODEL)
    y_rows = _moe_experts(x_rows, block_e, next_e, n_valid, expert_w(w_gate), b_gate[0], expert_w(w_up),
                          b_up[0], expert_w(w_down), b_down[0])
    expert_rows = y_rows[pos.T.reshape(-1)].reshape(TOP_K, t, D_MODEL)
    top_w_pad = jnp.pad(top_w, ((0, 0), (0, LANES - TOP_K)))

    ple_w = (g_ple[0], w_ple_gate[0], w_ple_proj[0], g_final)
    y_prompt = _ple_final(h1, expert_rows, top_w_pad, p_prompt[0].reshape(tp, D_PLE), 0, *ple_w)
    y_sample = _ple_final(h1, expert_rows, top_w_pad, p_sample[0].reshape(ts, D_PLE), tp, *ple_w)

    return (y_prompt.reshape(bp, lp, D_MODEL), y_sample.reshape(bs, ls, D_MODEL),
            re_p, im_p, c_p[None], n_p.reshape(1, bp, ML_HEADS, ML_DK), m_p[:, :, 0, 0][None],
            re_s, im_s, c_s[None], n_s.reshape(1, bs, ML_HEADS, ML_DK), m_s[:, :, 0, 0][None])
```

```python
import functools

import jax
import jax.numpy as jnp
from jax import lax
from jax.experimental import pallas as pl
from jax.experimental.pallas import tpu as pltpu

F32 = jnp.float32
BF16 = jnp.bfloat16
HIGHEST = lax.Precision.HIGHEST

D_MODEL = 1024
D_SSM = 512
SSM_GROUP = 16
N_GROUPS = 32
SSM_STATE = 64
ML_HEADS = 4
ML_DK = 128
D_ML = 512
N_EXPERTS = 32
TOP_K = 4
SWIGLU_LIMIT = 7.0
SWIGLU_ALPHA = 1.702
D_PLE = 256
RMS_EPS = 1e-6

LANES = 128
GATE_PAD = LANES
S5_CHUNK = 8
ML_CHUNK_PROMPT = 256
TM_TOKENS = 256
TM_MOE = 256
RANK_BLOCK = 256
VMEM_LIMIT = 56 * 1024 * 1024


def _cparams(*sem):
    return pltpu.CompilerParams(dimension_semantics=sem, vmem_limit_bytes=VMEM_LIMIT)


def _rms(x, g):
    return x * lax.rsqrt(jnp.mean(x * x, axis=-1, keepdims=True) + RMS_EPS) * g


def _dot(a, b):
    return jnp.dot(a, b, preferred_element_type=F32)


def _dot_hi(a, b):
    return jnp.dot(a, b, preferred_element_type=F32, precision=HIGHEST)


def _full(shape):
    n = len(shape)
    return pl.BlockSpec(shape, lambda *_: (0,) * n)


def _inproj_kernel(xp_ref, xs_ref, g_ref, wu_ref, wqkv_ref, wo_ref, wgt_ref, wgsm_ref,
                   u_ref, qkv_ref, o_ref, gt_ref, gsm_ref, *, n_prompt_blocks):
    i = pl.program_id(0)
    x = jnp.where(i < n_prompt_blocks, xp_ref[...], xs_ref[...])
    hn = _rms(x, g_ref[...]).astype(BF16)
    u_ref[...] = _dot(hn, wu_ref[...])
    qkv = _dot(hn, wqkv_ref[...])
    col = lax.broadcasted_iota(jnp.int32, (1, 3 * D_ML), 1)
    k_scale = jnp.where((col >= D_ML) & (col < 2 * D_ML), ML_DK ** -0.5, 1.0).astype(F32)
    qkv_ref[...] = (qkv * k_scale).astype(BF16)
    o_ref[...] = _dot(hn, wo_ref[...])
    gt_ref[...] = _dot(hn, wgt_ref[...])
    gsm_ref[...] = _dot(hn, wgsm_ref[...])


def _two_source_specs(tm, width, n_prompt_blocks):
    last = n_prompt_blocks - 1
    return (pl.BlockSpec((tm, width), lambda i: (jnp.minimum(i, last), 0)),
            pl.BlockSpec((tm, width), lambda i: (jnp.maximum(i - n_prompt_blocks, 0), 0)))


def _inproj(xp, xs, g_mix, w_in):
    tm = TM_TOKENS
    tp, ts = xp.shape[0], xs.shape[0]
    t = tp + ts
    npb = tp // tm
    w = w_in.astype(BF16)
    o0 = D_SSM
    wu = w[:, :o0]
    wqkv = w[:, o0:o0 + 3 * D_ML]
    wo = w[:, o0 + 3 * D_ML:o0 + 4 * D_ML]
    g0 = o0 + 4 * D_ML
    wgt = jnp.pad(w[:, g0:g0 + 2 * ML_HEADS], ((0, 0), (0, GATE_PAD - 2 * ML_HEADS)))
    wgsm = w[:, g0 + 2 * ML_HEADS:]
    xp_spec, xs_spec = _two_source_specs(tm, D_MODEL, npb)
    outs = (jax.ShapeDtypeStruct((t, D_SSM), F32), jax.ShapeDtypeStruct((t, 3 * D_ML), BF16),
            jax.ShapeDtypeStruct((t, D_ML), F32), jax.ShapeDtypeStruct((t, GATE_PAD), F32),
            jax.ShapeDtypeStruct((t, 2 * D_MODEL), F32))
    row = lambda width: pl.BlockSpec((tm, width), lambda i: (i, 0))
    return pl.pallas_call(
        functools.partial(_inproj_kernel, n_prompt_blocks=npb),
        grid=(t // tm,),
        in_specs=[xp_spec, xs_spec, _full((1, D_MODEL)), _full(wu.shape), _full(wqkv.shape),
                  _full(wo.shape), _full(wgt.shape), _full(wgsm.shape)],
        out_specs=[row(D_SSM), row(3 * D_ML), row(D_ML), row(GATE_PAD), row(2 * D_MODEL)],
        out_shape=outs,
        compiler_params=_cparams("arbitrary"),
        name="inproj",
    )(xp, xs, g_mix.reshape(1, D_MODEL), wu, wqkv, wo, wgt, wgsm)


S5_TILES = D_SSM // LANES
S5_TILE_GROUPS = LANES // SSM_GROUP
S5_TILE_STATE = S5_TILE_GROUPS * SSM_STATE


def _block_diag_tiles(x):
    gt = S5_TILE_GROUPS
    n, _, r, c = x.shape
    x5 = x.reshape(n, S5_TILES, gt, r, c)
    eye = jnp.eye(gt, dtype=x.dtype)
    return (x5[:, :, :, :, None, :] * eye[None, None, :, None, :, None]).reshape(n, S5_TILES, gt * r, gt * c)


def _s5_tables(a_re, a_im, log_dt, b_re, b_im, c_re, c_im, d_skip, tc):
    ein = functools.partial(jnp.einsum, precision=HIGHEST)
    dt = jnp.exp(log_dt)[:, None]
    mag = jnp.exp(a_re * dt)
    abar_r, abar_i = mag * jnp.cos(a_im * dt), mag * jnp.sin(a_im * dt)
    den = a_re * a_re + a_im * a_im
    nr, ni = abar_r - 1.0, abar_i
    coef_r = (nr * a_re + ni * a_im) / den
    coef_i = (ni * a_re - nr * a_im) / den
    bbar_r = coef_r[..., None] * b_re - coef_i[..., None] * b_im
    bbar_i = coef_r[..., None] * b_im + coef_i[..., None] * b_re

    def abar_pow(j):
        jj = j[..., None, None]
        mag_j = jnp.where(jj >= 0, jnp.exp(jj * (a_re * dt)), 0.0)
        return mag_j * jnp.cos(jj * (a_im * dt)), mag_j * jnp.sin(jj * (a_im * dt))

    half = tc // 2
    steps = jnp.arange(tc, dtype=F32)
    at_r, at_i = abar_pow(jnp.full((), tc, F32))
    lags = (2.0 * jnp.arange(half, dtype=F32)[:, None, None]
            + jnp.array([[0.0, 1.0], [-1.0, 0.0]], F32)[None])
    lag_r, lag_i = abar_pow(lags.reshape(-1))
    ab_r = lag_r[..., None] * bbar_r - lag_i[..., None] * bbar_i
    ab_i = lag_r[..., None] * bbar_i + lag_i[..., None] * bbar_r
    kern = ein('ghp,jgpk->jgkh', c_re, ab_r) - ein('ghp,jgpk->jgkh', c_im, ab_i)
    bd_lag = _block_diag_tiles(kern.astype(BF16)).reshape(half, 2, 2, S5_TILES, LANES, LANES)
    toe = jnp.transpose(bd_lag, (3, 0, 1, 4, 2, 5)).reshape(S5_TILES, half, 2 * LANES, 2 * LANES)
    rev_r, rev_i = abar_pow(tc - 1.0 - steps)
    s_r = jnp.transpose(rev_r[..., None] * bbar_r - rev_i[..., None] * bbar_i, (0, 1, 3, 2))
    s_i = jnp.transpose(rev_r[..., None] * bbar_i + rev_i[..., None] * bbar_r, (0, 1, 3, 2))
    s_step = jnp.concatenate([_block_diag_tiles(s_r.astype(BF16)), _block_diag_tiles(s_i.astype(BF16))],
                             axis=3)
    s_tab = jnp.transpose(s_step.reshape(half, 2, S5_TILES, LANES, 2 * S5_TILE_STATE),
                          (2, 0, 1, 3, 4)).reshape(S5_TILES, half, 2 * LANES, 2 * S5_TILE_STATE)
    a1_r, a1_i = abar_pow(steps + 1.0)
    c_re_t, c_im_t = jnp.transpose(c_re, (0, 2, 1)), jnp.transpose(c_im, (0, 2, 1))
    p_r = c_re_t[None] * a1_r[..., None] - c_im_t[None] * a1_i[..., None]
    p_i = -c_re_t[None] * a1_i[..., None] - c_im_t[None] * a1_r[..., None]
    p_step = jnp.concatenate([_block_diag_tiles(p_r.astype(BF16)), _block_diag_tiles(p_i.astype(BF16))],
                             axis=2)
    p_tab = jnp.transpose(p_step.reshape(half, 2, S5_TILES, 2 * S5_TILE_STATE, LANES),
                          (2, 0, 3, 1, 4)).reshape(S5_TILES, half, 2 * S5_TILE_STATE, 2 * LANES)
    a_tab = jnp.stack([at_r.reshape(S5_TILES, S5_TILE_STATE), at_i.reshape(S5_TILES, S5_TILE_STATE)], axis=1)
    d_tab = d_skip.reshape(S5_TILES, 1, LANES)
    return toe.astype(BF16), s_tab.astype(BF16), p_tab.astype(BF16), a_tab, d_tab


def _s5_kernel(u_ref, t_ref, s_ref, p_ref, a_ref, d_ref, h0_ref, y_ref, f_ref, loc, xprev,
               *, tc, n_chunks, nb):
    r = n_chunks * nb
    half = tc // 2
    ns = S5_TILE_STATE
    step_rows = lambda t: pl.ds(t, r, stride=tc)
    v = [u_ref[step_rows(t), :] for t in range(tc)]
    vp = [jnp.concatenate([v[2 * a].astype(BF16), v[2 * a + 1].astype(BF16)], axis=1) for a in range(half)]
    acc = _dot(vp[0], s_ref[0])
    for a in range(1, half):
        acc = acc + _dot(vp[a], s_ref[a])
    nt = ns // LANES
    lane_tile = lambda k: slice(k * LANES, (k + 1) * LANES)
    for k in range(2 * nt):
        loc[k] = acc[:, lane_tile(k)]
    abar = a_ref[...]

    def body(c, carry):
        rows = pl.ds(c, nb, stride=n_chunks) if n_chunks > 1 else pl.ds(0, nb)
        new = []
        for k in range(nt):
            xr, xi = carry[k], carry[nt + k]
            xprev[k, rows, :] = xr
            xprev[nt + k, rows, :] = xi
            ar, ai = abar[0:1, lane_tile(k)], abar[1:2, lane_tile(k)]
            new.append((ar * xr - ai * xi + loc[k, rows, :], ar * xi + ai * xr + loc[nt + k, rows, :]))
        return tuple(n[0] for n in new) + tuple(n[1] for n in new)

    x_end = lax.fori_loop(0, n_chunks, body, tuple(h0_ref[:, lane_tile(k)] for k in range(2 * nt)))
    for k in range(2 * nt):
        f_ref[:, lane_tile(k)] = x_end[k]
    xp = jnp.concatenate([xprev[k] for k in range(2 * nt)], axis=1).astype(BF16)
    d = d_ref[...]
    for a2 in range(half):
        acc = _dot(xp, p_ref[a2])
        for a in range(a2 + 1):
            acc = acc + _dot(vp[a], t_ref[a2 - a])
        for k in range(2):
            t = 2 * a2 + k
            y_ref[step_rows(t), :] = acc[:, k * LANES:(k + 1) * LANES] + v[t] * d


def _s5(u, row_block_offset, tables, h0, *, tc, n_chunks, nb, n_splits):
    toe, s_tab, p_tab, a_tab, d_tab = tables
    rows = nb * n_chunks * tc
    off = row_block_offset // rows
    half = tc // 2
    ns2 = 2 * S5_TILE_STATE
    tile = lambda *tail: pl.BlockSpec((None,) + tail, lambda j, s: (j,) + (0,) * len(tail))
    state = pl.BlockSpec((None, None, nb, ns2), lambda j, s: (j, s, 0, 0))
    y, f = pl.pallas_call(
        functools.partial(_s5_kernel, tc=tc, n_chunks=n_chunks, nb=nb),
        grid=(S5_TILES, n_splits),
        in_specs=[pl.BlockSpec((rows, LANES), lambda j, s: (off + s, j)),
                  tile(half, 2 * LANES, 2 * LANES), tile(half, 2 * LANES, ns2), tile(half, ns2, 2 * LANES),
                  tile(2, S5_TILE_STATE), tile(1, LANES), state],
        out_specs=[pl.BlockSpec((rows, LANES), lambda j, s: (s, j)), state],
        out_shape=(jax.ShapeDtypeStruct((rows * n_splits, D_SSM), F32),
                   jax.ShapeDtypeStruct((S5_TILES, n_splits, nb, ns2), F32)),
        scratch_shapes=[pltpu.VMEM((ns2 // LANES, nb * n_chunks, LANES), F32)] * 2,
        compiler_params=_cparams("arbitrary", "arbitrary"),
        name=f"s5_c{n_chunks}",
    )(u, toe, s_tab, p_tab, a_tab, d_tab, h0.reshape(S5_TILES, n_splits, nb, ns2))
    return y, f.reshape(S5_TILES, n_splits * nb, ns2)


def _s5_state_to_tiles(s_re, s_im):
    b = s_re.shape[0]
    f = lambda s: s.reshape(b, S5_TILES, S5_TILE_STATE).transpose(1, 0, 2)
    return jnp.concatenate([f(s_re), f(s_im)], axis=2)


def _s5_state_from_tiles(f):
    b = f.shape[1]
    g = lambda s: s.transpose(1, 0, 2).reshape(1, b, N_GROUPS, SSM_STATE)
    return g(f[:, :, :S5_TILE_STATE]), g(f[:, :, S5_TILE_STATE:])


def _log_sigmoid(x):
    return jnp.minimum(x, 0.0) - jnp.log1p(jnp.exp(-jnp.abs(x)))


def _mlstm_gates(gates, bias_row, lc):
    g = gates + bias_row
    col = lax.broadcasted_iota(jnp.int32, (1, GATE_PAD), 1)
    gl = jnp.where(col >= ML_HEADS, _log_sigmoid(g), g)
    r = lax.broadcasted_iota(jnp.int32, (lc, lc), 0)
    c = lax.broadcasted_iota(jnp.int32, (lc, lc), 1)
    tril = (r >= c).astype(F32)
    bcols = _dot_hi(tril, gl)
    sel = (lax.broadcasted_iota(jnp.int32, (8, GATE_PAD), 0)
           == lax.broadcasted_iota(jnp.int32, (8, GATE_PAD), 1)).astype(F32)
    nt = (((1,), (1,)), ((), ()))
    grows = lax.dot_general(sel, gl, nt, precision=HIGHEST, preferred_element_type=F32)
    brows = lax.dot_general(sel, bcols, nt, precision=HIGHEST, preferred_element_type=F32)
    return gl, bcols, grows, brows, (r >= c)


def _mlstm_head(q, k, v, ic, bc, ir, br, causal, c_state, n_state, m_state, lc):
    dmat = jnp.where(causal, bc - br + ir, -jnp.inf)
    inter = bc + m_state
    mt = jnp.maximum(inter, jnp.max(dmat, axis=1, keepdims=True))
    w_in = jnp.exp(dmat - mt)
    w_out = jnp.exp(inter - mt)
    nt = (((1,), (1,)), ((), ()))
    s = lax.dot_general(q, k, nt, preferred_element_type=F32) * w_in
    num = w_out * _dot(q, c_state.astype(BF16)) + _dot(s.astype(BF16), v)
    qf = q.astype(F32)
    qn = w_out * jnp.sum(qf * n_state, axis=1, keepdims=True) + jnp.sum(s, axis=1, keepdims=True)
    h = num / jnp.maximum(jnp.abs(qn), jnp.exp(-mt))
    m_end = mt[lc - 1:lc, :]
    b_last = bc[lc - 1:lc, :]
    decay = jnp.exp(b_last + m_state - m_end)
    ws = jnp.exp(b_last - bc + ic - m_end)
    kw = k.astype(F32) * ws
    tn = (((0,), (0,)), ((), ()))
    c_new = decay * c_state + lax.dot_general(kw.astype(BF16), v, tn, preferred_element_type=F32)
    n_new = decay * n_state + jnp.sum(kw, axis=0, keepdims=True)
    return h, c_new, n_new, m_end


def _mlstm_chunk(qkv, o, gates, bias_row, get_state, put_state, lc):
    gl, bcols, grows, brows, causal = _mlstm_gates(gates, bias_row, lc)
    outs = []
    for hd in range(ML_HEADS):
        q = qkv[:, hd * ML_DK:(hd + 1) * ML_DK]
        k = qkv[:, D_ML + hd * ML_DK:D_ML + (hd + 1) * ML_DK]
        v = qkv[:, 2 * D_ML + hd * ML_DK:2 * D_ML + (hd + 1) * ML_DK]
        f = ML_HEADS + hd
        c_state, n_state, m_state = get_state(hd)
        h, c_new, n_new, m_new = _mlstm_head(
            q, k, v, gl[:, hd:hd + 1], bcols[:, f:f + 1], grows[hd:hd + 1, :], brows[f:f + 1, :],
            causal, c_state, n_state, m_state, lc)
        put_state(hd, c_new, n_new, m_new)
        outs.append(jax.nn.sigmoid(o[:, hd * ML_DK:(hd + 1) * ML_DK]) * h)
    return jnp.concatenate(outs, axis=1)


def _mlstm_prompt_kernel(qkv_ref, o_ref, gt_ref, bias_ref, h_ref, c_ref, n_ref, m_ref, *, lc):
    @pl.when(pl.program_id(1) == 0)
    def _():
        c_ref[...] = jnp.zeros_like(c_ref)
        n_ref[...] = jnp.zeros_like(n_ref)
        m_ref[...] = jnp.zeros_like(m_ref)

    def get_state(hd):
        return c_ref[hd], n_ref[hd], m_ref[hd][:, 0:1]

    def put_state(hd, c_new, n_new, m_new):
        c_ref[hd] = c_new
        n_ref[hd] = n_new
        m_ref[hd] = jnp.broadcast_to(m_new, (1, LANES))

    h = _mlstm_chunk(qkv_ref[...], o_ref[...], gt_ref[...], bias_ref[...], get_state, put_state, lc)
    h_ref[...] = h.astype(BF16)


def _mlstm_prompt(qkv, o, gates, bias_row, bsz, seq):
    lc = ML_CHUNK_PROMPT
    nc = seq // lc
    row = lambda width: pl.BlockSpec((lc, width), lambda b, c: (b * nc + c, 0))
    st = lambda *tail: pl.BlockSpec((None, ML_HEADS) + tail, lambda b, c: (b, 0) + (0,) * len(tail))
    return pl.pallas_call(
        functools.partial(_mlstm_prompt_kernel, lc=lc),
        grid=(bsz, nc),
        in_specs=[row(3 * D_ML), row(D_ML), row(GATE_PAD), pl.BlockSpec((1, GATE_PAD), lambda b, c: (0, 0))],
        out_specs=[row(D_ML), st(ML_DK, ML_DK), st(1, ML_DK), st(1, LANES)],
        out_shape=(jax.ShapeDtypeStruct((bsz * seq, D_ML), BF16),
                   jax.ShapeDtypeStruct((bsz, ML_HEADS, ML_DK, ML_DK), F32),
                   jax.ShapeDtypeStruct((bsz, ML_HEADS, 1, ML_DK), F32),
                   jax.ShapeDtypeStruct((bsz, ML_HEADS, 1, LANES), F32)),
        compiler_params=_cparams("arbitrary", "arbitrary"),
        name="mlstm_prompt",
    )(qkv, o, gates, bias_row)


def _mlstm_sample_kernel(qkv_ref, o_ref, gt_ref, bias_ref, c0_ref, n0_ref, m0_ref,
                         h_ref, c_ref, n_ref, m_ref, qkv_s, *, lc, nb):
    qkv_s[...] = qkv_ref[...].astype(F32)

    def body(b, carry):
        rows = pl.ds(pl.multiple_of(b * lc, lc), lc)

        def get_state(hd):
            return c0_ref[b, hd], n0_ref[b, hd], m0_ref[b, hd][:, 0:1]

        def put_state(hd, c_new, n_new, m_new):
            c_ref[b, hd] = c_new
            n_ref[b, hd] = n_new
            m_ref[b, hd] = jnp.broadcast_to(m_new, (1, LANES))

        h = _mlstm_chunk(qkv_s[rows, :].astype(BF16), o_ref[rows, :], gt_ref[rows, :], bias_ref[...],
                         get_state, put_state, lc)
        h_ref[rows, :] = h
        return carry

    lax.fori_loop(0, nb, body, 0, unroll=True)


def _mlstm_sample(qkv, o, gates, bias_row, c0, n0, m0, row_block_offset, bsz, seq):
    nb = 8
    lc = seq
    rows = nb * lc
    off = row_block_offset // rows
    row = lambda width: pl.BlockSpec((rows, width), lambda i: (off + i, 0))
    st = lambda *tail: pl.BlockSpec((nb, ML_HEADS) + tail, lambda i: (i, 0) + (0,) * len(tail))
    return pl.pallas_call(
        functools.partial(_mlstm_sample_kernel, lc=lc, nb=nb),
        grid=(bsz // nb,),
        in_specs=[row(3 * D_ML), row(D_ML), row(GATE_PAD), pl.BlockSpec((1, GATE_PAD), lambda i: (0, 0)),
                  st(ML_DK, ML_DK), st(1, ML_DK), st(1, LANES)],
        out_specs=[pl.BlockSpec((rows, D_ML), lambda i: (i, 0)), st(ML_DK, ML_DK), st(1, ML_DK), st(1, LANES)],
        out_shape=(jax.ShapeDtypeStruct((bsz * seq, D_ML), F32),
                   jax.ShapeDtypeStruct((bsz, ML_HEADS, ML_DK, ML_DK), F32),
                   jax.ShapeDtypeStruct((bsz, ML_HEADS, 1, ML_DK), F32),
                   jax.ShapeDtypeStruct((bsz, ML_HEADS, 1, LANES), F32)),
        scratch_shapes=[pltpu.VMEM((rows, 3 * D_ML), F32)],
        compiler_params=_cparams("arbitrary"),
        name="mlstm_sample",
    )(qkv, o, gates, bias_row, c0, n0, m0)


def _merge_kernel(xp_ref, xs_ref, yp_ref, ys_ref, mp_ref, ms_ref, gsm_ref, wglu_ref, bglu_ref, wsu_ref, wmu_ref,
                  wout_ref, gffn_ref, wrh_ref, wrl_ref, br_ref, h1_ref, xn_ref, lg_ref, *, n_prompt_blocks):
    i = pl.program_id(0)
    is_prompt = i < n_prompt_blocks
    x = jnp.where(is_prompt, xp_ref[...], xs_ref[...])
    y = jax.nn.gelu(jnp.where(is_prompt, yp_ref[...], ys_ref[...]))
    ym = jnp.where(is_prompt, mp_ref[...], ms_ref[...].astype(BF16))
    ys = y * jax.nn.sigmoid(_dot(y.astype(BF16), wglu_ref[...]) + bglu_ref[...])
    gsm = gsm_ref[...]
    merged = (jax.nn.sigmoid(gsm[:, :D_MODEL]) * _dot(ys.astype(BF16), wsu_ref[...])
              + jax.nn.sigmoid(gsm[:, D_MODEL:]) * _dot(ym, wmu_ref[...]))
    h1 = x + _dot(merged.astype(BF16), wout_ref[...])
    h1_ref[...] = h1
    xn = _rms(h1, gffn_ref[...])
    _rows_to_tiles(xn, xn_ref)
    xn_hi = xn.astype(BF16)
    xn_lo = (xn - xn_hi.astype(F32)).astype(BF16)
    lg_ref[...] = (_dot(xn_hi, wrh_ref[...]) + _dot(xn_lo, wrh_ref[...]) + _dot(xn_hi, wrl_ref[...])
                   + br_ref[...])


def _merge(xp, xs, y_pre, ym, gsm, w_glu, b_glu, w_ssm_up, w_ml_up, w_out, g_ffn, w_router, b_router):
    tm = TM_TOKENS
    t = gsm.shape[0]
    npb = xp.shape[0] // tm
    xp_spec, xs_spec = _two_source_specs(tm, D_MODEL, npb)
    yp_spec, ys_spec = _two_source_specs(tm, D_SSM, npb)
    mp_spec, ms_spec = _two_source_specs(tm, D_ML, npb)
    row = lambda width: pl.BlockSpec((tm, width), lambda i: (i, 0))
    wr = jnp.pad(w_router, ((0, 0), (0, LANES - N_EXPERTS)))
    wr_hi = wr.astype(BF16)
    wr_lo = (wr - wr_hi.astype(F32)).astype(BF16)
    br =jnp.pad(b_router, (0, LANES - N_EXPERTS)).reshape(1, LANES)
    return pl.pallas_call(
        functools.partial(_merge_kernel, n_prompt_blocks=npb),
        grid=(t // tm,),
        in_specs=[xp_spec, xs_spec, yp_spec, ys_spec, mp_spec, ms_spec, row(2 * D_MODEL), _full((D_SSM, D_SSM)),
                  _full((1, D_SSM)), _full((D_SSM, D_MODEL)), _full((D_ML, D_MODEL)),
                  _full((D_MODEL, D_MODEL)), _full((1, D_MODEL)), _full((D_MODEL, LANES)),
                  _full((D_MODEL, LANES)), _full((1, LANES))],
        out_specs=[row(D_MODEL), pl.BlockSpec((tm,) + ROW_TILE, lambda i: (i, 0, 0)), row(LANES)],
        out_shape=(jax.ShapeDtypeStruct((t, D_MODEL), F32), jax.ShapeDtypeStruct((t,) + ROW_TILE, F32),
                   jax.ShapeDtypeStruct((t, LANES), F32)),
        compiler_params=_cparams("arbitrary"),
        name="merge",
    )(xp, xs, *y_pre, *ym, gsm, w_glu.astype(BF16), b_glu.reshape(1, D_SSM), w_ssm_up.astype(BF16),
      w_ml_up.astype(BF16), w_out.astype(BF16), g_ffn.reshape(1, D_MODEL), wr_hi, wr_lo, br)


def _route(logits, tm):
    t = logits.shape[0]
    top_val, top_idx = lax.top_k(logits, TOP_K)
    top_w = jax.nn.softmax(top_val, axis=-1)
    flat_e = top_idx.reshape(-1)
    n_assign = t * TOP_K
    onehot = (flat_e[:, None] == jnp.arange(N_EXPERTS, dtype=jnp.int32)[None, :])
    rb = RANK_BLOCK
    oh3 = onehot.astype(F32).reshape(n_assign // rb, rb, N_EXPERTS)
    tril = jnp.tril(jnp.ones((rb, rb), F32))
    within = jnp.einsum('ij,bjk->bik', tril, oh3).astype(jnp.int32)
    totals = within[:, -1, :]
    before = jnp.cumsum(totals, axis=0) - totals
    csum = (within + before[:, None, :]).reshape(n_assign, N_EXPERTS)
    rank = jnp.sum(jnp.where(onehot, csum, 0), axis=1) - 1
    counts = before[-1] + totals[-1]
    padded = (counts + tm - 1) // tm * tm
    pad_end = jnp.cumsum(padded)
    pad_start = pad_end - padded
    pos = jnp.sum(jnp.where(onehot, pad_start[None, :], 0), axis=1) + rank
    n_blocks = -(-(n_assign + N_EXPERTS * (tm - 1)) // tm)
    block_row0 = jnp.arange(n_blocks, dtype=jnp.int32) * tm
    block_e = jnp.minimum(jnp.sum((pad_end[None, :] <= block_row0[:, None]).astype(jnp.int32), axis=1),
                          N_EXPERTS - 1)
    n_valid = (pad_end[-1] // tm).astype(jnp.int32).reshape(1)
    ids = jnp.arange(N_EXPERTS, dtype=jnp.int32)
    later = (ids[None, :] > ids[:, None]) & (counts[None, :] > 0)
    next_e = jnp.min(jnp.where(later, ids[None, :], N_EXPERTS), axis=1)
    next_e = jnp.where(next_e == N_EXPERTS, -1, next_e)[block_e]
    fill_lo = pad_start + counts
    return top_w, pos.reshape(t, TOP_K), fill_lo, pad_end, n_blocks * tm, block_e, next_e, n_valid


ROW_TILE = (D_MODEL // LANES, LANES)


def _rows_to_tiles(x, ref):
    for c in range(ROW_TILE[0]):
        ref[:, c, :] = x[:, c * LANES:(c + 1) * LANES]


def _tiles_to_rows(ref):
    return jnp.concatenate([ref[:, c, :] for c in range(ROW_TILE[0])], axis=1)


def _dispatch_copy(x_ref, rows_hbm, sem, r, dst):
    return pltpu.make_async_copy(x_ref.at[r], rows_hbm.at[dst], sem)


def _dispatch_kernel(lo_ref, hi_ref, pos_ref, x_ref, rows_hbm, zero_s, sem, zero_sem):
    tm = x_ref.shape[0]

    @pl.when(pl.program_id(0) == 0)
    def _():
        zero_s[...] = jnp.zeros_like(zero_s)

        def tail(do):
            def body(b, c):
                rows = pl.ds(pl.multiple_of(b * tm, tm), tm)
                do(pltpu.make_async_copy(zero_s, rows_hbm.at[rows], zero_sem))
                return c
            lax.fori_loop(hi_ref[N_EXPERTS - 1] // tm, rows_hbm.shape[0] // tm, body, 0)

        tail(lambda c: c.start())
        tail(lambda c: c.wait())

        def fill(e, carry):
            def each(do):
                def body(r, c):
                    do(_dispatch_copy(zero_s, rows_hbm, zero_sem, 0, r))
                    return c
                lax.fori_loop(lo_ref[e], hi_ref[e], body, 0)
            each(lambda c: c.start())
            each(lambda c: c.wait())
            return carry

        lax.fori_loop(0, N_EXPERTS, fill, 0)

    def each(do):
        def body(r, c):
            for k in range(TOP_K):
                do(_dispatch_copy(x_ref, rows_hbm, sem, r, pos_ref[r * TOP_K + k]), k)
            return c
        lax.fori_loop(0, tm, body, 0, unroll=8)

    each(lambda c, k: c.start(priority=k % 2))
    each(lambda c, k: c.wait())


def _dispatch(xn, pos, fill_lo, pad_end, n_rows):
    tm = TM_TOKENS
    t = xn.shape[0]
    grid_spec = pltpu.PrefetchScalarGridSpec(
        num_scalar_prefetch=2,
        grid=(t // tm,),
        in_specs=[pl.BlockSpec((tm * TOP_K,), lambda i, lo, hi: (i,), memory_space=pltpu.SMEM),
                  pl.BlockSpec((tm,) + ROW_TILE, lambda i, lo, hi: (i, 0, 0))],
        out_specs=pl.BlockSpec(memory_space=pl.ANY),
        scratch_shapes=[pltpu.VMEM((tm,) + ROW_TILE, F32), pltpu.SemaphoreType.DMA(()),
                        pltpu.SemaphoreType.DMA(())],
    )
    return pl.pallas_call(
        _dispatch_kernel,
        grid_spec=grid_spec,
        out_shape=jax.ShapeDtypeStruct((n_rows,) + ROW_TILE, F32),
        compiler_params=_cparams("arbitrary"),
        name="moe_dispatch",
    )(fill_lo, pad_end, pos.reshape(-1), xn)


def _moe_weight_copies(e, w_hbm, wbuf, sems):
    return [pltpu.make_async_copy(w.at[e], wbuf.at[k], sems.at[k]) for k, w in enumerate(w_hbm)]


def _moe_kernel(be_ref, ne_ref, nv_ref, x_ref, wg_hbm, bg_ref, wu_hbm, bu_ref, wd_hbm, bd_ref, y_ref,
                w_bf, wbuf, sems):
    i = pl.program_id(0)
    e = be_ref[i]
    prev = be_ref[jnp.maximum(i - 1, 0)]
    valid = i < nv_ref[0]
    first = jnp.logical_or(i == 0, e != prev)
    copies = functools.partial(_moe_weight_copies, w_hbm=(wg_hbm, wu_hbm, wd_hbm), wbuf=wbuf, sems=sems)

    @pl.when(i == 0)
    def _():
        for c in copies(e):
            c.start()

    @pl.when(jnp.logical_and(valid, first))
    def _():
        nxt = ne_ref[i]
        for k, c in enumerate(copies(e)):
            c.wait()
            w_bf[k] = wbuf[k].astype(BF16)

        @pl.when(nxt >= 0)
        def _():
            for c in copies(nxt):
                c.start()

    @pl.when(valid)
    def _():
        x = _tiles_to_rows(x_ref).astype(BF16)
        g =jnp.minimum(_dot(x, w_bf[0]) + bg_ref[...], SWIGLU_LIMIT)
        u = jnp.clip(_dot(x, w_bf[1]) + bu_ref[...], -SWIGLU_LIMIT, SWIGLU_LIMIT)
        a = g * jax.nn.sigmoid(SWIGLU_ALPHA * g) * (u + 1.0)
        y_ref[...] = _dot(a.astype(BF16), w_bf[2]) + bd_ref[...]

    @pl.when(jnp.logical_not(valid))
    def _():
        y_ref[...] = jnp.zeros_like(y_ref)


def _moe_experts(x_rows, block_e, next_e, n_valid, w_gate, b_gate, w_up, b_up, w_down, b_down):
    tm = TM_MOE
    n_rows = x_rows.shape[0]
    wspec = pl.BlockSpec(memory_space=pl.ANY)
    bspec = pl.BlockSpec((None, 1, D_MODEL), lambda i, be, ne, nv: (be[i], 0, 0))
    rows = pl.BlockSpec((tm, D_MODEL), lambda i, be, ne, nv: (i, 0))
    grid_spec = pltpu.PrefetchScalarGridSpec(
        num_scalar_prefetch=3,
        grid=(n_rows // tm,),
        in_specs=[pl.BlockSpec((tm,) + ROW_TILE, lambda i, be, ne, nv: (i, 0, 0)),
                  wspec, bspec, wspec, bspec, wspec, bspec],
        out_specs=rows,
        scratch_shapes=[pltpu.VMEM((3, D_MODEL, D_MODEL), BF16), pltpu.VMEM((3, D_MODEL, D_MODEL), F32),
                        pltpu.SemaphoreType.DMA((3,))],
    )
    b3 = lambda b: b.reshape(N_EXPERTS, 1, D_MODEL)
    return pl.pallas_call(
        _moe_kernel,
        grid_spec=grid_spec,
        out_shape=jax.ShapeDtypeStruct((n_rows, D_MODEL), F32),
        compiler_params=_cparams("arbitrary"),
        name="moe_experts",
    )(block_e, next_e, n_valid, x_rows, w_gate, b3(b_gate), w_up, b3(b_up), w_down, b3(b_down))


def _ple_kernel(h1_ref, rows_ref, tw_ref, p_ref, gple_ref, wg_ref, wp_ref, gfin_ref, y_ref):
    tw = tw_ref[...]
    h2 = h1_ref[...]
    for k in range(TOP_K):
        h2 = h2 + rows_ref[k] * tw[:, k:k + 1]
    gate =jax.nn.sigmoid(_dot(_rms(h2, gple_ref[...]).astype(BF16), wg_ref[...]))
    h3 = h2 + gate * _dot(p_ref[...].astype(BF16), wp_ref[...])
    y_ref[...] = _rms(h3, gfin_ref[...])


def _ple_final(h1, expert_rows, top_w, p, row_block_offset, g_ple, w_ple_gate, w_ple_proj, g_final):
    tm = TM_TOKENS
    n = p.shape[0]
    off = row_block_offset // tm
    src = lambda width: pl.BlockSpec((tm, width), lambda i: (off + i, 0))
    loc = lambda width: pl.BlockSpec((tm, width), lambda i: (i, 0))
    return pl.pallas_call(
        _ple_kernel,
        grid=(n // tm,),
        in_specs=[src(D_MODEL), pl.BlockSpec((TOP_K, tm, D_MODEL), lambda i: (0, off + i, 0)), src(LANES),
                  loc(D_PLE), _full((1, D_MODEL)), _full((D_MODEL, D_MODEL)), _full((D_PLE, D_MODEL)),
                  _full((1, D_MODEL))],
        out_specs=loc(D_MODEL),
        out_shape=jax.ShapeDtypeStruct((n, D_MODEL), F32),
        compiler_params=_cparams("arbitrary"),
        name="ple_final",
    )(h1, expert_rows, top_w, p, g_ple.reshape(1, D_MODEL), w_ple_gate.astype(BF16),
      w_ple_proj.astype(BF16), g_final.reshape(1, D_MODEL))


def kernel(x_prompt, x_sample, p_prompt, p_sample, state_ssm_re, state_ssm_im, state_ml_c, state_ml_n, state_ml_m, g_mix, w_in, ssm_a_re, ssm_a_im, ssm_log_dt, ssm_b_re, ssm_b_im, ssm_c_re, ssm_c_im, ssm_d, ssm_w_glu, ssm_b_glu, ml_b_ig, ml_b_fg, w_ssm_up, w_ml_up, w_out, g_ffn, w_router, b_router, w_gate, b_gate, w_up, b_up, w_down, b_down, g_ple, w_ple_gate, w_ple_proj, g_final):
    assert g_mix.shape[0] == 1, "single-layer trunk"
    bp, lp, _ = x_prompt.shape
    bs, ls, _ = x_sample.shape
    tp, ts = bp * lp, bs * ls
    t = tp + ts
    xp = x_prompt.reshape(tp, D_MODEL)
    xs = x_sample.reshape(ts, D_MODEL)

    u, qkv, o, gates, gsm = _inproj(xp, xs, g_mix[0], w_in[0])

    s5_args = (ssm_a_re[0], ssm_a_im[0], ssm_log_dt[0], ssm_b_re[0], ssm_b_im[0], ssm_c_re[0],
               ssm_c_im[0], ssm_d[0])
    zero_state = jnp.zeros((S5_TILES, bp, 2 * S5_TILE_STATE), F32)
    tables = {tc: _s5_tables(*s5_args, tc) for tc in {S5_CHUNK, ls}}
    y_p, f_p = _s5(u, 0, tables[S5_CHUNK], zero_state,
                   tc=S5_CHUNK, n_chunks=lp // S5_CHUNK, nb=bp // 2, n_splits=2)
    y_s, f_s = _s5(u, tp, tables[ls], _s5_state_to_tiles(state_ssm_re[0], state_ssm_im[0]),
                   tc=ls, n_chunks=1, nb=bs, n_splits=1)
    re_p, im_p = _s5_state_from_tiles(f_p)
    re_s, im_s = _s5_state_from_tiles(f_s)

    bias_row = jnp.pad(jnp.concatenate([ml_b_ig[0], ml_b_fg[0]]), (0, GATE_PAD - 2 * ML_HEADS)).reshape(1, GATE_PAD)
    hm_p, c_p, n_p, m_p = _mlstm_prompt(qkv, o, gates, bias_row, bp, lp)
    m0 = jnp.broadcast_to(state_ml_m[0][:, :, None, None], (bs, ML_HEADS, 1, LANES))
    hm_s, c_s, n_s, m_s = _mlstm_sample(qkv, o, gates, bias_row, state_ml_c[0],
                                        state_ml_n[0].reshape(bs, ML_HEADS, 1, ML_DK), m0, tp, bs, ls)

    h1, xn, logits = _merge(xp, xs, (y_p, y_s), (hm_p, hm_s), gsm, ssm_w_glu[0], ssm_b_glu[0], w_ssm_up[0], w_ml_up[0],
                            w_out[0], g_ffn[0], w_router[0], b_router[0])

    top_w, pos, fill_lo, pad_end, n_rows, block_e, next_e, n_valid = _route(logits[:, :N_EXPERTS], TM_MOE)
    x_rows = _dispatch(xn, pos, fill_lo, pad_end, n_rows)
    expert_w = lambda w: w.reshape(N_EXPERTS, D_MODEL, D_MODEL)
    y_rows = _moe_experts(x_rows, block_e, next_e, n_valid, expert_w(w_gate), b_gate[0], expert_w(w_up),
                          b_up[0], expert_w(w_down), b_down[0])
    expert_rows = y_rows[pos.T.reshape(-1)].reshape(TOP_K, t, D_MODEL)
    top_w_pad = jnp.pad(top_w, ((0, 0), (0, LANES - TOP_K)))

    ple_w = (g_ple[0], w_ple_gate[0], w_ple_proj[0], g_final)
    y_prompt = _ple_final(h1, expert_rows, top_w_pad, p_prompt[0].reshape(tp, D_PLE), 0, *ple_w)
    y_sample = _ple_final(h1, expert_rows, top_w_pad, p_sample[0].reshape(ts, D_PLE), tp, *ple_w)

    return (y_prompt.reshape(bp, lp, D_MODEL), y_sample.reshape(bs, ls, D_MODEL),
            re_p, im_p, c_p[None], n_p.reshape(1, bp, ML_HEADS, ML_DK), m_p[:, :, 0, 0][None],
            re_s, im_s, c_s[None], n_s.reshape(1, bs, ML_HEADS, ML_DK), m_s[:, :, 0, 0][None])
```

```python
import functools

import jax
import jax.numpy as jnp
from jax import lax
from jax.experimental import pallas as pl
from jax.experimental.pallas import tpu as pltpu

F32 = jnp.float32
BF16 = jnp.bfloat16
HIGHEST = lax.Precision.HIGHEST

D_MODEL = 1024
D_SSM = 512
SSM_GROUP = 16
N_GROUPS = 32
SSM_STATE = 64
ML_HEADS = 4
ML_DK = 128
D_ML = 512
N_EXPERTS = 32
TOP_K = 4
SWIGLU_LIMIT = 7.0
SWIGLU_ALPHA = 1.702
D_PLE = 256
RMS_EPS = 1e-6

LANES = 128
GATE_PAD = LANES
S5_CHUNK = 8
ML_CHUNK_PROMPT = 256
TM_TOKENS = 256
TM_MOE = 256
RANK_BLOCK = 256
VMEM_LIMIT = 56 * 1024 * 1024


def _cparams(*sem):
    return pltpu.CompilerParams(dimension_semantics=sem, vmem_limit_bytes=VMEM_LIMIT)


def _rms(x, g):
    return x * lax.rsqrt(jnp.mean(x * x, axis=-1, keepdims=True) + RMS_EPS) * g


def _dot(a, b):
    return jnp.dot(a, b, preferred_element_type=F32)


def _dot_hi(a, b):
    return jnp.dot(a, b, preferred_element_type=F32, precision=HIGHEST)


def _full(shape):
    n = len(shape)
    return pl.BlockSpec(shape, lambda *_: (0,) * n)


def _inproj_kernel(xp_ref, xs_ref, g_ref, wu_ref, wqkv_ref, wo_ref, wgt_ref, wgsm_ref,
                   u_ref, qkv_ref, o_ref, gt_ref, gsm_ref, *, n_prompt_blocks):
    i = pl.program_id(0)
    x = jnp.where(i < n_prompt_blocks, xp_ref[...], xs_ref[...])
    hn = _rms(x, g_ref[...]).astype(BF16)
    u_ref[...] = _dot(hn, wu_ref[...])
    qkv = _dot(hn, wqkv_ref[...])
    col = lax.broadcasted_iota(jnp.int32, (1, 3 * D_ML), 1)
    k_scale = jnp.where((col >= D_ML) & (col < 2 * D_ML), ML_DK ** -0.5, 1.0).astype(F32)
    qkv_ref[...] = (qkv * k_scale).astype(BF16)
    o_ref[...] = _dot(hn, wo_ref[...])
    gt_ref[...] = _dot(hn, wgt_ref[...])
    gsm_ref[...] = _dot(hn, wgsm_ref[...])


def _two_source_specs(tm, width, n_prompt_blocks):
    last = n_prompt_blocks - 1
    return (pl.BlockSpec((tm, width), lambda i: (jnp.minimum(i, last), 0)),
            pl.BlockSpec((tm, width), lambda i: (jnp.maximum(i - n_prompt_blocks, 0), 0)))


def _inproj(xp, xs, g_mix, w_in):
    tm = TM_TOKENS
    tp, ts = xp.shape[0], xs.shape[0]
    t = tp + ts
    npb = tp // tm
    w = w_in.astype(BF16)
    o0 = D_SSM
    wu = w[:, :o0]
    wqkv = w[:, o0:o0 + 3 * D_ML]
    wo = w[:, o0 + 3 * D_ML:o0 + 4 * D_ML]
    g0 = o0 + 4 * D_ML
    wgt = jnp.pad(w[:, g0:g0 + 2 * ML_HEADS], ((0, 0), (0, GATE_PAD - 2 * ML_HEADS)))
    wgsm = w[:, g0 + 2 * ML_HEADS:]
    xp_spec, xs_spec = _two_source_specs(tm, D_MODEL, npb)
    outs = (jax.ShapeDtypeStruct((t, D_SSM), F32), jax.ShapeDtypeStruct((t, 3 * D_ML), BF16),
            jax.ShapeDtypeStruct((t, D_ML), F32), jax.ShapeDtypeStruct((t, GATE_PAD), F32),
            jax.ShapeDtypeStruct((t, 2 * D_MODEL), F32))
    row = lambda width: pl.BlockSpec((tm, width), lambda i: (i, 0))
    return pl.pallas_call(
        functools.partial(_inproj_kernel, n_prompt_blocks=npb),
        grid=(t // tm,),
        in_specs=[xp_spec, xs_spec, _full((1, D_MODEL)), _full(wu.shape), _full(wqkv.shape),
                  _full(wo.shape), _full(wgt.shape), _full(wgsm.shape)],
        out_specs=[row(D_SSM), row(3 * D_ML), row(D_ML), row(GATE_PAD), row(2 * D_MODEL)],
        out_shape=outs,
        compiler_params=_cparams("arbitrary"),
        name="inproj",
    )(xp, xs, g_mix.reshape(1, D_MODEL), wu, wqkv, wo, wgt, wgsm)


S5_TILES = D_SSM // LANES
S5_TILE_GROUPS = LANES // SSM_GROUP
S5_TILE_STATE = S5_TILE_GROUPS * SSM_STATE


def _block_diag_tiles(x):
    gt = S5_TILE_GROUPS
    n, _, r, c = x.shape
    x5 = x.reshape(n, S5_TILES, gt, r, c)
    eye = jnp.eye(gt, dtype=x.dtype)
    return (x5[:, :, :, :, None, :] * eye[None, None, :, None, :, None]).reshape(n, S5_TILES, gt * r, gt * c)


def _s5_tables(a_re, a_im, log_dt, b_re, b_im, c_re, c_im, d_skip, tc):
    ein = functools.partial(jnp.einsum, precision=HIGHEST)
    dt = jnp.exp(log_dt)[:, None]
    mag = jnp.exp(a_re * dt)
    abar_r, abar_i = mag * jnp.cos(a_im * dt), mag * jnp.sin(a_im * dt)
    den = a_re * a_re + a_im * a_im
    nr, ni = abar_r - 1.0, abar_i
    coef_r = (nr * a_re + ni * a_im) / den
    coef_i = (ni * a_re - nr * a_im) / den
    bbar_r = coef_r[..., None] * b_re - coef_i[..., None] * b_im
    bbar_i = coef_r[..., None] * b_im + coef_i[..., None] * b_re

    def abar_pow(j):
        jj = j[..., None, None]
        mag_j = jnp.where(jj >= 0, jnp.exp(jj * (a_re * dt)), 0.0)
        return mag_j * jnp.cos(jj * (a_im * dt)), mag_j * jnp.sin(jj * (a_im * dt))

    half = tc // 2
    steps = jnp.arange(tc, dtype=F32)
    at_r, at_i = abar_pow(jnp.full((), tc, F32))
    lags = (2.0 * jnp.arange(half, dtype=F32)[:, None, None]
            + jnp.array([[0.0, 1.0], [-1.0, 0.0]], F32)[None])
    lag_r, lag_i = abar_pow(lags.reshape(-1))
    ab_r = lag_r[..., None] * bbar_r - lag_i[..., None] * bbar_i
    ab_i = lag_r[..., None] * bbar_i + lag_i[..., None] * bbar_r
    kern = ein('ghp,jgpk->jgkh', c_re, ab_r) - ein('ghp,jgpk->jgkh', c_im, ab_i)
    bd_lag = _block_diag_tiles(kern.astype(BF16)).reshape(half, 2, 2, S5_TILES, LANES, LANES)
    toe = jnp.transpose(bd_lag, (3, 0, 1, 4, 2, 5)).reshape(S5_TILES, half, 2 * LANES, 2 * LANES)
    rev_r, rev_i = abar_pow(tc - 1.0 - steps)
    s_r = jnp.transpose(rev_r[..., None] * bbar_r - rev_i[..., None] * bbar_i, (0, 1, 3, 2))
    s_i = jnp.transpose(rev_r[..., None] * bbar_i + rev_i[..., None] * bbar_r, (0, 1, 3, 2))
    s_step = jnp.concatenate([_block_diag_tiles(s_r.astype(BF16)), _block_diag_tiles(s_i.astype(BF16))],
                             axis=3)
    s_tab = jnp.transpose(s_step.reshape(half, 2, S5_TILES, LANES, 2 * S5_TILE_STATE),
                          (2, 0, 1, 3, 4)).reshape(S5_TILES, half, 2 * LANES, 2 * S5_TILE_STATE)
    a1_r, a1_i = abar_pow(steps + 1.0)
    c_re_t, c_im_t = jnp.transpose(c_re, (0, 2, 1)), jnp.transpose(c_im, (0, 2, 1))
    p_r = c_re_t[None] * a1_r[..., None] - c_im_t[None] * a1_i[..., None]
    p_i = -c_re_t[None] * a1_i[..., None] - c_im_t[None] * a1_r[..., None]
    p_step = jnp.concatenate([_block_diag_tiles(p_r.astype(BF16)), _block_diag_tiles(p_i.astype(BF16))],
                             axis=2)
    p_tab = jnp.transpose(p_step.reshape(half, 2, S5_TILES, 2 * S5_TILE_STATE, LANES),
                          (2, 0, 3, 1, 4)).reshape(S5_TILES, half, 2 * S5_TILE_STATE, 2 * LANES)
    a_tab = jnp.stack([at_r.reshape(S5_TILES, S5_TILE_STATE), at_i.reshape(S5_TILES, S5_TILE_STATE)], axis=1)
    d_tab = d_skip.reshape(S5_TILES, 1, LANES)
    return toe.astype(BF16), s_tab.astype(BF16), p_tab.astype(BF16), a_tab, d_tab


def _s5_kernel(u_ref, t_ref, s_ref, p_ref, a_ref, d_ref, h0_ref, y_ref, f_ref, loc, xprev,
               *, tc, n_chunks, nb):
    r = n_chunks * nb
    half = tc // 2
    ns = S5_TILE_STATE
    step_rows = lambda t: pl.ds(t, r, stride=tc)
    v = [u_ref[step_rows(t), :] for t in range(tc)]
    vp = [jnp.concatenate([v[2 * a].astype(BF16), v[2 * a + 1].astype(BF16)], axis=1) for a in range(half)]
    acc = _dot(vp[0], s_ref[0])
    for a in range(1, half):
        acc = acc + _dot(vp[a], s_ref[a])
    nt = ns // LANES
    lane_tile = lambda k: slice(k * LANES, (k + 1) * LANES)
    for k in range(2 * nt):
        loc[k] = acc[:, lane_tile(k)]
    abar = a_ref[...]

    def body(c, carry):
        rows = pl.ds(c, nb, stride=n_chunks) if n_chunks > 1 else pl.ds(0, nb)
        new = []
        for k in range(nt):
            xr, xi = carry[k], carry[nt + k]
            xprev[k, rows, :] = xr
            xprev[nt + k, rows, :] = xi
            ar, ai = abar[0:1, lane_tile(k)], abar[1:2, lane_tile(k)]
            new.append((ar * xr - ai * xi + loc[k, rows, :], ar * xi + ai * xr + loc[nt + k, rows, :]))
        return tuple(n[0] for n in new) + tuple(n[1] for n in new)

    x_end = lax.fori_loop(0, n_chunks, body, tuple(h0_ref[:, lane_tile(k)] for k in range(2 * nt)))
    for k in range(2 * nt):
        f_ref[:, lane_tile(k)] = x_end[k]
    xp = jnp.concatenate([xprev[k] for k in range(2 * nt)], axis=1).astype(BF16)
    d = d_ref[...]
    for a2 in range(half):
        acc = _dot(xp, p_ref[a2])
        for a in range(a2 + 1):
            acc = acc + _dot(vp[a], t_ref[a2 - a])
        for k in range(2):
            t = 2 * a2 + k
            y_ref[step_rows(t), :] = acc[:, k * LANES:(k + 1) * LANES] + v[t] * d


def _s5(u, row_block_offset, tables, h0, *, tc, n_chunks, nb, n_splits):
    toe, s_tab, p_tab, a_tab, d_tab = tables
    rows = nb * n_chunks * tc
    off = row_block_offset // rows
    half = tc // 2
    ns2 = 2 * S5_TILE_STATE
    tile = lambda *tail: pl.BlockSpec((None,) + tail, lambda j, s: (j,) + (0,) * len(tail))
    state = pl.BlockSpec((None, None, nb, ns2), lambda j, s: (j, s, 0, 0))
    y, f = pl.pallas_call(
        functools.partial(_s5_kernel, tc=tc, n_chunks=n_chunks, nb=nb),
        grid=(S5_TILES, n_splits),
        in_specs=[pl.BlockSpec((rows, LANES), lambda j, s: (off + s, j)),
                  tile(half, 2 * LANES, 2 * LANES), tile(half, 2 * LANES, ns2), tile(half, ns2, 2 * LANES),
                  tile(2, S5_TILE_STATE), tile(1, LANES), state],
        out_specs=[pl.BlockSpec((rows, LANES), lambda j, s: (s, j)), state],
        out_shape=(jax.ShapeDtypeStruct((rows * n_splits, D_SSM), F32),
                   jax.ShapeDtypeStruct((S5_TILES, n_splits, nb, ns2), F32)),
        scratch_shapes=[pltpu.VMEM((ns2 // LANES, nb * n_chunks, LANES), F32)] * 2,
        compiler_params=_cparams("arbitrary", "arbitrary"),
        name=f"s5_c{n_chunks}",
    )(u, toe, s_tab, p_tab, a_tab, d_tab, h0.reshape(S5_TILES, n_splits, nb, ns2))
    return y, f.reshape(S5_TILES, n_splits * nb, ns2)


def _s5_state_to_tiles(s_re, s_im):
    b = s_re.shape[0]
    f = lambda s: s.reshape(b, S5_TILES, S5_TILE_STATE).transpose(1, 0, 2)
    return jnp.concatenate([f(s_re), f(s_im)], axis=2)


def _s5_state_from_tiles(f):
    b = f.shape[1]
    g = lambda s: s.transpose(1, 0, 2).reshape(1, b, N_GROUPS, SSM_STATE)
    return g(f[:, :, :S5_TILE_STATE]), g(f[:, :, S5_TILE_STATE:])


def _log_sigmoid(x):
    return jnp.minimum(x, 0.0) - jnp.log1p(jnp.exp(-jnp.abs(x)))


def _mlstm_gates(gates, bias_row, lc):
    g = gates + bias_row
    col = lax.broadcasted_iota(jnp.int32, (1, GATE_PAD), 1)
    gl = jnp.where(col >= ML_HEADS, _log_sigmoid(g), g)
    r = lax.broadcasted_iota(jnp.int32, (lc, lc), 0)
    c = lax.broadcasted_iota(jnp.int32, (lc, lc), 1)
    tril = (r >= c).astype(F32)
    bcols = _dot_hi(tril, gl)
    sel = (lax.broadcasted_iota(jnp.int32, (8, GATE_PAD), 0)
           == lax.broadcasted_iota(jnp.int32, (8, GATE_PAD), 1)).astype(F32)
    nt = (((1,), (1,)), ((), ()))
    grows = lax.dot_general(sel, gl, nt, precision=HIGHEST, preferred_element_type=F32)
    brows = lax.dot_general(sel, bcols, nt, precision=HIGHEST, preferred_element_type=F32)
    return gl, bcols, grows, brows, (r >= c)


def _mlstm_head(q, k, v, ic, bc, ir, br, causal, c_state, n_state, m_state, lc):
    dmat = jnp.where(causal, bc - br + ir, -jnp.inf)
    inter = bc + m_state
    mt = jnp.maximum(inter, jnp.max(dmat, axis=1, keepdims=True))
    w_in = jnp.exp(dmat - mt)
    w_out = jnp.exp(inter - mt)
    nt = (((1,), (1,)), ((), ()))
    s = lax.dot_general(q, k, nt, preferred_element_type=F32) * w_in
    num = w_out * _dot(q, c_state.astype(BF16)) + _dot(s.astype(BF16), v)
    qf = q.astype(F32)
    qn = w_out * jnp.sum(qf * n_state, axis=1, keepdims=True) + jnp.sum(s, axis=1, keepdims=True)
    h = num / jnp.maximum(jnp.abs(qn), jnp.exp(-mt))
    m_end = mt[lc - 1:lc, :]
    b_last = bc[lc - 1:lc, :]
    decay = jnp.exp(b_last + m_state - m_end)
    ws = jnp.exp(b_last - bc + ic - m_end)
    kw = k.astype(F32) * ws
    tn = (((0,), (0,)), ((), ()))
    c_new = decay * c_state + lax.dot_general(kw.astype(BF16), v, tn, preferred_element_type=F32)
    n_new = decay * n_state + jnp.sum(kw, axis=0, keepdims=True)
    return h, c_new, n_new, m_end


def _mlstm_chunk(qkv, o, gates, bias_row, get_state, put_state, lc):
    gl, bcols, grows, brows, causal = _mlstm_gates(gates, bias_row, lc)
    outs = []
    for hd in range(ML_HEADS):
        q = qkv[:, hd * ML_DK:(hd + 1) * ML_DK]
        k = qkv[:, D_ML + hd * ML_DK:D_ML + (hd + 1) * ML_DK]
        v = qkv[:, 2 * D_ML + hd * ML_DK:2 * D_ML + (hd + 1) * ML_DK]
        f = ML_HEADS + hd
        c_state, n_state, m_state = get_state(hd)
        h, c_new, n_new, m_new = _mlstm_head(
            q, k, v, gl[:, hd:hd + 1], bcols[:, f:f + 1], grows[hd:hd + 1, :], brows[f:f + 1, :],
            causal, c_state, n_state, m_state, lc)
        put_state(hd, c_new, n_new, m_new)
        outs.append(jax.nn.sigmoid(o[:, hd * ML_DK:(hd + 1) * ML_DK]) * h)
    return jnp.concatenate(outs, axis=1)


def _mlstm_prompt_kernel(qkv_ref, o_ref, gt_ref, bias_ref, h_ref, c_ref, n_ref, m_ref, *, lc):
    @pl.when(pl.program_id(1) == 0)
    def _():
        c_ref[...] = jnp.zeros_like(c_ref)
        n_ref[...] = jnp.zeros_like(n_ref)
        m_ref[...] = jnp.zeros_like(m_ref)

    def get_state(hd):
        return c_ref[hd], n_ref[hd], m_ref[hd][:, 0:1]

    def put_state(hd, c_new, n_new, m_new):
        c_ref[hd] = c_new
        n_ref[hd] = n_new
        m_ref[hd] = jnp.broadcast_to(m_new, (1, LANES))

    h = _mlstm_chunk(qkv_ref[...], o_ref[...], gt_ref[...], bias_ref[...], get_state, put_state, lc)
    h_ref[...] = h.astype(BF16)


def _mlstm_prompt(qkv, o, gates, bias_row, bsz, seq):
    lc = ML_CHUNK_PROMPT
    nc = seq // lc
    row = lambda width: pl.BlockSpec((lc, width), lambda b, c: (b * nc + c, 0))
    st = lambda *tail: pl.BlockSpec((None, ML_HEADS) + tail, lambda b, c: (b, 0) + (0,) * len(tail))
    return pl.pallas_call(
        functools.partial(_mlstm_prompt_kernel, lc=lc),
        grid=(bsz, nc),
        in_specs=[row(3 * D_ML), row(D_ML), row(GATE_PAD), pl.BlockSpec((1, GATE_PAD), lambda b, c: (0, 0))],
        out_specs=[row(D_ML), st(ML_DK, ML_DK), st(1, ML_DK), st(1, LANES)],
        out_shape=(jax.ShapeDtypeStruct((bsz * seq, D_ML), BF16),
                   jax.ShapeDtypeStruct((bsz, ML_HEADS, ML_DK, ML_DK), F32),
                   jax.ShapeDtypeStruct((bsz, ML_HEADS, 1, ML_DK), F32),
                   jax.ShapeDtypeStruct((bsz, ML_HEADS, 1, LANES), F32)),
        compiler_params=_cparams("arbitrary", "arbitrary"),
        name="mlstm_prompt",
    )(qkv, o, gates, bias_row)


def _mlstm_sample_kernel(qkv_ref, o_ref, gt_ref, bias_ref, c0_ref, n0_ref, m0_ref,
                         h_ref, c_ref, n_ref, m_ref, qkv_s, *, lc, nb):
    qkv_s[...] = qkv_ref[...].astype(F32)

    def body(b, carry):
        rows = pl.ds(pl.multiple_of(b * lc, lc), lc)

        def get_state(hd):
            return c0_ref[b, hd], n0_ref[b, hd], m0_ref[b, hd][:, 0:1]

        def put_state(hd, c_new, n_new, m_new):
            c_ref[b, hd] = c_new
            n_ref[b, hd] = n_new
            m_ref[b, hd] = jnp.broadcast_to(m_new, (1, LANES))

        h = _mlstm_chunk(qkv_s[rows, :].astype(BF16), o_ref[rows, :], gt_ref[rows, :], bias_ref[...],
                         get_state, put_state, lc)
        h_ref[rows, :] = h
        return carry

    lax.fori_loop(0, nb, body, 0, unroll=True)


def _mlstm_sample(qkv, o, gates, bias_row, c0, n0, m0, row_block_offset, bsz, seq):
    nb = 8
    lc = seq
    rows = nb * lc
    off = row_block_offset // rows
    row = lambda width: pl.BlockSpec((rows, width), lambda i: (off + i, 0))
    st = lambda *tail: pl.BlockSpec((nb, ML_HEADS) + tail, lambda i: (i, 0) + (0,) * len(tail))
    return pl.pallas_call(
        functools.partial(_mlstm_sample_kernel, lc=lc, nb=nb),
        grid=(bsz // nb,),
        in_specs=[row(3 * D_ML), row(D_ML), row(GATE_PAD), pl.BlockSpec((1, GATE_PAD), lambda i: (0, 0)),
                  st(ML_DK, ML_DK), st(1, ML_DK), st(1, LANES)],
        out_specs=[pl.BlockSpec((rows, D_ML), lambda i: (i, 0)), st(ML_DK, ML_DK), st(1, ML_DK), st(1, LANES)],
        out_shape=(jax.ShapeDtypeStruct((bsz * seq, D_ML), F32),
                   jax.ShapeDtypeStruct((bsz, ML_HEADS, ML_DK, ML_DK), F32),
                   jax.ShapeDtypeStruct((bsz, ML_HEADS, 1, ML_DK), F32),
                   jax.ShapeDtypeStruct((bsz, ML_HEADS, 1, LANES), F32)),
        scratch_shapes=[pltpu.VMEM((rows, 3 * D_ML), F32)],
        compiler_params=_cparams("arbitrary"),
        name="mlstm_sample",
    )(qkv, o, gates, bias_row, c0, n0, m0)


def _merge_kernel(xp_ref, xs_ref, yp_ref, ys_ref, mp_ref, ms_ref, gsm_ref, wglu_ref, bglu_ref, wsu_ref, wmu_ref,
                  wout_ref, gffn_ref, wrh_ref, wrl_ref, br_ref, h1_ref, xn_ref, lg_ref, *, n_prompt_blocks):
    i = pl.program_id(0)
    is_prompt = i < n_prompt_blocks
    x = jnp.where(is_prompt, xp_ref[...], xs_ref[...])
    y = jax.nn.gelu(jnp.where(is_prompt, yp_ref[...], ys_ref[...]))
    ym = jnp.where(is_prompt, mp_ref[...], ms_ref[...].astype(BF16))
    ys = y * jax.nn.sigmoid(_dot(y.astype(BF16), wglu_ref[...]) + bglu_ref[...])
    gsm = gsm_ref[...]
    merged = (jax.nn.sigmoid(gsm[:, :D_MODEL]) * _dot(ys.astype(BF16), wsu_ref[...])
              + jax.nn.sigmoid(gsm[:, D_MODEL:]) * _dot(ym, wmu_ref[...]))
    h1 = x + _dot(merged.astype(BF16), wout_ref[...])
    h1_ref[...] = h1
    xn = _rms(h1, gffn_ref[...])
    xn_ref[...] = xn
    xn_hi = xn.astype(BF16)
    xn_lo = (xn - xn_hi.astype(F32)).astype(BF16)
    lg_ref[...] = (_dot(xn_hi, wrh_ref[...]) + _dot(xn_lo, wrh_ref[...]) + _dot(xn_hi, wrl_ref[...])
                   + br_ref[...])


def _merge(xp, xs, y_pre, ym, gsm, w_glu, b_glu, w_ssm_up, w_ml_up, w_out, g_ffn, w_router, b_router):
    tm = TM_TOKENS
    t = gsm.shape[0]
    npb = xp.shape[0] // tm
    xp_spec, xs_spec = _two_source_specs(tm, D_MODEL, npb)
    yp_spec, ys_spec = _two_source_specs(tm, D_SSM, npb)
    mp_spec, ms_spec = _two_source_specs(tm, D_ML, npb)
    row = lambda width: pl.BlockSpec((tm, width), lambda i: (i, 0))
    wr = jnp.pad(w_router, ((0, 0), (0, LANES - N_EXPERTS)))
    wr_hi = wr.astype(BF16)
    wr_lo = (wr - wr_hi.astype(F32)).astype(BF16)
    br =jnp.pad(b_router, (0, LANES - N_EXPERTS)).reshape(1, LANES)
    return pl.pallas_call(
        functools.partial(_merge_kernel, n_prompt_blocks=npb),
        grid=(t // tm,),
        in_specs=[xp_spec, xs_spec, yp_spec, ys_spec, mp_spec, ms_spec, row(2 * D_MODEL), _full((D_SSM, D_SSM)),
                  _full((1, D_SSM)), _full((D_SSM, D_MODEL)), _full((D_ML, D_MODEL)),
                  _full((D_MODEL, D_MODEL)), _full((1, D_MODEL)), _full((D_MODEL, LANES)),
                  _full((D_MODEL, LANES)), _full((1, LANES))],
        out_specs=[row(D_MODEL), row(D_MODEL), row(LANES)],
        out_shape=(jax.ShapeDtypeStruct((t, D_MODEL), F32), jax.ShapeDtypeStruct((t, D_MODEL), F32),
                   jax.ShapeDtypeStruct((t, LANES), F32)),
        compiler_params=_cparams("arbitrary"),
        name="merge",
    )(xp, xs, *y_pre, *ym, gsm, w_glu.astype(BF16), b_glu.reshape(1, D_SSM), w_ssm_up.astype(BF16),
      w_ml_up.astype(BF16), w_out.astype(BF16), g_ffn.reshape(1, D_MODEL), wr_hi, wr_lo, br)


def _route(logits, tm):
    t = logits.shape[0]
    top_val, top_idx = lax.top_k(logits, TOP_K)
    top_w = jax.nn.softmax(top_val, axis=-1)
    flat_e = top_idx.reshape(-1)
    n_assign = t * TOP_K
    onehot = (flat_e[:, None] == jnp.arange(N_EXPERTS, dtype=jnp.int32)[None, :])
    rb = RANK_BLOCK
    oh3 = onehot.astype(F32).reshape(n_assign // rb, rb, N_EXPERTS)
    tril = jnp.tril(jnp.ones((rb, rb), F32))
    within = jnp.einsum('ij,bjk->bik', tril, oh3).astype(jnp.int32)
    totals = within[:, -1, :]
    before = jnp.cumsum(totals, axis=0) - totals
    csum = (within + before[:, None, :]).reshape(n_assign, N_EXPERTS)
    rank = jnp.sum(jnp.where(onehot, csum, 0), axis=1) - 1
    counts = before[-1] + totals[-1]
    padded = (counts + tm - 1) // tm * tm
    pad_end = jnp.cumsum(padded)
    pad_start = pad_end - padded
    pos = jnp.sum(jnp.where(onehot, pad_start[None, :], 0), axis=1) + rank
    n_blocks = -(-(n_assign + N_EXPERTS * (tm - 1)) // tm)
    block_row0 = jnp.arange(n_blocks, dtype=jnp.int32) * tm
    block_e = jnp.minimum(jnp.sum((pad_end[None, :] <= block_row0[:, None]).astype(jnp.int32), axis=1),
                          N_EXPERTS - 1)
    n_valid = (pad_end[-1] // tm).astype(jnp.int32).reshape(1)
    ids = jnp.arange(N_EXPERTS, dtype=jnp.int32)
    later = (ids[None, :] > ids[:, None]) & (counts[None, :] > 0)
    next_e = jnp.min(jnp.where(later, ids[None, :], N_EXPERTS), axis=1)
    next_e = jnp.where(next_e == N_EXPERTS, -1, next_e)[block_e]
    fill_lo = pad_start + counts
    return top_w, pos.reshape(t, TOP_K), fill_lo, pad_end, n_blocks * tm, block_e, next_e, n_valid


def _dispatch_copy(x_ref, rows_hbm, sem, r, dst):
    return pltpu.make_async_copy(x_ref.at[pl.ds(r, 1), :], rows_hbm.at[pl.ds(dst, 1), :], sem)


def _dispatch_kernel(lo_ref, hi_ref, pos_ref, x_ref, rows_hbm, zero_s, sem, zero_sem):
    tm = x_ref.shape[0]

    @pl.when(pl.program_id(0) == 0)
    def _():
        zero_s[...] = jnp.zeros_like(zero_s)

        def tail(do):
            def body(b, c):
                rows = pl.ds(pl.multiple_of(b * tm, tm), tm)
                do(pltpu.make_async_copy(zero_s, rows_hbm.at[rows, :], zero_sem))
                return c
            lax.fori_loop(hi_ref[N_EXPERTS - 1] // tm, rows_hbm.shape[0] // tm, body, 0)

        tail(lambda c: c.start())
        tail(lambda c: c.wait())

        def fill(e, carry):
            def each(do):
                def body(r, c):
                    do(_dispatch_copy(zero_s, rows_hbm, zero_sem, 0, r))
                    return c
                lax.fori_loop(lo_ref[e], hi_ref[e], body, 0)
            each(lambda c: c.start())
            each(lambda c: c.wait())
            return carry

        lax.fori_loop(0, N_EXPERTS, fill, 0)

    def each(do):
        for r in range(tm):
            for k in range(TOP_K):
                do(_dispatch_copy(x_ref, rows_hbm, sem, r, pos_ref[r * TOP_K + k]), k)

    each(lambda c, k: c.start(priority=k % 2))
    each(lambda c, k: c.wait())


def _dispatch(xn, pos, fill_lo, pad_end, n_rows):
    tm = TM_TOKENS
    t = xn.shape[0]
    grid_spec = pltpu.PrefetchScalarGridSpec(
        num_scalar_prefetch=2,
        grid=(t // tm,),
        in_specs=[pl.BlockSpec((tm * TOP_K,), lambda i, lo, hi: (i,), memory_space=pltpu.SMEM),
                  pl.BlockSpec((tm, D_MODEL), lambda i, lo, hi: (i, 0))],
        out_specs=pl.BlockSpec(memory_space=pl.ANY),
        scratch_shapes=[pltpu.VMEM((tm, D_MODEL), F32), pltpu.SemaphoreType.DMA(()), pltpu.SemaphoreType.DMA(())],
    )
    return pl.pallas_call(
        _dispatch_kernel,
        grid_spec=grid_spec,
        out_shape=jax.ShapeDtypeStruct((n_rows, D_MODEL), F32),
        compiler_params=_cparams("arbitrary"),
        name="moe_dispatch",
    )(fill_lo, pad_end, pos.reshape(-1), xn)


def _moe_weight_copies(e, w_hbm, wbuf, sems):
    return [pltpu.make_async_copy(w.at[e], wbuf.at[k], sems.at[k]) for k, w in enumerate(w_hbm)]


def _moe_kernel(be_ref, ne_ref, nv_ref, x_ref, wg_hbm, bg_ref, wu_hbm, bu_ref, wd_hbm, bd_ref, y_ref,
                w_bf, wbuf, sems):
    i = pl.program_id(0)
    e = be_ref[i]
    prev = be_ref[jnp.maximum(i - 1, 0)]
    valid = i < nv_ref[0]
    first = jnp.logical_or(i == 0, e != prev)
    copies = functools.partial(_moe_weight_copies, w_hbm=(wg_hbm, wu_hbm, wd_hbm), wbuf=wbuf, sems=sems)

    @pl.when(i == 0)
    def _():
        for c in copies(e):
            c.start()

    @pl.when(jnp.logical_and(valid, first))
    def _():
        nxt = ne_ref[i]
        for k, c in enumerate(copies(e)):
            c.wait()
            w_bf[k] = wbuf[k].astype(BF16)

        @pl.when(nxt >= 0)
        def _():
            for c in copies(nxt):
                c.start()

    @pl.when(valid)
    def _():
        x = x_ref[...].astype(BF16)
        g =jnp.minimum(_dot(x, w_bf[0]) + bg_ref[...], SWIGLU_LIMIT)
        u = jnp.clip(_dot(x, w_bf[1]) + bu_ref[...], -SWIGLU_LIMIT, SWIGLU_LIMIT)
        a = g * jax.nn.sigmoid(SWIGLU_ALPHA * g) * (u + 1.0)
        y_ref[...] = _dot(a.astype(BF16), w_bf[2]) + bd_ref[...]

    @pl.when(jnp.logical_not(valid))
    def _():
        y_ref[...] = jnp.zeros_like(y_ref)


def _moe_experts(x_rows, block_e, next_e, n_valid, w_gate, b_gate, w_up, b_up, w_down, b_down):
    tm = TM_MOE
    n_rows = x_rows.shape[0]
    wspec = pl.BlockSpec(memory_space=pl.ANY)
    bspec = pl.BlockSpec((None, 1, D_MODEL), lambda i, be, ne, nv: (be[i], 0, 0))
    rows = pl.BlockSpec((tm, D_MODEL), lambda i, be, ne, nv: (i, 0))
    grid_spec = pltpu.PrefetchScalarGridSpec(
        num_scalar_prefetch=3,
        grid=(n_rows // tm,),
        in_specs=[rows, wspec, bspec, wspec, bspec, wspec, bspec],
        out_specs=rows,
        scratch_shapes=[pltpu.VMEM((3, D_MODEL, D_MODEL), BF16), pltpu.VMEM((3, D_MODEL, D_MODEL), F32),
                        pltpu.SemaphoreType.DMA((3,))],
    )
    b3 = lambda b: b.reshape(N_EXPERTS, 1, D_MODEL)
    return pl.pallas_call(
        _moe_kernel,
        grid_spec=grid_spec,
        out_shape=jax.ShapeDtypeStruct((n_rows, D_MODEL), F32),
        compiler_params=_cparams("arbitrary"),
        name="moe_experts",
    )(block_e, next_e, n_valid, x_rows, w_gate, b3(b_gate), w_up, b3(b_up), w_down, b3(b_down))


def _ple_kernel(h1_ref, rows_ref, tw_ref, p_ref, gple_ref, wg_ref, wp_ref, gfin_ref, y_ref):
    tw = tw_ref[...]
    h2 = h1_ref[...]
    for k in range(TOP_K):
        h2 = h2 + rows_ref[k] * tw[:, k:k + 1]
    gate =jax.nn.sigmoid(_dot(_rms(h2, gple_ref[...]).astype(BF16), wg_ref[...]))
    h3 = h2 + gate * _dot(p_ref[...].astype(BF16), wp_ref[...])
    y_ref[...] = _rms(h3, gfin_ref[...])


def _ple_final(h1, expert_rows, top_w, p, row_block_offset, g_ple, w_ple_gate, w_ple_proj, g_final):
    tm = TM_TOKENS
    n = p.shape[0]
    off = row_block_offset // tm
    src = lambda width: pl.BlockSpec((tm, width), lambda i: (off + i, 0))
    loc = lambda width: pl.BlockSpec((tm, width), lambda i: (i, 0))
    return pl.pallas_call(
        _ple_kernel,
        grid=(n // tm,),
        in_specs=[src(D_MODEL), pl.BlockSpec((TOP_K, tm, D_MODEL), lambda i: (0, off + i, 0)), src(LANES),
                  loc(D_PLE), _full((1, D_MODEL)), _full((D_MODEL, D_MODEL)), _full((D_PLE, D_MODEL)),
                  _full((1, D_MODEL))],
        out_specs=loc(D_MODEL),
        out_shape=jax.ShapeDtypeStruct((n, D_MODEL), F32),
        compiler_params=_cparams("arbitrary"),
        name="ple_final",
    )(h1, expert_rows, top_w, p, g_ple.reshape(1, D_MODEL), w_ple_gate.astype(BF16),
      w_ple_proj.astype(BF16), g_final.reshape(1, D_MODEL))


def kernel(x_prompt, x_sample, p_prompt, p_sample, state_ssm_re, state_ssm_im, state_ml_c, state_ml_n, state_ml_m, g_mix, w_in, ssm_a_re, ssm_a_im, ssm_log_dt, ssm_b_re, ssm_b_im, ssm_c_re, ssm_c_im, ssm_d, ssm_w_glu, ssm_b_glu, ml_b_ig, ml_b_fg, w_ssm_up, w_ml_up, w_out, g_ffn, w_router, b_router, w_gate, b_gate, w_up, b_up, w_down, b_down, g_ple, w_ple_gate, w_ple_proj, g_final):
    assert g_mix.shape[0] == 1, "single-layer trunk"
    bp, lp, _ = x_prompt.shape
    bs, ls, _ = x_sample.shape
    tp, ts = bp * lp, bs * ls
    t = tp + ts
    xp = x_prompt.reshape(tp, D_MODEL)
    xs = x_sample.reshape(ts, D_MODEL)

    u, qkv, o, gates, gsm = _inproj(xp, xs, g_mix[0], w_in[0])

    s5_args = (ssm_a_re[0], ssm_a_im[0], ssm_log_dt[0], ssm_b_re[0], ssm_b_im[0], ssm_c_re[0],
               ssm_c_im[0], ssm_d[0])
    zero_state = jnp.zeros((S5_TILES, bp, 2 * S5_TILE_STATE), F32)
    tables = {tc: _s5_tables(*s5_args, tc) for tc in {S5_CHUNK, ls}}
    y_p, f_p = _s5(u, 0, tables[S5_CHUNK], zero_state,
                   tc=S5_CHUNK, n_chunks=lp // S5_CHUNK, nb=bp // 2, n_splits=2)
    y_s, f_s = _s5(u, tp, tables[ls], _s5_state_to_tiles(state_ssm_re[0], state_ssm_im[0]),
                   tc=ls, n_chunks=1, nb=bs, n_splits=1)
    re_p, im_p = _s5_state_from_tiles(f_p)
    re_s, im_s = _s5_state_from_tiles(f_s)

    bias_row = jnp.pad(jnp.concatenate([ml_b_ig[0], ml_b_fg[0]]), (0, GATE_PAD - 2 * ML_HEADS)).reshape(1, GATE_PAD)
    hm_p, c_p, n_p, m_p = _mlstm_prompt(qkv, o, gates, bias_row, bp, lp)
    m0 = jnp.broadcast_to(state_ml_m[0][:, :, None, None], (bs, ML_HEADS, 1, LANES))
    hm_s, c_s, n_s, m_s = _mlstm_sample(qkv, o, gates, bias_row, state_ml_c[0],
                                        state_ml_n[0].reshape(bs, ML_HEADS, 1, ML_DK), m0, tp, bs, ls)

    h1, xn, logits = _merge(xp, xs, (y_p, y_s), (hm_p, hm_s), gsm, ssm_w_glu[0], ssm_b_glu[0], w_ssm_up[0], w_ml_up[0],
                            w_out[0], g_ffn[0], w_router[0], b_router[0])

    top_w, pos, fill_lo, pad_end, n_rows, block_e, next_e, n_valid = _route(logits[:, :N_EXPERTS], TM_MOE)
    x_rows = _dispatch(xn, pos, fill_lo, pad_end, n_rows)
    expert_w = lambda w: w.reshape(N_EXPERTS, D_MODEL, D_MODEL)
    y_rows = _moe_experts(x_rows, block_e, next_e, n_valid, expert_w(w_gate), b_gate[0], expert_w(w_up),
                          b_up[0], expert_w(w_down), b_down[0])
    expert_rows = y_rows[pos.T.reshape(-1)].reshape(TOP_K, t, D_MODEL)
    top_w_pad = jnp.pad(top_w, ((0, 0), (0, LANES - TOP_K)))

    ple_w = (g_ple[0], w_ple_gate[0], w_ple_proj[0], g_final)
    y_prompt = _ple_final(h1, expert_rows, top_w_pad, p_prompt[0].reshape(tp, D_PLE), 0, *ple_w)
    y_sample = _ple_final(h1, expert_rows, top_w_pad, p_sample[0].reshape(ts, D_PLE), tp, *ple_w)

    return (y_prompt.reshape(bp, lp, D_MODEL), y_sample.reshape(bs, ls, D_MODEL),
            re_p, im_p, c_p[None], n_p.reshape(1, bp, ML_HEADS, ML_DK), m_p[:, :, 0, 0][None],
            re_s, im_s, c_s[None], n_s.reshape(1, bs, ML_HEADS, ML_DK), m_s[:, :, 0, 0][None])
```

```python
import functools

import jax
import jax.numpy as jnp
from jax import lax
from jax.experimental import pallas as pl
from jax.experimental.pallas import tpu as pltpu

F32 = jnp.float32
BF16 = jnp.bfloat16
HIGHEST = lax.Precision.HIGHEST

D_MODEL = 1024
D_SSM = 512
SSM_GROUP = 16
N_GROUPS = 32
SSM_STATE = 64
ML_HEADS = 4
ML_DK = 128
D_ML = 512
N_EXPERTS = 32
TOP_K = 4
SWIGLU_LIMIT = 7.0
SWIGLU_ALPHA = 1.702
D_PLE = 256
RMS_EPS = 1e-6

LANES = 128
GATE_PAD = LANES
S5_CHUNK = 8
ML_CHUNK_PROMPT = 256
TM_TOKENS = 256
TM_MOE = 256
RANK_BLOCK = 256
VMEM_LIMIT = 56 * 1024 * 1024


def _cparams(*sem):
    return pltpu.CompilerParams(dimension_semantics=sem, vmem_limit_bytes=VMEM_LIMIT)


def _rms(x, g):
    return x * lax.rsqrt(jnp.mean(x * x, axis=-1, keepdims=True) + RMS_EPS) * g


def _dot(a, b):
    return jnp.dot(a, b, preferred_element_type=F32)


def _dot_hi(a, b):
    return jnp.dot(a, b, preferred_element_type=F32, precision=HIGHEST)


def _full(shape):
    n = len(shape)
    return pl.BlockSpec(shape, lambda *_: (0,) * n)


def _inproj_kernel(xp_ref, xs_ref, g_ref, wu_ref, wqkv_ref, wo_ref, wgt_ref, wgsm_ref,
                   u_ref, qkv_ref, o_ref, gt_ref, gsm_ref, *, n_prompt_blocks):
    i = pl.program_id(0)
    x = jnp.where(i < n_prompt_blocks, xp_ref[...], xs_ref[...])
    hn = _rms(x, g_ref[...]).astype(BF16)
    u_ref[...] = _dot(hn, wu_ref[...])
    qkv = _dot(hn, wqkv_ref[...])
    col = lax.broadcasted_iota(jnp.int32, (1, 3 * D_ML), 1)
    k_scale = jnp.where((col >= D_ML) & (col < 2 * D_ML), ML_DK ** -0.5, 1.0).astype(F32)
    qkv_ref[...] = (qkv * k_scale).astype(BF16)
    o_ref[...] = _dot(hn, wo_ref[...])
    gt_ref[...] = _dot(hn, wgt_ref[...])
    gsm_ref[...] = _dot(hn, wgsm_ref[...])


def _two_source_specs(tm, width, n_prompt_blocks):
    last = n_prompt_blocks - 1
    return (pl.BlockSpec((tm, width), lambda i: (jnp.minimum(i, last), 0)),
            pl.BlockSpec((tm, width), lambda i: (jnp.maximum(i - n_prompt_blocks, 0), 0)))


def _inproj(xp, xs, g_mix, w_in):
    tm = TM_TOKENS
    tp, ts = xp.shape[0], xs.shape[0]
    t = tp + ts
    npb = tp // tm
    w = w_in.astype(BF16)
    o0 = D_SSM
    wu = w[:, :o0]
    wqkv = w[:, o0:o0 + 3 * D_ML]
    wo = w[:, o0 + 3 * D_ML:o0 + 4 * D_ML]
    g0 = o0 + 4 * D_ML
    wgt = jnp.pad(w[:, g0:g0 + 2 * ML_HEADS], ((0, 0), (0, GATE_PAD - 2 * ML_HEADS)))
    wgsm = w[:, g0 + 2 * ML_HEADS:]
    xp_spec, xs_spec = _two_source_specs(tm, D_MODEL, npb)
    outs = (jax.ShapeDtypeStruct((t, D_SSM), F32), jax.ShapeDtypeStruct((t, 3 * D_ML), BF16),
            jax.ShapeDtypeStruct((t, D_ML), F32), jax.ShapeDtypeStruct((t, GATE_PAD), F32),
            jax.ShapeDtypeStruct((t, 2 * D_MODEL), F32))
    row = lambda width: pl.BlockSpec((tm, width), lambda i: (i, 0))
    return pl.pallas_call(
        functools.partial(_inproj_kernel, n_prompt_blocks=npb),
        grid=(t // tm,),
        in_specs=[xp_spec, xs_spec, _full((1, D_MODEL)), _full(wu.shape), _full(wqkv.shape),
                  _full(wo.shape), _full(wgt.shape), _full(wgsm.shape)],
        out_specs=[row(D_SSM), row(3 * D_ML), row(D_ML), row(GATE_PAD), row(2 * D_MODEL)],
        out_shape=outs,
        compiler_params=_cparams("arbitrary"),
        name="inproj",
    )(xp, xs, g_mix.reshape(1, D_MODEL), wu, wqkv, wo, wgt, wgsm)


S5_TILES = D_SSM // LANES
S5_TILE_GROUPS = LANES // SSM_GROUP
S5_TILE_STATE = S5_TILE_GROUPS * SSM_STATE


def _block_diag_tiles(x):
    gt = S5_TILE_GROUPS
    n, _, r, c = x.shape
    x5 = x.reshape(n, S5_TILES, gt, r, c)
    eye = jnp.eye(gt, dtype=x.dtype)
    return (x5[:, :, :, :, None, :] * eye[None, None, :, None, :, None]).reshape(n, S5_TILES, gt * r, gt * c)


def _s5_tables(a_re, a_im, log_dt, b_re, b_im, c_re, c_im, d_skip, tc):
    ein = functools.partial(jnp.einsum, precision=HIGHEST)
    dt = jnp.exp(log_dt)[:, None]
    mag = jnp.exp(a_re * dt)
    abar_r, abar_i = mag * jnp.cos(a_im * dt), mag * jnp.sin(a_im * dt)
    den = a_re * a_re + a_im * a_im
    nr, ni = abar_r - 1.0, abar_i
    coef_r = (nr * a_re + ni * a_im) / den
    coef_i = (ni * a_re - nr * a_im) / den
    bbar_r = coef_r[..., None] * b_re - coef_i[..., None] * b_im
    bbar_i = coef_r[..., None] * b_im + coef_i[..., None] * b_re

    def abar_pow(j):
        jj = j[..., None, None]
        mag_j = jnp.where(jj >= 0, jnp.exp(jj * (a_re * dt)), 0.0)
        return mag_j * jnp.cos(jj * (a_im * dt)), mag_j * jnp.sin(jj * (a_im * dt))

    half = tc // 2
    steps = jnp.arange(tc, dtype=F32)
    at_r, at_i = abar_pow(jnp.full((), tc, F32))
    lags = (2.0 * jnp.arange(half, dtype=F32)[:, None, None]
            + jnp.array([[0.0, 1.0], [-1.0, 0.0]], F32)[None])
    lag_r, lag_i = abar_pow(lags.reshape(-1))
    ab_r = lag_r[..., None] * bbar_r - lag_i[..., None] * bbar_i
    ab_i = lag_r[..., None] * bbar_i + lag_i[..., None] * bbar_r
    kern = ein('ghp,jgpk->jgkh', c_re, ab_r) - ein('ghp,jgpk->jgkh', c_im, ab_i)
    bd_lag = _block_diag_tiles(kern.astype(BF16)).reshape(half, 2, 2, S5_TILES, LANES, LANES)
    toe = jnp.transpose(bd_lag, (3, 0, 1, 4, 2, 5)).reshape(S5_TILES, half, 2 * LANES, 2 * LANES)
    rev_r, rev_i = abar_pow(tc - 1.0 - steps)
    s_r = jnp.transpose(rev_r[..., None] * bbar_r - rev_i[..., None] * bbar_i, (0, 1, 3, 2))
    s_i = jnp.transpose(rev_r[..., None] * bbar_i + rev_i[..., None] * bbar_r, (0, 1, 3, 2))
    s_step = jnp.concatenate([_block_diag_tiles(s_r.astype(BF16)), _block_diag_tiles(s_i.astype(BF16))],
                             axis=3)
    s_tab = jnp.transpose(s_step.reshape(half, 2, S5_TILES, LANES, 2 * S5_TILE_STATE),
                          (2, 0, 1, 3, 4)).reshape(S5_TILES, half, 2 * LANES, 2 * S5_TILE_STATE)
    a1_r, a1_i = abar_pow(steps + 1.0)
    c_re_t, c_im_t = jnp.transpose(c_re, (0, 2, 1)), jnp.transpose(c_im, (0, 2, 1))
    p_r = c_re_t[None] * a1_r[..., None] - c_im_t[None] * a1_i[..., None]
    p_i = -c_re_t[None] * a1_i[..., None] - c_im_t[None] * a1_r[..., None]
    p_step = jnp.concatenate([_block_diag_tiles(p_r.astype(BF16)), _block_diag_tiles(p_i.astype(BF16))],
                             axis=2)
    p_tab = jnp.transpose(p_step.reshape(half, 2, S5_TILES, 2 * S5_TILE_STATE, LANES),
                          (2, 0, 3, 1, 4)).reshape(S5_TILES, half, 2 * S5_TILE_STATE, 2 * LANES)
    a_tab = jnp.stack([at_r.reshape(S5_TILES, S5_TILE_STATE), at_i.reshape(S5_TILES, S5_TILE_STATE)], axis=1)
    d_tab = d_skip.reshape(S5_TILES, 1, LANES)
    return toe.astype(BF16), s_tab.astype(BF16), p_tab.astype(BF16), a_tab, d_tab


def _s5_kernel(u_ref, t_ref, s_ref, p_ref, a_ref, d_ref, h0_ref, y_ref, f_ref, loc, xprev,
               *, tc, n_chunks, nb):
    r = n_chunks * nb
    half = tc // 2
    ns = S5_TILE_STATE
    step_rows = lambda t: pl.ds(t, r, stride=tc)
    v = [u_ref[step_rows(t), :] for t in range(tc)]
    vp = [jnp.concatenate([v[2 * a].astype(BF16), v[2 * a + 1].astype(BF16)], axis=1) for a in range(half)]
    acc = _dot(vp[0], s_ref[0])
    for a in range(1, half):
        acc = acc + _dot(vp[a], s_ref[a])
    nt = ns // LANES
    lane_tile = lambda k: slice(k * LANES, (k + 1) * LANES)
    for k in range(2 * nt):
        loc[k] = acc[:, lane_tile(k)]
    abar = a_ref[...]

    def body(c, carry):
        rows = pl.ds(c, nb, stride=n_chunks) if n_chunks > 1 else pl.ds(0, nb)
        new = []
        for k in range(nt):
            xr, xi = carry[k], carry[nt + k]
            xprev[k, rows, :] = xr
            xprev[nt + k, rows, :] = xi
            ar, ai = abar[0:1, lane_tile(k)], abar[1:2, lane_tile(k)]
            new.append((ar * xr - ai * xi + loc[k, rows, :], ar * xi + ai * xr + loc[nt + k, rows, :]))
        return tuple(n[0] for n in new) + tuple(n[1] for n in new)

    x_end = lax.fori_loop(0, n_chunks, body, tuple(h0_ref[:, lane_tile(k)] for k in range(2 * nt)))
    for k in range(2 * nt):
        f_ref[:, lane_tile(k)] = x_end[k]
    xp = jnp.concatenate([xprev[k] for k in range(2 * nt)], axis=1).astype(BF16)
    d = d_ref[...]
    for a2 in range(half):
        acc = _dot(xp, p_ref[a2])
        for a in range(a2 + 1):
            acc = acc + _dot(vp[a], t_ref[a2 - a])
        for k in range(2):
            t = 2 * a2 + k
            y_ref[step_rows(t), :] = acc[:, k * LANES:(k + 1) * LANES] + v[t] * d


def _s5(u, row_block_offset, tables, h0, *, tc, n_chunks, nb, n_splits):
    toe, s_tab, p_tab, a_tab, d_tab = tables
    rows = nb * n_chunks * tc
    off = row_block_offset // rows
    half = tc // 2
    ns2 = 2 * S5_TILE_STATE
    tile = lambda *tail: pl.BlockSpec((None,) + tail, lambda j, s: (j,) + (0,) * len(tail))
    state = pl.BlockSpec((None, None, nb, ns2), lambda j, s: (j, s, 0, 0))
    y, f = pl.pallas_call(
        functools.partial(_s5_kernel, tc=tc, n_chunks=n_chunks, nb=nb),
        grid=(S5_TILES, n_splits),
        in_specs=[pl.BlockSpec((rows, LANES), lambda j, s: (off + s, j)),
                  tile(half, 2 * LANES, 2 * LANES), tile(half, 2 * LANES, ns2), tile(half, ns2, 2 * LANES),
                  tile(2, S5_TILE_STATE), tile(1, LANES), state],
        out_specs=[pl.BlockSpec((rows, LANES), lambda j, s: (s, j)), state],
        out_shape=(jax.ShapeDtypeStruct((rows * n_splits, D_SSM), F32),
                   jax.ShapeDtypeStruct((S5_TILES, n_splits, nb, ns2), F32)),
        scratch_shapes=[pltpu.VMEM((ns2 // LANES, nb * n_chunks, LANES), F32)] * 2,
        compiler_params=_cparams("arbitrary", "arbitrary"),
        name=f"s5_c{n_chunks}",
    )(u, toe, s_tab, p_tab, a_tab, d_tab, h0.reshape(S5_TILES, n_splits, nb, ns2))
    return y, f.reshape(S5_TILES, n_splits * nb, ns2)


def _s5_state_to_tiles(s_re, s_im):
    b = s_re.shape[0]
    f = lambda s: s.reshape(b, S5_TILES, S5_TILE_STATE).transpose(1, 0, 2)
    return jnp.concatenate([f(s_re), f(s_im)], axis=2)


def _s5_state_from_tiles(f):
    b = f.shape[1]
    g = lambda s: s.transpose(1, 0, 2).reshape(1, b, N_GROUPS, SSM_STATE)
    return g(f[:, :, :S5_TILE_STATE]), g(f[:, :, S5_TILE_STATE:])


def _log_sigmoid(x):
    return jnp.minimum(x, 0.0) - jnp.log1p(jnp.exp(-jnp.abs(x)))


def _mlstm_gates(gates, bias_row, lc):
    g = gates + bias_row
    col = lax.broadcasted_iota(jnp.int32, (1, GATE_PAD), 1)
    gl = jnp.where(col >= ML_HEADS, _log_sigmoid(g), g)
    r = lax.broadcasted_iota(jnp.int32, (lc, lc), 0)
    c = lax.broadcasted_iota(jnp.int32, (lc, lc), 1)
    tril = (r >= c).astype(F32)
    bcols = _dot_hi(tril, gl)
    sel = (lax.broadcasted_iota(jnp.int32, (8, GATE_PAD), 0)
           == lax.broadcasted_iota(jnp.int32, (8, GATE_PAD), 1)).astype(F32)
    nt = (((1,), (1,)), ((), ()))
    grows = lax.dot_general(sel, gl, nt, precision=HIGHEST, preferred_element_type=F32)
    brows = lax.dot_general(sel, bcols, nt, precision=HIGHEST, preferred_element_type=F32)
    return gl, bcols, grows, brows, (r >= c)


def _mlstm_chunks(seqs, bias_row, lc):
    nt = (((1,), (1,)), ((), ()))
    tn = (((0,), (0,)), ((), ()))
    pairs = [(si, hd) for si in range(len(seqs)) for hd in range(ML_HEADS)]
    head = lambda x, part, hd: x[:, part * D_ML + hd * ML_DK:part * D_ML + (hd + 1) * ML_DK]
    q = {p: head(seqs[p[0]][0], 0, p[1]) for p in pairs}
    k = {p: head(seqs[p[0]][0], 1, p[1]) for p in pairs}
    v = {p: head(seqs[p[0]][0], 2, p[1]) for p in pairs}
    state = lambda p: seqs[p[0]][3](p[1])
    qk = {p: lax.dot_general(q[p], k[p], nt, preferred_element_type=F32) for p in pairs}
    qc = {p: _dot(q[p], state(p)[0].astype(BF16)) for p in pairs}
    tables = [_mlstm_gates(seq[2], bias_row, lc) for seq in seqs]
    w_in, w_out, mt, m_end, decay, kw = {}, {}, {}, {}, {}, {}
    for p in pairs:
        si, hd = p
        gl, bcols, grows, brows, causal = tables[si]
        f = ML_HEADS + hd
        ic, bc, ir, br = gl[:, hd:hd + 1], bcols[:, f:f + 1], grows[hd:hd + 1, :], brows[f:f + 1, :]
        m_state = state(p)[2]
        dmat = jnp.where(causal, bc - br + ir, -jnp.inf)
        inter = bc + m_state
        mt[p] = jnp.maximum(inter, jnp.max(dmat, axis=1, keepdims=True))
        w_in[p] = jnp.exp(dmat - mt[p])
        w_out[p] = jnp.exp(inter - mt[p])
        m_end[p] = mt[p][lc - 1:lc, :]
        b_last = bc[lc - 1:lc, :]
        decay[p] = jnp.exp(b_last + m_state - m_end[p])
        kw[p] = k[p].astype(F32) * jnp.exp(b_last - bc + ic - m_end[p])
    kv = {p: lax.dot_general(kw[p].astype(BF16), v[p], tn, preferred_element_type=F32) for p in pairs}
    s = {p: qk[p] * w_in[p] for p in pairs}
    sv = {p: _dot(s[p].astype(BF16), v[p]) for p in pairs}
    outs = [[] for _ in seqs]
    for p in pairs:
        si, hd = p
        c_state, n_state, _ = state(p)
        num = w_out[p] * qc[p] + sv[p]
        qn = (w_out[p] * jnp.sum(q[p].astype(F32) * n_state, axis=1, keepdims=True)
              + jnp.sum(s[p], axis=1, keepdims=True))
        h = num / jnp.maximum(jnp.abs(qn), jnp.exp(-mt[p]))
        seqs[si][4](hd, decay[p] * c_state + kv[p], decay[p] * n_state + jnp.sum(kw[p], axis=0, keepdims=True),
                    m_end[p])
        outs[si].append(jax.nn.sigmoid(head(seqs[si][1], 0, hd)) * h)
    return [jnp.concatenate(o, axis=1) for o in outs]


def _mlstm_prompt_kernel(qkv_ref, o_ref, gt_ref, bias_ref, h_ref, c_ref, n_ref, m_ref, *, lc):
    @pl.when(pl.program_id(1) == 0)
    def _():
        c_ref[...] = jnp.zeros_like(c_ref)
        n_ref[...] = jnp.zeros_like(n_ref)
        m_ref[...] = jnp.zeros_like(m_ref)

    def get_state(hd):
        return c_ref[hd], n_ref[hd], m_ref[hd][:, 0:1]

    def put_state(hd, c_new, n_new, m_new):
        c_ref[hd] = c_new
        n_ref[hd] = n_new
        m_ref[hd] = jnp.broadcast_to(m_new, (1, LANES))

    (h,) = _mlstm_chunks([(qkv_ref[...], o_ref[...], gt_ref[...], get_state, put_state)], bias_ref[...], lc)
    h_ref[...] = h.astype(BF16)


def _mlstm_prompt(qkv, o, gates, bias_row, bsz, seq):
    lc = ML_CHUNK_PROMPT
    nc = seq // lc
    row = lambda width: pl.BlockSpec((lc, width), lambda b, c: (b * nc + c, 0))
    st = lambda *tail: pl.BlockSpec((None, ML_HEADS) + tail, lambda b, c: (b, 0) + (0,) * len(tail))
    return pl.pallas_call(
        functools.partial(_mlstm_prompt_kernel, lc=lc),
        grid=(bsz, nc),
        in_specs=[row(3 * D_ML), row(D_ML), row(GATE_PAD), pl.BlockSpec((1, GATE_PAD), lambda b, c: (0, 0))],
        out_specs=[row(D_ML), st(ML_DK, ML_DK), st(1, ML_DK), st(1, LANES)],
        out_shape=(jax.ShapeDtypeStruct((bsz * seq, D_ML), BF16),
                   jax.ShapeDtypeStruct((bsz, ML_HEADS, ML_DK, ML_DK), F32),
                   jax.ShapeDtypeStruct((bsz, ML_HEADS, 1, ML_DK), F32),
                   jax.ShapeDtypeStruct((bsz, ML_HEADS, 1, LANES), F32)),
        compiler_params=_cparams("arbitrary", "arbitrary"),
        name="mlstm_prompt",
    )(qkv, o, gates, bias_row)


def _mlstm_sample_kernel(qkv_ref, o_ref, gt_ref, bias_ref, c0_ref, n0_ref, m0_ref,
                         h_ref, c_ref, n_ref, m_ref, qkv_s, *, lc, nb):
    qkv_s[...] = qkv_ref[...].astype(F32)

    def seq(b):
        rows = slice(b * lc, (b + 1) * lc)

        def get_state(hd):
            return c0_ref[b, hd], n0_ref[b, hd], m0_ref[b, hd][:, 0:1]

        def put_state(hd, c_new, n_new, m_new):
            c_ref[b, hd] = c_new
            n_ref[b, hd] = n_new
            m_ref[b, hd] = jnp.broadcast_to(m_new, (1, LANES))

        return (qkv_s[rows, :].astype(BF16), o_ref[rows, :], gt_ref[rows, :], get_state, put_state)

    hs = _mlstm_chunks([seq(b) for b in range(nb)], bias_ref[...], lc)
    for b, h in enumerate(hs):
        h_ref[b * lc:(b + 1) * lc, :] = h


def _mlstm_sample(qkv, o, gates, bias_row, c0, n0, m0, row_block_offset, bsz, seq):
    nb = 8
    lc = seq
    rows = nb * lc
    off = row_block_offset // rows
    row = lambda width: pl.BlockSpec((rows, width), lambda i: (off + i, 0))
    st = lambda *tail: pl.BlockSpec((nb, ML_HEADS) + tail, lambda i: (i, 0) + (0,) * len(tail))
    return pl.pallas_call(
        functools.partial(_mlstm_sample_kernel, lc=lc, nb=nb),
        grid=(bsz // nb,),
        in_specs=[row(3 * D_ML), row(D_ML), row(GATE_PAD), pl.BlockSpec((1, GATE_PAD), lambda i: (0, 0)),
                  st(ML_DK, ML_DK), st(1, ML_DK), st(1, LANES)],
        out_specs=[pl.BlockSpec((rows, D_ML), lambda i: (i, 0)), st(ML_DK, ML_DK), st(1, ML_DK), st(1, LANES)],
        out_shape=(jax.ShapeDtypeStruct((bsz * seq, D_ML), F32),
                   jax.ShapeDtypeStruct((bsz, ML_HEADS, ML_DK, ML_DK), F32),
                   jax.ShapeDtypeStruct((bsz, ML_HEADS, 1, ML_DK), F32),
                   jax.ShapeDtypeStruct((bsz, ML_HEADS, 1, LANES), F32)),
        scratch_shapes=[pltpu.VMEM((rows, 3 * D_ML), F32)],
        compiler_params=_cparams("arbitrary"),
        name="mlstm_sample",
    )(qkv, o, gates, bias_row, c0, n0, m0)


def _merge_kernel(xp_ref, xs_ref, yp_ref, ys_ref, mp_ref, ms_ref, gsm_ref, wglu_ref, bglu_ref, wsu_ref, wmu_ref,
                  wout_ref, gffn_ref, wrh_ref, wrl_ref, br_ref, h1_ref, xn_ref, lg_ref, *, n_prompt_blocks):
    i = pl.program_id(0)
    is_prompt = i < n_prompt_blocks
    x = jnp.where(is_prompt, xp_ref[...], xs_ref[...])
    y = jax.nn.gelu(jnp.where(is_prompt, yp_ref[...], ys_ref[...]))
    ym = jnp.where(is_prompt, mp_ref[...], ms_ref[...].astype(BF16))
    ys = y * jax.nn.sigmoid(_dot(y.astype(BF16), wglu_ref[...]) + bglu_ref[...])
    gsm = gsm_ref[...]
    merged = (jax.nn.sigmoid(gsm[:, :D_MODEL]) * _dot(ys.astype(BF16), wsu_ref[...])
              + jax.nn.sigmoid(gsm[:, D_MODEL:]) * _dot(ym, wmu_ref[...]))
    h1 = x + _dot(merged.astype(BF16), wout_ref[...])
    h1_ref[...] = h1
    xn = _rms(h1, gffn_ref[...])
    xn_ref[...] = xn
    xn_hi = xn.astype(BF16)
    xn_lo = (xn - xn_hi.astype(F32)).astype(BF16)
    lg_ref[...] = (_dot(xn_hi, wrh_ref[...]) + _dot(xn_lo, wrh_ref[...]) + _dot(xn_hi, wrl_ref[...])
                   + br_ref[...])


def _merge(xp, xs, y_pre, ym, gsm, w_glu, b_glu, w_ssm_up, w_ml_up, w_out, g_ffn, w_router, b_router):
    tm = TM_TOKENS
    t = gsm.shape[0]
    npb = xp.shape[0] // tm
    xp_spec, xs_spec = _two_source_specs(tm, D_MODEL, npb)
    yp_spec, ys_spec = _two_source_specs(tm, D_SSM, npb)
    mp_spec, ms_spec = _two_source_specs(tm, D_ML, npb)
    row = lambda width: pl.BlockSpec((tm, width), lambda i: (i, 0))
    wr = jnp.pad(w_router, ((0, 0), (0, LANES - N_EXPERTS)))
    wr_hi = wr.astype(BF16)
    wr_lo = (wr - wr_hi.astype(F32)).astype(BF16)
    br =jnp.pad(b_router, (0, LANES - N_EXPERTS)).reshape(1, LANES)
    return pl.pallas_call(
        functools.partial(_merge_kernel, n_prompt_blocks=npb),
        grid=(t // tm,),
        in_specs=[xp_spec, xs_spec, yp_spec, ys_spec, mp_spec, ms_spec, row(2 * D_MODEL), _full((D_SSM, D_SSM)),
                  _full((1, D_SSM)), _full((D_SSM, D_MODEL)), _full((D_ML, D_MODEL)),
                  _full((D_MODEL, D_MODEL)), _full((1, D_MODEL)), _full((D_MODEL, LANES)),
                  _full((D_MODEL, LANES)), _full((1, LANES))],
        out_specs=[row(D_MODEL), row(D_MODEL), row(LANES)],
        out_shape=(jax.ShapeDtypeStruct((t, D_MODEL), F32), jax.ShapeDtypeStruct((t, D_MODEL), F32),
                   jax.ShapeDtypeStruct((t, LANES), F32)),
        compiler_params=_cparams("arbitrary"),
        name="merge",
    )(xp, xs, *y_pre, *ym, gsm, w_glu.astype(BF16), b_glu.reshape(1, D_SSM), w_ssm_up.astype(BF16),
      w_ml_up.astype(BF16), w_out.astype(BF16), g_ffn.reshape(1, D_MODEL), wr_hi, wr_lo, br)


def _route(logits, tm):
    t = logits.shape[0]
    top_val, top_idx = lax.top_k(logits, TOP_K)
    top_w = jax.nn.softmax(top_val, axis=-1)
    flat_e = top_idx.reshape(-1)
    n_assign = t * TOP_K
    onehot = (flat_e[:, None] == jnp.arange(N_EXPERTS, dtype=jnp.int32)[None, :])
    rb = RANK_BLOCK
    oh3 = onehot.astype(F32).reshape(n_assign // rb, rb, N_EXPERTS)
    tril = jnp.tril(jnp.ones((rb, rb), F32))
    within = jnp.einsum('ij,bjk->bik', tril, oh3).astype(jnp.int32)
    totals = within[:, -1, :]
    before = jnp.cumsum(totals, axis=0) - totals
    csum = (within + before[:, None, :]).reshape(n_assign, N_EXPERTS)
    rank = jnp.sum(jnp.where(onehot, csum, 0), axis=1) - 1
    counts = before[-1] + totals[-1]
    padded = (counts + tm - 1) // tm * tm
    pad_end = jnp.cumsum(padded)
    pad_start = pad_end - padded
    pos = jnp.sum(jnp.where(onehot, pad_start[None, :], 0), axis=1) + rank
    n_blocks = -(-(n_assign + N_EXPERTS * (tm - 1)) // tm)
    block_row0 = jnp.arange(n_blocks, dtype=jnp.int32) * tm
    block_e = jnp.minimum(jnp.sum((pad_end[None, :] <= block_row0[:, None]).astype(jnp.int32), axis=1),
                          N_EXPERTS - 1)
    n_valid = (pad_end[-1] // tm).astype(jnp.int32).reshape(1)
    ids = jnp.arange(N_EXPERTS, dtype=jnp.int32)
    later = (ids[None, :] > ids[:, None]) & (counts[None, :] > 0)
    next_e = jnp.min(jnp.where(later, ids[None, :], N_EXPERTS), axis=1)
    next_e = jnp.where(next_e == N_EXPERTS, -1, next_e)[block_e]
    fill_lo = pad_start + counts
    return top_w, pos.reshape(t, TOP_K), fill_lo, pad_end, n_blocks * tm, block_e, next_e, n_valid


def _dispatch_copy(x_ref, rows_hbm, sem, r, dst):
    return pltpu.make_async_copy(x_ref.at[pl.ds(r, 1), :], rows_hbm.at[pl.ds(dst, 1), :], sem)


def _dispatch_kernel(lo_ref, hi_ref, pos_ref, x_ref, rows_hbm, zero_s, sem, zero_sem):
    tm = x_ref.shape[0]

    @pl.when(pl.program_id(0) == 0)
    def _():
        zero_s[...] = jnp.zeros_like(zero_s)

        def tail(do):
            def body(b, c):
                rows = pl.ds(pl.multiple_of(b * tm, tm), tm)
                do(pltpu.make_async_copy(zero_s, rows_hbm.at[rows, :], zero_sem))
                return c
            lax.fori_loop(hi_ref[N_EXPERTS - 1] // tm, rows_hbm.shape[0] // tm, body, 0)

        tail(lambda c: c.start())
        tail(lambda c: c.wait())

        def fill(e, carry):
            def each(do):
                def body(r, c):
                    do(_dispatch_copy(zero_s, rows_hbm, zero_sem, 0, r))
                    return c
                lax.fori_loop(lo_ref[e], hi_ref[e], body, 0)
            each(lambda c: c.start())
            each(lambda c: c.wait())
            return carry

        lax.fori_loop(0, N_EXPERTS, fill, 0)

    def each(do):
        for r in range(tm):
            for k in range(TOP_K):
                do(_dispatch_copy(x_ref, rows_hbm, sem, r, pos_ref[r * TOP_K + k]), k)

    each(lambda c, k: c.start(priority=k % 2))
    each(lambda c, k: c.wait())


def _dispatch(xn, pos, fill_lo, pad_end, n_rows):
    tm = TM_TOKENS
    t = xn.shape[0]
    grid_spec = pltpu.PrefetchScalarGridSpec(
        num_scalar_prefetch=2,
        grid=(t // tm,),
        in_specs=[pl.BlockSpec((tm * TOP_K,), lambda i, lo, hi: (i,), memory_space=pltpu.SMEM),
                  pl.BlockSpec((tm, D_MODEL), lambda i, lo, hi: (i, 0))],
        out_specs=pl.BlockSpec(memory_space=pl.ANY),
        scratch_shapes=[pltpu.VMEM((tm, D_MODEL), F32), pltpu.SemaphoreType.DMA(()), pltpu.SemaphoreType.DMA(())],
    )
    return pl.pallas_call(
        _dispatch_kernel,
        grid_spec=grid_spec,
        out_shape=jax.ShapeDtypeStruct((n_rows, D_MODEL), F32),
        compiler_params=_cparams("arbitrary"),
        name="moe_dispatch",
    )(fill_lo, pad_end, pos.reshape(-1), xn)


def _moe_weight_copies(e, w_hbm, wbuf, sems):
    return [pltpu.make_async_copy(w.at[e], wbuf.at[k], sems.at[k]) for k, w in enumerate(w_hbm)]


def _moe_kernel(be_ref, ne_ref, nv_ref, x_ref, wg_hbm, bg_ref, wu_hbm, bu_ref, wd_hbm, bd_ref, y_ref,
                w_bf, wbuf, sems):
    i = pl.program_id(0)
    e = be_ref[i]
    prev = be_ref[jnp.maximum(i - 1, 0)]
    valid = i < nv_ref[0]
    first = jnp.logical_or(i == 0, e != prev)
    copies = functools.partial(_moe_weight_copies, w_hbm=(wg_hbm, wu_hbm, wd_hbm), wbuf=wbuf, sems=sems)

    @pl.when(i == 0)
    def _():
        for c in copies(e):
            c.start()

    @pl.when(jnp.logical_and(valid, first))
    def _():
        nxt = ne_ref[i]
        for k, c in enumerate(copies(e)):
            c.wait()
            w_bf[k] = wbuf[k].astype(BF16)

        @pl.when(nxt >= 0)
        def _():
            for c in copies(nxt):
                c.start()

    @pl.when(valid)
    def _():
        x = x_ref[...].astype(BF16)
        g =jnp.minimum(_dot(x, w_bf[0]) + bg_ref[...], SWIGLU_LIMIT)
        u = jnp.clip(_dot(x, w_bf[1]) + bu_ref[...], -SWIGLU_LIMIT, SWIGLU_LIMIT)
        a = g * jax.nn.sigmoid(SWIGLU_ALPHA * g) * (u + 1.0)
        y_ref[...] = _dot(a.astype(BF16), w_bf[2]) + bd_ref[...]

    @pl.when(jnp.logical_not(valid))
    def _():
        y_ref[...] = jnp.zeros_like(y_ref)


def _moe_experts(x_rows, block_e, next_e, n_valid, w_gate, b_gate, w_up, b_up, w_down, b_down):
    tm = TM_MOE
    n_rows = x_rows.shape[0]
    wspec = pl.BlockSpec(memory_space=pl.ANY)
    bspec = pl.BlockSpec((None, 1, D_MODEL), lambda i, be, ne, nv: (be[i], 0, 0))
    rows = pl.BlockSpec((tm, D_MODEL), lambda i, be, ne, nv: (i, 0))
    grid_spec = pltpu.PrefetchScalarGridSpec(
        num_scalar_prefetch=3,
        grid=(n_rows // tm,),
        in_specs=[rows, wspec, bspec, wspec, bspec, wspec, bspec],
        out_specs=rows,
        scratch_shapes=[pltpu.VMEM((3, D_MODEL, D_MODEL), BF16), pltpu.VMEM((3, D_MODEL, D_MODEL), F32),
                        pltpu.SemaphoreType.DMA((3,))],
    )
    b3 = lambda b: b.reshape(N_EXPERTS, 1, D_MODEL)
    return pl.pallas_call(
        _moe_kernel,
        grid_spec=grid_spec,
        out_shape=jax.ShapeDtypeStruct((n_rows, D_MODEL), F32),
        compiler_params=_cparams("arbitrary"),
        name="moe_experts",
    )(block_e, next_e, n_valid, x_rows, w_gate, b3(b_gate), w_up, b3(b_up), w_down, b3(b_down))


def _ple_kernel(h1_ref, rows_ref, tw_ref, p_ref, gple_ref, wg_ref, wp_ref, gfin_ref, y_ref):
    tw = tw_ref[...]
    h2 = h1_ref[...]
    for k in range(TOP_K):
        h2 = h2 + rows_ref[k] * tw[:, k:k + 1]
    gate =jax.nn.sigmoid(_dot(_rms(h2, gple_ref[...]).astype(BF16), wg_ref[...]))
    h3 = h2 + gate * _dot(p_ref[...].astype(BF16), wp_ref[...])
    y_ref[...] = _rms(h3, gfin_ref[...])


def _ple_final(h1, expert_rows, top_w, p, row_block_offset, g_ple, w_ple_gate, w_ple_proj, g_final):
    tm = TM_TOKENS
    n = p.shape[0]
    off = row_block_offset // tm
    src = lambda width: pl.BlockSpec((tm, width), lambda i: (off + i, 0))
    loc = lambda width: pl.BlockSpec((tm, width), lambda i: (i, 0))
    return pl.pallas_call(
        _ple_kernel,
        grid=(n // tm,),
        in_specs=[src(D_MODEL), pl.BlockSpec((TOP_K, tm, D_MODEL), lambda i: (0, off + i, 0)), src(LANES),
                  loc(D_PLE), _full((1, D_MODEL)), _full((D_MODEL, D_MODEL)), _full((D_PLE, D_MODEL)),
                  _full((1, D_MODEL))],
        out_specs=loc(D_MODEL),
        out_shape=jax.ShapeDtypeStruct((n, D_MODEL), F32),
        compiler_params=_cparams("arbitrary"),
        name="ple_final",
    )(h1, expert_rows, top_w, p, g_ple.reshape(1, D_MODEL), w_ple_gate.astype(BF16),
      w_ple_proj.astype(BF16), g_final.reshape(1, D_MODEL))


def kernel(x_prompt, x_sample, p_prompt, p_sample, state_ssm_re, state_ssm_im, state_ml_c, state_ml_n, state_ml_m, g_mix, w_in, ssm_a_re, ssm_a_im, ssm_log_dt, ssm_b_re, ssm_b_im, ssm_c_re, ssm_c_im, ssm_d, ssm_w_glu, ssm_b_glu, ml_b_ig, ml_b_fg, w_ssm_up, w_ml_up, w_out, g_ffn, w_router, b_router, w_gate, b_gate, w_up, b_up, w_down, b_down, g_ple, w_ple_gate, w_ple_proj, g_final):
    assert g_mix.shape[0] == 1, "single-layer trunk"
    bp, lp, _ = x_prompt.shape
    bs, ls, _ = x_sample.shape
    tp, ts = bp * lp, bs * ls
    t = tp + ts
    xp = x_prompt.reshape(tp, D_MODEL)
    xs = x_sample.reshape(ts, D_MODEL)

    u, qkv, o, gates, gsm = _inproj(xp, xs, g_mix[0], w_in[0])

    s5_args = (ssm_a_re[0], ssm_a_im[0], ssm_log_dt[0], ssm_b_re[0], ssm_b_im[0], ssm_c_re[0],
               ssm_c_im[0], ssm_d[0])
    zero_state = jnp.zeros((S5_TILES, bp, 2 * S5_TILE_STATE), F32)
    tables = {tc: _s5_tables(*s5_args, tc) for tc in {S5_CHUNK, ls}}
    y_p, f_p = _s5(u, 0, tables[S5_CHUNK], zero_state,
                   tc=S5_CHUNK, n_chunks=lp // S5_CHUNK, nb=bp // 2, n_splits=2)
    y_s, f_s = _s5(u, tp, tables[ls], _s5_state_to_tiles(state_ssm_re[0], state_ssm_im[0]),
                   tc=ls, n_chunks=1, nb=bs, n_splits=1)
    re_p, im_p = _s5_state_from_tiles(f_p)
    re_s, im_s = _s5_state_from_tiles(f_s)

    bias_row = jnp.pad(jnp.concatenate([ml_b_ig[0], ml_b_fg[0]]), (0, GATE_PAD - 2 * ML_HEADS)).reshape(1, GATE_PAD)
    hm_p, c_p, n_p, m_p = _mlstm_prompt(qkv, o, gates, bias_row, bp, lp)
    m0 = jnp.broadcast_to(state_ml_m[0][:, :, None, None], (bs, ML_HEADS, 1, LANES))
    hm_s, c_s, n_s, m_s = _mlstm_sample(qkv, o, gates, bias_row, state_ml_c[0],
                                        state_ml_n[0].reshape(bs, ML_HEADS, 1, ML_DK), m0, tp, bs, ls)

    h1, xn, logits = _merge(xp, xs, (y_p, y_s), (hm_p, hm_s), gsm, ssm_w_glu[0], ssm_b_glu[0], w_ssm_up[0], w_ml_up[0],
                            w_out[0], g_ffn[0], w_router[0], b_router[0])

    top_w, pos, fill_lo, pad_end, n_rows, block_e, next_e, n_valid = _route(logits[:, :N_EXPERTS], TM_MOE)
    x_rows = _dispatch(xn, pos, fill_lo, pad_end, n_rows)
    expert_w = lambda w: w.reshape(N_EXPERTS, D_MODEL, D_MODEL)
    y_rows = _moe_experts(x_rows, block_e, next_e, n_valid, expert_w(w_gate), b_gate[0], expert_w(w_up),
                          b_up[0], expert_w(w_down), b_down[0])
    expert_rows = y_rows[pos.T.reshape(-1)].reshape(TOP_K, t, D_MODEL)
    top_w_pad = jnp.pad(top_w, ((0, 0), (0, LANES - TOP_K)))

    ple_w = (g_ple[0], w_ple_gate[0], w_ple_proj[0], g_final)
    y_prompt = _ple_final(h1, expert_rows, top_w_pad, p_prompt[0].reshape(tp, D_PLE), 0, *ple_w)
    y_sample = _ple_final(h1, expert_rows, top_w_pad, p_sample[0].reshape(ts, D_PLE), tp, *ple_w)

    return (y_prompt.reshape(bp, lp, D_MODEL), y_sample.reshape(bs, ls, D_MODEL),
            re_p, im_p, c_p[None], n_p.reshape(1, bp, ML_HEADS, ML_DK), m_p[:, :, 0, 0][None],
            re_s, im_s, c_s[None], n_s.reshape(1, bs, ML_HEADS, ML_DK), m_s[:, :, 0, 0][None])
```

```python
import functools

import jax
import jax.numpy as jnp
from jax import lax
from jax.experimental import pallas as pl
from jax.experimental.pallas import tpu as pltpu

F32 = jnp.float32
BF16 = jnp.bfloat16
HIGHEST = lax.Precision.HIGHEST

D_MODEL = 1024
D_SSM = 512
SSM_GROUP = 16
N_GROUPS = 32
SSM_STATE = 64
ML_HEADS = 4
ML_DK = 128
D_ML = 512
N_EXPERTS = 32
TOP_K = 4
SWIGLU_LIMIT = 7.0
SWIGLU_ALPHA = 1.702
D_PLE = 256
RMS_EPS = 1e-6

LANES = 128
GATE_PAD = LANES
S5_CHUNK = 8
ML_CHUNK_PROMPT = 256
TM_TOKENS = 256
TM_MOE = 256
VMEM_LIMIT = 56 * 1024 * 1024


def _cparams(*sem):
    return pltpu.CompilerParams(dimension_semantics=sem, vmem_limit_bytes=VMEM_LIMIT)


def _rms(x, g):
    return x * lax.rsqrt(jnp.mean(x * x, axis=-1, keepdims=True) + RMS_EPS) * g


def _dot(a, b):
    return jnp.dot(a, b, preferred_element_type=F32)


def _dot_hi(a, b):
    return jnp.dot(a, b, preferred_element_type=F32, precision=HIGHEST)


def _full(shape):
    n = len(shape)
    return pl.BlockSpec(shape, lambda *_: (0,) * n)


def _inproj_kernel(xp_ref, xs_ref, g_ref, wu_ref, wqkv_ref, wo_ref, wgt_ref, wgsm_ref,
                   u_ref, qkv_ref, o_ref, gt_ref, gsm_ref, *, n_prompt_blocks):
    i = pl.program_id(0)
    x = jnp.where(i < n_prompt_blocks, xp_ref[...], xs_ref[...])
    hn = _rms(x, g_ref[...]).astype(BF16)
    u_ref[...] = _dot(hn, wu_ref[...])
    qkv = _dot(hn, wqkv_ref[...])
    col = lax.broadcasted_iota(jnp.int32, (1, 3 * D_ML), 1)
    k_scale = jnp.where((col >= D_ML) & (col < 2 * D_ML), ML_DK ** -0.5, 1.0).astype(F32)
    qkv_ref[...] = (qkv * k_scale).astype(BF16)
    o_ref[...] = _dot(hn, wo_ref[...])
    gt_ref[...] = _dot(hn, wgt_ref[...])
    gsm_ref[...] = _dot(hn, wgsm_ref[...])


def _two_source_specs(tm, width, n_prompt_blocks):
    last = n_prompt_blocks - 1
    return (pl.BlockSpec((tm, width), lambda i: (jnp.minimum(i, last), 0)),
            pl.BlockSpec((tm, width), lambda i: (jnp.maximum(i - n_prompt_blocks, 0), 0)))


def _inproj(xp, xs, g_mix, w_in):
    tm = TM_TOKENS
    tp, ts = xp.shape[0], xs.shape[0]
    t = tp + ts
    npb = tp // tm
    w = w_in.astype(BF16)
    o0 = D_SSM
    wu = w[:, :o0]
    wqkv = w[:, o0:o0 + 3 * D_ML]
    wo = w[:, o0 + 3 * D_ML:o0 + 4 * D_ML]
    g0 = o0 + 4 * D_ML
    wgt = jnp.pad(w[:, g0:g0 + 2 * ML_HEADS], ((0, 0), (0, GATE_PAD - 2 * ML_HEADS)))
    wgsm = w[:, g0 + 2 * ML_HEADS:]
    xp_spec, xs_spec = _two_source_specs(tm, D_MODEL, npb)
    outs = (jax.ShapeDtypeStruct((t, D_SSM), F32), jax.ShapeDtypeStruct((t, 3 * D_ML), BF16),
            jax.ShapeDtypeStruct((t, D_ML), F32), jax.ShapeDtypeStruct((t, GATE_PAD), F32),
            jax.ShapeDtypeStruct((t, 2 * D_MODEL), F32))
    row = lambda width: pl.BlockSpec((tm, width), lambda i: (i, 0))
    return pl.pallas_call(
        functools.partial(_inproj_kernel, n_prompt_blocks=npb),
        grid=(t // tm,),
        in_specs=[xp_spec, xs_spec, _full((1, D_MODEL)), _full(wu.shape), _full(wqkv.shape),
                  _full(wo.shape), _full(wgt.shape), _full(wgsm.shape)],
        out_specs=[row(D_SSM), row(3 * D_ML), row(D_ML), row(GATE_PAD), row(2 * D_MODEL)],
        out_shape=outs,
        compiler_params=_cparams("arbitrary"),
        name="inproj",
    )(xp, xs, g_mix.reshape(1, D_MODEL), wu, wqkv, wo, wgt, wgsm)


S5_TILES = D_SSM // LANES
S5_TILE_GROUPS = LANES // SSM_GROUP
S5_TILE_STATE = S5_TILE_GROUPS * SSM_STATE


def _block_diag_tiles(x):
    gt = S5_TILE_GROUPS
    n, _, r, c = x.shape
    x5 = x.reshape(n, S5_TILES, gt, r, c)
    eye = jnp.eye(gt, dtype=x.dtype)
    return (x5[:, :, :, :, None, :] * eye[None, None, :, None, :, None]).reshape(n, S5_TILES, gt * r, gt * c)


def _s5_tables(a_re, a_im, log_dt, b_re, b_im, c_re, c_im, d_skip, tc):
    ein = functools.partial(jnp.einsum, precision=HIGHEST)
    dt = jnp.exp(log_dt)[:, None]
    mag = jnp.exp(a_re * dt)
    abar_r, abar_i = mag * jnp.cos(a_im * dt), mag * jnp.sin(a_im * dt)
    den = a_re * a_re + a_im * a_im
    nr, ni = abar_r - 1.0, abar_i
    coef_r = (nr * a_re + ni * a_im) / den
    coef_i = (ni * a_re - nr * a_im) / den
    bbar_r = coef_r[..., None] * b_re - coef_i[..., None] * b_im
    bbar_i = coef_r[..., None] * b_im + coef_i[..., None] * b_re

    def abar_pow(j):
        jj = j[..., None, None]
        mag_j = jnp.where(jj >= 0, jnp.exp(jj * (a_re * dt)), 0.0)
        return mag_j * jnp.cos(jj * (a_im * dt)), mag_j * jnp.sin(jj * (a_im * dt))

    half = tc // 2
    steps = jnp.arange(tc, dtype=F32)
    at_r, at_i = abar_pow(jnp.full((), tc, F32))
    lags = (2.0 * jnp.arange(half, dtype=F32)[:, None, None]
            + jnp.array([[0.0, 1.0], [-1.0, 0.0]], F32)[None])
    lag_r, lag_i = abar_pow(lags.reshape(-1))
    ab_r = lag_r[..., None] * bbar_r - lag_i[..., None] * bbar_i
    ab_i = lag_r[..., None] * bbar_i + lag_i[..., None] * bbar_r
    kern = ein('ghp,jgpk->jgkh', c_re, ab_r) - ein('ghp,jgpk->jgkh', c_im, ab_i)
    bd_lag = _block_diag_tiles(kern.astype(BF16)).reshape(half, 2, 2, S5_TILES, LANES, LANES)
    toe = jnp.transpose(bd_lag, (3, 0, 1, 4, 2, 5)).reshape(S5_TILES, half, 2 * LANES, 2 * LANES)
    rev_r, rev_i = abar_pow(tc - 1.0 - steps)
    s_r = jnp.transpose(rev_r[..., None] * bbar_r - rev_i[..., None] * bbar_i, (0, 1, 3, 2))
    s_i = jnp.transpose(rev_r[..., None] * bbar_i + rev_i[..., None] * bbar_r, (0, 1, 3, 2))
    s_step = jnp.concatenate([_block_diag_tiles(s_r.astype(BF16)), _block_diag_tiles(s_i.astype(BF16))],
                             axis=3)
    s_tab = jnp.transpose(s_step.reshape(half, 2, S5_TILES, LANES, 2 * S5_TILE_STATE),
                          (2, 0, 1, 3, 4)).reshape(S5_TILES, half, 2 * LANES, 2 * S5_TILE_STATE)
    a1_r, a1_i = abar_pow(steps + 1.0)
    c_re_t, c_im_t = jnp.transpose(c_re, (0, 2, 1)), jnp.transpose(c_im, (0, 2, 1))
    p_r = c_re_t[None] * a1_r[..., None] - c_im_t[None] * a1_i[..., None]
    p_i = -c_re_t[None] * a1_i[..., None] - c_im_t[None] * a1_r[..., None]
    p_step = jnp.concatenate([_block_diag_tiles(p_r.astype(BF16)), _block_diag_tiles(p_i.astype(BF16))],
                             axis=2)
    p_tab = jnp.transpose(p_step.reshape(half, 2, S5_TILES, 2 * S5_TILE_STATE, LANES),
                          (2, 0, 3, 1, 4)).reshape(S5_TILES, half, 2 * S5_TILE_STATE, 2 * LANES)
    a_tab = jnp.stack([at_r.reshape(S5_TILES, S5_TILE_STATE), at_i.reshape(S5_TILES, S5_TILE_STATE)], axis=1)
    d_tab = d_skip.reshape(S5_TILES, 1, LANES)
    return toe.astype(BF16), s_tab.astype(BF16), p_tab.astype(BF16), a_tab, d_tab


def _s5_kernel(u_ref, t_ref, s_ref, p_ref, a_ref, d_ref, h0_ref, y_ref, f_ref, loc, xprev,
               *, tc, n_chunks, nb):
    r = n_chunks * nb
    half = tc // 2
    ns = S5_TILE_STATE
    step_rows = lambda t: pl.ds(t, r, stride=tc)
    v = [u_ref[step_rows(t), :] for t in range(tc)]
    vp = [jnp.concatenate([v[2 * a].astype(BF16), v[2 * a + 1].astype(BF16)], axis=1) for a in range(half)]
    acc = _dot(vp[0], s_ref[0])
    for a in range(1, half):
        acc = acc + _dot(vp[a], s_ref[a])
    nt = ns // LANES
    lane_tile = lambda k: slice(k * LANES, (k + 1) * LANES)
    for k in range(2 * nt):
        loc[k] = acc[:, lane_tile(k)]
    abar = a_ref[...]

    def body(c, carry):
        rows = pl.ds(c, nb, stride=n_chunks) if n_chunks > 1 else pl.ds(0, nb)
        new = []
        for k in range(nt):
            xr, xi = carry[k], carry[nt + k]
            xprev[k, rows, :] = xr
            xprev[nt + k, rows, :] = xi
            ar, ai = abar[0:1, lane_tile(k)], abar[1:2, lane_tile(k)]
            new.append((ar * xr - ai * xi + loc[k, rows, :], ar * xi + ai * xr + loc[nt + k, rows, :]))
        return tuple(n[0] for n in new) + tuple(n[1] for n in new)

    x_end = lax.fori_loop(0, n_chunks, body, tuple(h0_ref[:, lane_tile(k)] for k in range(2 * nt)),
                          unroll=min(4, n_chunks))
    for k in range(2 * nt):
        f_ref[:, lane_tile(k)] = x_end[k]
    xp = jnp.concatenate([xprev[k] for k in range(2 * nt)], axis=1).astype(BF16)
    d = d_ref[...]
    for a2 in range(half):
        acc = _dot(xp, p_ref[a2])
        for a in range(a2 + 1):
            acc = acc + _dot(vp[a], t_ref[a2 - a])
        for k in range(2):
            t = 2 * a2 + k
            y_ref[step_rows(t), :] = acc[:, k * LANES:(k + 1) * LANES] + v[t] * d


def _s5(u, row_block_offset, tables, h0, *, tc, n_chunks, nb, n_splits):
    toe, s_tab, p_tab, a_tab, d_tab = tables
    rows = nb * n_chunks * tc
    off = row_block_offset // rows
    half = tc // 2
    ns2 = 2 * S5_TILE_STATE
    tile = lambda *tail: pl.BlockSpec((None,) + tail, lambda j, s: (j,) + (0,) * len(tail))
    state = pl.BlockSpec((None, None, nb, ns2), lambda j, s: (j, s, 0, 0))
    y, f = pl.pallas_call(
        functools.partial(_s5_kernel, tc=tc, n_chunks=n_chunks, nb=nb),
        grid=(S5_TILES, n_splits),
        in_specs=[pl.BlockSpec((rows, LANES), lambda j, s: (off + s, j)),
                  tile(half, 2 * LANES, 2 * LANES), tile(half, 2 * LANES, ns2), tile(half, ns2, 2 * LANES),
                  tile(2, S5_TILE_STATE), tile(1, LANES), state],
        out_specs=[pl.BlockSpec((rows, LANES), lambda j, s: (s, j)), state],
        out_shape=(jax.ShapeDtypeStruct((rows * n_splits, D_SSM), F32),
                   jax.ShapeDtypeStruct((S5_TILES, n_splits, nb, ns2), F32)),
        scratch_shapes=[pltpu.VMEM((ns2 // LANES, nb * n_chunks, LANES), F32)] * 2,
        compiler_params=_cparams("arbitrary", "arbitrary"),
        name=f"s5_c{n_chunks}",
    )(u, toe, s_tab, p_tab, a_tab, d_tab, h0.reshape(S5_TILES, n_splits, nb, ns2))
    return y, f.reshape(S5_TILES, n_splits * nb, ns2)


def _s5_state_to_tiles(s_re, s_im):
    b = s_re.shape[0]
    f = lambda s: s.reshape(b, S5_TILES, S5_TILE_STATE).transpose(1, 0, 2)
    return jnp.concatenate([f(s_re), f(s_im)], axis=2)


def _s5_state_from_tiles(f):
    b = f.shape[1]
    g = lambda s: s.transpose(1, 0, 2).reshape(1, b, N_GROUPS, SSM_STATE)
    return g(f[:, :, :S5_TILE_STATE]), g(f[:, :, S5_TILE_STATE:])


def _log_sigmoid(x):
    return jnp.minimum(x, 0.0) - jnp.log1p(jnp.exp(-jnp.abs(x)))


def _mlstm_gates(gates, bias_row, lc):
    g = gates + bias_row
    col = lax.broadcasted_iota(jnp.int32, (1, GATE_PAD), 1)
    gl = jnp.where(col >= ML_HEADS, _log_sigmoid(g), g)
    r = lax.broadcasted_iota(jnp.int32, (lc, lc), 0)
    c = lax.broadcasted_iota(jnp.int32, (lc, lc), 1)
    tril = (r >= c).astype(F32)
    bcols = _dot_hi(tril, gl)
    sel = (lax.broadcasted_iota(jnp.int32, (8, GATE_PAD), 0)
           == lax.broadcasted_iota(jnp.int32, (8, GATE_PAD), 1)).astype(F32)
    nt = (((1,), (1,)), ((), ()))
    grows = lax.dot_general(sel, gl, nt, precision=HIGHEST, preferred_element_type=F32)
    brows = lax.dot_general(sel, bcols, nt, precision=HIGHEST, preferred_element_type=F32)
    return gl, bcols, grows, brows, (r >= c)


def _mlstm_chunks(seqs, bias_row, lc):
    nt = (((1,), (1,)), ((), ()))
    tn = (((0,), (0,)), ((), ()))
    pairs = [(si, hd) for si in range(len(seqs)) for hd in range(ML_HEADS)]
    head = lambda x, part, hd: x[:, part * D_ML + hd * ML_DK:part * D_ML + (hd + 1) * ML_DK]
    q = {p: head(seqs[p[0]][0], 0, p[1]) for p in pairs}
    k = {p: head(seqs[p[0]][0], 1, p[1]) for p in pairs}
    v = {p: head(seqs[p[0]][0], 2, p[1]) for p in pairs}
    state = lambda p: seqs[p[0]][3](p[1])
    qk = {p: lax.dot_general(q[p], k[p], nt, preferred_element_type=F32) for p in pairs}
    qc = {p: _dot(q[p], state(p)[0].astype(BF16)) for p in pairs}
    tables = [_mlstm_gates(seq[2], bias_row, lc) for seq in seqs]
    w_in, w_out, mt, m_end, decay, kw = {}, {}, {}, {}, {}, {}
    for p in pairs:
        si, hd = p
        gl, bcols, grows, brows, causal = tables[si]
        f = ML_HEADS + hd
        ic, bc, ir, br = gl[:, hd:hd + 1], bcols[:, f:f + 1], grows[hd:hd + 1, :], brows[f:f + 1, :]
        m_state = state(p)[2]
        dmat = jnp.where(causal, bc - br + ir, -jnp.inf)
        inter = bc + m_state
        mt[p] = jnp.maximum(inter, jnp.max(dmat, axis=1, keepdims=True))
        w_in[p] = jnp.exp(dmat - mt[p])
        w_out[p] = jnp.exp(inter - mt[p])
        m_end[p] = mt[p][lc - 1:lc, :]
        b_last = bc[lc - 1:lc, :]
        decay[p] = jnp.exp(b_last + m_state - m_end[p])
        kw[p] = k[p].astype(F32) * jnp.exp(b_last - bc + ic - m_end[p])
    kv = {p: lax.dot_general(kw[p].astype(BF16), v[p], tn, preferred_element_type=F32) for p in pairs}
    s = {p: qk[p] * w_in[p] for p in pairs}
    sv = {p: _dot(s[p].astype(BF16), v[p]) for p in pairs}
    outs = [[] for _ in seqs]
    for p in pairs:
        si, hd = p
        c_state, n_state, _ = state(p)
        num = w_out[p] * qc[p] + sv[p]
        qn = (w_out[p] * jnp.sum(q[p].astype(F32) * n_state, axis=1, keepdims=True)
              + jnp.sum(s[p], axis=1, keepdims=True))
        h = num / jnp.maximum(jnp.abs(qn), jnp.exp(-mt[p]))
        seqs[si][4](hd, decay[p] * c_state + kv[p], decay[p] * n_state + jnp.sum(kw[p], axis=0, keepdims=True),
                    m_end[p])
        outs[si].append(jax.nn.sigmoid(head(seqs[si][1], 0, hd)) * h)
    return [jnp.concatenate(o, axis=1) for o in outs]


def _mlstm_prompt_kernel(qkv_ref, o_ref, gt_ref, bias_ref, h_ref, c_ref, n_ref, m_ref, *, lc):
    @pl.when(pl.program_id(1) == 0)
    def _():
        c_ref[...] = jnp.zeros_like(c_ref)
        n_ref[...] = jnp.zeros_like(n_ref)
        m_ref[...] = jnp.zeros_like(m_ref)

    def get_state(hd):
        return c_ref[hd], n_ref[hd], m_ref[hd][:, 0:1]

    def put_state(hd, c_new, n_new, m_new):
        c_ref[hd] = c_new
        n_ref[hd] = n_new
        m_ref[hd] = jnp.broadcast_to(m_new, (1, LANES))

    (h,) = _mlstm_chunks([(qkv_ref[...], o_ref[...], gt_ref[...], get_state, put_state)], bias_ref[...], lc)
    h_ref[...] = h.astype(BF16)


def _mlstm_prompt(qkv, o, gates, bias_row, bsz, seq):
    lc = ML_CHUNK_PROMPT
    nc = seq // lc
    row = lambda width: pl.BlockSpec((lc, width), lambda b, c: (b * nc + c, 0))
    st = lambda *tail: pl.BlockSpec((None, ML_HEADS) + tail, lambda b, c: (b, 0) + (0,) * len(tail))
    return pl.pallas_call(
        functools.partial(_mlstm_prompt_kernel, lc=lc),
        grid=(bsz, nc),
        in_specs=[row(3 * D_ML), row(D_ML), row(GATE_PAD), pl.BlockSpec((1, GATE_PAD), lambda b, c: (0, 0))],
        out_specs=[row(D_ML), st(ML_DK, ML_DK), st(1, ML_DK), st(1, LANES)],
        out_shape=(jax.ShapeDtypeStruct((bsz * seq, D_ML), BF16),
                   jax.ShapeDtypeStruct((bsz, ML_HEADS, ML_DK, ML_DK), F32),
                   jax.ShapeDtypeStruct((bsz, ML_HEADS, 1, ML_DK), F32),
                   jax.ShapeDtypeStruct((bsz, ML_HEADS, 1, LANES), F32)),
        compiler_params=_cparams("arbitrary", "arbitrary"),
        name="mlstm_prompt",
    )(qkv, o, gates, bias_row)


def _mlstm_sample_kernel(qkv_ref, o_ref, gt_ref, bias_ref, c0_ref, n0_ref, m0_ref,
                         h_ref, c_ref, n_ref, m_ref, qkv_s, *, lc, nb):
    qkv_s[...] = qkv_ref[...].astype(F32)

    def seq(b):
        rows = slice(b * lc, (b + 1) * lc)

        def get_state(hd):
            return c0_ref[b, hd], n0_ref[b, hd], m0_ref[b, hd][:, 0:1]

        def put_state(hd, c_new, n_new, m_new):
            c_ref[b, hd] = c_new
            n_ref[b, hd] = n_new
            m_ref[b, hd] = jnp.broadcast_to(m_new, (1, LANES))

        return (qkv_s[rows, :].astype(BF16), o_ref[rows, :], gt_ref[rows, :], get_state, put_state)

    hs = _mlstm_chunks([seq(b) for b in range(nb)], bias_ref[...], lc)
    for b, h in enumerate(hs):
        h_ref[b * lc:(b + 1) * lc, :] = h


def _mlstm_sample(qkv, o, gates, bias_row, c0, n0, m0, row_block_offset, bsz, seq):
    nb = 8
    lc = seq
    rows = nb * lc
    off = row_block_offset // rows
    row = lambda width: pl.BlockSpec((rows, width), lambda i: (off + i, 0))
    st = lambda *tail: pl.BlockSpec((nb, ML_HEADS) + tail, lambda i: (i, 0) + (0,) * len(tail))
    return pl.pallas_call(
        functools.partial(_mlstm_sample_kernel, lc=lc, nb=nb),
        grid=(bsz // nb,),
        in_specs=[row(3 * D_ML), row(D_ML), row(GATE_PAD), pl.BlockSpec((1, GATE_PAD), lambda i: (0, 0)),
                  st(ML_DK, ML_DK), st(1, ML_DK), st(1, LANES)],
        out_specs=[pl.BlockSpec((rows, D_ML), lambda i: (i, 0)), st(ML_DK, ML_DK), st(1, ML_DK), st(1, LANES)],
        out_shape=(jax.ShapeDtypeStruct((bsz * seq, D_ML), F32),
                   jax.ShapeDtypeStruct((bsz, ML_HEADS, ML_DK, ML_DK), F32),
                   jax.ShapeDtypeStruct((bsz, ML_HEADS, 1, ML_DK), F32),
                   jax.ShapeDtypeStruct((bsz, ML_HEADS, 1, LANES), F32)),
        scratch_shapes=[pltpu.VMEM((rows, 3 * D_ML), F32)],
        compiler_params=_cparams("arbitrary"),
        name="mlstm_sample",
    )(qkv, o, gates, bias_row, c0, n0, m0)


def _merge_kernel(xp_ref, xs_ref, yp_ref, ys_ref, mp_ref, ms_ref, gsm_ref, wglu_ref, bglu_ref, wsu_ref, wmu_ref,
                  wout_ref, gffn_ref, wrh_ref, wrl_ref, br_ref, h1_ref, xn_ref, lg_ref, *, n_prompt_blocks):
    i = pl.program_id(0)
    is_prompt = i < n_prompt_blocks
    x = jnp.where(is_prompt, xp_ref[...], xs_ref[...])
    y = jax.nn.gelu(jnp.where(is_prompt, yp_ref[...], ys_ref[...]))
    ym = jnp.where(is_prompt, mp_ref[...], ms_ref[...].astype(BF16))
    ys = y * jax.nn.sigmoid(_dot(y.astype(BF16), wglu_ref[...]) + bglu_ref[...])
    gsm = gsm_ref[...]
    merged = (jax.nn.sigmoid(gsm[:, :D_MODEL]) * _dot(ys.astype(BF16), wsu_ref[...])
              + jax.nn.sigmoid(gsm[:, D_MODEL:]) * _dot(ym, wmu_ref[...]))
    h1 = x + _dot(merged.astype(BF16), wout_ref[...])
    h1_ref[...] = h1
    xn = _rms(h1, gffn_ref[...])
    xn_ref[...] = xn
    xn_hi = xn.astype(BF16)
    xn_lo = (xn - xn_hi.astype(F32)).astype(BF16)
    lg_ref[...] = (_dot(xn_hi, wrh_ref[...]) + _dot(xn_lo, wrh_ref[...]) + _dot(xn_hi, wrl_ref[...])
                   + br_ref[...])


def _merge(xp, xs, y_pre, ym, gsm, w_glu, b_glu, w_ssm_up, w_ml_up, w_out, g_ffn, w_router, b_router):
    tm = TM_TOKENS
    t = gsm.shape[0]
    npb = xp.shape[0] // tm
    xp_spec, xs_spec = _two_source_specs(tm, D_MODEL, npb)
    yp_spec, ys_spec = _two_source_specs(tm, D_SSM, npb)
    mp_spec, ms_spec = _two_source_specs(tm, D_ML, npb)
    row = lambda width: pl.BlockSpec((tm, width), lambda i: (i, 0))
    wr = jnp.pad(w_router, ((0, 0), (0, LANES - N_EXPERTS)))
    wr_hi = wr.astype(BF16)
    wr_lo = (wr - wr_hi.astype(F32)).astype(BF16)
    br =jnp.pad(b_router, (0, LANES - N_EXPERTS)).reshape(1, LANES)
    return pl.pallas_call(
        functools.partial(_merge_kernel, n_prompt_blocks=npb),
        grid=(t // tm,),
        in_specs=[xp_spec, xs_spec, yp_spec, ys_spec, mp_spec, ms_spec, row(2 * D_MODEL), _full((D_SSM, D_SSM)),
                  _full((1, D_SSM)), _full((D_SSM, D_MODEL)), _full((D_ML, D_MODEL)),
                  _full((D_MODEL, D_MODEL)), _full((1, D_MODEL)), _full((D_MODEL, LANES)),
                  _full((D_MODEL, LANES)), _full((1, LANES))],
        out_specs=[row(D_MODEL), row(D_MODEL), row(LANES)],
        out_shape=(jax.ShapeDtypeStruct((t, D_MODEL), F32), jax.ShapeDtypeStruct((t, D_MODEL), F32),
                   jax.ShapeDtypeStruct((t, LANES), F32)),
        compiler_params=_cparams("arbitrary"),
        name="merge",
    )(xp, xs, *y_pre, *ym, gsm, w_glu.astype(BF16), b_glu.reshape(1, D_SSM), w_ssm_up.astype(BF16),
      w_ml_up.astype(BF16), w_out.astype(BF16), g_ffn.reshape(1, D_MODEL), wr_hi, wr_lo, br)


RANK_BLOCK = 256


def _route(logits, tm):
    t = logits.shape[0]
    top_val, top_idx = lax.top_k(logits, TOP_K)
    top_w = jax.nn.softmax(top_val, axis=-1)
    flat_e = top_idx.reshape(-1)
    n_assign = t * TOP_K
    onehot = (flat_e[:, None] == jnp.arange(N_EXPERTS, dtype=jnp.int32)[None, :])
    rb = RANK_BLOCK
    oh3 = onehot.astype(F32).reshape(n_assign // rb, rb, N_EXPERTS)
    tril = jnp.tril(jnp.ones((rb, rb), F32))
    within = jnp.einsum('ij,bjk->bik', tril, oh3).astype(jnp.int32)
    totals = within[:, -1, :]
    before = jnp.cumsum(totals, axis=0) - totals
    csum = (within + before[:, None, :]).reshape(n_assign, N_EXPERTS)
    rank = jnp.sum(jnp.where(onehot, csum, 0), axis=1) - 1
    counts = before[-1] + totals[-1]
    padded = (counts + tm - 1) // tm * tm
    pad_end = jnp.cumsum(padded)
    pad_start = pad_end - padded
    pos = (jnp.sum(jnp.where(onehot, pad_start[None, :], 0), axis=1) + rank).reshape(t, TOP_K)
    n_blocks = -(-(n_assign + N_EXPERTS * (tm - 1)) // tm)
    block_row0 = jnp.arange(n_blocks, dtype=jnp.int32) * tm
    block_e = jnp.minimum(jnp.sum((pad_end[None, :] <= block_row0[:, None]).astype(jnp.int32), axis=1),
                          N_EXPERTS - 1)
    n_valid = (pad_end[-1] // tm).astype(jnp.int32).reshape(1)
    ids = jnp.arange(N_EXPERTS, dtype=jnp.int32)
    later = (ids[None, :] > ids[:, None]) & (counts[None, :] > 0)
    next_e = jnp.min(jnp.where(later, ids[None, :], N_EXPERTS), axis=1)
    next_e = jnp.where(next_e == N_EXPERTS, -1, next_e)[block_e]
    fill_lo = pad_start + counts
    top_w_pad = jnp.pad(top_w, ((0, 0), (0, LANES - TOP_K)))
    return top_w_pad, pos, fill_lo, pad_end, n_blocks * tm, block_e, next_e, n_valid


def _dispatch_copy(x_ref, rows_hbm, sem, r, dst):
    return pltpu.make_async_copy(x_ref.at[pl.ds(r, 1), :], rows_hbm.at[pl.ds(dst, 1), :], sem)


def _dispatch_kernel(lo_ref, hi_ref, pos_ref, x_ref, rows_hbm, zero_s, sem, zero_sem):
    tm = x_ref.shape[0]

    @pl.when(pl.program_id(0) == 0)
    def _():
        zero_s[...] = jnp.zeros_like(zero_s)

        def tail(do):
            def body(b, c):
                rows = pl.ds(pl.multiple_of(b * tm, tm), tm)
                do(pltpu.make_async_copy(zero_s, rows_hbm.at[rows, :], zero_sem))
                return c
            lax.fori_loop(hi_ref[N_EXPERTS - 1] // tm, rows_hbm.shape[0] // tm, body, 0)

        tail(lambda c: c.start())
        tail(lambda c: c.wait())

        def fill(e, carry):
            def each(do):
                def body(r, c):
                    do(_dispatch_copy(zero_s, rows_hbm, zero_sem, 0, r))
                    return c
                lax.fori_loop(lo_ref[e], hi_ref[e], body, 0)
            each(lambda c: c.start())
            each(lambda c: c.wait())
            return carry

        lax.fori_loop(0, N_EXPERTS, fill, 0)

    def each(do):
        for r in range(tm):
            for k in range(TOP_K):
                do(_dispatch_copy(x_ref, rows_hbm, sem, r, pos_ref[r * TOP_K + k]), k)

    each(lambda c, k: c.start(priority=k % 2))
    each(lambda c, k: c.wait())


def _dispatch(xn, pos, fill_lo, pad_end, n_rows):
    tm = TM_TOKENS
    t = xn.shape[0]
    grid_spec = pltpu.PrefetchScalarGridSpec(
        num_scalar_prefetch=2,
        grid=(t // tm,),
        in_specs=[pl.BlockSpec((tm * TOP_K,), lambda i, lo, hi: (i,), memory_space=pltpu.SMEM),
                  pl.BlockSpec((tm, D_MODEL), lambda i, lo, hi: (i, 0))],
        out_specs=pl.BlockSpec(memory_space=pl.ANY),
        scratch_shapes=[pltpu.VMEM((tm, D_MODEL), F32), pltpu.SemaphoreType.DMA(()), pltpu.SemaphoreType.DMA(())],
    )
    return pl.pallas_call(
        _dispatch_kernel,
        grid_spec=grid_spec,
        out_shape=jax.ShapeDtypeStruct((n_rows, D_MODEL), F32),
        compiler_params=_cparams("arbitrary"),
        name="moe_dispatch",
    )(fill_lo, pad_end, pos.reshape(-1), xn)


def _moe_weight_copies(e, w_hbm, wbuf, sems):
    return [pltpu.make_async_copy(w.at[e], wbuf.at[k], sems.at[k]) for k, w in enumerate(w_hbm)]


def _moe_kernel(be_ref, ne_ref, nv_ref, x_ref, wg_hbm, bg_ref, wu_hbm, bu_ref, wd_hbm, bd_ref, y_ref,
                w_bf, wbuf, sems):
    i = pl.program_id(0)
    e = be_ref[i]
    prev = be_ref[jnp.maximum(i - 1, 0)]
    valid = i < nv_ref[0]
    first = jnp.logical_or(i == 0, e != prev)
    copies = functools.partial(_moe_weight_copies, w_hbm=(wg_hbm, wu_hbm, wd_hbm), wbuf=wbuf, sems=sems)

    @pl.when(i == 0)
    def _():
        for c in copies(e):
            c.start()

    @pl.when(jnp.logical_and(valid, first))
    def _():
        nxt = ne_ref[i]
        for k, c in enumerate(copies(e)):
            c.wait()
            w_bf[k] = wbuf[k].astype(BF16)

        @pl.when(nxt >= 0)
        def _():
            for c in copies(nxt):
                c.start(priority=1)

    @pl.when(valid)
    def _():
        x = x_ref[...].astype(BF16)
        g =jnp.minimum(_dot(x, w_bf[0]) + bg_ref[...], SWIGLU_LIMIT)
        u = jnp.clip(_dot(x, w_bf[1]) + bu_ref[...], -SWIGLU_LIMIT, SWIGLU_LIMIT)
        a = g * jax.nn.sigmoid(SWIGLU_ALPHA * g) * (u + 1.0)
        y_ref[...] = _dot(a.astype(BF16), w_bf[2]) + bd_ref[...]

    @pl.when(jnp.logical_not(valid))
    def _():
        y_ref[...] = jnp.zeros_like(y_ref)


def _moe_experts(x_rows, block_e, next_e, n_valid, w_gate, b_gate, w_up, b_up, w_down, b_down):
    tm = TM_MOE
    n_rows = x_rows.shape[0]
    wspec = pl.BlockSpec(memory_space=pl.ANY)
    bspec = pl.BlockSpec((None, 1, D_MODEL), lambda i, be, ne, nv: (be[i], 0, 0))
    rows = pl.BlockSpec((tm, D_MODEL), lambda i, be, ne, nv: (i, 0))
    grid_spec = pltpu.PrefetchScalarGridSpec(
        num_scalar_prefetch=3,
        grid=(n_rows // tm,),
        in_specs=[rows, wspec, bspec, wspec, bspec, wspec, bspec],
        out_specs=rows,
        scratch_shapes=[pltpu.VMEM((3, D_MODEL, D_MODEL), BF16), pltpu.VMEM((3, D_MODEL, D_MODEL), F32),
                        pltpu.SemaphoreType.DMA((3,))],
    )
    b3 = lambda b: b.reshape(N_EXPERTS, 1, D_MODEL)
    return pl.pallas_call(
        _moe_kernel,
        grid_spec=grid_spec,
        out_shape=jax.ShapeDtypeStruct((n_rows, D_MODEL), F32),
        compiler_params=_cparams("arbitrary"),
        name="moe_experts",
    )(block_e, next_e, n_valid, x_rows, w_gate, b3(b_gate), w_up, b3(b_up), w_down, b3(b_down))


def _ple_kernel(h1_ref, rows_ref, tw_ref, p_ref, gple_ref, wg_ref, wp_ref, gfin_ref, y_ref):
    tw = tw_ref[...]
    h2 = h1_ref[...]
    for k in range(TOP_K):
        h2 = h2 + rows_ref[k] * tw[:, k:k + 1]
    gate =jax.nn.sigmoid(_dot(_rms(h2, gple_ref[...]).astype(BF16), wg_ref[...]))
    h3 = h2 + gate * _dot(p_ref[...].astype(BF16), wp_ref[...])
    y_ref[...] = _rms(h3, gfin_ref[...])


def _ple_final(h1, expert_rows, top_w, p, row_block_offset, g_ple, w_ple_gate, w_ple_proj, g_final):
    tm = TM_TOKENS
    n = p.shape[0]
    off = row_block_offset // tm
    src = lambda width: pl.BlockSpec((tm, width), lambda i: (off + i, 0))
    loc = lambda width: pl.BlockSpec((tm, width), lambda i: (i, 0))
    return pl.pallas_call(
        _ple_kernel,
        grid=(n // tm,),
        in_specs=[src(D_MODEL), pl.BlockSpec((TOP_K, tm, D_MODEL), lambda i: (0, off + i, 0)), src(LANES),
                  loc(D_PLE), _full((1, D_MODEL)), _full((D_MODEL, D_MODEL)), _full((D_PLE, D_MODEL)),
                  _full((1, D_MODEL))],
        out_specs=loc(D_MODEL),
        out_shape=jax.ShapeDtypeStruct((n, D_MODEL), F32),
        compiler_params=_cparams("arbitrary"),
        name="ple_final",
    )(h1, expert_rows, top_w, p, g_ple.reshape(1, D_MODEL), w_ple_gate.astype(BF16),
      w_ple_proj.astype(BF16), g_final.reshape(1, D_MODEL))


def kernel(x_prompt, x_sample, p_prompt, p_sample, state_ssm_re, state_ssm_im, state_ml_c, state_ml_n, state_ml_m, g_mix, w_in, ssm_a_re, ssm_a_im, ssm_log_dt, ssm_b_re, ssm_b_im, ssm_c_re, ssm_c_im, ssm_d, ssm_w_glu, ssm_b_glu, ml_b_ig, ml_b_fg, w_ssm_up, w_ml_up, w_out, g_ffn, w_router, b_router, w_gate, b_gate, w_up, b_up, w_down, b_down, g_ple, w_ple_gate, w_ple_proj, g_final):
    assert g_mix.shape[0] == 1, "single-layer trunk"
    bp, lp, _ = x_prompt.shape
    bs, ls, _ = x_sample.shape
    tp, ts = bp * lp, bs * ls
    t = tp + ts
    xp = x_prompt.reshape(tp, D_MODEL)
    xs = x_sample.reshape(ts, D_MODEL)

    u, qkv, o, gates, gsm = _inproj(xp, xs, g_mix[0], w_in[0])

    s5_args = (ssm_a_re[0], ssm_a_im[0], ssm_log_dt[0], ssm_b_re[0], ssm_b_im[0], ssm_c_re[0],
               ssm_c_im[0], ssm_d[0])
    zero_state = jnp.zeros((S5_TILES, bp, 2 * S5_TILE_STATE), F32)
    tables = {tc: _s5_tables(*s5_args, tc) for tc in {S5_CHUNK, ls}}
    y_p, f_p = _s5(u, 0, tables[S5_CHUNK], zero_state,
                   tc=S5_CHUNK, n_chunks=lp // S5_CHUNK, nb=bp // 2, n_splits=2)
    y_s, f_s = _s5(u, tp, tables[ls], _s5_state_to_tiles(state_ssm_re[0], state_ssm_im[0]),
                   tc=ls, n_chunks=1, nb=bs, n_splits=1)
    re_p, im_p = _s5_state_from_tiles(f_p)
    re_s, im_s = _s5_state_from_tiles(f_s)

    bias_row = jnp.pad(jnp.concatenate([ml_b_ig[0], ml_b_fg[0]]), (0, GATE_PAD - 2 * ML_HEADS)).reshape(1, GATE_PAD)
    hm_p, c_p, n_p, m_p = _mlstm_prompt(qkv, o, gates, bias_row, bp, lp)
    m0 = jnp.broadcast_to(state_ml_m[0][:, :, None, None], (bs, ML_HEADS, 1, LANES))
    hm_s, c_s, n_s, m_s = _mlstm_sample(qkv, o, gates, bias_row, state_ml_c[0],
                                        state_ml_n[0].reshape(bs, ML_HEADS, 1, ML_DK), m0, tp, bs, ls)

    h1, xn, logits = _merge(xp, xs, (y_p, y_s), (hm_p, hm_s), gsm, ssm_w_glu[0], ssm_b_glu[0], w_ssm_up[0], w_ml_up[0],
                            w_out[0], g_ffn[0], w_router[0], b_router[0])

    top_w_pad, pos, fill_lo, pad_end, n_rows, block_e, next_e, n_valid = _route(logits[:, :N_EXPERTS], TM_MOE)
    x_rows = _dispatch(xn, pos, fill_lo, pad_end, n_rows)
    expert_w = lambda w: w.reshape(N_EXPERTS, D_MODEL, D_MODEL)
    y_rows = _moe_experts(x_rows, block_e, next_e, n_valid, expert_w(w_gate), b_gate[0], expert_w(w_up),
                          b_up[0], expert_w(w_down), b_down[0])
    expert_rows = y_rows[pos.T.reshape(-1)].reshape(TOP_K, t, D_MODEL)

    ple_w = (g_ple[0], w_ple_gate[0], w_ple_proj[0], g_final)
    y_prompt = _ple_final(h1, expert_rows, top_w_pad, p_prompt[0].reshape(tp, D_PLE), 0, *ple_w)
    y_sample = _ple_final(h1, expert_rows, top_w_pad, p_sample[0].reshape(ts, D_PLE), tp, *ple_w)

    return (y_prompt.reshape(bp, lp, D_MODEL), y_sample.reshape(bs, ls, D_MODEL),
            re_p, im_p, c_p[None], n_p.reshape(1, bp, ML_HEADS, ML_DK), m_p[:, :, 0, 0][None],
            re_s, im_s, c_s[None], n_s.reshape(1, bs, ML_HEADS, ML_DK), m_s[:, :, 0, 0][None])
```

```python
import functools

import jax
import jax.numpy as jnp
from jax import lax
from jax.experimental import pallas as pl
from jax.experimental.pallas import tpu as pltpu

F32 = jnp.float32
BF16 = jnp.bfloat16
HIGHEST = lax.Precision.HIGHEST

D_MODEL = 1024
D_SSM = 512
SSM_GROUP = 16
N_GROUPS = 32
SSM_STATE = 64
ML_HEADS = 4
ML_DK = 128
D_ML = 512
N_EXPERTS = 32
TOP_K = 4
SWIGLU_LIMIT = 7.0
SWIGLU_ALPHA = 1.702
D_PLE = 256
RMS_EPS = 1e-6

LANES = 128
GATE_PAD = LANES
S5_CHUNK = 8
ML_CHUNK_PROMPT = 256
TM_TOKENS = 256
TM_MOE = 256
VMEM_LIMIT = 56 * 1024 * 1024


def _cparams(*sem):
    return pltpu.CompilerParams(dimension_semantics=sem, vmem_limit_bytes=VMEM_LIMIT)


def _rms(x, g):
    return x * lax.rsqrt(jnp.mean(x * x, axis=-1, keepdims=True) + RMS_EPS) * g


def _dot(a, b):
    return jnp.dot(a, b, preferred_element_type=F32)


def _dot_hi(a, b):
    return jnp.dot(a, b, preferred_element_type=F32, precision=HIGHEST)


def _full(shape):
    n = len(shape)
    return pl.BlockSpec(shape, lambda *_: (0,) * n)


def _inproj_kernel(xp_ref, xs_ref, g_ref, wu_ref, wqkv_ref, wo_ref, wgt_ref, wgsm_ref,
                   u_ref, qkv_ref, o_ref, gt_ref, gsm_ref, *, n_prompt_blocks):
    i = pl.program_id(0)
    x = jnp.where(i < n_prompt_blocks, xp_ref[...], xs_ref[...])
    hn = _rms(x, g_ref[...]).astype(BF16)
    u_ref[...] = _dot(hn, wu_ref[...])
    qkv = _dot(hn, wqkv_ref[...])
    col = lax.broadcasted_iota(jnp.int32, (1, 3 * D_ML), 1)
    k_scale = jnp.where((col >= D_ML) & (col < 2 * D_ML), ML_DK ** -0.5, 1.0).astype(F32)
    qkv_ref[...] = (qkv * k_scale).astype(BF16)
    o_ref[...] = _dot(hn, wo_ref[...])
    gt_ref[...] = _dot(hn, wgt_ref[...])
    gsm_ref[...] = _dot(hn, wgsm_ref[...])


def _two_source_specs(tm, width, n_prompt_blocks):
    last = n_prompt_blocks - 1
    return (pl.BlockSpec((tm, width), lambda i: (jnp.minimum(i, last), 0)),
            pl.BlockSpec((tm, width), lambda i: (jnp.maximum(i - n_prompt_blocks, 0), 0)))


def _inproj(xp, xs, g_mix, w_in):
    tm = TM_TOKENS
    tp, ts = xp.shape[0], xs.shape[0]
    t = tp + ts
    npb = tp // tm
    w = w_in.astype(BF16)
    o0 = D_SSM
    wu = w[:, :o0]
    wqkv = w[:, o0:o0 + 3 * D_ML]
    wo = w[:, o0 + 3 * D_ML:o0 + 4 * D_ML]
    g0 = o0 + 4 * D_ML
    wgt = jnp.pad(w[:, g0:g0 + 2 * ML_HEADS], ((0, 0), (0, GATE_PAD - 2 * ML_HEADS)))
    wgsm = w[:, g0 + 2 * ML_HEADS:]
    xp_spec, xs_spec = _two_source_specs(tm, D_MODEL, npb)
    outs = (jax.ShapeDtypeStruct((t, D_SSM), F32), jax.ShapeDtypeStruct((t, 3 * D_ML), BF16),
            jax.ShapeDtypeStruct((t, D_ML), F32), jax.ShapeDtypeStruct((t, GATE_PAD), F32),
            jax.ShapeDtypeStruct((t, 2 * D_MODEL), F32))
    row = lambda width: pl.BlockSpec((tm, width), lambda i: (i, 0))
    return pl.pallas_call(
        functools.partial(_inproj_kernel, n_prompt_blocks=npb),
        grid=(t // tm,),
        in_specs=[xp_spec, xs_spec, _full((1, D_MODEL)), _full(wu.shape), _full(wqkv.shape),
                  _full(wo.shape), _full(wgt.shape), _full(wgsm.shape)],
        out_specs=[row(D_SSM), row(3 * D_ML), row(D_ML), row(GATE_PAD), row(2 * D_MODEL)],
        out_shape=outs,
        compiler_params=_cparams("arbitrary"),
        name="inproj",
    )(xp, xs, g_mix.reshape(1, D_MODEL), wu, wqkv, wo, wgt, wgsm)


S5_TILES = D_SSM // LANES
S5_TILE_GROUPS = LANES // SSM_GROUP
S5_TILE_STATE = S5_TILE_GROUPS * SSM_STATE


def _block_diag_tiles(x):
    gt = S5_TILE_GROUPS
    n, _, r, c = x.shape
    x5 = x.reshape(n, S5_TILES, gt, r, c)
    eye = jnp.eye(gt, dtype=x.dtype)
    return (x5[:, :, :, :, None, :] * eye[None, None, :, None, :, None]).reshape(n, S5_TILES, gt * r, gt * c)


def _s5_tables(a_re, a_im, log_dt, b_re, b_im, c_re, c_im, d_skip, tc):
    ein = functools.partial(jnp.einsum, precision=HIGHEST)
    dt = jnp.exp(log_dt)[:, None]
    mag = jnp.exp(a_re * dt)
    abar_r, abar_i = mag * jnp.cos(a_im * dt), mag * jnp.sin(a_im * dt)
    den = a_re * a_re + a_im * a_im
    nr, ni = abar_r - 1.0, abar_i
    coef_r = (nr * a_re + ni * a_im) / den
    coef_i = (ni * a_re - nr * a_im) / den
    bbar_r = coef_r[..., None] * b_re - coef_i[..., None] * b_im
    bbar_i = coef_r[..., None] * b_im + coef_i[..., None] * b_re

    def abar_pow(j):
        jj = j[..., None, None]
        mag_j = jnp.where(jj >= 0, jnp.exp(jj * (a_re * dt)), 0.0)
        return mag_j * jnp.cos(jj * (a_im * dt)), mag_j * jnp.sin(jj * (a_im * dt))

    half = tc // 2
    steps = jnp.arange(tc, dtype=F32)
    at_r, at_i = abar_pow(jnp.full((), tc, F32))
    lags = (2.0 * jnp.arange(half, dtype=F32)[:, None, None]
            + jnp.array([[0.0, 1.0], [-1.0, 0.0]], F32)[None])
    lag_r, lag_i = abar_pow(lags.reshape(-1))
    ab_r = lag_r[..., None] * bbar_r - lag_i[..., None] * bbar_i
    ab_i = lag_r[..., None] * bbar_i + lag_i[..., None] * bbar_r
    kern = ein('ghp,jgpk->jgkh', c_re, ab_r) - ein('ghp,jgpk->jgkh', c_im, ab_i)
    bd_lag = _block_diag_tiles(kern.astype(BF16)).reshape(half, 2, 2, S5_TILES, LANES, LANES)
    toe = jnp.transpose(bd_lag, (3, 0, 1, 4, 2, 5)).reshape(S5_TILES, half, 2 * LANES, 2 * LANES)
    rev_r, rev_i = abar_pow(tc - 1.0 - steps)
    s_r = jnp.transpose(rev_r[..., None] * bbar_r - rev_i[..., None] * bbar_i, (0, 1, 3, 2))
    s_i = jnp.transpose(rev_r[..., None] * bbar_i + rev_i[..., None] * bbar_r, (0, 1, 3, 2))
    a1_r, a1_i = abar_pow(steps + 1.0)
    p_r = c_re[None] * a1_r[:, :, None, :] - c_im[None] * a1_i[:, :, None, :]
    p_i = -c_re[None] * a1_i[:, :, None, :] - c_im[None] * a1_r[:, :, None, :]

    def compact(x):
        x = x.astype(BF16).reshape(half, 2, S5_TILES, S5_TILE_GROUPS, SSM_GROUP, SSM_STATE)
        x = jnp.transpose(x, (2, 0, 1, 3, 4, 5)).reshape(S5_TILES, half, 2 * LANES, SSM_STATE)
        return jnp.concatenate([x, x], axis=3)

    s_tab = jnp.stack([compact(s_r), compact(s_i)], axis=2)
    p_tab = jnp.stack([compact(p_r), compact(p_i)], axis=2)
    a_tab =jnp.stack([at_r.reshape(S5_TILES, S5_TILE_STATE), at_i.reshape(S5_TILES, S5_TILE_STATE)], axis=1)
    d_tab = d_skip.reshape(S5_TILES, 1, LANES)
    return toe, s_tab, p_tab, a_tab, d_tab


def _s5_kernel(u_ref, t_ref, s_ref, p_ref, a_ref, d_ref, h0_ref, y_ref, f_ref, loc, xprev,
               *, tc, n_chunks, nb):
    r = n_chunks * nb
    half = tc // 2
    ns = S5_TILE_STATE
    step_rows = lambda t: pl.ds(t, r, stride=tc)
    v = [u_ref[step_rows(t), :] for t in range(tc)]
    vp = [jnp.concatenate([v[2 * a].astype(BF16), v[2 * a + 1].astype(BF16)], axis=1) for a in range(half)]

    row_group = (lax.broadcasted_iota(jnp.int32, (2 * LANES, ns), 0) >> 4) & (S5_TILE_GROUPS - 1)
    col_group = lax.broadcasted_iota(jnp.int32, (2 * LANES, ns), 1) >> 6
    own_group = jnp.where(row_group == col_group, 1.0, 0.0).astype(BF16)

    def block_diag(tab):
        reps = ns // LANES
        return jnp.concatenate([jnp.tile(tab[0], (1, reps)) * own_group, jnp.tile(tab[1], (1, reps)) * own_group],
                               axis=1)

    acc = _dot(vp[0], block_diag(s_ref[0]))
    for a in range(1, half):
        acc = acc + _dot(vp[a], block_diag(s_ref[a]))
    nt = ns // LANES
    lane_tile = lambda k: slice(k * LANES, (k + 1) * LANES)
    for k in range(2 * nt):
        loc[k] = acc[:, lane_tile(k)]
    abar = a_ref[...]

    def body(c, carry):
        rows = pl.ds(c, nb, stride=n_chunks) if n_chunks > 1 else pl.ds(0, nb)
        new = []
        for k in range(nt):
            xr, xi = carry[k], carry[nt + k]
            xprev[k, rows, :] = xr
            xprev[nt + k, rows, :] = xi
            ar, ai = abar[0:1, lane_tile(k)], abar[1:2, lane_tile(k)]
            new.append((ar * xr - ai * xi + loc[k, rows, :], ar * xi + ai * xr + loc[nt + k, rows, :]))
        return tuple(n[0] for n in new) + tuple(n[1] for n in new)

    x_end = lax.fori_loop(0, n_chunks, body, tuple(h0_ref[:, lane_tile(k)] for k in range(2 * nt)),
                          unroll=min(4, n_chunks))
    for k in range(2 * nt):
        f_ref[:, lane_tile(k)] = x_end[k]
    xp = jnp.concatenate([xprev[k] for k in range(2 * nt)], axis=1).astype(BF16)
    d = d_ref[...]
    nt_dims = (((1,), (1,)), ((), ()))
    for a2 in range(half):
        acc = lax.dot_general(xp, block_diag(p_ref[a2]), nt_dims, preferred_element_type=F32)
        for a in range(a2 + 1):
            acc = acc + _dot(vp[a], t_ref[a2 - a])
        for k in range(2):
            t = 2 * a2 + k
            y_ref[step_rows(t), :] = acc[:, k * LANES:(k + 1) * LANES] + v[t] * d


def _s5(u, row_block_offset, tables, h0, *, tc, n_chunks, nb, n_splits):
    toe, s_tab, p_tab, a_tab, d_tab = tables
    rows = nb * n_chunks * tc
    off = row_block_offset // rows
    half = tc // 2
    ns2 = 2 * S5_TILE_STATE
    tile = lambda *tail: pl.BlockSpec((None,) + tail, lambda j, s: (j,) + (0,) * len(tail))
    state = pl.BlockSpec((None, None, nb, ns2), lambda j, s: (j, s, 0, 0))
    y, f = pl.pallas_call(
        functools.partial(_s5_kernel, tc=tc, n_chunks=n_chunks, nb=nb),
        grid=(S5_TILES, n_splits),
        in_specs=[pl.BlockSpec((rows, LANES), lambda j, s: (off + s, j)),
                  tile(half, 2 * LANES, 2 * LANES), tile(half, 2, 2 * LANES, LANES), tile(half, 2, 2 * LANES, LANES),
                  tile(2, S5_TILE_STATE), tile(1, LANES), state],
        out_specs=[pl.BlockSpec((rows, LANES), lambda j, s: (s, j)), state],
        out_shape=(jax.ShapeDtypeStruct((rows * n_splits, D_SSM), F32),
                   jax.ShapeDtypeStruct((S5_TILES, n_splits, nb, ns2), F32)),
        scratch_shapes=[pltpu.VMEM((ns2 // LANES, nb * n_chunks, LANES), F32)] * 2,
        compiler_params=_cparams("arbitrary", "arbitrary"),
        name=f"s5_c{n_chunks}",
    )(u, toe, s_tab, p_tab, a_tab, d_tab, h0.reshape(S5_TILES, n_splits, nb, ns2))
    return y, f.reshape(S5_TILES, n_splits * nb, ns2)


def _s5_state_to_tiles(s_re, s_im):
    b = s_re.shape[0]
    f = lambda s: s.reshape(b, S5_TILES, S5_TILE_STATE).transpose(1, 0, 2)
    return jnp.concatenate([f(s_re), f(s_im)], axis=2)


def _s5_state_from_tiles(f):
    b = f.shape[1]
    g = lambda s: s.transpose(1, 0, 2).reshape(1, b, N_GROUPS, SSM_STATE)
    return g(f[:, :, :S5_TILE_STATE]), g(f[:, :, S5_TILE_STATE:])


def _log_sigmoid(x):
    return jnp.minimum(x, 0.0) - jnp.log1p(jnp.exp(-jnp.abs(x)))


def _mlstm_gates(gates, bias_row, lc):
    g = gates + bias_row
    col = lax.broadcasted_iota(jnp.int32, (1, GATE_PAD), 1)
    gl = jnp.where(col >= ML_HEADS, _log_sigmoid(g), g)
    r = lax.broadcasted_iota(jnp.int32, (lc, lc), 0)
    c = lax.broadcasted_iota(jnp.int32, (lc, lc), 1)
    tril = (r >= c).astype(F32)
    bcols = _dot_hi(tril, gl)
    sel = (lax.broadcasted_iota(jnp.int32, (8, GATE_PAD), 0)
           == lax.broadcasted_iota(jnp.int32, (8, GATE_PAD), 1)).astype(F32)
    nt = (((1,), (1,)), ((), ()))
    grows = lax.dot_general(sel, gl, nt, precision=HIGHEST, preferred_element_type=F32)
    brows = lax.dot_general(sel, bcols, nt, precision=HIGHEST, preferred_element_type=F32)
    return gl, bcols, grows, brows, (r >= c)


def _mlstm_chunks(seqs, bias_row, lc):
    nt = (((1,), (1,)), ((), ()))
    tn = (((0,), (0,)), ((), ()))
    pairs = [(si, hd) for si in range(len(seqs)) for hd in range(ML_HEADS)]
    head = lambda x, part, hd: x[:, part * D_ML + hd * ML_DK:part * D_ML + (hd + 1) * ML_DK]
    q = {p: head(seqs[p[0]][0], 0, p[1]) for p in pairs}
    k = {p: head(seqs[p[0]][0], 1, p[1]) for p in pairs}
    v = {p: head(seqs[p[0]][0], 2, p[1]) for p in pairs}
    state = lambda p: seqs[p[0]][3](p[1])
    qk = {p: lax.dot_general(q[p], k[p], nt, preferred_element_type=F32) for p in pairs}
    qc = {p: _dot(q[p], state(p)[0].astype(BF16)) for p in pairs}
    tables = [_mlstm_gates(seq[2], bias_row, lc) for seq in seqs]
    w_in, w_out, mt, m_end, decay, kw = {}, {}, {}, {}, {}, {}
    for p in pairs:
        si, hd = p
        gl, bcols, grows, brows, causal = tables[si]
        f = ML_HEADS + hd
        ic, bc, ir, br = gl[:, hd:hd + 1], bcols[:, f:f + 1], grows[hd:hd + 1, :], brows[f:f + 1, :]
        m_state = state(p)[2]
        dmat = jnp.where(causal, bc - br + ir, -jnp.inf)
        inter = bc + m_state
        mt[p] = jnp.maximum(inter, jnp.max(dmat, axis=1, keepdims=True))
        w_in[p] = jnp.exp(dmat - mt[p])
        w_out[p] = jnp.exp(inter - mt[p])
        m_end[p] = mt[p][lc - 1:lc, :]
        b_last = bc[lc - 1:lc, :]
        decay[p] = jnp.exp(b_last + m_state - m_end[p])
        kw[p] = k[p].astype(F32) * jnp.exp(b_last - bc + ic - m_end[p])
    kv = {p: lax.dot_general(kw[p].astype(BF16), v[p], tn, preferred_element_type=F32) for p in pairs}
    s = {p: qk[p] * w_in[p] for p in pairs}
    sv = {p: _dot(s[p].astype(BF16), v[p]) for p in pairs}
    outs = [[] for _ in seqs]
    for p in pairs:
        si, hd = p
        c_state, n_state, _ = state(p)
        num = w_out[p] * qc[p] + sv[p]
        qn = (w_out[p] * jnp.sum(q[p].astype(F32) * n_state, axis=1, keepdims=True)
              + jnp.sum(s[p], axis=1, keepdims=True))
        h = num / jnp.maximum(jnp.abs(qn), jnp.exp(-mt[p]))
        seqs[si][4](hd, decay[p] * c_state + kv[p], decay[p] * n_state + jnp.sum(kw[p], axis=0, keepdims=True),
                    m_end[p])
        outs[si].append(jax.nn.sigmoid(head(seqs[si][1], 0, hd)) * h)
    return [jnp.concatenate(o, axis=1) for o in outs]


def _mlstm_prompt_kernel(qkv_ref, o_ref, gt_ref, bias_ref, h_ref, c_ref, n_ref, m_ref, *, lc):
    @pl.when(pl.program_id(1) == 0)
    def _():
        c_ref[...] = jnp.zeros_like(c_ref)
        n_ref[...] = jnp.zeros_like(n_ref)
        m_ref[...] = jnp.zeros_like(m_ref)

    def get_state(hd):
        return c_ref[hd], n_ref[hd], m_ref[hd][:, 0:1]

    def put_state(hd, c_new, n_new, m_new):
        c_ref[hd] = c_new
        n_ref[hd] = n_new
        m_ref[hd] = jnp.broadcast_to(m_new, (1, LANES))

    (h,) = _mlstm_chunks([(qkv_ref[...], o_ref[...], gt_ref[...], get_state, put_state)], bias_ref[...], lc)
    h_ref[...] = h.astype(BF16)


def _mlstm_prompt(qkv, o, gates, bias_row, bsz, seq):
    lc = ML_CHUNK_PROMPT
    nc = seq // lc
    row = lambda width: pl.BlockSpec((lc, width), lambda b, c: (b * nc + c, 0))
    st = lambda *tail: pl.BlockSpec((None, ML_HEADS) + tail, lambda b, c: (b, 0) + (0,) * len(tail))
    return pl.pallas_call(
        functools.partial(_mlstm_prompt_kernel, lc=lc),
        grid=(bsz, nc),
        in_specs=[row(3 * D_ML), row(D_ML), row(GATE_PAD), pl.BlockSpec((1, GATE_PAD), lambda b, c: (0, 0))],
        out_specs=[row(D_ML), st(ML_DK, ML_DK), st(1, ML_DK), st(1, LANES)],
        out_shape=(jax.ShapeDtypeStruct((bsz * seq, D_ML), BF16),
                   jax.ShapeDtypeStruct((bsz, ML_HEADS, ML_DK, ML_DK), F32),
                   jax.ShapeDtypeStruct((bsz, ML_HEADS, 1, ML_DK), F32),
                   jax.ShapeDtypeStruct((bsz, ML_HEADS, 1, LANES), F32)),
        compiler_params=_cparams("arbitrary", "arbitrary"),
        name="mlstm_prompt",
    )(qkv, o, gates, bias_row)


def _mlstm_sample_kernel(qkv_ref, o_ref, gt_ref, bias_ref, c0_ref, n0_ref, m0_ref,
                         h_ref, c_ref, n_ref, m_ref, qkv_s, *, lc, nb):
    qkv_s[...] = qkv_ref[...].astype(F32)

    def seq(b):
        rows = slice(b * lc, (b + 1) * lc)

        def get_state(hd):
            return c0_ref[b, hd], n0_ref[b, hd], m0_ref[b, hd][:, 0:1]

        def put_state(hd, c_new, n_new, m_new):
            c_ref[b, hd] = c_new
            n_ref[b, hd] = n_new
            m_ref[b, hd] = jnp.broadcast_to(m_new, (1, LANES))

        return (qkv_s[rows, :].astype(BF16), o_ref[rows, :], gt_ref[rows, :], get_state, put_state)

    hs = _mlstm_chunks([seq(b) for b in range(nb)], bias_ref[...], lc)
    for b, h in enumerate(hs):
        h_ref[b * lc:(b + 1) * lc, :] = h


def _mlstm_sample(qkv, o, gates, bias_row, c0, n0, m0, row_block_offset, bsz, seq):
    nb = 8
    lc = seq
    rows = nb * lc
    off = row_block_offset // rows
    row = lambda width: pl.BlockSpec((rows, width), lambda i: (off + i, 0))
    st = lambda *tail: pl.BlockSpec((nb, ML_HEADS) + tail, lambda i: (i, 0) + (0,) * len(tail))
    return pl.pallas_call(
        functools.partial(_mlstm_sample_kernel, lc=lc, nb=nb),
        grid=(bsz // nb,),
        in_specs=[row(3 * D_ML), row(D_ML), row(GATE_PAD), pl.BlockSpec((1, GATE_PAD), lambda i: (0, 0)),
                  st(ML_DK, ML_DK), st(1, ML_DK), st(1, LANES)],
        out_specs=[pl.BlockSpec((rows, D_ML), lambda i: (i, 0)), st(ML_DK, ML_DK), st(1, ML_DK), st(1, LANES)],
        out_shape=(jax.ShapeDtypeStruct((bsz * seq, D_ML), F32),
                   jax.ShapeDtypeStruct((bsz, ML_HEADS, ML_DK, ML_DK), F32),
                   jax.ShapeDtypeStruct((bsz, ML_HEADS, 1, ML_DK), F32),
                   jax.ShapeDtypeStruct((bsz, ML_HEADS, 1, LANES), F32)),
        scratch_shapes=[pltpu.VMEM((rows, 3 * D_ML), F32)],
        compiler_params=_cparams("arbitrary"),
        name="mlstm_sample",
    )(qkv, o, gates, bias_row, c0, n0, m0)


def _merge_kernel(xp_ref, xs_ref, yp_ref, ys_ref, mp_ref, ms_ref, gsm_ref, wglu_ref, bglu_ref, wsu_ref, wmu_ref,
                  wout_ref, gffn_ref, wrh_ref, wrl_ref, br_ref, h1_ref, xn_ref, lg_ref, *, n_prompt_blocks):
    i = pl.program_id(0)
    is_prompt = i < n_prompt_blocks
    x = jnp.where(is_prompt, xp_ref[...], xs_ref[...])
    y = jax.nn.gelu(jnp.where(is_prompt, yp_ref[...], ys_ref[...]))
    ym = jnp.where(is_prompt, mp_ref[...], ms_ref[...].astype(BF16))
    ys = y * jax.nn.sigmoid(_dot(y.astype(BF16), wglu_ref[...]) + bglu_ref[...])
    gsm = gsm_ref[...]
    merged = (jax.nn.sigmoid(gsm[:, :D_MODEL]) * _dot(ys.astype(BF16), wsu_ref[...])
              + jax.nn.sigmoid(gsm[:, D_MODEL:]) * _dot(ym, wmu_ref[...]))
    h1 = x + _dot(merged.astype(BF16), wout_ref[...])
    h1_ref[...] = h1
    xn = _rms(h1, gffn_ref[...])
    xn_ref[...] = xn
    xn_hi = xn.astype(BF16)
    xn_lo = (xn - xn_hi.astype(F32)).astype(BF16)
    lg_ref[...] = (_dot(xn_hi, wrh_ref[...]) + _dot(xn_lo, wrh_ref[...]) + _dot(xn_hi, wrl_ref[...])
                   + br_ref[...])


def _merge(xp, xs, y_pre, ym, gsm, w_glu, b_glu, w_ssm_up, w_ml_up, w_out, g_ffn, w_router, b_router):
    tm = TM_TOKENS
    t = gsm.shape[0]
    npb = xp.shape[0] // tm
    xp_spec, xs_spec = _two_source_specs(tm, D_MODEL, npb)
    yp_spec, ys_spec = _two_source_specs(tm, D_SSM, npb)
    mp_spec, ms_spec = _two_source_specs(tm, D_ML, npb)
    row = lambda width: pl.BlockSpec((tm, width), lambda i: (i, 0))
    wr = jnp.pad(w_router, ((0, 0), (0, LANES - N_EXPERTS)))
    wr_hi = wr.astype(BF16)
    wr_lo = (wr - wr_hi.astype(F32)).astype(BF16)
    br =jnp.pad(b_router, (0, LANES - N_EXPERTS)).reshape(1, LANES)
    return pl.pallas_call(
        functools.partial(_merge_kernel, n_prompt_blocks=npb),
        grid=(t // tm,),
        in_specs=[xp_spec, xs_spec, yp_spec, ys_spec, mp_spec, ms_spec, row(2 * D_MODEL), _full((D_SSM, D_SSM)),
                  _full((1, D_SSM)), _full((D_SSM, D_MODEL)), _full((D_ML, D_MODEL)),
                  _full((D_MODEL, D_MODEL)), _full((1, D_MODEL)), _full((D_MODEL, LANES)),
                  _full((D_MODEL, LANES)), _full((1, LANES))],
        out_specs=[row(D_MODEL), row(D_MODEL), row(LANES)],
        out_shape=(jax.ShapeDtypeStruct((t, D_MODEL), F32), jax.ShapeDtypeStruct((t, D_MODEL), F32),
                   jax.ShapeDtypeStruct((t, LANES), F32)),
        compiler_params=_cparams("arbitrary"),
        name="merge",
    )(xp, xs, *y_pre, *ym, gsm, w_glu.astype(BF16), b_glu.reshape(1, D_SSM), w_ssm_up.astype(BF16),
      w_ml_up.astype(BF16), w_out.astype(BF16), g_ffn.reshape(1, D_MODEL), wr_hi, wr_lo, br)


RANK_BLOCK = 256


def _route(logits, tm):
    t = logits.shape[0]
    top_val, top_idx = lax.top_k(logits, TOP_K)
    top_w = jax.nn.softmax(top_val, axis=-1)
    flat_e = top_idx.reshape(-1)
    n_assign = t * TOP_K
    onehot = (flat_e[:, None] == jnp.arange(N_EXPERTS, dtype=jnp.int32)[None, :])
    rb = RANK_BLOCK
    oh3 = onehot.astype(F32).reshape(n_assign // rb, rb, N_EXPERTS)
    tril = jnp.tril(jnp.ones((rb, rb), F32))
    within = jnp.einsum('ij,bjk->bik', tril, oh3).astype(jnp.int32)
    totals = within[:, -1, :]
    before = jnp.cumsum(totals, axis=0) - totals
    csum = (within + before[:, None, :]).reshape(n_assign, N_EXPERTS)
    rank = jnp.sum(jnp.where(onehot, csum, 0), axis=1) - 1
    counts = before[-1] + totals[-1]
    padded = (counts + tm - 1) // tm * tm
    pad_end = jnp.cumsum(padded)
    pad_start = pad_end - padded
    pos = (jnp.sum(jnp.where(onehot, pad_start[None, :], 0), axis=1) + rank).reshape(t, TOP_K)
    n_blocks = -(-(n_assign + N_EXPERTS * (tm - 1)) // tm)
    block_row0 = jnp.arange(n_blocks, dtype=jnp.int32) * tm
    block_e = jnp.minimum(jnp.sum((pad_end[None, :] <= block_row0[:, None]).astype(jnp.int32), axis=1),
                          N_EXPERTS - 1)
    n_valid = (pad_end[-1] // tm).astype(jnp.int32).reshape(1)
    ids = jnp.arange(N_EXPERTS, dtype=jnp.int32)
    later = (ids[None, :] > ids[:, None]) & (counts[None, :] > 0)
    next_e = jnp.min(jnp.where(later, ids[None, :], N_EXPERTS), axis=1)
    next_e = jnp.where(next_e == N_EXPERTS, -1, next_e)[block_e]
    fill_lo = pad_start + counts
    top_w_pad = jnp.pad(top_w, ((0, 0), (0, LANES - TOP_K)))
    return top_w_pad, pos, fill_lo, pad_end, n_blocks * tm, block_e, next_e, n_valid


def _dispatch_copy(x_ref, rows_hbm, sem, r, dst):
    return pltpu.make_async_copy(x_ref.at[pl.ds(r, 1), :], rows_hbm.at[pl.ds(dst, 1), :], sem)


def _dispatch_kernel(lo_ref, hi_ref, pos_ref, x_ref, rows_hbm, zero_s, sem, zero_sem):
    tm = x_ref.shape[0]

    @pl.when(pl.program_id(0) == 0)
    def _():
        zero_s[...] = jnp.zeros_like(zero_s)

        def tail(do):
            def body(b, c):
                rows = pl.ds(pl.multiple_of(b * tm, tm), tm)
                do(pltpu.make_async_copy(zero_s, rows_hbm.at[rows, :], zero_sem))
                return c
            lax.fori_loop(hi_ref[N_EXPERTS - 1] // tm, rows_hbm.shape[0] // tm, body, 0)

        tail(lambda c: c.start())
        tail(lambda c: c.wait())

        def fill(e, carry):
            def each(do):
                def body(r, c):
                    do(_dispatch_copy(zero_s, rows_hbm, zero_sem, 0, r))
                    return c
                lax.fori_loop(lo_ref[e], hi_ref[e], body, 0)
            each(lambda c: c.start())
            each(lambda c: c.wait())
            return carry

        lax.fori_loop(0, N_EXPERTS, fill, 0)

    def each(do):
        for r in range(tm):
            for k in range(TOP_K):
                do(_dispatch_copy(x_ref, rows_hbm, sem, r, pos_ref[r * TOP_K + k]), k)

    each(lambda c, k: c.start(priority=k % 2))
    each(lambda c, k: c.wait())


def _dispatch(xn, pos, fill_lo, pad_end, n_rows):
    tm = TM_TOKENS
    t = xn.shape[0]
    grid_spec = pltpu.PrefetchScalarGridSpec(
        num_scalar_prefetch=2,
        grid=(t // tm,),
        in_specs=[pl.BlockSpec((tm * TOP_K,), lambda i, lo, hi: (i,), memory_space=pltpu.SMEM),
                  pl.BlockSpec((tm, D_MODEL), lambda i, lo, hi: (i, 0))],
        out_specs=pl.BlockSpec(memory_space=pl.ANY),
        scratch_shapes=[pltpu.VMEM((tm, D_MODEL), F32), pltpu.SemaphoreType.DMA(()), pltpu.SemaphoreType.DMA(())],
    )
    return pl.pallas_call(
        _dispatch_kernel,
        grid_spec=grid_spec,
        out_shape=jax.ShapeDtypeStruct((n_rows, D_MODEL), F32),
        compiler_params=_cparams("arbitrary"),
        name="moe_dispatch",
    )(fill_lo, pad_end, pos.reshape(-1), xn)


def _moe_weight_copies(e, w_hbm, wbuf, sems):
    return [pltpu.make_async_copy(w.at[e], wbuf.at[k], sems.at[k]) for k, w in enumerate(w_hbm)]


def _moe_kernel(be_ref, ne_ref, nv_ref, x_ref, wg_hbm, bg_ref, wu_hbm, bu_ref, wd_hbm, bd_ref, y_ref,
                w_bf, wbuf, sems):
    i = pl.program_id(0)
    e = be_ref[i]
    prev = be_ref[jnp.maximum(i - 1, 0)]
    valid = i < nv_ref[0]
    first = jnp.logical_or(i == 0, e != prev)
    copies = functools.partial(_moe_weight_copies, w_hbm=(wg_hbm, wu_hbm, wd_hbm), wbuf=wbuf, sems=sems)

    @pl.when(i == 0)
    def _():
        for c in copies(e):
            c.start()

    @pl.when(jnp.logical_and(valid, first))
    def _():
        nxt = ne_ref[i]
        for k, c in enumerate(copies(e)):
            c.wait()
            w_bf[k] = wbuf[k].astype(BF16)

        @pl.when(nxt >= 0)
        def _():
            for c in copies(nxt):
                c.start(priority=1)

    @pl.when(valid)
    def _():
        x = x_ref[...].astype(BF16)
        g =jnp.minimum(_dot(x, w_bf[0]) + bg_ref[...], SWIGLU_LIMIT)
        u = jnp.clip(_dot(x, w_bf[1]) + bu_ref[...], -SWIGLU_LIMIT, SWIGLU_LIMIT)
        a = g * jax.nn.sigmoid(SWIGLU_ALPHA * g) * (u + 1.0)
        y_ref[...] = _dot(a.astype(BF16), w_bf[2]) + bd_ref[...]

    @pl.when(jnp.logical_not(valid))
    def _():
        y_ref[...] = jnp.zeros_like(y_ref)


def _moe_experts(x_rows, block_e, next_e, n_valid, w_gate, b_gate, w_up, b_up, w_down, b_down):
    tm = TM_MOE
    n_rows = x_rows.shape[0]
    wspec = pl.BlockSpec(memory_space=pl.ANY)
    bspec = pl.BlockSpec((None, 1, D_MODEL), lambda i, be, ne, nv: (be[i], 0, 0))
    rows = pl.BlockSpec((tm, D_MODEL), lambda i, be, ne, nv: (i, 0))
    grid_spec = pltpu.PrefetchScalarGridSpec(
        num_scalar_prefetch=3,
        grid=(n_rows // tm,),
        in_specs=[rows, wspec, bspec, wspec, bspec, wspec, bspec],
        out_specs=rows,
        scratch_shapes=[pltpu.VMEM((3, D_MODEL, D_MODEL), BF16), pltpu.VMEM((3, D_MODEL, D_MODEL), F32),
                        pltpu.SemaphoreType.DMA((3,))],
    )
    b3 = lambda b: b.reshape(N_EXPERTS, 1, D_MODEL)
    return pl.pallas_call(
        _moe_kernel,
        grid_spec=grid_spec,
        out_shape=jax.ShapeDtypeStruct((n_rows, D_MODEL), F32),
        compiler_params=_cparams("arbitrary"),
        name="moe_experts",
    )(block_e, next_e, n_valid, x_rows, w_gate, b3(b_gate), w_up, b3(b_up), w_down, b3(b_down))


def _ple_kernel(h1_ref, rows_ref, tw_ref, p_ref, gple_ref, wg_ref, wp_ref, gfin_ref, y_ref):
    tw = tw_ref[...]
    h2 = h1_ref[...]
    for k in range(TOP_K):
        h2 = h2 + rows_ref[k] * tw[:, k:k + 1]
    gate =jax.nn.sigmoid(_dot(_rms(h2, gple_ref[...]).astype(BF16), wg_ref[...]))
    h3 = h2 + gate * _dot(p_ref[...].astype(BF16), wp_ref[...])
    y_ref[...] = _rms(h3, gfin_ref[...])


def _ple_final(h1, expert_rows, top_w, p, row_block_offset, g_ple, w_ple_gate, w_ple_proj, g_final):
    tm = TM_TOKENS
    n = p.shape[0]
    off = row_block_offset // tm
    src = lambda width: pl.BlockSpec((tm, width), lambda i: (off + i, 0))
    loc = lambda width: pl.BlockSpec((tm, width), lambda i: (i, 0))
    return pl.pallas_call(
        _ple_kernel,
        grid=(n // tm,),
        in_specs=[src(D_MODEL), pl.BlockSpec((TOP_K, tm, D_MODEL), lambda i: (0, off + i, 0)), src(LANES),
                  loc(D_PLE), _full((1, D_MODEL)), _full((D_MODEL, D_MODEL)), _full((D_PLE, D_MODEL)),
                  _full((1, D_MODEL))],
        out_specs=loc(D_MODEL),
        out_shape=jax.ShapeDtypeStruct((n, D_MODEL), F32),
        compiler_params=_cparams("arbitrary"),
        name="ple_final",
    )(h1, expert_rows, top_w, p, g_ple.reshape(1, D_MODEL), w_ple_gate.astype(BF16),
      w_ple_proj.astype(BF16), g_final.reshape(1, D_MODEL))


def kernel(x_prompt, x_sample, p_prompt, p_sample, state_ssm_re, state_ssm_im, state_ml_c, state_ml_n, state_ml_m, g_mix, w_in, ssm_a_re, ssm_a_im, ssm_log_dt, ssm_b_re, ssm_b_im, ssm_c_re, ssm_c_im, ssm_d, ssm_w_glu, ssm_b_glu, ml_b_ig, ml_b_fg, w_ssm_up, w_ml_up, w_out, g_ffn, w_router, b_router, w_gate, b_gate, w_up, b_up, w_down, b_down, g_ple, w_ple_gate, w_ple_proj, g_final):
    assert g_mix.shape[0] == 1, "single-layer trunk"
    bp, lp, _ = x_prompt.shape
    bs, ls, _ = x_sample.shape
    tp, ts = bp * lp, bs * ls
    t = tp + ts
    xp = x_prompt.reshape(tp, D_MODEL)
    xs = x_sample.reshape(ts, D_MODEL)

    u, qkv, o, gates, gsm = _inproj(xp, xs, g_mix[0], w_in[0])

    s5_args = (ssm_a_re[0], ssm_a_im[0], ssm_log_dt[0], ssm_b_re[0], ssm_b_im[0], ssm_c_re[0],
               ssm_c_im[0], ssm_d[0])
    zero_state = jnp.zeros((S5_TILES, bp, 2 * S5_TILE_STATE), F32)
    tables = {tc: _s5_tables(*s5_args, tc) for tc in {S5_CHUNK, ls}}
    y_p, f_p = _s5(u, 0, tables[S5_CHUNK], zero_state,
                   tc=S5_CHUNK, n_chunks=lp // S5_CHUNK, nb=bp // 2, n_splits=2)
    y_s, f_s = _s5(u, tp, tables[ls], _s5_state_to_tiles(state_ssm_re[0], state_ssm_im[0]),
                   tc=ls, n_chunks=1, nb=bs, n_splits=1)
    re_p, im_p = _s5_state_from_tiles(f_p)
    re_s, im_s = _s5_state_from_tiles(f_s)

    bias_row = jnp.pad(jnp.concatenate([ml_b_ig[0], ml_b_fg[0]]), (0, GATE_PAD - 2 * ML_HEADS)).reshape(1, GATE_PAD)
    hm_p, c_p, n_p, m_p = _mlstm_prompt(qkv, o, gates, bias_row, bp, lp)
    m0 = jnp.broadcast_to(state_ml_m[0][:, :, None, None], (bs, ML_HEADS, 1, LANES))
    hm_s, c_s, n_s, m_s = _mlstm_sample(qkv, o, gates, bias_row, state_ml_c[0],
                                        state_ml_n[0].reshape(bs, ML_HEADS, 1, ML_DK), m0, tp, bs, ls)

    h1, xn, logits = _merge(xp, xs, (y_p, y_s), (hm_p, hm_s), gsm, ssm_w_glu[0], ssm_b_glu[0], w_ssm_up[0], w_ml_up[0],
                            w_out[0], g_ffn[0], w_router[0], b_router[0])

    top_w_pad, pos, fill_lo, pad_end, n_rows, block_e, next_e, n_valid = _route(logits[:, :N_EXPERTS], TM_MOE)
    x_rows = _dispatch(xn, pos, fill_lo, pad_end, n_rows)
    expert_w = lambda w: w.reshape(N_EXPERTS, D_MODEL, D_MODEL)
    y_rows = _moe_experts(x_rows, block_e, next_e, n_valid, expert_w(w_gate), b_gate[0], expert_w(w_up),
                          b_up[0], expert_w(w_down), b_down[0])
    expert_rows = y_rows[pos.T.reshape(-1)].reshape(TOP_K, t, D_MODEL)

    ple_w = (g_ple[0], w_ple_gate[0], w_ple_proj[0], g_final)
    y_prompt = _ple_final(h1, expert_rows, top_w_pad, p_prompt[0].reshape(tp, D_PLE), 0, *ple_w)
    y_sample = _ple_final(h1, expert_rows, top_w_pad, p_sample[0].reshape(ts, D_PLE), tp, *ple_w)

    return (y_prompt.reshape(bp, lp, D_MODEL), y_sample.reshape(bs, ls, D_MODEL),
            re_p, im_p, c_p[None], n_p.reshape(1, bp, ML_HEADS, ML_DK), m_p[:, :, 0, 0][None],
            re_s, im_s, c_s[None], n_s.reshape(1, bs, ML_HEADS, ML_DK), m_s[:, :, 0, 0][None])
```

```python
import functools

import jax
import jax.numpy as jnp
from jax import lax
from jax.experimental import pallas as pl
from jax.experimental.pallas import tpu as pltpu

F32 = jnp.float32
BF16 = jnp.bfloat16
HIGHEST = lax.Precision.HIGHEST

D_MODEL = 1024
D_SSM = 512
SSM_GROUP = 16
N_GROUPS = 32
SSM_STATE = 64
ML_HEADS = 4
ML_DK = 128
D_ML = 512
N_EXPERTS = 32
TOP_K = 4
SWIGLU_LIMIT = 7.0
SWIGLU_ALPHA = 1.702
D_PLE = 256
RMS_EPS = 1e-6

LANES = 128
GATE_PAD = LANES
S5_CHUNK = 8
ML_CHUNK_PROMPT = 256
TM_TOKENS = 256
TM_MOE = 256
VMEM_LIMIT = 56 * 1024 * 1024


def _cparams(*sem):
    return pltpu.CompilerParams(dimension_semantics=sem, vmem_limit_bytes=VMEM_LIMIT)


def _rms(x, g):
    return x * lax.rsqrt(jnp.mean(x * x, axis=-1, keepdims=True) + RMS_EPS) * g


def _dot(a, b):
    return jnp.dot(a, b, preferred_element_type=F32)


def _dot_hi(a, b):
    return jnp.dot(a, b, preferred_element_type=F32, precision=HIGHEST)


def _full(shape):
    n = len(shape)
    return pl.BlockSpec(shape, lambda *_: (0,) * n)


def _inproj_kernel(xp_ref, xs_ref, g_ref, wu_ref, wqkv_ref, wo_ref, wgt_ref, wgsm_ref,
                   u_ref, qkv_ref, o_ref, gt_ref, gsm_ref, *, n_prompt_blocks):
    i = pl.program_id(0)
    x = jnp.where(i < n_prompt_blocks, xp_ref[...], xs_ref[...])
    hn = _rms(x, g_ref[...]).astype(BF16)
    u_ref[...] = _dot(hn, wu_ref[...])
    qkv = _dot(hn, wqkv_ref[...])
    col = lax.broadcasted_iota(jnp.int32, (1, 3 * D_ML), 1)
    k_scale = jnp.where((col >= D_ML) & (col < 2 * D_ML), ML_DK ** -0.5, 1.0).astype(F32)
    qkv_ref[...] = (qkv * k_scale).astype(BF16)
    o_ref[...] = _dot(hn, wo_ref[...])
    gt_ref[...] = _dot(hn, wgt_ref[...])
    gsm_ref[...] = _dot(hn, wgsm_ref[...])


def _two_source_specs(tm, width, n_prompt_blocks, n_sample_blocks):
    last_p, last_s = n_prompt_blocks - 1, n_sample_blocks - 1
    return (pl.BlockSpec((tm, width), lambda i: (jnp.minimum(i, last_p), 0)),
            pl.BlockSpec((tm, width), lambda i: (jnp.clip(i - n_prompt_blocks, 0, last_s), 0)))


def _inproj(xp, xs, g_mix, w_in):
    tm = TM_TOKENS
    tp, ts = xp.shape[0], xs.shape[0]
    t = tp + ts
    npb = tp // tm
    w = w_in.astype(BF16)
    o0 = D_SSM
    wu = w[:, :o0]
    wqkv = w[:, o0:o0 + 3 * D_ML]
    wo = w[:, o0 + 3 * D_ML:o0 + 4 * D_ML]
    g0 = o0 + 4 * D_ML
    wgt = jnp.pad(w[:, g0:g0 + 2 * ML_HEADS], ((0, 0), (0, GATE_PAD - 2 * ML_HEADS)))
    wgsm = w[:, g0 + 2 * ML_HEADS:]
    xp_spec, xs_spec = _two_source_specs(tm, D_MODEL, npb, ts // tm)
    outs =(jax.ShapeDtypeStruct((t, D_SSM), F32), jax.ShapeDtypeStruct((t, 3 * D_ML), BF16),
            jax.ShapeDtypeStruct((t, D_ML), F32), jax.ShapeDtypeStruct((t, GATE_PAD), F32),
            jax.ShapeDtypeStruct((t, 2 * D_MODEL), F32))
    row = lambda width: pl.BlockSpec((tm, width), lambda i: (i, 0))
    return pl.pallas_call(
        functools.partial(_inproj_kernel, n_prompt_blocks=npb),
        grid=(t // tm,),
        in_specs=[xp_spec, xs_spec, _full((1, D_MODEL)), _full(wu.shape), _full(wqkv.shape),
                  _full(wo.shape), _full(wgt.shape), _full(wgsm.shape)],
        out_specs=[row(D_SSM), row(3 * D_ML), row(D_ML), row(GATE_PAD), row(2 * D_MODEL)],
        out_shape=outs,
        compiler_params=_cparams("arbitrary"),
        name="inproj",
    )(xp, xs, g_mix.reshape(1, D_MODEL), wu, wqkv, wo, wgt, wgsm)


S5_TILES = D_SSM // LANES
S5_TILE_GROUPS = LANES // SSM_GROUP
S5_TILE_STATE = S5_TILE_GROUPS * SSM_STATE


def _block_diag_tiles(x):
    gt = S5_TILE_GROUPS
    n, _, r, c = x.shape
    x5 = x.reshape(n, S5_TILES, gt, r, c)
    eye = jnp.eye(gt, dtype=x.dtype)
    return (x5[:, :, :, :, None, :] * eye[None, None, :, None, :, None]).reshape(n, S5_TILES, gt * r, gt * c)


def _s5_tables(a_re, a_im, log_dt, b_re, b_im, c_re, c_im, d_skip, tc):
    ein = functools.partial(jnp.einsum, precision=HIGHEST)
    dt = jnp.exp(log_dt)[:, None]
    mag = jnp.exp(a_re * dt)
    abar_r, abar_i = mag * jnp.cos(a_im * dt), mag * jnp.sin(a_im * dt)
    den = a_re * a_re + a_im * a_im
    nr, ni = abar_r - 1.0, abar_i
    coef_r = (nr * a_re + ni * a_im) / den
    coef_i = (ni * a_re - nr * a_im) / den
    bbar_r = coef_r[..., None] * b_re - coef_i[..., None] * b_im
    bbar_i = coef_r[..., None] * b_im + coef_i[..., None] * b_re

    def abar_pow(j):
        jj = j[..., None, None]
        mag_j = jnp.where(jj >= 0, jnp.exp(jj * (a_re * dt)), 0.0)
        return mag_j * jnp.cos(jj * (a_im * dt)), mag_j * jnp.sin(jj * (a_im * dt))

    half = tc // 2
    steps = jnp.arange(tc, dtype=F32)
    at_r, at_i = abar_pow(jnp.full((), tc, F32))
    lags = (2.0 * jnp.arange(half, dtype=F32)[:, None, None]
            + jnp.array([[0.0, 1.0], [-1.0, 0.0]], F32)[None])
    lag_r, lag_i = abar_pow(lags.reshape(-1))
    ab_r = lag_r[..., None] * bbar_r - lag_i[..., None] * bbar_i
    ab_i = lag_r[..., None] * bbar_i + lag_i[..., None] * bbar_r
    kern = ein('ghp,jgpk->jgkh', c_re, ab_r) - ein('ghp,jgpk->jgkh', c_im, ab_i)
    bd_lag = _block_diag_tiles(kern.astype(BF16)).reshape(half, 2, 2, S5_TILES, LANES, LANES)
    toe = jnp.transpose(bd_lag, (3, 0, 1, 4, 2, 5)).reshape(S5_TILES, half, 2 * LANES, 2 * LANES)
    rev_r, rev_i = abar_pow(tc - 1.0 - steps)
    s_r = jnp.transpose(rev_r[..., None] * bbar_r - rev_i[..., None] * bbar_i, (0, 1, 3, 2))
    s_i = jnp.transpose(rev_r[..., None] * bbar_i + rev_i[..., None] * bbar_r, (0, 1, 3, 2))
    a1_r, a1_i = abar_pow(steps + 1.0)
    p_r = c_re[None] * a1_r[:, :, None, :] - c_im[None] * a1_i[:, :, None, :]
    p_i = -c_re[None] * a1_i[:, :, None, :] - c_im[None] * a1_r[:, :, None, :]

    def compact(x):
        x = x.astype(BF16).reshape(half, 2, S5_TILES, S5_TILE_GROUPS, SSM_GROUP, SSM_STATE)
        x = jnp.transpose(x, (2, 0, 1, 3, 4, 5)).reshape(S5_TILES, half, 2 * LANES, SSM_STATE)
        return jnp.concatenate([x, x], axis=3)

    s_tab = jnp.stack([compact(s_r), compact(s_i)], axis=2)
    p_tab = jnp.stack([compact(p_r), compact(p_i)], axis=2)
    a_tab =jnp.stack([at_r.reshape(S5_TILES, S5_TILE_STATE), at_i.reshape(S5_TILES, S5_TILE_STATE)], axis=1)
    d_tab = d_skip.reshape(S5_TILES, 1, LANES)
    return toe, s_tab, p_tab, a_tab, d_tab


def _s5_kernel(u_ref, t_ref, s_ref, p_ref, a_ref, d_ref, h0_ref, y_ref, f_ref, loc, xprev,
               *, tc, n_chunks, nb):
    r = n_chunks * nb
    half = tc // 2
    ns = S5_TILE_STATE
    step_rows = lambda t: pl.ds(t, r, stride=tc)
    v = [u_ref[step_rows(t), :] for t in range(tc)]
    vp = [jnp.concatenate([v[2 * a].astype(BF16), v[2 * a + 1].astype(BF16)], axis=1) for a in range(half)]

    row_group = (lax.broadcasted_iota(jnp.int32, (2 * LANES, ns), 0) >> 4) & (S5_TILE_GROUPS - 1)
    col_group = lax.broadcasted_iota(jnp.int32, (2 * LANES, ns), 1) >> 6
    own_group = jnp.where(row_group == col_group, 1.0, 0.0).astype(BF16)

    def block_diag(tab):
        reps = ns // LANES
        return jnp.concatenate([jnp.tile(tab[0], (1, reps)) * own_group, jnp.tile(tab[1], (1, reps)) * own_group],
                               axis=1)

    acc = _dot(vp[0], block_diag(s_ref[0]))
    for a in range(1, half):
        acc = acc + _dot(vp[a], block_diag(s_ref[a]))
    nt = ns // LANES
    lane_tile = lambda k: slice(k * LANES, (k + 1) * LANES)
    for k in range(2 * nt):
        loc[k] = acc[:, lane_tile(k)]
    abar = a_ref[...]

    def body(c, carry):
        rows = pl.ds(c, nb, stride=n_chunks) if n_chunks > 1 else pl.ds(0, nb)
        new = []
        for k in range(nt):
            xr, xi = carry[k], carry[nt + k]
            xprev[k, rows, :] = xr
            xprev[nt + k, rows, :] = xi
            ar, ai = abar[0:1, lane_tile(k)], abar[1:2, lane_tile(k)]
            new.append((ar * xr - ai * xi + loc[k, rows, :], ar * xi + ai * xr + loc[nt + k, rows, :]))
        return tuple(n[0] for n in new) + tuple(n[1] for n in new)

    x_end = lax.fori_loop(0, n_chunks, body, tuple(h0_ref[:, lane_tile(k)] for k in range(2 * nt)),
                          unroll=min(4, n_chunks))
    for k in range(2 * nt):
        f_ref[:, lane_tile(k)] = x_end[k]
    xp = jnp.concatenate([xprev[k] for k in range(2 * nt)], axis=1).astype(BF16)
    d = d_ref[...]
    nt_dims = (((1,), (1,)), ((), ()))
    for a2 in range(half):
        acc = lax.dot_general(xp, block_diag(p_ref[a2]), nt_dims, preferred_element_type=F32)
        for a in range(a2 + 1):
            acc = acc + _dot(vp[a], t_ref[a2 - a])
        for k in range(2):
            t = 2 * a2 + k
            y_ref[step_rows(t), :] = acc[:, k * LANES:(k + 1) * LANES] + v[t] * d


def _s5(u, row_block_offset, tables, h0, *, tc, n_chunks, nb, n_splits):
    toe, s_tab, p_tab, a_tab, d_tab = tables
    rows = nb * n_chunks * tc
    off = row_block_offset // rows
    half = tc // 2
    ns2 = 2 * S5_TILE_STATE
    tile = lambda *tail: pl.BlockSpec((None,) + tail, lambda j, s: (j,) + (0,) * len(tail))
    state = pl.BlockSpec((None, None, nb, ns2), lambda j, s: (j, s, 0, 0))
    y, f = pl.pallas_call(
        functools.partial(_s5_kernel, tc=tc, n_chunks=n_chunks, nb=nb),
        grid=(S5_TILES, n_splits),
        in_specs=[pl.BlockSpec((rows, LANES), lambda j, s: (off + s, j)),
                  tile(half, 2 * LANES, 2 * LANES), tile(half, 2, 2 * LANES, LANES), tile(half, 2, 2 * LANES, LANES),
                  tile(2, S5_TILE_STATE), tile(1, LANES), state],
        out_specs=[pl.BlockSpec((rows, LANES), lambda j, s: (s, j)), state],
        out_shape=(jax.ShapeDtypeStruct((rows * n_splits, D_SSM), F32),
                   jax.ShapeDtypeStruct((S5_TILES, n_splits, nb, ns2), F32)),
        scratch_shapes=[pltpu.VMEM((ns2 // LANES, nb * n_chunks, LANES), F32)] * 2,
        compiler_params=_cparams("arbitrary", "arbitrary"),
        name=f"s5_c{n_chunks}",
    )(u, toe, s_tab, p_tab, a_tab, d_tab, h0.reshape(S5_TILES, n_splits, nb, ns2))
    return y, f.reshape(S5_TILES, n_splits * nb, ns2)


def _s5_state_to_tiles(s_re, s_im):
    b = s_re.shape[0]
    f = lambda s: s.reshape(b, S5_TILES, S5_TILE_STATE).transpose(1, 0, 2)
    return jnp.concatenate([f(s_re), f(s_im)], axis=2)


def _s5_state_from_tiles(f):
    b = f.shape[1]
    g = lambda s: s.transpose(1, 0, 2).reshape(1, b, N_GROUPS, SSM_STATE)
    return g(f[:, :, :S5_TILE_STATE]), g(f[:, :, S5_TILE_STATE:])


def _log_sigmoid(x):
    return jnp.minimum(x, 0.0) - jnp.log1p(jnp.exp(-jnp.abs(x)))


def _mlstm_gates(gates, bias_row, lc):
    g = gates + bias_row
    col = lax.broadcasted_iota(jnp.int32, (1, GATE_PAD), 1)
    gl = jnp.where(col >= ML_HEADS, _log_sigmoid(g), g)
    r = lax.broadcasted_iota(jnp.int32, (lc, lc), 0)
    c = lax.broadcasted_iota(jnp.int32, (lc, lc), 1)
    tril = (r >= c).astype(F32)
    bcols = _dot_hi(tril, gl)
    sel = (lax.broadcasted_iota(jnp.int32, (8, GATE_PAD), 0)
           == lax.broadcasted_iota(jnp.int32, (8, GATE_PAD), 1)).astype(F32)
    nt = (((1,), (1,)), ((), ()))
    grows = lax.dot_general(sel, gl, nt, precision=HIGHEST, preferred_element_type=F32)
    brows = lax.dot_general(sel, bcols, nt, precision=HIGHEST, preferred_element_type=F32)
    return gl, bcols, grows, brows, (r >= c)


def _mlstm_chunks(seqs, bias_row, lc):
    nt = (((1,), (1,)), ((), ()))
    tn = (((0,), (0,)), ((), ()))
    pairs = [(si, hd) for si in range(len(seqs)) for hd in range(ML_HEADS)]
    head = lambda x, part, hd: x[:, part * D_ML + hd * ML_DK:part * D_ML + (hd + 1) * ML_DK]
    q = {p: head(seqs[p[0]][0], 0, p[1]) for p in pairs}
    k = {p: head(seqs[p[0]][0], 1, p[1]) for p in pairs}
    v = {p: head(seqs[p[0]][0], 2, p[1]) for p in pairs}
    state = lambda p: seqs[p[0]][3](p[1])
    qk = {p: lax.dot_general(q[p], k[p], nt, preferred_element_type=F32) for p in pairs}
    qc = {p: _dot(q[p], state(p)[0].astype(BF16)) for p in pairs}
    tables = [_mlstm_gates(seq[2], bias_row, lc) for seq in seqs]
    w_in, w_out, mt, m_end, decay, kw = {}, {}, {}, {}, {}, {}
    for p in pairs:
        si, hd = p
        gl, bcols, grows, brows, causal = tables[si]
        f = ML_HEADS + hd
        ic, bc, ir, br = gl[:, hd:hd + 1], bcols[:, f:f + 1], grows[hd:hd + 1, :], brows[f:f + 1, :]
        m_state = state(p)[2]
        dmat = jnp.where(causal, bc - br + ir, -jnp.inf)
        inter = bc + m_state
        mt[p] = jnp.maximum(inter, jnp.max(dmat, axis=1, keepdims=True))
        w_in[p] = jnp.exp(dmat - mt[p])
        w_out[p] = jnp.exp(inter - mt[p])
        m_end[p] = mt[p][lc - 1:lc, :]
        b_last = bc[lc - 1:lc, :]
        decay[p] = jnp.exp(b_last + m_state - m_end[p])
        kw[p] = k[p].astype(F32) * jnp.exp(b_last - bc + ic - m_end[p])
    kv = {p: lax.dot_general(kw[p].astype(BF16), v[p], tn, preferred_element_type=F32) for p in pairs}
    s = {p: qk[p] * w_in[p] for p in pairs}
    sv = {p: _dot(s[p].astype(BF16), v[p]) for p in pairs}
    outs = [[] for _ in seqs]
    for p in pairs:
        si, hd = p
        c_state, n_state, _ = state(p)
        num = w_out[p] * qc[p] + sv[p]
        qn = (w_out[p] * jnp.sum(q[p].astype(F32) * n_state, axis=1, keepdims=True)
              + jnp.sum(s[p], axis=1, keepdims=True))
        h = num / jnp.maximum(jnp.abs(qn), jnp.exp(-mt[p]))
        seqs[si][4](hd, decay[p] * c_state + kv[p], decay[p] * n_state + jnp.sum(kw[p], axis=0, keepdims=True),
                    m_end[p])
        outs[si].append(jax.nn.sigmoid(head(seqs[si][1], 0, hd)) * h)
    return [jnp.concatenate(o, axis=1) for o in outs]


def _mlstm_prompt_kernel(qkv_ref, o_ref, gt_ref, bias_ref, h_ref, c_ref, n_ref, m_ref, *, lc):
    @pl.when(pl.program_id(1) == 0)
    def _():
        c_ref[...] = jnp.zeros_like(c_ref)
        n_ref[...] = jnp.zeros_like(n_ref)
        m_ref[...] = jnp.zeros_like(m_ref)

    def get_state(hd):
        return c_ref[hd], n_ref[hd], m_ref[hd][:, 0:1]

    def put_state(hd, c_new, n_new, m_new):
        c_ref[hd] = c_new
        n_ref[hd] = n_new
        m_ref[hd] = jnp.broadcast_to(m_new, (1, LANES))

    (h,) = _mlstm_chunks([(qkv_ref[...], o_ref[...], gt_ref[...], get_state, put_state)], bias_ref[...], lc)
    h_ref[...] = h.astype(BF16)


def _mlstm_prompt(qkv, o, gates, bias_row, bsz, seq):
    lc = ML_CHUNK_PROMPT
    nc = seq // lc
    row = lambda width: pl.BlockSpec((lc, width), lambda b, c: (b * nc + c, 0))
    st = lambda *tail: pl.BlockSpec((None, ML_HEADS) + tail, lambda b, c: (b, 0) + (0,) * len(tail))
    return pl.pallas_call(
        functools.partial(_mlstm_prompt_kernel, lc=lc),
        grid=(bsz, nc),
        in_specs=[row(3 * D_ML), row(D_ML), row(GATE_PAD), pl.BlockSpec((1, GATE_PAD), lambda b, c: (0, 0))],
        out_specs=[row(D_ML), st(ML_DK, ML_DK), st(1, ML_DK), st(1, LANES)],
        out_shape=(jax.ShapeDtypeStruct((bsz * seq, D_ML), BF16),
                   jax.ShapeDtypeStruct((bsz, ML_HEADS, ML_DK, ML_DK), F32),
                   jax.ShapeDtypeStruct((bsz, ML_HEADS, 1, ML_DK), F32),
                   jax.ShapeDtypeStruct((bsz, ML_HEADS, 1, LANES), F32)),
        compiler_params=_cparams("arbitrary", "arbitrary"),
        name="mlstm_prompt",
    )(qkv, o, gates, bias_row)


def _mlstm_sample_kernel(qkv_ref, o_ref, gt_ref, bias_ref, c0_ref, n0_ref, m0_ref,
                         h_ref, c_ref, n_ref, m_ref, qkv_s, *, lc, nb):
    qkv_s[...] = qkv_ref[...].astype(F32)

    def seq(b):
        rows = slice(b * lc, (b + 1) * lc)

        def get_state(hd):
            return c0_ref[b, hd], n0_ref[b, hd], m0_ref[b, hd][:, 0:1]

        def put_state(hd, c_new, n_new, m_new):
            c_ref[b, hd] = c_new
            n_ref[b, hd] = n_new
            m_ref[b, hd] = jnp.broadcast_to(m_new, (1, LANES))

        return (qkv_s[rows, :].astype(BF16), o_ref[rows, :], gt_ref[rows, :], get_state, put_state)

    hs = _mlstm_chunks([seq(b) for b in range(nb)], bias_ref[...], lc)
    for b, h in enumerate(hs):
        h_ref[b * lc:(b + 1) * lc, :] = h


def _mlstm_sample(qkv, o, gates, bias_row, c0, n0, m0, row_block_offset, bsz, seq):
    nb = 8
    lc = seq
    rows = nb * lc
    off = row_block_offset // rows
    row = lambda width: pl.BlockSpec((rows, width), lambda i: (off + i, 0))
    st = lambda *tail: pl.BlockSpec((nb, ML_HEADS) + tail, lambda i: (i, 0) + (0,) * len(tail))
    return pl.pallas_call(
        functools.partial(_mlstm_sample_kernel, lc=lc, nb=nb),
        grid=(bsz // nb,),
        in_specs=[row(3 * D_ML), row(D_ML), row(GATE_PAD), pl.BlockSpec((1, GATE_PAD), lambda i: (0, 0)),
                  st(ML_DK, ML_DK), st(1, ML_DK), st(1, LANES)],
        out_specs=[pl.BlockSpec((rows, D_ML), lambda i: (i, 0)), st(ML_DK, ML_DK), st(1, ML_DK), st(1, LANES)],
        out_shape=(jax.ShapeDtypeStruct((bsz * seq, D_ML), F32),
                   jax.ShapeDtypeStruct((bsz, ML_HEADS, ML_DK, ML_DK), F32),
                   jax.ShapeDtypeStruct((bsz, ML_HEADS, 1, ML_DK), F32),
                   jax.ShapeDtypeStruct((bsz, ML_HEADS, 1, LANES), F32)),
        scratch_shapes=[pltpu.VMEM((rows, 3 * D_ML), F32)],
        compiler_params=_cparams("arbitrary"),
        name="mlstm_sample",
    )(qkv, o, gates, bias_row, c0, n0, m0)


def _merge_kernel(xp_ref, xs_ref, yp_ref, ys_ref, mp_ref, ms_ref, gsm_ref, wglu_ref, bglu_ref, wsu_ref, wmu_ref,
                  wout_ref, gffn_ref, wrh_ref, wrl_ref, br_ref, h1_ref, xn_ref, tw_ref, ti_ref, rk_ref, cnt_ref,
                  seen, logits_s, *, n_prompt_blocks):
    i = pl.program_id(0)

    @pl.when(i == 0)
    def _():
        seen[...] = jnp.zeros_like(seen)
        logits_s[...] = jnp.zeros_like(logits_s)

    _route_tile(logits_s[...], tw_ref, ti_ref, rk_ref, cnt_ref, seen, i)

    is_prompt = i < n_prompt_blocks
    x = jnp.where(is_prompt, xp_ref[...], xs_ref[...])
    y = jax.nn.gelu(jnp.where(is_prompt, yp_ref[...], ys_ref[...]))
    ym = jnp.where(is_prompt, mp_ref[...], ms_ref[...].astype(BF16))
    ys = y * jax.nn.sigmoid(_dot(y.astype(BF16), wglu_ref[...]) + bglu_ref[...])
    gsm = gsm_ref[...]
    merged = (jax.nn.sigmoid(gsm[:, :D_MODEL]) * _dot(ys.astype(BF16), wsu_ref[...])
              + jax.nn.sigmoid(gsm[:, D_MODEL:]) * _dot(ym, wmu_ref[...]))
    h1 = x + _dot(merged.astype(BF16), wout_ref[...])
    h1_ref[...] = h1
    xn = _rms(h1, gffn_ref[...])
    xn_ref[...] = xn
    xn_hi = xn.astype(BF16)
    xn_lo = (xn - xn_hi.astype(F32)).astype(BF16)
    nt = (((1,), (1,)), ((), ()))
    dot_nt = lambda a, b: lax.dot_general(a, b, nt, preferred_element_type=F32)
    logits_s[...] = (dot_nt(wrh_ref[...], xn_hi) + dot_nt(wrh_ref[...], xn_lo) + dot_nt(wrl_ref[...], xn_hi)
                     + br_ref[...])


def _route_tile(logits_t, tw_ref, ti_ref, rk_ref, cnt_ref, seen, step):
    ne, tm = logits_t.shape
    seen_in = jnp.where(step <= 1, 0.0, seen[:, 0:1])
    row =lax.broadcasted_iota(jnp.int32, (ne, tm), 0)
    work = logits_t
    vals, idxs, hits = [], [], []
    for _ in range(TOP_K):
        m = jnp.max(work, axis=0, keepdims=True)
        idx = jnp.min(jnp.where(work == m, row, ne), axis=0, keepdims=True)
        hit = row == idx
        vals.append(m)
        idxs.append(idx)
        hits.append(hit)
        work = jnp.where(hit, -jnp.inf, work)
    ex = [jnp.exp(v - vals[0]) for v in vals]
    den = ex[0] + ex[1] + ex[2] + ex[3]
    multi = sum(jnp.where(h, 1.0, 0.0) for h in hits)
    r = lax.broadcasted_iota(jnp.int32, (tm, tm), 0)
    c = lax.broadcasted_iota(jnp.int32, (tm, tm), 1)
    strict_upper = jnp.where(r < c, 1.0, 0.0).astype(BF16)
    earlier = _dot(multi.astype(BF16), strict_upper) + seen_in
    ranks = [jnp.sum(jnp.where(h, earlier, 0.0), axis=0, keepdims=True) for h in hits]
    pad = 8 - TOP_K
    tw_ref[...] = jnp.concatenate([e / den for e in ex] + [jnp.zeros((pad, tm), F32)], axis=0)
    ti_ref[...] = jnp.concatenate(idxs + [jnp.zeros((pad, tm), jnp.int32)], axis=0)
    rk_ref[...] = jnp.concatenate([rk.astype(jnp.int32) for rk in ranks] + [jnp.zeros((pad, tm), jnp.int32)],
                                  axis=0)
    seen[...] = jnp.broadcast_to(seen_in + jnp.sum(multi, axis=1, keepdims=True), seen.shape)
    cnt_ref[...] = seen[...]


def _merge(xp, xs, y_pre, ym, gsm, w_glu, b_glu, w_ssm_up, w_ml_up, w_out, g_ffn, w_router, b_router):
    tm = TM_TOKENS
    t = gsm.shape[0]
    npb = xp.shape[0] // tm
    n_tiles = t // tm
    nsb = n_tiles - npb
    xp_spec, xs_spec = _two_source_specs(tm, D_MODEL, npb, nsb)
    yp_spec, ys_spec = _two_source_specs(tm, D_SSM, npb, nsb)
    mp_spec, ms_spec = _two_source_specs(tm, D_ML, npb, nsb)
    row = lambda width: pl.BlockSpec((tm, width), lambda i: (jnp.minimum(i, n_tiles - 1), 0))
    wr = w_router.T
    wr_hi = wr.astype(BF16)
    wr_lo = (wr - wr_hi.astype(F32)).astype(BF16)
    br = b_router.reshape(N_EXPERTS, 1)
    tok = lambda: pl.BlockSpec((8, tm), lambda i: (0, jnp.maximum(i - 1, 0)))
    return pl.pallas_call(
        functools.partial(_merge_kernel, n_prompt_blocks=npb),
        grid=(n_tiles + 1,),
        in_specs=[xp_spec, xs_spec, yp_spec, ys_spec, mp_spec, ms_spec, row(2 * D_MODEL), _full((D_SSM, D_SSM)),
                  _full((1, D_SSM)), _full((D_SSM, D_MODEL)), _full((D_ML, D_MODEL)),
                  _full((D_MODEL, D_MODEL)), _full((1, D_MODEL)), _full((N_EXPERTS, D_MODEL)),
                  _full((N_EXPERTS, D_MODEL)), _full((N_EXPERTS, 1))],
        out_specs=[row(D_MODEL), row(D_MODEL), tok(), tok(), tok(), _full((N_EXPERTS, LANES))],
        out_shape=(jax.ShapeDtypeStruct((t, D_MODEL), F32), jax.ShapeDtypeStruct((t, D_MODEL), F32),
                   jax.ShapeDtypeStruct((8, t), F32), jax.ShapeDtypeStruct((8, t), jnp.int32),
                   jax.ShapeDtypeStruct((8, t), jnp.int32), jax.ShapeDtypeStruct((N_EXPERTS, LANES), F32)),
        scratch_shapes=[pltpu.VMEM((N_EXPERTS, LANES), F32), pltpu.VMEM((N_EXPERTS, tm), F32)],
        compiler_params=_cparams("arbitrary"),
        name="merge",
    )(xp, xs, *y_pre, *ym, gsm, w_glu.astype(BF16), b_glu.reshape(1, D_SSM), w_ssm_up.astype(BF16),
      w_ml_up.astype(BF16), w_out.astype(BF16), g_ffn.reshape(1, D_MODEL), wr_hi, wr_lo, br)


def _route(top_idx, rank, counts, tm):
    n_assign = top_idx.shape[0] * top_idx.shape[1]
    padded = (counts + tm - 1) // tm * tm
    pad_end = jnp.cumsum(padded)
    pad_start = pad_end - padded
    onehot = top_idx[:, :, None] == jnp.arange(N_EXPERTS, dtype=jnp.int32)[None, None, :]
    pos = jnp.sum(jnp.where(onehot, pad_start[None, None, :], 0), axis=2) + rank
    n_blocks = -(-(n_assign + N_EXPERTS * (tm - 1)) // tm)
    block_row0 = jnp.arange(n_blocks, dtype=jnp.int32) * tm
    block_e = jnp.minimum(jnp.sum((pad_end[None, :] <= block_row0[:, None]).astype(jnp.int32), axis=1),
                          N_EXPERTS - 1)
    n_valid = (pad_end[-1] // tm).astype(jnp.int32).reshape(1)
    ids = jnp.arange(N_EXPERTS, dtype=jnp.int32)
    later = (ids[None, :] > ids[:, None]) & (counts[None, :] > 0)
    next_e = jnp.min(jnp.where(later, ids[None, :], N_EXPERTS), axis=1)
    next_e = jnp.where(next_e == N_EXPERTS, -1, next_e)[block_e]
    fill_lo = pad_start + counts
    return pos, fill_lo, pad_end, n_blocks * tm, block_e, next_e, n_valid


def _dispatch_copy(x_ref, rows_hbm, sem, r, dst):
    return pltpu.make_async_copy(x_ref.at[pl.ds(r, 1), :], rows_hbm.at[pl.ds(dst, 1), :], sem)


def _dispatch_kernel(lo_ref, hi_ref, pos_ref, x_ref, rows_hbm, zero_s, sem, zero_sem):
    tm = x_ref.shape[0]

    @pl.when(pl.program_id(0) == 0)
    def _():
        zero_s[...] = jnp.zeros_like(zero_s)

        def tail(do):
            def body(b, c):
                rows = pl.ds(pl.multiple_of(b * tm, tm), tm)
                do(pltpu.make_async_copy(zero_s, rows_hbm.at[rows, :], zero_sem))
                return c
            lax.fori_loop(hi_ref[N_EXPERTS - 1] // tm, rows_hbm.shape[0] // tm, body, 0)

        tail(lambda c: c.start())
        tail(lambda c: c.wait())

        def fill(e, carry):
            def each(do):
                def body(r, c):
                    do(_dispatch_copy(zero_s, rows_hbm, zero_sem, 0, r))
                    return c
                lax.fori_loop(lo_ref[e], hi_ref[e], body, 0)
            each(lambda c: c.start())
            each(lambda c: c.wait())
            return carry

        lax.fori_loop(0, N_EXPERTS, fill, 0)

    def each(do):
        for r in range(tm):
            for k in range(TOP_K):
                do(_dispatch_copy(x_ref, rows_hbm, sem, r, pos_ref[r * TOP_K + k]), k)

    each(lambda c, k: c.start(priority=k % 2))
    each(lambda c, k: c.wait())


def _dispatch(xn, pos, fill_lo, pad_end, n_rows):
    tm = TM_TOKENS
    t = xn.shape[0]
    grid_spec = pltpu.PrefetchScalarGridSpec(
        num_scalar_prefetch=2,
        grid=(t // tm,),
        in_specs=[pl.BlockSpec((tm * TOP_K,), lambda i, lo, hi: (i,), memory_space=pltpu.SMEM),
                  pl.BlockSpec((tm, D_MODEL), lambda i, lo, hi: (i, 0))],
        out_specs=pl.BlockSpec(memory_space=pl.ANY),
        scratch_shapes=[pltpu.VMEM((tm, D_MODEL), F32), pltpu.SemaphoreType.DMA(()), pltpu.SemaphoreType.DMA(())],
    )
    return pl.pallas_call(
        _dispatch_kernel,
        grid_spec=grid_spec,
        out_shape=jax.ShapeDtypeStruct((n_rows, D_MODEL), F32),
        compiler_params=_cparams("arbitrary"),
        name="moe_dispatch",
    )(fill_lo, pad_end, pos.reshape(-1), xn)


def _moe_weight_copies(e, w_hbm, wbuf, sems):
    return [pltpu.make_async_copy(w.at[e], wbuf.at[k], sems.at[k]) for k, w in enumerate(w_hbm)]


def _moe_kernel(be_ref, ne_ref, nv_ref, x_ref, wg_hbm, bg_ref, wu_hbm, bu_ref, wd_hbm, bd_ref, y_ref,
                w_bf, wbuf, sems):
    i = pl.program_id(0)
    e = be_ref[i]
    prev = be_ref[jnp.maximum(i - 1, 0)]
    valid = i < nv_ref[0]
    first = jnp.logical_or(i == 0, e != prev)
    copies = functools.partial(_moe_weight_copies, w_hbm=(wg_hbm, wu_hbm, wd_hbm), wbuf=wbuf, sems=sems)

    @pl.when(i == 0)
    def _():
        for c in copies(e):
            c.start()

    @pl.when(jnp.logical_and(valid, first))
    def _():
        nxt = ne_ref[i]
        for k, c in enumerate(copies(e)):
            c.wait()
            w_bf[k] = wbuf[k].astype(BF16)

        @pl.when(nxt >= 0)
        def _():
            for c in copies(nxt):
                c.start(priority=1)

    @pl.when(valid)
    def _():
        x = x_ref[...].astype(BF16)
        g =jnp.minimum(_dot(x, w_bf[0]) + bg_ref[...], SWIGLU_LIMIT)
        u = jnp.clip(_dot(x, w_bf[1]) + bu_ref[...], -SWIGLU_LIMIT, SWIGLU_LIMIT)
        a = g * jax.nn.sigmoid(SWIGLU_ALPHA * g) * (u + 1.0)
        y_ref[...] = _dot(a.astype(BF16), w_bf[2]) + bd_ref[...]

    @pl.when(jnp.logical_not(valid))
    def _():
        y_ref[...] = jnp.zeros_like(y_ref)


def _moe_experts(x_rows, block_e, next_e, n_valid, w_gate, b_gate, w_up, b_up, w_down, b_down):
    tm = TM_MOE
    n_rows = x_rows.shape[0]
    wspec = pl.BlockSpec(memory_space=pl.ANY)
    bspec = pl.BlockSpec((None, 1, D_MODEL), lambda i, be, ne, nv: (be[i], 0, 0))
    rows = pl.BlockSpec((tm, D_MODEL), lambda i, be, ne, nv: (i, 0))
    grid_spec = pltpu.PrefetchScalarGridSpec(
        num_scalar_prefetch=3,
        grid=(n_rows // tm,),
        in_specs=[rows, wspec, bspec, wspec, bspec, wspec, bspec],
        out_specs=rows,
        scratch_shapes=[pltpu.VMEM((3, D_MODEL, D_MODEL), BF16), pltpu.VMEM((3, D_MODEL, D_MODEL), F32),
                        pltpu.SemaphoreType.DMA((3,))],
    )
    b3 = lambda b: b.reshape(N_EXPERTS, 1, D_MODEL)
    return pl.pallas_call(
        _moe_kernel,
        grid_spec=grid_spec,
        out_shape=jax.ShapeDtypeStruct((n_rows, D_MODEL), F32),
        compiler_params=_cparams("arbitrary"),
        name="moe_experts",
    )(block_e, next_e, n_valid, x_rows, w_gate, b3(b_gate), w_up, b3(b_up), w_down, b3(b_down))


def _ple_kernel(h1_ref, rows_ref, tw_ref, p_ref, gple_ref, wg_ref, wp_ref, gfin_ref, y_ref):
    tw = tw_ref[...]
    h2 = h1_ref[...]
    for k in range(TOP_K):
        h2 = h2 + rows_ref[k] * tw[:, k:k + 1]
    gate =jax.nn.sigmoid(_dot(_rms(h2, gple_ref[...]).astype(BF16), wg_ref[...]))
    h3 = h2 + gate * _dot(p_ref[...].astype(BF16), wp_ref[...])
    y_ref[...] = _rms(h3, gfin_ref[...])


def _ple_final(h1, expert_rows, top_w, p, row_block_offset, g_ple, w_ple_gate, w_ple_proj, g_final):
    tm = TM_TOKENS
    n = p.shape[0]
    off = row_block_offset // tm
    src = lambda width: pl.BlockSpec((tm, width), lambda i: (off + i, 0))
    loc = lambda width: pl.BlockSpec((tm, width), lambda i: (i, 0))
    return pl.pallas_call(
        _ple_kernel,
        grid=(n // tm,),
        in_specs=[src(D_MODEL), pl.BlockSpec((TOP_K, tm, D_MODEL), lambda i: (0, off + i, 0)), src(top_w.shape[1]),
                  loc(D_PLE), _full((1, D_MODEL)), _full((D_MODEL, D_MODEL)), _full((D_PLE, D_MODEL)),
                  _full((1, D_MODEL))],
        out_specs=loc(D_MODEL),
        out_shape=jax.ShapeDtypeStruct((n, D_MODEL), F32),
        compiler_params=_cparams("arbitrary"),
        name="ple_final",
    )(h1, expert_rows, top_w, p, g_ple.reshape(1, D_MODEL), w_ple_gate.astype(BF16),
      w_ple_proj.astype(BF16), g_final.reshape(1, D_MODEL))


def kernel(x_prompt, x_sample, p_prompt, p_sample, state_ssm_re, state_ssm_im, state_ml_c, state_ml_n, state_ml_m, g_mix, w_in, ssm_a_re, ssm_a_im, ssm_log_dt, ssm_b_re, ssm_b_im, ssm_c_re, ssm_c_im, ssm_d, ssm_w_glu, ssm_b_glu, ml_b_ig, ml_b_fg, w_ssm_up, w_ml_up, w_out, g_ffn, w_router, b_router, w_gate, b_gate, w_up, b_up, w_down, b_down, g_ple, w_ple_gate, w_ple_proj, g_final):
    assert g_mix.shape[0] == 1, "single-layer trunk"
    bp, lp, _ = x_prompt.shape
    bs, ls, _ = x_sample.shape
    tp, ts = bp * lp, bs * ls
    t = tp + ts
    xp = x_prompt.reshape(tp, D_MODEL)
    xs = x_sample.reshape(ts, D_MODEL)

    u, qkv, o, gates, gsm = _inproj(xp, xs, g_mix[0], w_in[0])

    s5_args = (ssm_a_re[0], ssm_a_im[0], ssm_log_dt[0], ssm_b_re[0], ssm_b_im[0], ssm_c_re[0],
               ssm_c_im[0], ssm_d[0])
    zero_state = jnp.zeros((S5_TILES, bp, 2 * S5_TILE_STATE), F32)
    tables = {tc: _s5_tables(*s5_args, tc) for tc in {S5_CHUNK, ls}}
    y_p, f_p = _s5(u, 0, tables[S5_CHUNK], zero_state,
                   tc=S5_CHUNK, n_chunks=lp // S5_CHUNK, nb=bp // 2, n_splits=2)
    y_s, f_s = _s5(u, tp, tables[ls], _s5_state_to_tiles(state_ssm_re[0], state_ssm_im[0]),
                   tc=ls, n_chunks=1, nb=bs, n_splits=1)
    re_p, im_p = _s5_state_from_tiles(f_p)
    re_s, im_s = _s5_state_from_tiles(f_s)

    bias_row = jnp.pad(jnp.concatenate([ml_b_ig[0], ml_b_fg[0]]), (0, GATE_PAD - 2 * ML_HEADS)).reshape(1, GATE_PAD)
    hm_p, c_p, n_p, m_p = _mlstm_prompt(qkv, o, gates, bias_row, bp, lp)
    m0 = jnp.broadcast_to(state_ml_m[0][:, :, None, None], (bs, ML_HEADS, 1, LANES))
    hm_s, c_s, n_s, m_s = _mlstm_sample(qkv, o, gates, bias_row, state_ml_c[0],
                                        state_ml_n[0].reshape(bs, ML_HEADS, 1, ML_DK), m0, tp, bs, ls)

    h1, xn, top_w_t, top_idx_t, rank_t, counts = _merge(xp, xs, (y_p, y_s), (hm_p, hm_s), gsm, ssm_w_glu[0], ssm_b_glu[0], w_ssm_up[0], w_ml_up[0],
                            w_out[0], g_ffn[0], w_router[0], b_router[0])

    pos, fill_lo, pad_end, n_rows, block_e, next_e, n_valid = _route(
        top_idx_t[:TOP_K], rank_t[:TOP_K], counts[:, 0].astype(jnp.int32), TM_MOE)
    x_rows = _dispatch(xn, pos.T, fill_lo, pad_end, n_rows)
    expert_w = lambda w: w.reshape(N_EXPERTS, D_MODEL, D_MODEL)
    y_rows = _moe_experts(x_rows, block_e, next_e, n_valid, expert_w(w_gate), b_gate[0], expert_w(w_up),
                          b_up[0], expert_w(w_down), b_down[0])
    expert_rows = y_rows[pos.reshape(-1)].reshape(TOP_K, t, D_MODEL)
    top_w_pad = top_w_t.T

    ple_w = (g_ple[0], w_ple_gate[0], w_ple_proj[0], g_final)
    y_prompt = _ple_final(h1, expert_rows, top_w_pad, p_prompt[0].reshape(tp, D_PLE), 0, *ple_w)
    y_sample = _ple_final(h1, expert_rows, top_w_pad, p_sample[0].reshape(ts, D_PLE), tp, *ple_w)

    return (y_prompt.reshape(bp, lp, D_MODEL), y_sample.reshape(bs, ls, D_MODEL),
            re_p, im_p, c_p[None], n_p.reshape(1, bp, ML_HEADS, ML_DK), m_p[:, :, 0, 0][None],
            re_s, im_s, c_s[None], n_s.reshape(1, bs, ML_HEADS, ML_DK), m_s[:, :, 0, 0][None])
```

```python
import functools

import jax
import jax.numpy as jnp
from jax import lax
from jax.experimental import pallas as pl
from jax.experimental.pallas import tpu as pltpu

F32 = jnp.float32
BF16 = jnp.bfloat16
HIGHEST = lax.Precision.HIGHEST

D_MODEL = 1024
D_SSM = 512
SSM_GROUP = 16
N_GROUPS = 32
SSM_STATE = 64
ML_HEADS = 4
ML_DK = 128
D_ML = 512
N_EXPERTS = 32
TOP_K = 4
SWIGLU_LIMIT = 7.0
SWIGLU_ALPHA = 1.702
D_PLE = 256
RMS_EPS = 1e-6

LANES = 128
GATE_PAD = LANES
S5_CHUNK = 8
ML_CHUNK_PROMPT = 256
TM_TOKENS = 256
TM_WIDE = 512
TM_MOE = 256
VMEM_LIMIT = 56 * 1024 * 1024


def _cparams(*sem):
    return pltpu.CompilerParams(dimension_semantics=sem, vmem_limit_bytes=VMEM_LIMIT)


def _rms(x, g):
    return x * lax.rsqrt(jnp.mean(x * x, axis=-1, keepdims=True) + RMS_EPS) * g


def _dot(a, b):
    return jnp.dot(a, b, preferred_element_type=F32)


def _dot_hi(a, b):
    return jnp.dot(a, b, preferred_element_type=F32, precision=HIGHEST)


def _full(shape):
    n = len(shape)
    return pl.BlockSpec(shape, lambda *_: (0,) * n)


def _inproj_kernel(xp_ref, xs_ref, g_ref, wu_ref, wqkv_ref, wo_ref, wgt_ref, wgsm_ref,
                   u_ref, qkv_ref, o_ref, gt_ref, gsm_ref, *, n_prompt_blocks):
    i = pl.program_id(0)
    x = jnp.where(i < n_prompt_blocks, xp_ref[...], xs_ref[...])
    hn = _rms(x, g_ref[...]).astype(BF16)
    u_ref[...] = _dot(hn, wu_ref[...])
    qkv = _dot(hn, wqkv_ref[...])
    col = lax.broadcasted_iota(jnp.int32, (1, 3 * D_ML), 1)
    k_scale = jnp.where((col >= D_ML) & (col < 2 * D_ML), ML_DK ** -0.5, 1.0).astype(F32)
    qkv_ref[...] = (qkv * k_scale).astype(BF16)
    o_ref[...] = _dot(hn, wo_ref[...])
    gt_ref[...] = _dot(hn, wgt_ref[...])
    gsm_ref[...] = _dot(hn, wgsm_ref[...])


def _two_source_specs(tm, width, n_prompt_blocks, n_sample_blocks):
    last_p, last_s = n_prompt_blocks - 1, n_sample_blocks - 1
    return (pl.BlockSpec((tm, width), lambda i: (jnp.minimum(i, last_p), 0)),
            pl.BlockSpec((tm, width), lambda i: (jnp.clip(i - n_prompt_blocks, 0, last_s), 0)))


def _inproj(xp, xs, g_mix, w_in):
    tm = TM_TOKENS
    tp, ts = xp.shape[0], xs.shape[0]
    t = tp + ts
    npb = tp // tm
    w = w_in.astype(BF16)
    o0 = D_SSM
    wu = w[:, :o0]
    wqkv = w[:, o0:o0 + 3 * D_ML]
    wo = w[:, o0 + 3 * D_ML:o0 + 4 * D_ML]
    g0 = o0 + 4 * D_ML
    wgt = jnp.pad(w[:, g0:g0 + 2 * ML_HEADS], ((0, 0), (0, GATE_PAD - 2 * ML_HEADS)))
    wgsm = w[:, g0 + 2 * ML_HEADS:]
    xp_spec, xs_spec = _two_source_specs(tm, D_MODEL, npb, ts // tm)
    outs =(jax.ShapeDtypeStruct((t, D_SSM), F32), jax.ShapeDtypeStruct((t, 3 * D_ML), BF16),
            jax.ShapeDtypeStruct((t, D_ML), F32), jax.ShapeDtypeStruct((t, GATE_PAD), F32),
            jax.ShapeDtypeStruct((t, 2 * D_MODEL), F32))
    row = lambda width: pl.BlockSpec((tm, width), lambda i: (i, 0))
    return pl.pallas_call(
        functools.partial(_inproj_kernel, n_prompt_blocks=npb),
        grid=(t // tm,),
        in_specs=[xp_spec, xs_spec, _full((1, D_MODEL)), _full(wu.shape), _full(wqkv.shape),
                  _full(wo.shape), _full(wgt.shape), _full(wgsm.shape)],
        out_specs=[row(D_SSM), row(3 * D_ML), row(D_ML), row(GATE_PAD), row(2 * D_MODEL)],
        out_shape=outs,
        compiler_params=_cparams("arbitrary"),
        name="inproj",
    )(xp, xs, g_mix.reshape(1, D_MODEL), wu, wqkv, wo, wgt, wgsm)


S5_TILES = D_SSM // LANES
S5_TILE_GROUPS = LANES // SSM_GROUP
S5_TILE_STATE = S5_TILE_GROUPS * SSM_STATE


def _block_diag_tiles(x):
    gt = S5_TILE_GROUPS
    n, _, r, c = x.shape
    x5 = x.reshape(n, S5_TILES, gt, r, c)
    eye = jnp.eye(gt, dtype=x.dtype)
    return (x5[:, :, :, :, None, :] * eye[None, None, :, None, :, None]).reshape(n, S5_TILES, gt * r, gt * c)


def _s5_tables(a_re, a_im, log_dt, b_re, b_im, c_re, c_im, d_skip, tc):
    ein = functools.partial(jnp.einsum, precision=HIGHEST)
    dt = jnp.exp(log_dt)[:, None]
    mag = jnp.exp(a_re * dt)
    abar_r, abar_i = mag * jnp.cos(a_im * dt), mag * jnp.sin(a_im * dt)
    den = a_re * a_re + a_im * a_im
    nr, ni = abar_r - 1.0, abar_i
    coef_r = (nr * a_re + ni * a_im) / den
    coef_i = (ni * a_re - nr * a_im) / den
    bbar_r = coef_r[..., None] * b_re - coef_i[..., None] * b_im
    bbar_i = coef_r[..., None] * b_im + coef_i[..., None] * b_re

    def abar_pow(j):
        jj = j[..., None, None]
        mag_j = jnp.where(jj >= 0, jnp.exp(jj * (a_re * dt)), 0.0)
        return mag_j * jnp.cos(jj * (a_im * dt)), mag_j * jnp.sin(jj * (a_im * dt))

    half = tc // 2
    steps = jnp.arange(tc, dtype=F32)
    at_r, at_i = abar_pow(jnp.full((), tc, F32))
    lags = (2.0 * jnp.arange(half, dtype=F32)[:, None, None]
            + jnp.array([[0.0, 1.0], [-1.0, 0.0]], F32)[None])
    lag_r, lag_i = abar_pow(lags.reshape(-1))
    ab_r = lag_r[..., None] * bbar_r - lag_i[..., None] * bbar_i
    ab_i = lag_r[..., None] * bbar_i + lag_i[..., None] * bbar_r
    kern = ein('ghp,jgpk->jgkh', c_re, ab_r) - ein('ghp,jgpk->jgkh', c_im, ab_i)
    bd_lag = _block_diag_tiles(kern.astype(BF16)).reshape(half, 2, 2, S5_TILES, LANES, LANES)
    toe = jnp.transpose(bd_lag, (3, 0, 1, 4, 2, 5)).reshape(S5_TILES, half, 2 * LANES, 2 * LANES)
    rev_r, rev_i = abar_pow(tc - 1.0 - steps)
    s_r = jnp.transpose(rev_r[..., None] * bbar_r - rev_i[..., None] * bbar_i, (0, 1, 3, 2))
    s_i = jnp.transpose(rev_r[..., None] * bbar_i + rev_i[..., None] * bbar_r, (0, 1, 3, 2))
    a1_r, a1_i = abar_pow(steps + 1.0)
    p_r = c_re[None] * a1_r[:, :, None, :] - c_im[None] * a1_i[:, :, None, :]
    p_i = -c_re[None] * a1_i[:, :, None, :] - c_im[None] * a1_r[:, :, None, :]

    def compact(x):
        x = x.astype(BF16).reshape(half, 2, S5_TILES, S5_TILE_GROUPS, SSM_GROUP, SSM_STATE)
        x = jnp.transpose(x, (2, 0, 1, 3, 4, 5)).reshape(S5_TILES, half, 2 * LANES, SSM_STATE)
        return jnp.concatenate([x, x], axis=3)

    s_tab = jnp.stack([compact(s_r), compact(s_i)], axis=2)
    p_tab = jnp.stack([compact(p_r), compact(p_i)], axis=2)
    a_tab =jnp.stack([at_r.reshape(S5_TILES, S5_TILE_STATE), at_i.reshape(S5_TILES, S5_TILE_STATE)], axis=1)
    d_tab = d_skip.reshape(S5_TILES, 1, LANES)
    return toe, s_tab, p_tab, a_tab, d_tab


def _s5_kernel(u_ref, t_ref, s_ref, p_ref, a_ref, d_ref, h0_ref, y_ref, f_ref, loc, xprev,
               *, tc, n_chunks, nb):
    r = n_chunks * nb
    half = tc // 2
    ns = S5_TILE_STATE
    step_rows = lambda t: pl.ds(t, r, stride=tc)
    v = [u_ref[step_rows(t), :] for t in range(tc)]
    vp = [jnp.concatenate([v[2 * a].astype(BF16), v[2 * a + 1].astype(BF16)], axis=1) for a in range(half)]

    row_group = (lax.broadcasted_iota(jnp.int32, (2 * LANES, ns), 0) >> 4) & (S5_TILE_GROUPS - 1)
    col_group = lax.broadcasted_iota(jnp.int32, (2 * LANES, ns), 1) >> 6
    own_group = jnp.where(row_group == col_group, 1.0, 0.0).astype(BF16)

    def block_diag(tab):
        reps = ns // LANES
        return jnp.concatenate([jnp.tile(tab[0], (1, reps)) * own_group, jnp.tile(tab[1], (1, reps)) * own_group],
                               axis=1)

    acc = _dot(vp[0], block_diag(s_ref[0]))
    for a in range(1, half):
        acc = acc + _dot(vp[a], block_diag(s_ref[a]))
    nt = ns // LANES
    lane_tile = lambda k: slice(k * LANES, (k + 1) * LANES)
    for k in range(2 * nt):
        loc[k] = acc[:, lane_tile(k)]
    abar = a_ref[...]

    def body(c, carry):
        rows = pl.ds(c, nb, stride=n_chunks) if n_chunks > 1 else pl.ds(0, nb)
        new = []
        for k in range(nt):
            xr, xi = carry[k], carry[nt + k]
            xprev[k, rows, :] = xr
            xprev[nt + k, rows, :] = xi
            ar, ai = abar[0:1, lane_tile(k)], abar[1:2, lane_tile(k)]
            new.append((ar * xr - ai * xi + loc[k, rows, :], ar * xi + ai * xr + loc[nt + k, rows, :]))
        return tuple(n[0] for n in new) + tuple(n[1] for n in new)

    x_end = lax.fori_loop(0, n_chunks, body, tuple(h0_ref[:, lane_tile(k)] for k in range(2 * nt)),
                          unroll=min(8, n_chunks))
    for k in range(2 * nt):
        f_ref[:, lane_tile(k)] = x_end[k]
    xp = jnp.concatenate([xprev[k] for k in range(2 * nt)], axis=1).astype(BF16)
    d = d_ref[...]
    nt_dims = (((1,), (1,)), ((), ()))
    for a2 in range(half):
        acc = lax.dot_general(xp, block_diag(p_ref[a2]), nt_dims, preferred_element_type=F32)
        for a in range(a2 + 1):
            acc = acc + _dot(vp[a], t_ref[a2 - a])
        for k in range(2):
            t = 2 * a2 + k
            y_ref[step_rows(t), :] = acc[:, k * LANES:(k + 1) * LANES] + v[t] * d


def _s5(u, row_block_offset, tables, h0, *, tc, n_chunks, nb, n_splits):
    toe, s_tab, p_tab, a_tab, d_tab = tables
    rows = nb * n_chunks * tc
    off = row_block_offset // rows
    half = tc // 2
    ns2 = 2 * S5_TILE_STATE
    tile = lambda *tail: pl.BlockSpec((None,) + tail, lambda j, s: (j,) + (0,) * len(tail))
    state = pl.BlockSpec((None, None, nb, ns2), lambda j, s: (j, s, 0, 0))
    y, f = pl.pallas_call(
        functools.partial(_s5_kernel, tc=tc, n_chunks=n_chunks, nb=nb),
        grid=(S5_TILES, n_splits),
        in_specs=[pl.BlockSpec((rows, LANES), lambda j, s: (off + s, j)),
                  tile(half, 2 * LANES, 2 * LANES), tile(half, 2, 2 * LANES, LANES), tile(half, 2, 2 * LANES, LANES),
                  tile(2, S5_TILE_STATE), tile(1, LANES), state],
        out_specs=[pl.BlockSpec((rows, LANES), lambda j, s: (s, j)), state],
        out_shape=(jax.ShapeDtypeStruct((rows * n_splits, D_SSM), F32),
                   jax.ShapeDtypeStruct((S5_TILES, n_splits, nb, ns2), F32)),
        scratch_shapes=[pltpu.VMEM((ns2 // LANES, nb * n_chunks, LANES), F32)] * 2,
        compiler_params=_cparams("arbitrary", "arbitrary"),
        name=f"s5_c{n_chunks}",
    )(u, toe, s_tab, p_tab, a_tab, d_tab, h0.reshape(S5_TILES, n_splits, nb, ns2))
    return y, f.reshape(S5_TILES, n_splits * nb, ns2)


def _s5_state_to_tiles(s_re, s_im):
    b = s_re.shape[0]
    f = lambda s: s.reshape(b, S5_TILES, S5_TILE_STATE).transpose(1, 0, 2)
    return jnp.concatenate([f(s_re), f(s_im)], axis=2)


def _s5_state_from_tiles(f):
    b = f.shape[1]
    g = lambda s: s.transpose(1, 0, 2).reshape(1, b, N_GROUPS, SSM_STATE)
    return g(f[:, :, :S5_TILE_STATE]), g(f[:, :, S5_TILE_STATE:])


def _log_sigmoid(x):
    return jnp.minimum(x, 0.0) - jnp.log1p(jnp.exp(-jnp.abs(x)))


def _mlstm_gates(gates, bias_row, lc):
    g = gates + bias_row
    col = lax.broadcasted_iota(jnp.int32, (1, GATE_PAD), 1)
    gl = jnp.where(col >= ML_HEADS, _log_sigmoid(g), g)
    r = lax.broadcasted_iota(jnp.int32, (lc, lc), 0)
    c = lax.broadcasted_iota(jnp.int32, (lc, lc), 1)
    tril = (r >= c).astype(F32)
    bcols = _dot_hi(tril, gl)
    sel = (lax.broadcasted_iota(jnp.int32, (8, GATE_PAD), 0)
           == lax.broadcasted_iota(jnp.int32, (8, GATE_PAD), 1)).astype(F32)
    nt = (((1,), (1,)), ((), ()))
    grows = lax.dot_general(sel, gl, nt, precision=HIGHEST, preferred_element_type=F32)
    brows = lax.dot_general(sel, bcols, nt, precision=HIGHEST, preferred_element_type=F32)
    return gl, bcols, grows, brows, (r >= c)


def _mlstm_chunks(seqs, bias_row, lc):
    nt = (((1,), (1,)), ((), ()))
    tn = (((0,), (0,)), ((), ()))
    pairs = [(si, hd) for si in range(len(seqs)) for hd in range(ML_HEADS)]
    head = lambda x, part, hd: x[:, part * D_ML + hd * ML_DK:part * D_ML + (hd + 1) * ML_DK]
    q = {p: head(seqs[p[0]][0], 0, p[1]) for p in pairs}
    k = {p: head(seqs[p[0]][0], 1, p[1]) for p in pairs}
    v = {p: head(seqs[p[0]][0], 2, p[1]) for p in pairs}
    state = lambda p: seqs[p[0]][3](p[1])
    qk = {p: lax.dot_general(q[p], k[p], nt, preferred_element_type=F32) for p in pairs}
    qc = {p: _dot(q[p], state(p)[0].astype(BF16)) for p in pairs}
    tables = [_mlstm_gates(seq[2], bias_row, lc) for seq in seqs]
    w_in, w_out, mt, m_end, decay, kw = {}, {}, {}, {}, {}, {}
    for p in pairs:
        si, hd = p
        gl, bcols, grows, brows, causal = tables[si]
        f = ML_HEADS + hd
        ic, bc, ir, br = gl[:, hd:hd + 1], bcols[:, f:f + 1], grows[hd:hd + 1, :], brows[f:f + 1, :]
        m_state = state(p)[2]
        dmat = jnp.where(causal, bc - br + ir, -jnp.inf)
        inter = bc + m_state
        mt[p] = jnp.maximum(inter, jnp.max(dmat, axis=1, keepdims=True))
        w_in[p] = jnp.exp(dmat - mt[p])
        w_out[p] = jnp.exp(inter - mt[p])
        m_end[p] = mt[p][lc - 1:lc, :]
        b_last = bc[lc - 1:lc, :]
        decay[p] = jnp.exp(b_last + m_state - m_end[p])
        kw[p] = k[p].astype(F32) * jnp.exp(b_last - bc + ic - m_end[p])
    kv = {p: lax.dot_general(kw[p].astype(BF16), v[p], tn, preferred_element_type=F32) for p in pairs}
    s = {p: qk[p] * w_in[p] for p in pairs}
    sv = {p: _dot(s[p].astype(BF16), v[p]) for p in pairs}
    outs = [[] for _ in seqs]
    for p in pairs:
        si, hd = p
        c_state, n_state, _ = state(p)
        num = w_out[p] * qc[p] + sv[p]
        qn = (w_out[p] * jnp.sum(q[p].astype(F32) * n_state, axis=1, keepdims=True)
              + jnp.sum(s[p], axis=1, keepdims=True))
        h = num / jnp.maximum(jnp.abs(qn), jnp.exp(-mt[p]))
        seqs[si][4](hd, decay[p] * c_state + kv[p], decay[p] * n_state + jnp.sum(kw[p], axis=0, keepdims=True),
                    m_end[p])
        outs[si].append(jax.nn.sigmoid(head(seqs[si][1], 0, hd)) * h)
    return [jnp.concatenate(o, axis=1) for o in outs]


def _mlstm_prompt_kernel(qkv_ref, o_ref, gt_ref, bias_ref, h_ref, c_ref, n_ref, m_ref, *, lc):
    @pl.when(pl.program_id(1) == 0)
    def _():
        c_ref[...] = jnp.zeros_like(c_ref)
        n_ref[...] = jnp.zeros_like(n_ref)
        m_ref[...] = jnp.zeros_like(m_ref)

    def get_state(hd):
        return c_ref[hd], n_ref[hd], m_ref[hd][:, 0:1]

    def put_state(hd, c_new, n_new, m_new):
        c_ref[hd] = c_new
        n_ref[hd] = n_new
        m_ref[hd] = jnp.broadcast_to(m_new, (1, LANES))

    (h,) = _mlstm_chunks([(qkv_ref[...], o_ref[...], gt_ref[...], get_state, put_state)], bias_ref[...], lc)
    h_ref[...] = h.astype(BF16)


def _mlstm_prompt(qkv, o, gates, bias_row, bsz, seq):
    lc = ML_CHUNK_PROMPT
    nc = seq // lc
    row = lambda width: pl.BlockSpec((lc, width), lambda b, c: (b * nc + c, 0))
    st = lambda *tail: pl.BlockSpec((None, ML_HEADS) + tail, lambda b, c: (b, 0) + (0,) * len(tail))
    return pl.pallas_call(
        functools.partial(_mlstm_prompt_kernel, lc=lc),
        grid=(bsz, nc),
        in_specs=[row(3 * D_ML), row(D_ML), row(GATE_PAD), pl.BlockSpec((1, GATE_PAD), lambda b, c: (0, 0))],
        out_specs=[row(D_ML), st(ML_DK, ML_DK), st(1, ML_DK), st(1, LANES)],
        out_shape=(jax.ShapeDtypeStruct((bsz * seq, D_ML), BF16),
                   jax.ShapeDtypeStruct((bsz, ML_HEADS, ML_DK, ML_DK), F32),
                   jax.ShapeDtypeStruct((bsz, ML_HEADS, 1, ML_DK), F32),
                   jax.ShapeDtypeStruct((bsz, ML_HEADS, 1, LANES), F32)),
        compiler_params=_cparams("arbitrary", "arbitrary"),
        name="mlstm_prompt",
    )(qkv, o, gates, bias_row)


def _mlstm_sample_kernel(qkv_ref, o_ref, gt_ref, bias_ref, c0_ref, n0_ref, m0_ref,
                         h_ref, c_ref, n_ref, m_ref, qkv_s, *, lc, nb):
    qkv_s[...] = qkv_ref[...].astype(F32)

    def seq(b):
        rows = slice(b * lc, (b + 1) * lc)

        def get_state(hd):
            return c0_ref[b, hd], n0_ref[b, hd], m0_ref[b, hd][:, 0:1]

        def put_state(hd, c_new, n_new, m_new):
            c_ref[b, hd] = c_new
            n_ref[b, hd] = n_new
            m_ref[b, hd] = jnp.broadcast_to(m_new, (1, LANES))

        return (qkv_s[rows, :].astype(BF16), o_ref[rows, :], gt_ref[rows, :], get_state, put_state)

    hs = _mlstm_chunks([seq(b) for b in range(nb)], bias_ref[...], lc)
    for b, h in enumerate(hs):
        h_ref[b * lc:(b + 1) * lc, :] = h


def _mlstm_sample(qkv, o, gates, bias_row, c0, n0, m0, row_block_offset, bsz, seq):
    nb = 8
    lc = seq
    rows = nb * lc
    off = row_block_offset // rows
    row = lambda width: pl.BlockSpec((rows, width), lambda i: (off + i, 0))
    st = lambda *tail: pl.BlockSpec((nb, ML_HEADS) + tail, lambda i: (i, 0) + (0,) * len(tail))
    return pl.pallas_call(
        functools.partial(_mlstm_sample_kernel, lc=lc, nb=nb),
        grid=(bsz // nb,),
        in_specs=[row(3 * D_ML), row(D_ML), row(GATE_PAD), pl.BlockSpec((1, GATE_PAD), lambda i: (0, 0)),
                  st(ML_DK, ML_DK), st(1, ML_DK), st(1, LANES)],
        out_specs=[pl.BlockSpec((rows, D_ML), lambda i: (i, 0)), st(ML_DK, ML_DK), st(1, ML_DK), st(1, LANES)],
        out_shape=(jax.ShapeDtypeStruct((bsz * seq, D_ML), F32),
                   jax.ShapeDtypeStruct((bsz, ML_HEADS, ML_DK, ML_DK), F32),
                   jax.ShapeDtypeStruct((bsz, ML_HEADS, 1, ML_DK), F32),
                   jax.ShapeDtypeStruct((bsz, ML_HEADS, 1, LANES), F32)),
        scratch_shapes=[pltpu.VMEM((rows, 3 * D_ML), F32)],
        compiler_params=_cparams("arbitrary"),
        name="mlstm_sample",
    )(qkv, o, gates, bias_row, c0, n0, m0)


def _merge_kernel(xp_ref, xs_ref, yp_ref, ys_ref, mp_ref, ms_ref, gsm_ref, wglu_ref, bglu_ref, wsu_ref, wmu_ref,
                  wout_ref, gffn_ref, wrh_ref, wrl_ref, br_ref, h1_ref, xn_ref, tw_ref, ti_ref, rk_ref, cnt_ref,
                  seen, logits_s, *, n_prompt_blocks):
    i = pl.program_id(0)

    @pl.when(i == 0)
    def _():
        seen[...] = jnp.zeros_like(seen)
        logits_s[...] = jnp.zeros_like(logits_s)

    _route_tile(logits_s[...], tw_ref, ti_ref, rk_ref, cnt_ref, seen, i)

    is_prompt = i < n_prompt_blocks
    x = jnp.where(is_prompt, xp_ref[...], xs_ref[...])
    y = jax.nn.gelu(jnp.where(is_prompt, yp_ref[...], ys_ref[...]))
    ym = jnp.where(is_prompt, mp_ref[...], ms_ref[...].astype(BF16))
    ys = y * jax.nn.sigmoid(_dot(y.astype(BF16), wglu_ref[...]) + bglu_ref[...])
    gsm = gsm_ref[...]
    merged = (jax.nn.sigmoid(gsm[:, :D_MODEL]) * _dot(ys.astype(BF16), wsu_ref[...])
              + jax.nn.sigmoid(gsm[:, D_MODEL:]) * _dot(ym, wmu_ref[...]))
    h1 = x + _dot(merged.astype(BF16), wout_ref[...])
    h1_ref[...] = h1
    xn = _rms(h1, gffn_ref[...])
    xn_ref[...] = xn
    xn_hi = xn.astype(BF16)
    xn_lo = (xn - xn_hi.astype(F32)).astype(BF16)
    nt = (((1,), (1,)), ((), ()))
    dot_nt = lambda a, b: lax.dot_general(a, b, nt, preferred_element_type=F32)
    logits_s[...] = (dot_nt(wrh_ref[...], xn_hi) + dot_nt(wrh_ref[...], xn_lo) + dot_nt(wrl_ref[...], xn_hi)
                     + br_ref[...])


def _route_tile(logits_t, tw_ref, ti_ref, rk_ref, cnt_ref, seen, step):
    ne, tm = logits_t.shape
    seen_in = jnp.where(step <= 1, 0.0, seen[:, 0:1])
    row =lax.broadcasted_iota(jnp.int32, (ne, tm), 0)
    work = logits_t
    vals, idxs, hits = [], [], []
    for _ in range(TOP_K):
        m = jnp.max(work, axis=0, keepdims=True)
        idx = jnp.min(jnp.where(work == m, row, ne), axis=0, keepdims=True)
        hit = row == idx
        vals.append(m)
        idxs.append(idx)
        hits.append(hit)
        work = jnp.where(hit, -jnp.inf, work)
    ex = [jnp.exp(v - vals[0]) for v in vals]
    den = ex[0] + ex[1] + ex[2] + ex[3]
    multi = sum(jnp.where(h, 1.0, 0.0) for h in hits)
    r = lax.broadcasted_iota(jnp.int32, (tm, tm), 0)
    c = lax.broadcasted_iota(jnp.int32, (tm, tm), 1)
    strict_upper = jnp.where(r < c, 1.0, 0.0).astype(BF16)
    earlier = _dot(multi.astype(BF16), strict_upper) + seen_in
    ranks = [jnp.sum(jnp.where(h, earlier, 0.0), axis=0, keepdims=True) for h in hits]
    pad = 8 - TOP_K
    tw_ref[...] = jnp.concatenate([e / den for e in ex] + [jnp.zeros((pad, tm), F32)], axis=0)
    ti_ref[...] = jnp.concatenate(idxs + [jnp.zeros((pad, tm), jnp.int32)], axis=0)
    rk_ref[...] = jnp.concatenate([rk.astype(jnp.int32) for rk in ranks] + [jnp.zeros((pad, tm), jnp.int32)],
                                  axis=0)
    seen[...] = jnp.broadcast_to(seen_in + jnp.sum(multi, axis=1, keepdims=True), seen.shape)
    cnt_ref[...] = seen[...]


def _merge(xp, xs, y_pre, ym, gsm, w_glu, b_glu, w_ssm_up, w_ml_up, w_out, g_ffn, w_router, b_router):
    tm = TM_WIDE
    t = gsm.shape[0]
    npb = xp.shape[0] // tm
    n_tiles = t // tm
    nsb = n_tiles - npb
    xp_spec, xs_spec = _two_source_specs(tm, D_MODEL, npb, nsb)
    yp_spec, ys_spec = _two_source_specs(tm, D_SSM, npb, nsb)
    mp_spec, ms_spec = _two_source_specs(tm, D_ML, npb, nsb)
    row = lambda width: pl.BlockSpec((tm, width), lambda i: (jnp.minimum(i, n_tiles - 1), 0))
    wr = w_router.T
    wr_hi = wr.astype(BF16)
    wr_lo = (wr - wr_hi.astype(F32)).astype(BF16)
    br = b_router.reshape(N_EXPERTS, 1)
    tok = lambda: pl.BlockSpec((8, tm), lambda i: (0, jnp.maximum(i - 1, 0)))
    return pl.pallas_call(
        functools.partial(_merge_kernel, n_prompt_blocks=npb),
        grid=(n_tiles + 1,),
        in_specs=[xp_spec, xs_spec, yp_spec, ys_spec, mp_spec, ms_spec, row(2 * D_MODEL), _full((D_SSM, D_SSM)),
                  _full((1, D_SSM)), _full((D_SSM, D_MODEL)), _full((D_ML, D_MODEL)),
                  _full((D_MODEL, D_MODEL)), _full((1, D_MODEL)), _full((N_EXPERTS, D_MODEL)),
                  _full((N_EXPERTS, D_MODEL)), _full((N_EXPERTS, 1))],
        out_specs=[row(D_MODEL), row(D_MODEL), tok(), tok(), tok(), _full((N_EXPERTS, LANES))],
        out_shape=(jax.ShapeDtypeStruct((t, D_MODEL), F32), jax.ShapeDtypeStruct((t, D_MODEL), F32),
                   jax.ShapeDtypeStruct((8, t), F32), jax.ShapeDtypeStruct((8, t), jnp.int32),
                   jax.ShapeDtypeStruct((8, t), jnp.int32), jax.ShapeDtypeStruct((N_EXPERTS, LANES), F32)),
        scratch_shapes=[pltpu.VMEM((N_EXPERTS, LANES), F32), pltpu.VMEM((N_EXPERTS, tm), F32)],
        compiler_params=_cparams("arbitrary"),
        name="merge",
    )(xp, xs, *y_pre, *ym, gsm, w_glu.astype(BF16), b_glu.reshape(1, D_SSM), w_ssm_up.astype(BF16),
      w_ml_up.astype(BF16), w_out.astype(BF16), g_ffn.reshape(1, D_MODEL), wr_hi, wr_lo, br)


def _route(top_idx, rank, counts, tm):
    n_assign = top_idx.shape[0] * top_idx.shape[1]
    padded = (counts + tm - 1) // tm * tm
    pad_end = jnp.cumsum(padded)
    pad_start = pad_end - padded
    onehot = top_idx[:, :, None] == jnp.arange(N_EXPERTS, dtype=jnp.int32)[None, None, :]
    pos = jnp.sum(jnp.where(onehot, pad_start[None, None, :], 0), axis=2) + rank
    n_blocks = -(-(n_assign + N_EXPERTS * (tm - 1)) // tm)
    block_row0 = jnp.arange(n_blocks, dtype=jnp.int32) * tm
    block_e = jnp.minimum(jnp.sum((pad_end[None, :] <= block_row0[:, None]).astype(jnp.int32), axis=1),
                          N_EXPERTS - 1)
    n_valid = (pad_end[-1] // tm).astype(jnp.int32).reshape(1)
    ids = jnp.arange(N_EXPERTS, dtype=jnp.int32)
    later = (ids[None, :] > ids[:, None]) & (counts[None, :] > 0)
    next_e = jnp.min(jnp.where(later, ids[None, :], N_EXPERTS), axis=1)
    next_e = jnp.where(next_e == N_EXPERTS, -1, next_e)[block_e]
    fill_lo = pad_start + counts
    return pos, fill_lo, pad_end, n_blocks * tm, block_e, next_e, n_valid


def _dispatch_copy(x_ref, rows_hbm, sem, r, dst):
    return pltpu.make_async_copy(x_ref.at[pl.ds(r, 1), :], rows_hbm.at[pl.ds(dst, 1), :], sem)


def _dispatch_kernel(lo_ref, hi_ref, pos_ref, x_ref, rows_hbm, zero_s, sem, zero_sem):
    tm = x_ref.shape[0]

    @pl.when(pl.program_id(0) == 0)
    def _():
        zero_s[...] = jnp.zeros_like(zero_s)

        def tail(do):
            def body(b, c):
                rows = pl.ds(pl.multiple_of(b * tm, tm), tm)
                do(pltpu.make_async_copy(zero_s, rows_hbm.at[rows, :], zero_sem))
                return c
            lax.fori_loop(hi_ref[N_EXPERTS - 1] // tm, rows_hbm.shape[0] // tm, body, 0)

        tail(lambda c: c.start())
        tail(lambda c: c.wait())

        def fill(e, carry):
            def each(do):
                def body(r, c):
                    do(_dispatch_copy(zero_s, rows_hbm, zero_sem, 0, r))
                    return c
                lax.fori_loop(lo_ref[e], hi_ref[e], body, 0)
            each(lambda c: c.start())
            each(lambda c: c.wait())
            return carry

        lax.fori_loop(0, N_EXPERTS, fill, 0)

    def each(do):
        for r in range(tm):
            for k in range(TOP_K):
                do(_dispatch_copy(x_ref, rows_hbm, sem, r, pos_ref[r * TOP_K + k]), k)

    each(lambda c, k: c.start(priority=k % 2))
    each(lambda c, k: c.wait())


def _dispatch(xn, pos, fill_lo, pad_end, n_rows):
    tm = TM_TOKENS
    t = xn.shape[0]
    grid_spec = pltpu.PrefetchScalarGridSpec(
        num_scalar_prefetch=2,
        grid=(t // tm,),
        in_specs=[pl.BlockSpec((tm * TOP_K,), lambda i, lo, hi: (i,), memory_space=pltpu.SMEM),
                  pl.BlockSpec((tm, D_MODEL), lambda i, lo, hi: (i, 0))],
        out_specs=pl.BlockSpec(memory_space=pl.ANY),
        scratch_shapes=[pltpu.VMEM((tm, D_MODEL), F32), pltpu.SemaphoreType.DMA(()), pltpu.SemaphoreType.DMA(())],
    )
    return pl.pallas_call(
        _dispatch_kernel,
        grid_spec=grid_spec,
        out_shape=jax.ShapeDtypeStruct((n_rows, D_MODEL), F32),
        compiler_params=_cparams("arbitrary"),
        name="moe_dispatch",
    )(fill_lo, pad_end, pos.reshape(-1), xn)


def _moe_weight_copies(e, w_hbm, wbuf, sems):
    return [pltpu.make_async_copy(w.at[e], wbuf.at[k], sems.at[k]) for k, w in enumerate(w_hbm)]


def _moe_kernel(be_ref, ne_ref, nv_ref, x_ref, wg_hbm, bg_ref, wu_hbm, bu_ref, wd_hbm, bd_ref, y_ref,
                w_bf, wbuf, sems):
    i = pl.program_id(0)
    e = be_ref[i]
    prev = be_ref[jnp.maximum(i - 1, 0)]
    valid = i < nv_ref[0]
    first = jnp.logical_or(i == 0, e != prev)
    copies = functools.partial(_moe_weight_copies, w_hbm=(wg_hbm, wu_hbm, wd_hbm), wbuf=wbuf, sems=sems)

    @pl.when(i == 0)
    def _():
        for c in copies(e):
            c.start()

    @pl.when(jnp.logical_and(valid, first))
    def _():
        nxt = ne_ref[i]
        for k, c in enumerate(copies(e)):
            c.wait()
            w_bf[k] = wbuf[k].astype(BF16)

        @pl.when(nxt >= 0)
        def _():
            for c in copies(nxt):
                c.start(priority=1)

    @pl.when(valid)
    def _():
        x = x_ref[...].astype(BF16)
        g =jnp.minimum(_dot(x, w_bf[0]) + bg_ref[...], SWIGLU_LIMIT)
        u = jnp.clip(_dot(x, w_bf[1]) + bu_ref[...], -SWIGLU_LIMIT, SWIGLU_LIMIT)
        a = g * jax.nn.sigmoid(SWIGLU_ALPHA * g) * (u + 1.0)
        y_ref[...] = _dot(a.astype(BF16), w_bf[2]) + bd_ref[...]

    @pl.when(jnp.logical_not(valid))
    def _():
        y_ref[...] = jnp.zeros_like(y_ref)


def _moe_experts(x_rows, block_e, next_e, n_valid, w_gate, b_gate, w_up, b_up, w_down, b_down):
    tm = TM_MOE
    n_rows = x_rows.shape[0]
    wspec = pl.BlockSpec(memory_space=pl.ANY)
    bspec = pl.BlockSpec((None, 1, D_MODEL), lambda i, be, ne, nv: (be[i], 0, 0))
    rows = pl.BlockSpec((tm, D_MODEL), lambda i, be, ne, nv: (i, 0))
    grid_spec = pltpu.PrefetchScalarGridSpec(
        num_scalar_prefetch=3,
        grid=(n_rows // tm,),
        in_specs=[rows, wspec, bspec, wspec, bspec, wspec, bspec],
        out_specs=rows,
        scratch_shapes=[pltpu.VMEM((3, D_MODEL, D_MODEL), BF16), pltpu.VMEM((3, D_MODEL, D_MODEL), F32),
                        pltpu.SemaphoreType.DMA((3,))],
    )
    b3 = lambda b: b.reshape(N_EXPERTS, 1, D_MODEL)
    return pl.pallas_call(
        _moe_kernel,
        grid_spec=grid_spec,
        out_shape=jax.ShapeDtypeStruct((n_rows, D_MODEL), F32),
        compiler_params=_cparams("arbitrary"),
        name="moe_experts",
    )(block_e, next_e, n_valid, x_rows, w_gate, b3(b_gate), w_up, b3(b_up), w_down, b3(b_down))


def _ple_kernel(h1_ref, rows_ref, tw_ref, p_ref, gple_ref, wg_ref, wp_ref, gfin_ref, y_ref):
    tw = tw_ref[...]
    h2 = h1_ref[...]
    for k in range(TOP_K):
        h2 = h2 + rows_ref[k] * tw[:, k:k + 1]
    gate =jax.nn.sigmoid(_dot(_rms(h2, gple_ref[...]).astype(BF16), wg_ref[...]))
    h3 = h2 + gate * _dot(p_ref[...].astype(BF16), wp_ref[...])
    y_ref[...] = _rms(h3, gfin_ref[...])


def _ple_final(h1, expert_rows, top_w, p, row_block_offset, g_ple, w_ple_gate, w_ple_proj, g_final):
    tm = TM_WIDE
    n = p.shape[0]
    off = row_block_offset // tm
    src = lambda width: pl.BlockSpec((tm, width), lambda i: (off + i, 0))
    loc = lambda width: pl.BlockSpec((tm, width), lambda i: (i, 0))
    return pl.pallas_call(
        _ple_kernel,
        grid=(n // tm,),
        in_specs=[src(D_MODEL), pl.BlockSpec((TOP_K, tm, D_MODEL), lambda i: (0, off + i, 0)), src(top_w.shape[1]),
                  loc(D_PLE), _full((1, D_MODEL)), _full((D_MODEL, D_MODEL)), _full((D_PLE, D_MODEL)),
                  _full((1, D_MODEL))],
        out_specs=loc(D_MODEL),
        out_shape=jax.ShapeDtypeStruct((n, D_MODEL), F32),
        compiler_params=_cparams("arbitrary"),
        name="ple_final",
    )(h1, expert_rows, top_w, p, g_ple.reshape(1, D_MODEL), w_ple_gate.astype(BF16),
      w_ple_proj.astype(BF16), g_final.reshape(1, D_MODEL))


def kernel(x_prompt, x_sample, p_prompt, p_sample, state_ssm_re, state_ssm_im, state_ml_c, state_ml_n, state_ml_m, g_mix, w_in, ssm_a_re, ssm_a_im, ssm_log_dt, ssm_b_re, ssm_b_im, ssm_c_re, ssm_c_im, ssm_d, ssm_w_glu, ssm_b_glu, ml_b_ig, ml_b_fg, w_ssm_up, w_ml_up, w_out, g_ffn, w_router, b_router, w_gate, b_gate, w_up, b_up, w_down, b_down, g_ple, w_ple_gate, w_ple_proj, g_final):
    assert g_mix.shape[0] == 1, "single-layer trunk"
    bp, lp, _ = x_prompt.shape
    bs, ls, _ = x_sample.shape
    tp, ts = bp * lp, bs * ls
    t = tp + ts
    xp = x_prompt.reshape(tp, D_MODEL)
    xs = x_sample.reshape(ts, D_MODEL)

    u, qkv, o, gates, gsm = _inproj(xp, xs, g_mix[0], w_in[0])

    s5_args = (ssm_a_re[0], ssm_a_im[0], ssm_log_dt[0], ssm_b_re[0], ssm_b_im[0], ssm_c_re[0],
               ssm_c_im[0], ssm_d[0])
    zero_state = jnp.zeros((S5_TILES, bp, 2 * S5_TILE_STATE), F32)
    tables = {tc: _s5_tables(*s5_args, tc) for tc in {S5_CHUNK, ls}}
    y_p, f_p = _s5(u, 0, tables[S5_CHUNK], zero_state,
                   tc=S5_CHUNK, n_chunks=lp // S5_CHUNK, nb=bp // 2, n_splits=2)
    y_s, f_s = _s5(u, tp, tables[ls], _s5_state_to_tiles(state_ssm_re[0], state_ssm_im[0]),
                   tc=ls, n_chunks=1, nb=bs, n_splits=1)
    re_p, im_p = _s5_state_from_tiles(f_p)
    re_s, im_s = _s5_state_from_tiles(f_s)

    bias_row = jnp.pad(jnp.concatenate([ml_b_ig[0], ml_b_fg[0]]), (0, GATE_PAD - 2 * ML_HEADS)).reshape(1, GATE_PAD)
    hm_p, c_p, n_p, m_p = _mlstm_prompt(qkv, o, gates, bias_row, bp, lp)
    m0 = jnp.broadcast_to(state_ml_m[0][:, :, None, None], (bs, ML_HEADS, 1, LANES))
    hm_s, c_s, n_s, m_s = _mlstm_sample(qkv, o, gates, bias_row, state_ml_c[0],
                                        state_ml_n[0].reshape(bs, ML_HEADS, 1, ML_DK), m0, tp, bs, ls)

    h1, xn, top_w_t, top_idx_t, rank_t, counts = _merge(xp, xs, (y_p, y_s), (hm_p, hm_s), gsm, ssm_w_glu[0], ssm_b_glu[0], w_ssm_up[0], w_ml_up[0],
                            w_out[0], g_ffn[0], w_router[0], b_router[0])

    pos, fill_lo, pad_end, n_rows, block_e, next_e, n_valid = _route(
        top_idx_t[:TOP_K], rank_t[:TOP_K], counts[:, 0].astype(jnp.int32), TM_MOE)
    x_rows = _dispatch(xn, pos.T, fill_lo, pad_end, n_rows)
    expert_w = lambda w: w.reshape(N_EXPERTS, D_MODEL, D_MODEL)
    y_rows = _moe_experts(x_rows, block_e, next_e, n_valid, expert_w(w_gate), b_gate[0], expert_w(w_up),
                          b_up[0], expert_w(w_down), b_down[0])
    expert_rows = y_rows[pos.reshape(-1)].reshape(TOP_K, t, D_MODEL)
    top_w_pad = top_w_t.T

    ple_w = (g_ple[0], w_ple_gate[0], w_ple_proj[0], g_final)
    y_prompt = _ple_final(h1, expert_rows, top_w_pad, p_prompt[0].reshape(tp, D_PLE), 0, *ple_w)
    y_sample = _ple_final(h1, expert_rows, top_w_pad, p_sample[0].reshape(ts, D_PLE), tp, *ple_w)

    return (y_prompt.reshape(bp, lp, D_MODEL), y_sample.reshape(bs, ls, D_MODEL),
            re_p, im_p, c_p[None], n_p.reshape(1, bp, ML_HEADS, ML_DK), m_p[:, :, 0, 0][None],
            re_s, im_s, c_s[None], n_s.reshape(1, bs, ML_HEADS, ML_DK), m_s[:, :, 0, 0][None])
```

```python
import functools

import jax
import jax.numpy as jnp
from jax import lax
from jax.experimental import pallas as pl
from jax.experimental.pallas import tpu as pltpu

F32 = jnp.float32
BF16 = jnp.bfloat16
HIGHEST = lax.Precision.HIGHEST

D_MODEL = 1024
D_SSM = 512
SSM_GROUP = 16
N_GROUPS = 32
SSM_STATE = 64
ML_HEADS = 4
ML_DK = 128
D_ML = 512
N_EXPERTS = 32
TOP_K = 4
SWIGLU_LIMIT = 7.0
SWIGLU_ALPHA = 1.702
D_PLE = 256
RMS_EPS = 1e-6

LANES = 128
GATE_PAD = LANES
S5_CHUNK = 8
ML_CHUNK_PROMPT = 256
TM_TOKENS = 256
TM_WIDE = 512
TM_MOE = 512
VMEM_LIMIT = 56 * 1024 * 1024


def _cparams(*sem):
    return pltpu.CompilerParams(dimension_semantics=sem, vmem_limit_bytes=VMEM_LIMIT)


def _rms(x, g):
    return x * lax.rsqrt(jnp.mean(x * x, axis=-1, keepdims=True) + RMS_EPS) * g


def _dot(a, b):
    return jnp.dot(a, b, preferred_element_type=F32)


def _dot_hi(a, b):
    return jnp.dot(a, b, preferred_element_type=F32, precision=HIGHEST)


def _full(shape):
    n = len(shape)
    return pl.BlockSpec(shape, lambda *_: (0,) * n)


def _inproj_kernel(xp_ref, xs_ref, g_ref, wu_ref, wqkv_ref, wo_ref, wgt_ref, wgsm_ref,
                   u_ref, qkv_ref, o_ref, gt_ref, gsm_ref, *, n_prompt_blocks):
    i = pl.program_id(0)
    x = jnp.where(i < n_prompt_blocks, xp_ref[...], xs_ref[...])
    hn = _rms(x, g_ref[...]).astype(BF16)
    u_ref[...] = _dot(hn, wu_ref[...])
    qkv = _dot(hn, wqkv_ref[...])
    col = lax.broadcasted_iota(jnp.int32, (1, 3 * D_ML), 1)
    k_scale = jnp.where((col >= D_ML) & (col < 2 * D_ML), ML_DK ** -0.5, 1.0).astype(F32)
    qkv_ref[...] = (qkv * k_scale).astype(BF16)
    o_ref[...] = _dot(hn, wo_ref[...])
    gt_ref[...] = _dot(hn, wgt_ref[...])
    gsm_ref[...] = _dot(hn, wgsm_ref[...])


def _two_source_specs(tm, width, n_prompt_blocks, n_sample_blocks):
    last_p, last_s = n_prompt_blocks - 1, n_sample_blocks - 1
    return (pl.BlockSpec((tm, width), lambda i: (jnp.minimum(i, last_p), 0)),
            pl.BlockSpec((tm, width), lambda i: (jnp.clip(i - n_prompt_blocks, 0, last_s), 0)))


def _inproj(xp, xs, g_mix, w_in):
    tm = TM_WIDE
    tp, ts = xp.shape[0], xs.shape[0]
    t = tp + ts
    npb = tp // tm
    w = w_in.astype(BF16)
    o0 = D_SSM
    wu = w[:, :o0]
    wqkv = w[:, o0:o0 + 3 * D_ML]
    wo = w[:, o0 + 3 * D_ML:o0 + 4 * D_ML]
    g0 = o0 + 4 * D_ML
    wgt = jnp.pad(w[:, g0:g0 + 2 * ML_HEADS], ((0, 0), (0, GATE_PAD - 2 * ML_HEADS)))
    wgsm = w[:, g0 + 2 * ML_HEADS:]
    xp_spec, xs_spec = _two_source_specs(tm, D_MODEL, npb, ts // tm)
    outs =(jax.ShapeDtypeStruct((t, D_SSM), F32), jax.ShapeDtypeStruct((t, 3 * D_ML), BF16),
            jax.ShapeDtypeStruct((t, D_ML), F32), jax.ShapeDtypeStruct((t, GATE_PAD), F32),
            jax.ShapeDtypeStruct((t, 2 * D_MODEL), F32))
    row = lambda width: pl.BlockSpec((tm, width), lambda i: (i, 0))
    return pl.pallas_call(
        functools.partial(_inproj_kernel, n_prompt_blocks=npb),
        grid=(t // tm,),
        in_specs=[xp_spec, xs_spec, _full((1, D_MODEL)), _full(wu.shape), _full(wqkv.shape),
                  _full(wo.shape), _full(wgt.shape), _full(wgsm.shape)],
        out_specs=[row(D_SSM), row(3 * D_ML), row(D_ML), row(GATE_PAD), row(2 * D_MODEL)],
        out_shape=outs,
        compiler_params=_cparams("arbitrary"),
        name="inproj",
    )(xp, xs, g_mix.reshape(1, D_MODEL), wu, wqkv, wo, wgt, wgsm)


S5_TILES = D_SSM // LANES
S5_TILE_GROUPS = LANES // SSM_GROUP
S5_TILE_STATE = S5_TILE_GROUPS * SSM_STATE


def _block_diag_tiles(x):
    gt = S5_TILE_GROUPS
    n, _, r, c = x.shape
    x5 = x.reshape(n, S5_TILES, gt, r, c)
    eye = jnp.eye(gt, dtype=x.dtype)
    return (x5[:, :, :, :, None, :] * eye[None, None, :, None, :, None]).reshape(n, S5_TILES, gt * r, gt * c)


def _s5_tables(a_re, a_im, log_dt, b_re, b_im, c_re, c_im, d_skip, tc):
    ein = functools.partial(jnp.einsum, precision=HIGHEST)
    dt = jnp.exp(log_dt)[:, None]
    mag = jnp.exp(a_re * dt)
    abar_r, abar_i = mag * jnp.cos(a_im * dt), mag * jnp.sin(a_im * dt)
    den = a_re * a_re + a_im * a_im
    nr, ni = abar_r - 1.0, abar_i
    coef_r = (nr * a_re + ni * a_im) / den
    coef_i = (ni * a_re - nr * a_im) / den
    bbar_r = coef_r[..., None] * b_re - coef_i[..., None] * b_im
    bbar_i = coef_r[..., None] * b_im + coef_i[..., None] * b_re

    def abar_pow(j):
        jj = j[..., None, None]
        mag_j = jnp.where(jj >= 0, jnp.exp(jj * (a_re * dt)), 0.0)
        return mag_j * jnp.cos(jj * (a_im * dt)), mag_j * jnp.sin(jj * (a_im * dt))

    half = tc // 2
    steps = jnp.arange(tc, dtype=F32)
    at_r, at_i = abar_pow(jnp.full((), tc, F32))
    lags = (2.0 * jnp.arange(half, dtype=F32)[:, None, None]
            + jnp.array([[0.0, 1.0], [-1.0, 0.0]], F32)[None])
    lag_r, lag_i = abar_pow(lags.reshape(-1))
    ab_r = lag_r[..., None] * bbar_r - lag_i[..., None] * bbar_i
    ab_i = lag_r[..., None] * bbar_i + lag_i[..., None] * bbar_r
    kern = ein('ghp,jgpk->jgkh', c_re, ab_r) - ein('ghp,jgpk->jgkh', c_im, ab_i)
    bd_lag = _block_diag_tiles(kern.astype(BF16)).reshape(half, 2, 2, S5_TILES, LANES, LANES)
    toe = jnp.transpose(bd_lag, (3, 0, 1, 4, 2, 5)).reshape(S5_TILES, half, 2 * LANES, 2 * LANES)
    rev_r, rev_i = abar_pow(tc - 1.0 - steps)
    s_r = jnp.transpose(rev_r[..., None] * bbar_r - rev_i[..., None] * bbar_i, (0, 1, 3, 2))
    s_i = jnp.transpose(rev_r[..., None] * bbar_i + rev_i[..., None] * bbar_r, (0, 1, 3, 2))
    a1_r, a1_i = abar_pow(steps + 1.0)
    p_r = c_re[None] * a1_r[:, :, None, :] - c_im[None] * a1_i[:, :, None, :]
    p_i = -c_re[None] * a1_i[:, :, None, :] - c_im[None] * a1_r[:, :, None, :]

    def compact(x):
        x = x.astype(BF16).reshape(half, 2, S5_TILES, S5_TILE_GROUPS, SSM_GROUP, SSM_STATE)
        x = jnp.transpose(x, (2, 0, 1, 3, 4, 5)).reshape(S5_TILES, half, 2 * LANES, SSM_STATE)
        return jnp.concatenate([x, x], axis=3)

    s_tab = jnp.stack([compact(s_r), compact(s_i)], axis=2)
    p_tab = jnp.stack([compact(p_r), compact(p_i)], axis=2)
    a_tab =jnp.stack([at_r.reshape(S5_TILES, S5_TILE_STATE), at_i.reshape(S5_TILES, S5_TILE_STATE)], axis=1)
    d_tab = d_skip.reshape(S5_TILES, 1, LANES)
    return toe, s_tab, p_tab, a_tab, d_tab


def _s5_kernel(u_ref, t_ref, s_ref, p_ref, a_ref, d_ref, h0_ref, y_ref, f_ref, loc, xprev,
               *, tc, n_chunks, nb):
    r = n_chunks * nb
    half = tc // 2
    ns = S5_TILE_STATE
    step_rows = lambda t: pl.ds(t, r, stride=tc)
    v = [u_ref[step_rows(t), :] for t in range(tc)]
    vp = [jnp.concatenate([v[2 * a].astype(BF16), v[2 * a + 1].astype(BF16)], axis=1) for a in range(half)]

    row_group = (lax.broadcasted_iota(jnp.int32, (2 * LANES, ns), 0) >> 4) & (S5_TILE_GROUPS - 1)
    col_group = lax.broadcasted_iota(jnp.int32, (2 * LANES, ns), 1) >> 6
    own_group = jnp.where(row_group == col_group, 1.0, 0.0).astype(BF16)

    def block_diag(tab):
        reps = ns // LANES
        return jnp.concatenate([jnp.tile(tab[0], (1, reps)) * own_group, jnp.tile(tab[1], (1, reps)) * own_group],
                               axis=1)

    acc = _dot(vp[0], block_diag(s_ref[0]))
    for a in range(1, half):
        acc = acc + _dot(vp[a], block_diag(s_ref[a]))
    nt = ns // LANES
    lane_tile = lambda k: slice(k * LANES, (k + 1) * LANES)
    for k in range(2 * nt):
        loc[k] = acc[:, lane_tile(k)]
    abar = a_ref[...]

    def body(c, carry):
        rows = pl.ds(c, nb, stride=n_chunks) if n_chunks > 1 else pl.ds(0, nb)
        new = []
        for k in range(nt):
            xr, xi = carry[k], carry[nt + k]
            xprev[k, rows, :] = xr
            xprev[nt + k, rows, :] = xi
            ar, ai = abar[0:1, lane_tile(k)], abar[1:2, lane_tile(k)]
            new.append((ar * xr - ai * xi + loc[k, rows, :], ar * xi + ai * xr + loc[nt + k, rows, :]))
        return tuple(n[0] for n in new) + tuple(n[1] for n in new)

    x_end = lax.fori_loop(0, n_chunks, body, tuple(h0_ref[:, lane_tile(k)] for k in range(2 * nt)),
                          unroll=min(8, n_chunks))
    for k in range(2 * nt):
        f_ref[:, lane_tile(k)] = x_end[k]
    xp = jnp.concatenate([xprev[k] for k in range(2 * nt)], axis=1).astype(BF16)
    d = d_ref[...]
    nt_dims = (((1,), (1,)), ((), ()))
    for a2 in range(half):
        acc = lax.dot_general(xp, block_diag(p_ref[a2]), nt_dims, preferred_element_type=F32)
        for a in range(a2 + 1):
            acc = acc + _dot(vp[a], t_ref[a2 - a])
        for k in range(2):
            t = 2 * a2 + k
            y_ref[step_rows(t), :] = acc[:, k * LANES:(k + 1) * LANES] + v[t] * d


def _s5(u, row_block_offset, tables, h0, *, tc, n_chunks, nb, n_splits):
    toe, s_tab, p_tab, a_tab, d_tab = tables
    rows = nb * n_chunks * tc
    off = row_block_offset // rows
    half = tc // 2
    ns2 = 2 * S5_TILE_STATE
    tile = lambda *tail: pl.BlockSpec((None,) + tail, lambda j, s: (j,) + (0,) * len(tail))
    state = pl.BlockSpec((None, None, nb, ns2), lambda j, s: (j, s, 0, 0))
    y, f = pl.pallas_call(
        functools.partial(_s5_kernel, tc=tc, n_chunks=n_chunks, nb=nb),
        grid=(S5_TILES, n_splits),
        in_specs=[pl.BlockSpec((rows, LANES), lambda j, s: (off + s, j)),
                  tile(half, 2 * LANES, 2 * LANES), tile(half, 2, 2 * LANES, LANES), tile(half, 2, 2 * LANES, LANES),
                  tile(2, S5_TILE_STATE), tile(1, LANES), state],
        out_specs=[pl.BlockSpec((rows, LANES), lambda j, s: (s, j)), state],
        out_shape=(jax.ShapeDtypeStruct((rows * n_splits, D_SSM), F32),
                   jax.ShapeDtypeStruct((S5_TILES, n_splits, nb, ns2), F32)),
        scratch_shapes=[pltpu.VMEM((ns2 // LANES, nb * n_chunks, LANES), F32)] * 2,
        compiler_params=_cparams("arbitrary", "arbitrary"),
        name=f"s5_c{n_chunks}",
    )(u, toe, s_tab, p_tab, a_tab, d_tab, h0.reshape(S5_TILES, n_splits, nb, ns2))
    return y, f.reshape(S5_TILES, n_splits * nb, ns2)


def _s5_state_to_tiles(s_re, s_im):
    b = s_re.shape[0]
    f = lambda s: s.reshape(b, S5_TILES, S5_TILE_STATE).transpose(1, 0, 2)
    return jnp.concatenate([f(s_re), f(s_im)], axis=2)


def _s5_state_from_tiles(f):
    b = f.shape[1]
    g = lambda s: s.transpose(1, 0, 2).reshape(1, b, N_GROUPS, SSM_STATE)
    return g(f[:, :, :S5_TILE_STATE]), g(f[:, :, S5_TILE_STATE:])


def _log_sigmoid(x):
    return jnp.minimum(x, 0.0) - jnp.log1p(jnp.exp(-jnp.abs(x)))


def _mlstm_gates(gates, bias_row, lc):
    g = gates + bias_row
    col = lax.broadcasted_iota(jnp.int32, (1, GATE_PAD), 1)
    gl = jnp.where(col >= ML_HEADS, _log_sigmoid(g), g)
    r = lax.broadcasted_iota(jnp.int32, (lc, lc), 0)
    c = lax.broadcasted_iota(jnp.int32, (lc, lc), 1)
    tril = (r >= c).astype(F32)
    bcols = _dot_hi(tril, gl)
    sel = (lax.broadcasted_iota(jnp.int32, (8, GATE_PAD), 0)
           == lax.broadcasted_iota(jnp.int32, (8, GATE_PAD), 1)).astype(F32)
    nt = (((1,), (1,)), ((), ()))
    grows = lax.dot_general(sel, gl, nt, precision=HIGHEST, preferred_element_type=F32)
    brows = lax.dot_general(sel, bcols, nt, precision=HIGHEST, preferred_element_type=F32)
    return gl, bcols, grows, brows, (r >= c)


def _mlstm_chunks(seqs, bias_row, lc):
    nt = (((1,), (1,)), ((), ()))
    tn = (((0,), (0,)), ((), ()))
    pairs = [(si, hd) for si in range(len(seqs)) for hd in range(ML_HEADS)]
    head = lambda x, part, hd: x[:, part * D_ML + hd * ML_DK:part * D_ML + (hd + 1) * ML_DK]
    q = {p: head(seqs[p[0]][0], 0, p[1]) for p in pairs}
    k = {p: head(seqs[p[0]][0], 1, p[1]) for p in pairs}
    v = {p: head(seqs[p[0]][0], 2, p[1]) for p in pairs}
    state = lambda p: seqs[p[0]][3](p[1])
    qk = {p: lax.dot_general(q[p], k[p], nt, preferred_element_type=F32) for p in pairs}
    qc = {p: _dot(q[p], state(p)[0].astype(BF16)) for p in pairs}
    tables = [_mlstm_gates(seq[2], bias_row, lc) for seq in seqs]
    w_in, w_out, mt, m_end, decay, kw = {}, {}, {}, {}, {}, {}
    for p in pairs:
        si, hd = p
        gl, bcols, grows, brows, causal = tables[si]
        f = ML_HEADS + hd
        ic, bc, ir, br = gl[:, hd:hd + 1], bcols[:, f:f + 1], grows[hd:hd + 1, :], brows[f:f + 1, :]
        m_state = state(p)[2]
        dmat = jnp.where(causal, bc - br + ir, -jnp.inf)
        inter = bc + m_state
        mt[p] = jnp.maximum(inter, jnp.max(dmat, axis=1, keepdims=True))
        w_in[p] = jnp.exp(dmat - mt[p])
        w_out[p] = jnp.exp(inter - mt[p])
        m_end[p] = mt[p][lc - 1:lc, :]
        b_last = bc[lc - 1:lc, :]
        decay[p] = jnp.exp(b_last + m_state - m_end[p])
        kw[p] = k[p].astype(F32) * jnp.exp(b_last - bc + ic - m_end[p])
    kv = {p: lax.dot_general(kw[p].astype(BF16), v[p], tn, preferred_element_type=F32) for p in pairs}
    s = {p: qk[p] * w_in[p] for p in pairs}
    sv = {p: _dot(s[p].astype(BF16), v[p]) for p in pairs}
    outs = [[] for _ in seqs]
    for p in pairs:
        si, hd = p
        c_state, n_state, _ = state(p)
        num = w_out[p] * qc[p] + sv[p]
        qn = (w_out[p] * jnp.sum(q[p].astype(F32) * n_state, axis=1, keepdims=True)
              + jnp.sum(s[p], axis=1, keepdims=True))
        h = num / jnp.maximum(jnp.abs(qn), jnp.exp(-mt[p]))
        seqs[si][4](hd, decay[p] * c_state + kv[p], decay[p] * n_state + jnp.sum(kw[p], axis=0, keepdims=True),
                    m_end[p])
        outs[si].append(jax.nn.sigmoid(head(seqs[si][1], 0, hd)) * h)
    return [jnp.concatenate(o, axis=1) for o in outs]


def _mlstm_prompt_kernel(qkv_ref, o_ref, gt_ref, bias_ref, h_ref, c_ref, n_ref, m_ref, *, lc):
    @pl.when(pl.program_id(1) == 0)
    def _():
        c_ref[...] = jnp.zeros_like(c_ref)
        n_ref[...] = jnp.zeros_like(n_ref)
        m_ref[...] = jnp.zeros_like(m_ref)

    def get_state(hd):
        return c_ref[hd], n_ref[hd], m_ref[hd][:, 0:1]

    def put_state(hd, c_new, n_new, m_new):
        c_ref[hd] = c_new
        n_ref[hd] = n_new
        m_ref[hd] = jnp.broadcast_to(m_new, (1, LANES))

    (h,) = _mlstm_chunks([(qkv_ref[...], o_ref[...], gt_ref[...], get_state, put_state)], bias_ref[...], lc)
    h_ref[...] = h.astype(BF16)


def _mlstm_prompt(qkv, o, gates, bias_row, bsz, seq):
    lc = ML_CHUNK_PROMPT
    nc = seq // lc
    row = lambda width: pl.BlockSpec((lc, width), lambda b, c: (b * nc + c, 0))
    st = lambda *tail: pl.BlockSpec((None, ML_HEADS) + tail, lambda b, c: (b, 0) + (0,) * len(tail))
    return pl.pallas_call(
        functools.partial(_mlstm_prompt_kernel, lc=lc),
        grid=(bsz, nc),
        in_specs=[row(3 * D_ML), row(D_ML), row(GATE_PAD), pl.BlockSpec((1, GATE_PAD), lambda b, c: (0, 0))],
        out_specs=[row(D_ML), st(ML_DK, ML_DK), st(1, ML_DK), st(1, LANES)],
        out_shape=(jax.ShapeDtypeStruct((bsz * seq, D_ML), BF16),
                   jax.ShapeDtypeStruct((bsz, ML_HEADS, ML_DK, ML_DK), F32),
                   jax.ShapeDtypeStruct((bsz, ML_HEADS, 1, ML_DK), F32),
                   jax.ShapeDtypeStruct((bsz, ML_HEADS, 1, LANES), F32)),
        compiler_params=_cparams("arbitrary", "arbitrary"),
        name="mlstm_prompt",
    )(qkv, o, gates, bias_row)


def _mlstm_sample_kernel(qkv_ref, o_ref, gt_ref, bias_ref, c0_ref, n0_ref, m0_ref,
                         h_ref, c_ref, n_ref, m_ref, qkv_s, *, lc, nb):
    qkv_s[...] = qkv_ref[...].astype(F32)

    def seq(b):
        rows = slice(b * lc, (b + 1) * lc)

        def get_state(hd):
            return c0_ref[b, hd], n0_ref[b, hd], m0_ref[b, hd][:, 0:1]

        def put_state(hd, c_new, n_new, m_new):
            c_ref[b, hd] = c_new
            n_ref[b, hd] = n_new
            m_ref[b, hd] = jnp.broadcast_to(m_new, (1, LANES))

        return (qkv_s[rows, :].astype(BF16), o_ref[rows, :], gt_ref[rows, :], get_state, put_state)

    hs = _mlstm_chunks([seq(b) for b in range(nb)], bias_ref[...], lc)
    for b, h in enumerate(hs):
        h_ref[b * lc:(b + 1) * lc, :] = h


def _mlstm_sample(qkv, o, gates, bias_row, c0, n0, m0, row_block_offset, bsz, seq):
    nb = 8
    lc = seq
    rows = nb * lc
    off = row_block_offset // rows
    row = lambda width: pl.BlockSpec((rows, width), lambda i: (off + i, 0))
    st = lambda *tail: pl.BlockSpec((nb, ML_HEADS) + tail, lambda i: (i, 0) + (0,) * len(tail))
    return pl.pallas_call(
        functools.partial(_mlstm_sample_kernel, lc=lc, nb=nb),
        grid=(bsz // nb,),
        in_specs=[row(3 * D_ML), row(D_ML), row(GATE_PAD), pl.BlockSpec((1, GATE_PAD), lambda i: (0, 0)),
                  st(ML_DK, ML_DK), st(1, ML_DK), st(1, LANES)],
        out_specs=[pl.BlockSpec((rows, D_ML), lambda i: (i, 0)), st(ML_DK, ML_DK), st(1, ML_DK), st(1, LANES)],
        out_shape=(jax.ShapeDtypeStruct((bsz * seq, D_ML), F32),
                   jax.ShapeDtypeStruct((bsz, ML_HEADS, ML_DK, ML_DK), F32),
                   jax.ShapeDtypeStruct((bsz, ML_HEADS, 1, ML_DK), F32),
                   jax.ShapeDtypeStruct((bsz, ML_HEADS, 1, LANES), F32)),
        scratch_shapes=[pltpu.VMEM((rows, 3 * D_ML), F32)],
        compiler_params=_cparams("arbitrary"),
        name="mlstm_sample",
    )(qkv, o, gates, bias_row, c0, n0, m0)


def _merge_kernel(xp_ref, xs_ref, yp_ref, ys_ref, mp_ref, ms_ref, gsm_ref, wglu_ref, bglu_ref, wsu_ref, wmu_ref,
                  wout_ref, gffn_ref, wrh_ref, wrl_ref, br_ref, h1_ref, xn_ref, tw_ref, ti_ref, rk_ref, cnt_ref,
                  seen, logits_s, *, n_prompt_blocks):
    i = pl.program_id(0)

    @pl.when(i == 0)
    def _():
        seen[...] = jnp.zeros_like(seen)
        logits_s[...] = jnp.zeros_like(logits_s)

    _route_tile(logits_s[...], tw_ref, ti_ref, rk_ref, cnt_ref, seen, i)

    is_prompt = i < n_prompt_blocks
    x = jnp.where(is_prompt, xp_ref[...], xs_ref[...])
    y = jax.nn.gelu(jnp.where(is_prompt, yp_ref[...], ys_ref[...]))
    ym = jnp.where(is_prompt, mp_ref[...], ms_ref[...].astype(BF16))
    ys = y * jax.nn.sigmoid(_dot(y.astype(BF16), wglu_ref[...]) + bglu_ref[...])
    gsm = gsm_ref[...]
    merged = (jax.nn.sigmoid(gsm[:, :D_MODEL]) * _dot(ys.astype(BF16), wsu_ref[...])
              + jax.nn.sigmoid(gsm[:, D_MODEL:]) * _dot(ym, wmu_ref[...]))
    h1 = x + _dot(merged.astype(BF16), wout_ref[...])
    h1_ref[...] = h1
    xn = _rms(h1, gffn_ref[...])
    xn_ref[...] = xn
    xn_hi = xn.astype(BF16)
    xn_lo = (xn - xn_hi.astype(F32)).astype(BF16)
    nt = (((1,), (1,)), ((), ()))
    dot_nt = lambda a, b: lax.dot_general(a, b, nt, preferred_element_type=F32)
    logits_s[...] = (dot_nt(wrh_ref[...], xn_hi) + dot_nt(wrh_ref[...], xn_lo) + dot_nt(wrl_ref[...], xn_hi)
                     + br_ref[...])


def _route_tile(logits_t, tw_ref, ti_ref, rk_ref, cnt_ref, seen, step):
    ne, tm = logits_t.shape
    seen_in = jnp.where(step <= 1, 0.0, seen[:, 0:1])
    row =lax.broadcasted_iota(jnp.int32, (ne, tm), 0)
    work = logits_t
    vals, idxs, hits = [], [], []
    for _ in range(TOP_K):
        m = jnp.max(work, axis=0, keepdims=True)
        idx = jnp.min(jnp.where(work == m, row, ne), axis=0, keepdims=True)
        hit = row == idx
        vals.append(m)
        idxs.append(idx)
        hits.append(hit)
        work = jnp.where(hit, -jnp.inf, work)
    ex = [jnp.exp(v - vals[0]) for v in vals]
    den = ex[0] + ex[1] + ex[2] + ex[3]
    multi = sum(jnp.where(h, 1.0, 0.0) for h in hits)
    r = lax.broadcasted_iota(jnp.int32, (tm, tm), 0)
    c = lax.broadcasted_iota(jnp.int32, (tm, tm), 1)
    strict_upper = jnp.where(r < c, 1.0, 0.0).astype(BF16)
    earlier = _dot(multi.astype(BF16), strict_upper) + seen_in
    ranks = [jnp.sum(jnp.where(h, earlier, 0.0), axis=0, keepdims=True) for h in hits]
    pad = 8 - TOP_K
    tw_ref[...] = jnp.concatenate([e / den for e in ex] + [jnp.zeros((pad, tm), F32)], axis=0)
    ti_ref[...] = jnp.concatenate(idxs + [jnp.zeros((pad, tm), jnp.int32)], axis=0)
    rk_ref[...] = jnp.concatenate([rk.astype(jnp.int32) for rk in ranks] + [jnp.zeros((pad, tm), jnp.int32)],
                                  axis=0)
    seen[...] = jnp.broadcast_to(seen_in + jnp.sum(multi, axis=1, keepdims=True), seen.shape)
    cnt_ref[...] = seen[...]


def _merge(xp, xs, y_pre, ym, gsm, w_glu, b_glu, w_ssm_up, w_ml_up, w_out, g_ffn, w_router, b_router):
    tm = TM_WIDE
    t = gsm.shape[0]
    npb = xp.shape[0] // tm
    n_tiles = t // tm
    nsb = n_tiles - npb
    xp_spec, xs_spec = _two_source_specs(tm, D_MODEL, npb, nsb)
    yp_spec, ys_spec = _two_source_specs(tm, D_SSM, npb, nsb)
    mp_spec, ms_spec = _two_source_specs(tm, D_ML, npb, nsb)
    row = lambda width: pl.BlockSpec((tm, width), lambda i: (jnp.minimum(i, n_tiles - 1), 0))
    wr = w_router.T
    wr_hi = wr.astype(BF16)
    wr_lo = (wr - wr_hi.astype(F32)).astype(BF16)
    br = b_router.reshape(N_EXPERTS, 1)
    tok = lambda: pl.BlockSpec((8, tm), lambda i: (0, jnp.maximum(i - 1, 0)))
    return pl.pallas_call(
        functools.partial(_merge_kernel, n_prompt_blocks=npb),
        grid=(n_tiles + 1,),
        in_specs=[xp_spec, xs_spec, yp_spec, ys_spec, mp_spec, ms_spec, row(2 * D_MODEL), _full((D_SSM, D_SSM)),
                  _full((1, D_SSM)), _full((D_SSM, D_MODEL)), _full((D_ML, D_MODEL)),
                  _full((D_MODEL, D_MODEL)), _full((1, D_MODEL)), _full((N_EXPERTS, D_MODEL)),
                  _full((N_EXPERTS, D_MODEL)), _full((N_EXPERTS, 1))],
        out_specs=[row(D_MODEL), row(D_MODEL), tok(), tok(), tok(), _full((N_EXPERTS, LANES))],
        out_shape=(jax.ShapeDtypeStruct((t, D_MODEL), F32), jax.ShapeDtypeStruct((t, D_MODEL), F32),
                   jax.ShapeDtypeStruct((8, t), F32), jax.ShapeDtypeStruct((8, t), jnp.int32),
                   jax.ShapeDtypeStruct((8, t), jnp.int32), jax.ShapeDtypeStruct((N_EXPERTS, LANES), F32)),
        scratch_shapes=[pltpu.VMEM((N_EXPERTS, LANES), F32), pltpu.VMEM((N_EXPERTS, tm), F32)],
        compiler_params=_cparams("arbitrary"),
        name="merge",
    )(xp, xs, *y_pre, *ym, gsm, w_glu.astype(BF16), b_glu.reshape(1, D_SSM), w_ssm_up.astype(BF16),
      w_ml_up.astype(BF16), w_out.astype(BF16), g_ffn.reshape(1, D_MODEL), wr_hi, wr_lo, br)


def _route(top_idx, rank, counts, tm):
    n_assign = top_idx.shape[0] * top_idx.shape[1]
    padded = (counts + tm - 1) // tm * tm
    pad_end = jnp.cumsum(padded)
    pad_start = pad_end - padded
    onehot = top_idx[:, :, None] == jnp.arange(N_EXPERTS, dtype=jnp.int32)[None, None, :]
    pos = jnp.sum(jnp.where(onehot, pad_start[None, None, :], 0), axis=2) + rank
    n_blocks = -(-(n_assign + N_EXPERTS * (tm - 1)) // tm)
    block_row0 = jnp.arange(n_blocks, dtype=jnp.int32) * tm
    block_e = jnp.minimum(jnp.sum((pad_end[None, :] <= block_row0[:, None]).astype(jnp.int32), axis=1),
                          N_EXPERTS - 1)
    n_valid = (pad_end[-1] // tm).astype(jnp.int32).reshape(1)
    ids = jnp.arange(N_EXPERTS, dtype=jnp.int32)
    later = (ids[None, :] > ids[:, None]) & (counts[None, :] > 0)
    next_e = jnp.min(jnp.where(later, ids[None, :], N_EXPERTS), axis=1)
    next_e = jnp.where(next_e == N_EXPERTS, -1, next_e)[block_e]
    fill_lo = pad_start + counts
    return pos, fill_lo, pad_end, n_blocks * tm, block_e, next_e, n_valid


def _dispatch_copy(x_ref, rows_hbm, sem, r, dst):
    return pltpu.make_async_copy(x_ref.at[pl.ds(r, 1), :], rows_hbm.at[pl.ds(dst, 1), :], sem)


def _dispatch_kernel(lo_ref, hi_ref, pos_ref, x_ref, rows_hbm, zero_s, sem, zero_sem):
    tm = x_ref.shape[0]

    @pl.when(pl.program_id(0) == 0)
    def _():
        zero_s[...] = jnp.zeros_like(zero_s)

        def tail(do):
            def body(b, c):
                rows = pl.ds(pl.multiple_of(b * tm, tm), tm)
                do(pltpu.make_async_copy(zero_s, rows_hbm.at[rows, :], zero_sem))
                return c
            lax.fori_loop(hi_ref[N_EXPERTS - 1] // tm, rows_hbm.shape[0] // tm, body, 0)

        tail(lambda c: c.start())
        tail(lambda c: c.wait())

        def fill(e, carry):
            def each(do):
                def body(r, c):
                    do(_dispatch_copy(zero_s, rows_hbm, zero_sem, 0, r))
                    return c
                lax.fori_loop(lo_ref[e], hi_ref[e], body, 0)
            each(lambda c: c.start())
            each(lambda c: c.wait())
            return carry

        lax.fori_loop(0, N_EXPERTS, fill, 0)

    def each(do):
        for r in range(tm):
            for k in range(TOP_K):
                do(_dispatch_copy(x_ref, rows_hbm, sem, r, pos_ref[r * TOP_K + k]), k)

    each(lambda c, k: c.start(priority=k % 2))
    each(lambda c, k: c.wait())


def _dispatch(xn, pos, fill_lo, pad_end, n_rows):
    tm = TM_TOKENS
    t = xn.shape[0]
    grid_spec = pltpu.PrefetchScalarGridSpec(
        num_scalar_prefetch=2,
        grid=(t // tm,),
        in_specs=[pl.BlockSpec((tm * TOP_K,), lambda i, lo, hi: (i,), memory_space=pltpu.SMEM),
                  pl.BlockSpec((tm, D_MODEL), lambda i, lo, hi: (i, 0))],
        out_specs=pl.BlockSpec(memory_space=pl.ANY),
        scratch_shapes=[pltpu.VMEM((tm, D_MODEL), F32), pltpu.SemaphoreType.DMA(()), pltpu.SemaphoreType.DMA(())],
    )
    return pl.pallas_call(
        _dispatch_kernel,
        grid_spec=grid_spec,
        out_shape=jax.ShapeDtypeStruct((n_rows, D_MODEL), F32),
        compiler_params=_cparams("arbitrary"),
        name="moe_dispatch",
    )(fill_lo, pad_end, pos.reshape(-1), xn)


def _moe_weight_copies(e, w_hbm, wbuf, sems):
    return [pltpu.make_async_copy(w.at[e], wbuf.at[k], sems.at[k]) for k, w in enumerate(w_hbm)]


def _moe_kernel(be_ref, ne_ref, nv_ref, x_ref, wg_hbm, bg_ref, wu_hbm, bu_ref, wd_hbm, bd_ref, y_ref,
                w_bf, wbuf, sems):
    i = pl.program_id(0)
    e = be_ref[i]
    prev = be_ref[jnp.maximum(i - 1, 0)]
    valid = i < nv_ref[0]
    first = jnp.logical_or(i == 0, e != prev)
    copies = functools.partial(_moe_weight_copies, w_hbm=(wg_hbm, wu_hbm, wd_hbm), wbuf=wbuf, sems=sems)

    @pl.when(i == 0)
    def _():
        for c in copies(e):
            c.start()

    @pl.when(jnp.logical_and(valid, first))
    def _():
        nxt = ne_ref[i]
        for k, c in enumerate(copies(e)):
            c.wait()
            w_bf[k] = wbuf[k].astype(BF16)

        @pl.when(nxt >= 0)
        def _():
            for c in copies(nxt):
                c.start(priority=1)

    @pl.when(valid)
    def _():
        x = x_ref[...].astype(BF16)
        g =jnp.minimum(_dot(x, w_bf[0]) + bg_ref[...], SWIGLU_LIMIT)
        u = jnp.clip(_dot(x, w_bf[1]) + bu_ref[...], -SWIGLU_LIMIT, SWIGLU_LIMIT)
        a = g * jax.nn.sigmoid(SWIGLU_ALPHA * g) * (u + 1.0)
        y_ref[...] = _dot(a.astype(BF16), w_bf[2]) + bd_ref[...]

    @pl.when(jnp.logical_not(valid))
    def _():
        y_ref[...] = jnp.zeros_like(y_ref)


def _moe_experts(x_rows, block_e, next_e, n_valid, w_gate, b_gate, w_up, b_up, w_down, b_down):
    tm = TM_MOE
    n_rows = x_rows.shape[0]
    wspec = pl.BlockSpec(memory_space=pl.ANY)
    bspec = pl.BlockSpec((None, 1, D_MODEL), lambda i, be, ne, nv: (be[i], 0, 0))
    rows = pl.BlockSpec((tm, D_MODEL), lambda i, be, ne, nv: (i, 0))
    grid_spec = pltpu.PrefetchScalarGridSpec(
        num_scalar_prefetch=3,
        grid=(n_rows // tm,),
        in_specs=[rows, wspec, bspec, wspec, bspec, wspec, bspec],
        out_specs=rows,
        scratch_shapes=[pltpu.VMEM((3, D_MODEL, D_MODEL), BF16), pltpu.VMEM((3, D_MODEL, D_MODEL), F32),
                        pltpu.SemaphoreType.DMA((3,))],
    )
    b3 = lambda b: b.reshape(N_EXPERTS, 1, D_MODEL)
    return pl.pallas_call(
        _moe_kernel,
        grid_spec=grid_spec,
        out_shape=jax.ShapeDtypeStruct((n_rows, D_MODEL), F32),
        compiler_params=_cparams("arbitrary"),
        name="moe_experts",
    )(block_e, next_e, n_valid, x_rows, w_gate, b3(b_gate), w_up, b3(b_up), w_down, b3(b_down))


def _ple_kernel(h1_ref, rows_ref, tw_ref, p_ref, gple_ref, wg_ref, wp_ref, gfin_ref, y_ref):
    tw = tw_ref[...]
    h2 = h1_ref[...]
    for k in range(TOP_K):
        h2 = h2 + rows_ref[k] * tw[:, k:k + 1]
    gate =jax.nn.sigmoid(_dot(_rms(h2, gple_ref[...]).astype(BF16), wg_ref[...]))
    h3 = h2 + gate * _dot(p_ref[...].astype(BF16), wp_ref[...])
    y_ref[...] = _rms(h3, gfin_ref[...])


def _ple_final(h1, expert_rows, top_w, p, row_block_offset, g_ple, w_ple_gate, w_ple_proj, g_final):
    tm = TM_WIDE
    n = p.shape[0]
    off = row_block_offset // tm
    src = lambda width: pl.BlockSpec((tm, width), lambda i: (off + i, 0))
    loc = lambda width: pl.BlockSpec((tm, width), lambda i: (i, 0))
    return pl.pallas_call(
        _ple_kernel,
        grid=(n // tm,),
        in_specs=[src(D_MODEL), pl.BlockSpec((TOP_K, tm, D_MODEL), lambda i: (0, off + i, 0)), src(top_w.shape[1]),
                  loc(D_PLE), _full((1, D_MODEL)), _full((D_MODEL, D_MODEL)), _full((D_PLE, D_MODEL)),
                  _full((1, D_MODEL))],
        out_specs=loc(D_MODEL),
        out_shape=jax.ShapeDtypeStruct((n, D_MODEL), F32),
        compiler_params=_cparams("arbitrary"),
        name="ple_final",
    )(h1, expert_rows, top_w, p, g_ple.reshape(1, D_MODEL), w_ple_gate.astype(BF16),
      w_ple_proj.astype(BF16), g_final.reshape(1, D_MODEL))


def kernel(x_prompt, x_sample, p_prompt, p_sample, state_ssm_re, state_ssm_im, state_ml_c, state_ml_n, state_ml_m, g_mix, w_in, ssm_a_re, ssm_a_im, ssm_log_dt, ssm_b_re, ssm_b_im, ssm_c_re, ssm_c_im, ssm_d, ssm_w_glu, ssm_b_glu, ml_b_ig, ml_b_fg, w_ssm_up, w_ml_up, w_out, g_ffn, w_router, b_router, w_gate, b_gate, w_up, b_up, w_down, b_down, g_ple, w_ple_gate, w_ple_proj, g_final):
    assert g_mix.shape[0] == 1, "single-layer trunk"
    bp, lp, _ = x_prompt.shape
    bs, ls, _ = x_sample.shape
    tp, ts = bp * lp, bs * ls
    t = tp + ts
    xp = x_prompt.reshape(tp, D_MODEL)
    xs = x_sample.reshape(ts, D_MODEL)

    u, qkv, o, gates, gsm = _inproj(xp, xs, g_mix[0], w_in[0])

    s5_args = (ssm_a_re[0], ssm_a_im[0], ssm_log_dt[0], ssm_b_re[0], ssm_b_im[0], ssm_c_re[0],
               ssm_c_im[0], ssm_d[0])
    zero_state = jnp.zeros((S5_TILES, bp, 2 * S5_TILE_STATE), F32)
    tables = {tc: _s5_tables(*s5_args, tc) for tc in {S5_CHUNK, ls}}
    y_p, f_p = _s5(u, 0, tables[S5_CHUNK], zero_state,
                   tc=S5_CHUNK, n_chunks=lp // S5_CHUNK, nb=bp // 2, n_splits=2)
    y_s, f_s = _s5(u, tp, tables[ls], _s5_state_to_tiles(state_ssm_re[0], state_ssm_im[0]),
                   tc=ls, n_chunks=1, nb=bs, n_splits=1)
    re_p, im_p = _s5_state_from_tiles(f_p)
    re_s, im_s = _s5_state_from_tiles(f_s)

    bias_row = jnp.pad(jnp.concatenate([ml_b_ig[0], ml_b_fg[0]]), (0, GATE_PAD - 2 * ML_HEADS)).reshape(1, GATE_PAD)
    hm_p, c_p, n_p, m_p = _mlstm_prompt(qkv, o, gates, bias_row, bp, lp)
    m0 = jnp.broadcast_to(state_ml_m[0][:, :, None, None], (bs, ML_HEADS, 1, LANES))
    hm_s, c_s, n_s, m_s = _mlstm_sample(qkv, o, gates, bias_row, state_ml_c[0],
                                        state_ml_n[0].reshape(bs, ML_HEADS, 1, ML_DK), m0, tp, bs, ls)

    h1, xn, top_w_t, top_idx_t, rank_t, counts = _merge(xp, xs, (y_p, y_s), (hm_p, hm_s), gsm, ssm_w_glu[0], ssm_b_glu[0], w_ssm_up[0], w_ml_up[0],
                            w_out[0], g_ffn[0], w_router[0], b_router[0])

    pos, fill_lo, pad_end, n_rows, block_e, next_e, n_valid = _route(
        top_idx_t[:TOP_K], rank_t[:TOP_K], counts[:, 0].astype(jnp.int32), TM_MOE)
    x_rows = _dispatch(xn, pos.T, fill_lo, pad_end, n_rows)
    expert_w = lambda w: w.reshape(N_EXPERTS, D_MODEL, D_MODEL)
    y_rows = _moe_experts(x_rows, block_e, next_e, n_valid, expert_w(w_gate), b_gate[0], expert_w(w_up),
                          b_up[0], expert_w(w_down), b_down[0])
    expert_rows = y_rows[pos.reshape(-1)].reshape(TOP_K, t, D_MODEL)
    top_w_pad = top_w_t.T

    ple_w = (g_ple[0], w_ple_gate[0], w_ple_proj[0], g_final)
    y_prompt = _ple_final(h1, expert_rows, top_w_pad, p_prompt[0].reshape(tp, D_PLE), 0, *ple_w)
    y_sample = _ple_final(h1, expert_rows, top_w_pad, p_sample[0].reshape(ts, D_PLE), tp, *ple_w)

    return (y_prompt.reshape(bp, lp, D_MODEL), y_sample.reshape(bs, ls, D_MODEL),
            re_p, im_p, c_p[None], n_p.reshape(1, bp, ML_HEADS, ML_DK), m_p[:, :, 0, 0][None],
            re_s, im_s, c_s[None], n_s.reshape(1, bs, ML_HEADS, ML_DK), m_s[:, :, 0, 0][None])
```

```python
import functools

import jax
import jax.numpy as jnp
from jax import lax
from jax.experimental import pallas as pl
from jax.experimental.pallas import tpu as pltpu

F32 = jnp.float32
BF16 = jnp.bfloat16
HIGHEST = lax.Precision.HIGHEST

D_MODEL = 1024
D_SSM = 512
SSM_GROUP = 16
N_GROUPS = 32
SSM_STATE = 64
ML_HEADS = 4
ML_DK = 128
D_ML = 512
N_EXPERTS = 32
TOP_K = 4
SWIGLU_LIMIT = 7.0
SWIGLU_ALPHA = 1.702
D_PLE = 256
RMS_EPS = 1e-6

LANES = 128
GATE_PAD = LANES
S5_CHUNK = 8
ML_CHUNK_PROMPT = 256
TM_TOKENS = 256
TM_WIDE = 512
TM_MOE = 512
VMEM_LIMIT = 56 * 1024 * 1024


def _cparams(*sem):
    return pltpu.CompilerParams(dimension_semantics=sem, vmem_limit_bytes=VMEM_LIMIT)


def _rms(x, g):
    return x * lax.rsqrt(jnp.mean(x * x, axis=-1, keepdims=True) + RMS_EPS) * g


def _dot(a, b):
    return jnp.dot(a, b, preferred_element_type=F32)


def _dot_hi(a, b):
    return jnp.dot(a, b, preferred_element_type=F32, precision=HIGHEST)


def _full(shape):
    n = len(shape)
    return pl.BlockSpec(shape, lambda *_: (0,) * n)


def _inproj_kernel(xp_ref, xs_ref, g_ref, wu_ref, wqkv_ref, wo_ref, wgt_ref, wgsm_ref,
                   u_ref, qkv_ref, o_ref, gt_ref, gsm_ref, *, n_prompt_blocks):
    i = pl.program_id(0)
    x = jnp.where(i < n_prompt_blocks, xp_ref[...], xs_ref[...])
    hn = _rms(x, g_ref[...]).astype(BF16)
    u_ref[...] = _dot(hn, wu_ref[...])
    qkv = _dot(hn, wqkv_ref[...])
    col = lax.broadcasted_iota(jnp.int32, (1, 3 * D_ML), 1)
    k_scale = jnp.where((col >= D_ML) & (col < 2 * D_ML), ML_DK ** -0.5, 1.0).astype(F32)
    qkv_ref[...] = (qkv * k_scale).astype(BF16)
    o_ref[...] = _dot(hn, wo_ref[...])
    gt_ref[...] = _dot(hn, wgt_ref[...])
    gsm_ref[...] = _dot(hn, wgsm_ref[...])


def _two_source_specs(tm, width, n_prompt_blocks, n_sample_blocks):
    last_p, last_s = n_prompt_blocks - 1, n_sample_blocks - 1
    return (pl.BlockSpec((tm, width), lambda i: (jnp.minimum(i, last_p), 0)),
            pl.BlockSpec((tm, width), lambda i: (jnp.clip(i - n_prompt_blocks, 0, last_s), 0)))


def _inproj(xp, xs, g_mix, w_in):
    tm = TM_WIDE
    tp, ts = xp.shape[0], xs.shape[0]
    t = tp + ts
    npb = tp // tm
    w = w_in.astype(BF16)
    o0 = D_SSM
    wu = w[:, :o0]
    wqkv = w[:, o0:o0 + 3 * D_ML]
    wo = w[:, o0 + 3 * D_ML:o0 + 4 * D_ML]
    g0 = o0 + 4 * D_ML
    wgt = jnp.pad(w[:, g0:g0 + 2 * ML_HEADS], ((0, 0), (0, GATE_PAD - 2 * ML_HEADS)))
    wgsm = w[:, g0 + 2 * ML_HEADS:]
    xp_spec, xs_spec = _two_source_specs(tm, D_MODEL, npb, ts // tm)
    outs =(jax.ShapeDtypeStruct((t, D_SSM), F32), jax.ShapeDtypeStruct((t, 3 * D_ML), BF16),
            jax.ShapeDtypeStruct((t, D_ML), F32), jax.ShapeDtypeStruct((t, GATE_PAD), F32),
            jax.ShapeDtypeStruct((t, 2 * D_MODEL), F32))
    row = lambda width: pl.BlockSpec((tm, width), lambda i: (i, 0))
    return pl.pallas_call(
        functools.partial(_inproj_kernel, n_prompt_blocks=npb),
        grid=(t // tm,),
        in_specs=[xp_spec, xs_spec, _full((1, D_MODEL)), _full(wu.shape), _full(wqkv.shape),
                  _full(wo.shape), _full(wgt.shape), _full(wgsm.shape)],
        out_specs=[row(D_SSM), row(3 * D_ML), row(D_ML), row(GATE_PAD), row(2 * D_MODEL)],
        out_shape=outs,
        compiler_params=_cparams("arbitrary"),
        name="inproj",
    )(xp, xs, g_mix.reshape(1, D_MODEL), wu, wqkv, wo, wgt, wgsm)


S5_TILES = D_SSM // LANES
S5_TILE_GROUPS = LANES // SSM_GROUP
S5_TILE_STATE = S5_TILE_GROUPS * SSM_STATE


def _block_diag_tiles(x):
    gt = S5_TILE_GROUPS
    n, _, r, c = x.shape
    x5 = x.reshape(n, S5_TILES, gt, r, c)
    eye = jnp.eye(gt, dtype=x.dtype)
    return (x5[:, :, :, :, None, :] * eye[None, None, :, None, :, None]).reshape(n, S5_TILES, gt * r, gt * c)


def _s5_tables(a_re, a_im, log_dt, b_re, b_im, c_re, c_im, d_skip, tc):
    ein = functools.partial(jnp.einsum, precision=HIGHEST)
    dt = jnp.exp(log_dt)[:, None]
    mag = jnp.exp(a_re * dt)
    abar_r, abar_i = mag * jnp.cos(a_im * dt), mag * jnp.sin(a_im * dt)
    den = a_re * a_re + a_im * a_im
    nr, ni = abar_r - 1.0, abar_i
    coef_r = (nr * a_re + ni * a_im) / den
    coef_i = (ni * a_re - nr * a_im) / den
    bbar_r = coef_r[..., None] * b_re - coef_i[..., None] * b_im
    bbar_i = coef_r[..., None] * b_im + coef_i[..., None] * b_re

    def abar_pow(j):
        jj = j[..., None, None]
        mag_j = jnp.where(jj >= 0, jnp.exp(jj * (a_re * dt)), 0.0)
        return mag_j * jnp.cos(jj * (a_im * dt)), mag_j * jnp.sin(jj * (a_im * dt))

    half = tc // 2
    steps = jnp.arange(tc, dtype=F32)
    at_r, at_i = abar_pow(jnp.full((), tc, F32))
    lags = (2.0 * jnp.arange(half, dtype=F32)[:, None, None]
            + jnp.array([[0.0, 1.0], [-1.0, 0.0]], F32)[None])
    lag_r, lag_i = abar_pow(lags.reshape(-1))
    ab_r = lag_r[..., None] * bbar_r - lag_i[..., None] * bbar_i
    ab_i = lag_r[..., None] * bbar_i + lag_i[..., None] * bbar_r
    kern = ein('ghp,jgpk->jgkh', c_re, ab_r) - ein('ghp,jgpk->jgkh', c_im, ab_i)
    bd_lag = _block_diag_tiles(kern.astype(BF16)).reshape(half, 2, 2, S5_TILES, LANES, LANES)
    toe = jnp.transpose(bd_lag, (3, 0, 1, 4, 2, 5)).reshape(S5_TILES, half, 2 * LANES, 2 * LANES)
    rev_r, rev_i = abar_pow(tc - 1.0 - steps)
    s_r = jnp.transpose(rev_r[..., None] * bbar_r - rev_i[..., None] * bbar_i, (0, 1, 3, 2))
    s_i = jnp.transpose(rev_r[..., None] * bbar_i + rev_i[..., None] * bbar_r, (0, 1, 3, 2))
    a1_r, a1_i = abar_pow(steps + 1.0)
    p_r = c_re[None] * a1_r[:, :, None, :] - c_im[None] * a1_i[:, :, None, :]
    p_i = -c_re[None] * a1_i[:, :, None, :] - c_im[None] * a1_r[:, :, None, :]

    def compact(x):
        x = x.astype(BF16).reshape(half, 2, S5_TILES, S5_TILE_GROUPS, SSM_GROUP, SSM_STATE)
        x = jnp.transpose(x, (2, 0, 1, 3, 4, 5)).reshape(S5_TILES, half, 2 * LANES, SSM_STATE)
        return jnp.concatenate([x, x], axis=3)

    s_tab = jnp.stack([compact(s_r), compact(s_i)], axis=2)
    p_tab = jnp.stack([compact(p_r), compact(p_i)], axis=2)
    a_tab =jnp.stack([at_r.reshape(S5_TILES, S5_TILE_STATE), at_i.reshape(S5_TILES, S5_TILE_STATE)], axis=1)
    d_tab = d_skip.reshape(S5_TILES, 1, LANES)
    return toe, s_tab, p_tab, a_tab, d_tab


def _s5_kernel(u_ref, t_ref, s_ref, p_ref, a_ref, d_ref, h0_ref, y_ref, f_ref, loc, xprev,
               *, tc, n_chunks, nb):
    r = n_chunks * nb
    half = tc // 2
    ns = S5_TILE_STATE
    step_rows = lambda t: pl.ds(t, r, stride=tc)
    v = [u_ref[step_rows(t), :] for t in range(tc)]
    vp = [jnp.concatenate([v[2 * a].astype(BF16), v[2 * a + 1].astype(BF16)], axis=1) for a in range(half)]

    row_group = (lax.broadcasted_iota(jnp.int32, (2 * LANES, ns), 0) >> 4) & (S5_TILE_GROUPS - 1)
    col_group = lax.broadcasted_iota(jnp.int32, (2 * LANES, ns), 1) >> 6
    own_group = jnp.where(row_group == col_group, 1.0, 0.0).astype(BF16)

    def block_diag(tab):
        reps = ns // LANES
        return jnp.concatenate([jnp.tile(tab[0], (1, reps)) * own_group, jnp.tile(tab[1], (1, reps)) * own_group],
                               axis=1)

    acc = _dot(vp[0], block_diag(s_ref[0]))
    for a in range(1, half):
        acc = acc + _dot(vp[a], block_diag(s_ref[a]))
    nt = ns // LANES
    lane_tile = lambda k: slice(k * LANES, (k + 1) * LANES)
    for k in range(2 * nt):
        loc[k] = acc[:, lane_tile(k)]
    abar = a_ref[...]

    def body(c, carry):
        rows = pl.ds(c, nb, stride=n_chunks) if n_chunks > 1 else pl.ds(0, nb)
        new = []
        for k in range(nt):
            xr, xi = carry[k], carry[nt + k]
            xprev[k, rows, :] = xr
            xprev[nt + k, rows, :] = xi
            ar, ai = abar[0:1, lane_tile(k)], abar[1:2, lane_tile(k)]
            new.append((ar * xr - ai * xi + loc[k, rows, :], ar * xi + ai * xr + loc[nt + k, rows, :]))
        return tuple(n[0] for n in new) + tuple(n[1] for n in new)

    x_end = lax.fori_loop(0, n_chunks, body, tuple(h0_ref[:, lane_tile(k)] for k in range(2 * nt)),
                          unroll=min(8, n_chunks))
    for k in range(2 * nt):
        f_ref[:, lane_tile(k)] = x_end[k]
    xp = jnp.concatenate([xprev[k] for k in range(2 * nt)], axis=1).astype(BF16)
    d = d_ref[...]
    nt_dims = (((1,), (1,)), ((), ()))
    for a2 in range(half):
        acc = lax.dot_general(xp, block_diag(p_ref[a2]), nt_dims, preferred_element_type=F32)
        for a in range(a2 + 1):
            acc = acc + _dot(vp[a], t_ref[a2 - a])
        for k in range(2):
            t = 2 * a2 + k
            y_ref[step_rows(t), :] = acc[:, k * LANES:(k + 1) * LANES] + v[t] * d


def _s5(u, row_block_offset, tables, h0, *, tc, n_chunks, nb, n_splits):
    toe, s_tab, p_tab, a_tab, d_tab = tables
    rows = nb * n_chunks * tc
    off = row_block_offset // rows
    half = tc // 2
    ns2 = 2 * S5_TILE_STATE
    tile = lambda *tail: pl.BlockSpec((None,) + tail, lambda j, s: (j,) + (0,) * len(tail))
    state = pl.BlockSpec((None, None, nb, ns2), lambda j, s: (j, s, 0, 0))
    y, f = pl.pallas_call(
        functools.partial(_s5_kernel, tc=tc, n_chunks=n_chunks, nb=nb),
        grid=(S5_TILES, n_splits),
        in_specs=[pl.BlockSpec((rows, LANES), lambda j, s: (off + s, j)),
                  tile(half, 2 * LANES, 2 * LANES), tile(half, 2, 2 * LANES, LANES), tile(half, 2, 2 * LANES, LANES),
                  tile(2, S5_TILE_STATE), tile(1, LANES), state],
        out_specs=[pl.BlockSpec((rows, LANES), lambda j, s: (s, j)), state],
        out_shape=(jax.ShapeDtypeStruct((rows * n_splits, D_SSM), F32),
                   jax.ShapeDtypeStruct((S5_TILES, n_splits, nb, ns2), F32)),
        scratch_shapes=[pltpu.VMEM((ns2 // LANES, nb * n_chunks, LANES), F32)] * 2,
        compiler_params=_cparams("arbitrary", "arbitrary"),
        name=f"s5_c{n_chunks}",
    )(u, toe, s_tab, p_tab, a_tab, d_tab, h0.reshape(S5_TILES, n_splits, nb, ns2))
    return y, f.reshape(S5_TILES, n_splits * nb, ns2)


def _s5_state_to_tiles(s_re, s_im):
    b = s_re.shape[0]
    f = lambda s: s.reshape(b, S5_TILES, S5_TILE_STATE).transpose(1, 0, 2)
    return jnp.concatenate([f(s_re), f(s_im)], axis=2)


def _s5_state_from_tiles(f):
    b = f.shape[1]
    g = lambda s: s.transpose(1, 0, 2).reshape(1, b, N_GROUPS, SSM_STATE)
    return g(f[:, :, :S5_TILE_STATE]), g(f[:, :, S5_TILE_STATE:])


def _log_sigmoid(x):
    return jnp.minimum(x, 0.0) - jnp.log1p(jnp.exp(-jnp.abs(x)))


def _mlstm_gates(gates, bias_row, lc):
    g = gates + bias_row
    col = lax.broadcasted_iota(jnp.int32, (1, GATE_PAD), 1)
    gl = jnp.where(col >= ML_HEADS, _log_sigmoid(g), g)
    r = lax.broadcasted_iota(jnp.int32, (lc, lc), 0)
    c = lax.broadcasted_iota(jnp.int32, (lc, lc), 1)
    tril = (r >= c).astype(F32)
    bcols = _dot_hi(tril, gl)
    sel = (lax.broadcasted_iota(jnp.int32, (8, GATE_PAD), 0)
           == lax.broadcasted_iota(jnp.int32, (8, GATE_PAD), 1)).astype(F32)
    nt = (((1,), (1,)), ((), ()))
    grows = lax.dot_general(sel, gl, nt, precision=HIGHEST, preferred_element_type=F32)
    brows = lax.dot_general(sel, bcols, nt, precision=HIGHEST, preferred_element_type=F32)
    return gl, bcols, grows, brows, (r >= c)


def _mlstm_chunks(seqs, bias_row, lc):
    nt = (((1,), (1,)), ((), ()))
    tn = (((0,), (0,)), ((), ()))
    pairs = [(si, hd) for si in range(len(seqs)) for hd in range(ML_HEADS)]
    head = lambda x, part, hd: x[:, part * D_ML + hd * ML_DK:part * D_ML + (hd + 1) * ML_DK]
    q = {p: head(seqs[p[0]][0], 0, p[1]) for p in pairs}
    k = {p: head(seqs[p[0]][0], 1, p[1]) for p in pairs}
    v = {p: head(seqs[p[0]][0], 2, p[1]) for p in pairs}
    state = lambda p: seqs[p[0]][3](p[1])
    qk = {p: lax.dot_general(q[p], k[p], nt, preferred_element_type=F32) for p in pairs}
    qc = {p: _dot(q[p], state(p)[0].astype(BF16)) for p in pairs}
    tables = [_mlstm_gates(seq[2], bias_row, lc) for seq in seqs]
    w_in, w_out, mt, m_end, decay, kw = {}, {}, {}, {}, {}, {}
    for p in pairs:
        si, hd = p
        gl, bcols, grows, brows, causal = tables[si]
        f = ML_HEADS + hd
        ic, bc, ir, br = gl[:, hd:hd + 1], bcols[:, f:f + 1], grows[hd:hd + 1, :], brows[f:f + 1, :]
        m_state = state(p)[2]
        dmat = jnp.where(causal, bc - br + ir, -jnp.inf)
        inter = bc + m_state
        mt[p] = jnp.maximum(inter, jnp.max(dmat, axis=1, keepdims=True))
        w_in[p] = jnp.exp(dmat - mt[p])
        w_out[p] = jnp.exp(inter - mt[p])
        m_end[p] = mt[p][lc - 1:lc, :]
        b_last = bc[lc - 1:lc, :]
        decay[p] = jnp.exp(b_last + m_state - m_end[p])
        kw[p] = k[p].astype(F32) * jnp.exp(b_last - bc + ic - m_end[p])
    kv = {p: lax.dot_general(kw[p].astype(BF16), v[p], tn, preferred_element_type=F32) for p in pairs}
    s = {p: qk[p] * w_in[p] for p in pairs}
    sv = {p: _dot(s[p].astype(BF16), v[p]) for p in pairs}
    outs = [[] for _ in seqs]
    for p in pairs:
        si, hd = p
        c_state, n_state, _ = state(p)
        num = w_out[p] * qc[p] + sv[p]
        qn = (w_out[p] * jnp.sum(q[p].astype(F32) * n_state, axis=1, keepdims=True)
              + jnp.sum(s[p], axis=1, keepdims=True))
        h = num / jnp.maximum(jnp.abs(qn), jnp.exp(-mt[p]))
        seqs[si][4](hd, decay[p] * c_state + kv[p], decay[p] * n_state + jnp.sum(kw[p], axis=0, keepdims=True),
                    m_end[p])
        outs[si].append(jax.nn.sigmoid(head(seqs[si][1], 0, hd)) * h)
    return [jnp.concatenate(o, axis=1) for o in outs]


def _mlstm_prompt_kernel(qkv_ref, o_ref, gt_ref, bias_ref, h_ref, c_ref, n_ref, m_ref, *, lc):
    @pl.when(pl.program_id(1) == 0)
    def _():
        c_ref[...] = jnp.zeros_like(c_ref)
        n_ref[...] = jnp.zeros_like(n_ref)
        m_ref[...] = jnp.zeros_like(m_ref)

    def get_state(hd):
        return c_ref[hd], n_ref[hd], m_ref[hd][:, 0:1]

    def put_state(hd, c_new, n_new, m_new):
        c_ref[hd] = c_new
        n_ref[hd] = n_new
        m_ref[hd] = jnp.broadcast_to(m_new, (1, LANES))

    (h,) = _mlstm_chunks([(qkv_ref[...], o_ref[...], gt_ref[...], get_state, put_state)], bias_ref[...], lc)
    h_ref[...] = h.astype(BF16)


def _mlstm_prompt(qkv, o, gates, bias_row, bsz, seq):
    lc = ML_CHUNK_PROMPT
    nc = seq // lc
    row = lambda width: pl.BlockSpec((lc, width), lambda b, c: (b * nc + c, 0))
    st = lambda *tail: pl.BlockSpec((None, ML_HEADS) + tail, lambda b, c: (b, 0) + (0,) * len(tail))
    return pl.pallas_call(
        functools.partial(_mlstm_prompt_kernel, lc=lc),
        grid=(bsz, nc),
        in_specs=[row(3 * D_ML), row(D_ML), row(GATE_PAD), pl.BlockSpec((1, GATE_PAD), lambda b, c: (0, 0))],
        out_specs=[row(D_ML), st(ML_DK, ML_DK), st(1, ML_DK), st(1, LANES)],
        out_shape=(jax.ShapeDtypeStruct((bsz * seq, D_ML), BF16),
                   jax.ShapeDtypeStruct((bsz, ML_HEADS, ML_DK, ML_DK), F32),
                   jax.ShapeDtypeStruct((bsz, ML_HEADS, 1, ML_DK), F32),
                   jax.ShapeDtypeStruct((bsz, ML_HEADS, 1, LANES), F32)),
        compiler_params=_cparams("arbitrary", "arbitrary"),
        name="mlstm_prompt",
    )(qkv, o, gates, bias_row)


def _mlstm_sample_kernel(qkv_ref, o_ref, gt_ref, bias_ref, c0_ref, n0_ref, m0_ref,
                         h_ref, c_ref, n_ref, m_ref, qkv_s, *, lc, nb):
    qkv_s[...] = qkv_ref[...].astype(F32)

    def seq(b):
        rows = slice(b * lc, (b + 1) * lc)

        def get_state(hd):
            return c0_ref[b, hd], n0_ref[b, hd], m0_ref[b, hd][:, 0:1]

        def put_state(hd, c_new, n_new, m_new):
            c_ref[b, hd] = c_new
            n_ref[b, hd] = n_new
            m_ref[b, hd] = jnp.broadcast_to(m_new, (1, LANES))

        return (qkv_s[rows, :].astype(BF16), o_ref[rows, :], gt_ref[rows, :], get_state, put_state)

    hs = _mlstm_chunks([seq(b) for b in range(nb)], bias_ref[...], lc)
    for b, h in enumerate(hs):
        h_ref[b * lc:(b + 1) * lc, :] = h


def _mlstm_sample(qkv, o, gates, bias_row, c0, n0, m0, row_block_offset, bsz, seq):
    nb = 8
    lc = seq
    rows = nb * lc
    off = row_block_offset // rows
    row = lambda width: pl.BlockSpec((rows, width), lambda i: (off + i, 0))
    st = lambda *tail: pl.BlockSpec((nb, ML_HEADS) + tail, lambda i: (i, 0) + (0,) * len(tail))
    return pl.pallas_call(
        functools.partial(_mlstm_sample_kernel, lc=lc, nb=nb),
        grid=(bsz // nb,),
        in_specs=[row(3 * D_ML), row(D_ML), row(GATE_PAD), pl.BlockSpec((1, GATE_PAD), lambda i: (0, 0)),
                  st(ML_DK, ML_DK), st(1, ML_DK), st(1, LANES)],
        out_specs=[pl.BlockSpec((rows, D_ML), lambda i: (i, 0)), st(ML_DK, ML_DK), st(1, ML_DK), st(1, LANES)],
        out_shape=(jax.ShapeDtypeStruct((bsz * seq, D_ML), F32),
                   jax.ShapeDtypeStruct((bsz, ML_HEADS, ML_DK, ML_DK), F32),
                   jax.ShapeDtypeStruct((bsz, ML_HEADS, 1, ML_DK), F32),
                   jax.ShapeDtypeStruct((bsz, ML_HEADS, 1, LANES), F32)),
        scratch_shapes=[pltpu.VMEM((rows, 3 * D_ML), F32)],
        compiler_params=_cparams("arbitrary"),
        name="mlstm_sample",
    )(qkv, o, gates, bias_row, c0, n0, m0)


def _merge_kernel(xp_ref, xs_ref, yp_ref, ys_ref, mp_ref, ms_ref, gsm_ref, wglu_ref, bglu_ref, wsu_ref, wmu_ref,
                  wout_ref, gffn_ref, wrh_ref, wrl_ref, br_ref, h1_ref, xn_ref, tw_ref, ti_ref, rk_ref, cnt_ref,
                  seen, logits_s, *, n_prompt_blocks):
    i = pl.program_id(0)

    @pl.when(i == 0)
    def _():
        seen[...] = jnp.zeros_like(seen)
        logits_s[...] = jnp.zeros_like(logits_s)

    _route_tile(logits_s[...], tw_ref, ti_ref, rk_ref, cnt_ref, seen, i)

    is_prompt = i < n_prompt_blocks
    x = jnp.where(is_prompt, xp_ref[...], xs_ref[...])
    y = jax.nn.gelu(jnp.where(is_prompt, yp_ref[...], ys_ref[...]))
    ym = jnp.where(is_prompt, mp_ref[...], ms_ref[...].astype(BF16))
    ys = y * jax.nn.sigmoid(_dot(y.astype(BF16), wglu_ref[...]) + bglu_ref[...])
    gsm = gsm_ref[...]
    merged = (jax.nn.sigmoid(gsm[:, :D_MODEL]) * _dot(ys.astype(BF16), wsu_ref[...])
              + jax.nn.sigmoid(gsm[:, D_MODEL:]) * _dot(ym, wmu_ref[...]))
    h1 = x + _dot(merged.astype(BF16), wout_ref[...])
    h1_ref[...] = h1
    xn = _rms(h1, gffn_ref[...])
    xn_ref[...] = xn
    xn_hi = xn.astype(BF16)
    xn_lo = (xn - xn_hi.astype(F32)).astype(BF16)
    nt = (((1,), (1,)), ((), ()))
    dot_nt = lambda a, b: lax.dot_general(a, b, nt, preferred_element_type=F32)
    logits_s[...] = (dot_nt(wrh_ref[...], xn_hi) + dot_nt(wrh_ref[...], xn_lo) + dot_nt(wrl_ref[...], xn_hi)
                     + br_ref[...])


def _route_tile(logits_t, tw_ref, ti_ref, rk_ref, cnt_ref, seen, step):
    ne, tm = logits_t.shape
    seen_in = jnp.where(step <= 1, 0.0, seen[:, 0:1])
    row =lax.broadcasted_iota(jnp.int32, (ne, tm), 0)
    work = logits_t
    vals, idxs, hits = [], [], []
    for _ in range(TOP_K):
        m = jnp.max(work, axis=0, keepdims=True)
        idx = jnp.min(jnp.where(work == m, row, ne), axis=0, keepdims=True)
        hit = row == idx
        vals.append(m)
        idxs.append(idx)
        hits.append(hit)
        work = jnp.where(hit, -jnp.inf, work)
    ex = [jnp.exp(v - vals[0]) for v in vals]
    den = ex[0] + ex[1] + ex[2] + ex[3]
    multi = sum(jnp.where(h, 1.0, 0.0) for h in hits)
    r = lax.broadcasted_iota(jnp.int32, (tm, tm), 0)
    c = lax.broadcasted_iota(jnp.int32, (tm, tm), 1)
    strict_upper = jnp.where(r < c, 1.0, 0.0).astype(BF16)
    earlier = _dot(multi.astype(BF16), strict_upper) + seen_in
    ranks = [jnp.sum(jnp.where(h, earlier, 0.0), axis=0, keepdims=True) for h in hits]
    pad = 8 - TOP_K
    tw_ref[...] = jnp.concatenate([e / den for e in ex] + [jnp.zeros((pad, tm), F32)], axis=0)
    ti_ref[...] = jnp.concatenate(idxs + [jnp.zeros((pad, tm), jnp.int32)], axis=0)
    rk_ref[...] = jnp.concatenate([rk.astype(jnp.int32) for rk in ranks] + [jnp.zeros((pad, tm), jnp.int32)],
                                  axis=0)
    seen[...] = jnp.broadcast_to(seen_in + jnp.sum(multi, axis=1, keepdims=True), seen.shape)
    cnt_ref[...] = seen[...]


def _merge(xp, xs, y_pre, ym, gsm, w_glu, b_glu, w_ssm_up, w_ml_up, w_out, g_ffn, w_router, b_router):
    tm = TM_WIDE
    t = gsm.shape[0]
    npb = xp.shape[0] // tm
    n_tiles = t // tm
    nsb = n_tiles - npb
    xp_spec, xs_spec = _two_source_specs(tm, D_MODEL, npb, nsb)
    yp_spec, ys_spec = _two_source_specs(tm, D_SSM, npb, nsb)
    mp_spec, ms_spec = _two_source_specs(tm, D_ML, npb, nsb)
    row = lambda width: pl.BlockSpec((tm, width), lambda i: (jnp.minimum(i, n_tiles - 1), 0))
    wr = w_router.T
    wr_hi = wr.astype(BF16)
    wr_lo = (wr - wr_hi.astype(F32)).astype(BF16)
    br = b_router.reshape(N_EXPERTS, 1)
    tok = lambda: pl.BlockSpec((8, tm), lambda i: (0, jnp.maximum(i - 1, 0)))
    return pl.pallas_call(
        functools.partial(_merge_kernel, n_prompt_blocks=npb),
        grid=(n_tiles + 1,),
        in_specs=[xp_spec, xs_spec, yp_spec, ys_spec, mp_spec, ms_spec, row(2 * D_MODEL), _full((D_SSM, D_SSM)),
                  _full((1, D_SSM)), _full((D_SSM, D_MODEL)), _full((D_ML, D_MODEL)),
                  _full((D_MODEL, D_MODEL)), _full((1, D_MODEL)), _full((N_EXPERTS, D_MODEL)),
                  _full((N_EXPERTS, D_MODEL)), _full((N_EXPERTS, 1))],
        out_specs=[row(D_MODEL), row(D_MODEL), tok(), tok(), tok(), _full((N_EXPERTS, LANES))],
        out_shape=(jax.ShapeDtypeStruct((t, D_MODEL), F32), jax.ShapeDtypeStruct((t, D_MODEL), F32),
                   jax.ShapeDtypeStruct((8, t), F32), jax.ShapeDtypeStruct((8, t), jnp.int32),
                   jax.ShapeDtypeStruct((8, t), jnp.int32), jax.ShapeDtypeStruct((N_EXPERTS, LANES), F32)),
        scratch_shapes=[pltpu.VMEM((N_EXPERTS, LANES), F32), pltpu.VMEM((N_EXPERTS, tm), F32)],
        compiler_params=_cparams("arbitrary"),
        name="merge",
    )(xp, xs, *y_pre, *ym, gsm, w_glu.astype(BF16), b_glu.reshape(1, D_SSM), w_ssm_up.astype(BF16),
      w_ml_up.astype(BF16), w_out.astype(BF16), g_ffn.reshape(1, D_MODEL), wr_hi, wr_lo, br)


def _route(top_idx, rank, counts, tm):
    n_assign = top_idx.shape[0] * top_idx.shape[1]
    padded = (counts + tm - 1) // tm * tm
    pad_end = jnp.cumsum(padded)
    pad_start = pad_end - padded
    onehot = top_idx[:, :, None] == jnp.arange(N_EXPERTS, dtype=jnp.int32)[None, None, :]
    pos = jnp.sum(jnp.where(onehot, pad_start[None, None, :], 0), axis=2) + rank
    n_blocks = -(-(n_assign + N_EXPERTS * (tm - 1)) // tm)
    block_row0 = jnp.arange(n_blocks, dtype=jnp.int32) * tm
    block_e = jnp.minimum(jnp.sum((pad_end[None, :] <= block_row0[:, None]).astype(jnp.int32), axis=1),
                          N_EXPERTS - 1)
    n_valid = (pad_end[-1] // tm).astype(jnp.int32).reshape(1)
    ids = jnp.arange(N_EXPERTS, dtype=jnp.int32)
    later = (ids[None, :] > ids[:, None]) & (counts[None, :] > 0)
    next_e = jnp.min(jnp.where(later, ids[None, :], N_EXPERTS), axis=1)
    next_e = jnp.where(next_e == N_EXPERTS, -1, next_e)[block_e]
    fill_lo = pad_start + counts
    return pos, fill_lo, pad_end, n_blocks * tm, block_e, next_e, n_valid


def _dispatch_copy(x_ref, rows_hbm, sem, r, dst):
    return pltpu.make_async_copy(x_ref.at[pl.ds(r, 1), :], rows_hbm.at[pl.ds(dst, 1), :], sem)


def _dispatch_kernel(lo_ref, hi_ref, pos_ref, x_ref, rows_hbm, zero_s, sem, zero_sem):
    tm = x_ref.shape[0]

    @pl.when(pl.program_id(0) == 0)
    def _():
        zero_s[...] = jnp.zeros_like(zero_s)

        def tail(do):
            def body(b, c):
                rows = pl.ds(pl.multiple_of(b * tm, tm), tm)
                do(pltpu.make_async_copy(zero_s, rows_hbm.at[rows, :], zero_sem))
                return c
            lax.fori_loop(hi_ref[N_EXPERTS - 1] // tm, rows_hbm.shape[0] // tm, body, 0)

        tail(lambda c: c.start())
        tail(lambda c: c.wait())

        def fill(e, carry):
            lo, hi = lo_ref[e], hi_ref[e]
            aligned = jnp.minimum(jnp.bitwise_and(lo + 7, -8), hi)

            def each(do):
                def body(r, c):
                    do(_dispatch_copy(zero_s, rows_hbm, zero_sem, 0, r))
                    return c
                lax.fori_loop(lo, aligned, body, 0)
                left = hi - aligned
                cur = aligned
                size = TM_MOE // 2
                while size >= 8:
                    take = jnp.bitwise_and(left, size) != 0

                    @pl.when(take)
                    def _(cur=cur, size=size):
                        rows = pl.ds(pl.multiple_of(cur, 8), size)
                        do(pltpu.make_async_copy(zero_s.at[pl.ds(0, size), :], rows_hbm.at[rows, :], zero_sem))

                    cur = cur + jnp.where(take, size, 0)
                    size //= 2

            each(lambda c: c.start())
            each(lambda c: c.wait())
            return carry

        lax.fori_loop(0, N_EXPERTS, fill, 0)

    def each(do):
        for r in range(tm):
            for k in range(TOP_K):
                do(_dispatch_copy(x_ref, rows_hbm, sem, r, pos_ref[r * TOP_K + k]), k)

    each(lambda c, k: c.start(priority=k % 2))
    each(lambda c, k: c.wait())


def _dispatch(xn, pos, fill_lo, pad_end, n_rows):
    tm = TM_TOKENS
    t = xn.shape[0]
    grid_spec = pltpu.PrefetchScalarGridSpec(
        num_scalar_prefetch=2,
        grid=(t // tm,),
        in_specs=[pl.BlockSpec((tm * TOP_K,), lambda i, lo, hi: (i,), memory_space=pltpu.SMEM),
                  pl.BlockSpec((tm, D_MODEL), lambda i, lo, hi: (i, 0))],
        out_specs=pl.BlockSpec(memory_space=pl.ANY),
        scratch_shapes=[pltpu.VMEM((tm, D_MODEL), F32), pltpu.SemaphoreType.DMA(()), pltpu.SemaphoreType.DMA(())],
    )
    return pl.pallas_call(
        _dispatch_kernel,
        grid_spec=grid_spec,
        out_shape=jax.ShapeDtypeStruct((n_rows, D_MODEL), F32),
        compiler_params=_cparams("arbitrary"),
        name="moe_dispatch",
    )(fill_lo, pad_end, pos.reshape(-1), xn)


def _moe_weight_copies(e, w_hbm, wbuf, sems):
    return [pltpu.make_async_copy(w.at[e], wbuf.at[k], sems.at[k]) for k, w in enumerate(w_hbm)]


def _moe_kernel(be_ref, ne_ref, nv_ref, x_ref, wg_hbm, bg_ref, wu_hbm, bu_ref, wd_hbm, bd_ref, y_ref,
                w_bf, wbuf, sems):
    i = pl.program_id(0)
    e = be_ref[i]
    prev = be_ref[jnp.maximum(i - 1, 0)]
    valid = i < nv_ref[0]
    first = jnp.logical_or(i == 0, e != prev)
    copies = functools.partial(_moe_weight_copies, w_hbm=(wg_hbm, wu_hbm, wd_hbm), wbuf=wbuf, sems=sems)

    @pl.when(i == 0)
    def _():
        for c in copies(e):
            c.start()

    @pl.when(jnp.logical_and(valid, first))
    def _():
        nxt = ne_ref[i]
        for k, c in enumerate(copies(e)):
            c.wait()
            w_bf[k] = wbuf[k].astype(BF16)

        @pl.when(nxt >= 0)
        def _():
            for c in copies(nxt):
                c.start(priority=1)

    @pl.when(valid)
    def _():
        x = x_ref[...].astype(BF16)
        g =jnp.minimum(_dot(x, w_bf[0]) + bg_ref[...], SWIGLU_LIMIT)
        u = jnp.clip(_dot(x, w_bf[1]) + bu_ref[...], -SWIGLU_LIMIT, SWIGLU_LIMIT)
        a = g * jax.nn.sigmoid(SWIGLU_ALPHA * g) * (u + 1.0)
        y_ref[...] = _dot(a.astype(BF16), w_bf[2]) + bd_ref[...]

    @pl.when(jnp.logical_not(valid))
    def _():
        y_ref[...] = jnp.zeros_like(y_ref)


def _moe_experts(x_rows, block_e, next_e, n_valid, w_gate, b_gate, w_up, b_up, w_down, b_down):
    tm = TM_MOE
    n_rows = x_rows.shape[0]
    wspec = pl.BlockSpec(memory_space=pl.ANY)
    bspec = pl.BlockSpec((None, 1, D_MODEL), lambda i, be, ne, nv: (be[i], 0, 0))
    rows = pl.BlockSpec((tm, D_MODEL), lambda i, be, ne, nv: (i, 0))
    grid_spec = pltpu.PrefetchScalarGridSpec(
        num_scalar_prefetch=3,
        grid=(n_rows // tm,),
        in_specs=[rows, wspec, bspec, wspec, bspec, wspec, bspec],
        out_specs=rows,
        scratch_shapes=[pltpu.VMEM((3, D_MODEL, D_MODEL), BF16), pltpu.VMEM((3, D_MODEL, D_MODEL), F32),
                        pltpu.SemaphoreType.DMA((3,))],
    )
    b3 = lambda b: b.reshape(N_EXPERTS, 1, D_MODEL)
    return pl.pallas_call(
        _moe_kernel,
        grid_spec=grid_spec,
        out_shape=jax.ShapeDtypeStruct((n_rows, D_MODEL), F32),
        compiler_params=_cparams("arbitrary"),
        name="moe_experts",
    )(block_e, next_e, n_valid, x_rows, w_gate, b3(b_gate), w_up, b3(b_up), w_down, b3(b_down))


def _ple_kernel(h1_ref, rows_ref, tw_ref, p_ref, gple_ref, wg_ref, wp_ref, gfin_ref, y_ref):
    tw = tw_ref[...]
    h2 = h1_ref[...]
    for k in range(TOP_K):
        h2 = h2 + rows_ref[k] * tw[:, k:k + 1]
    gate =jax.nn.sigmoid(_dot(_rms(h2, gple_ref[...]).astype(BF16), wg_ref[...]))
    h3 = h2 + gate * _dot(p_ref[...].astype(BF16), wp_ref[...])
    y_ref[...] = _rms(h3, gfin_ref[...])


def _ple_final(h1, expert_rows, top_w, p, row_block_offset, g_ple, w_ple_gate, w_ple_proj, g_final):
    tm = TM_WIDE
    n = p.shape[0]
    off = row_block_offset // tm
    src = lambda width: pl.BlockSpec((tm, width), lambda i: (off + i, 0))
    loc = lambda width: pl.BlockSpec((tm, width), lambda i: (i, 0))
    return pl.pallas_call(
        _ple_kernel,
        grid=(n // tm,),
        in_specs=[src(D_MODEL), pl.BlockSpec((TOP_K, tm, D_MODEL), lambda i: (0, off + i, 0)), src(top_w.shape[1]),
                  loc(D_PLE), _full((1, D_MODEL)), _full((D_MODEL, D_MODEL)), _full((D_PLE, D_MODEL)),
                  _full((1, D_MODEL))],
        out_specs=loc(D_MODEL),
        out_shape=jax.ShapeDtypeStruct((n, D_MODEL), F32),
        compiler_params=_cparams("arbitrary"),
        name="ple_final",
    )(h1, expert_rows, top_w, p, g_ple.reshape(1, D_MODEL), w_ple_gate.astype(BF16),
      w_ple_proj.astype(BF16), g_final.reshape(1, D_MODEL))


def kernel(x_prompt, x_sample, p_prompt, p_sample, state_ssm_re, state_ssm_im, state_ml_c, state_ml_n, state_ml_m, g_mix, w_in, ssm_a_re, ssm_a_im, ssm_log_dt, ssm_b_re, ssm_b_im, ssm_c_re, ssm_c_im, ssm_d, ssm_w_glu, ssm_b_glu, ml_b_ig, ml_b_fg, w_ssm_up, w_ml_up, w_out, g_ffn, w_router, b_router, w_gate, b_gate, w_up, b_up, w_down, b_down, g_ple, w_ple_gate, w_ple_proj, g_final):
    assert g_mix.shape[0] == 1, "single-layer trunk"
    bp, lp, _ = x_prompt.shape
    bs, ls, _ = x_sample.shape
    tp, ts = bp * lp, bs * ls
    t = tp + ts
    xp = x_prompt.reshape(tp, D_MODEL)
    xs = x_sample.reshape(ts, D_MODEL)

    u, qkv, o, gates, gsm = _inproj(xp, xs, g_mix[0], w_in[0])

    s5_args = (ssm_a_re[0], ssm_a_im[0], ssm_log_dt[0], ssm_b_re[0], ssm_b_im[0], ssm_c_re[0],
               ssm_c_im[0], ssm_d[0])
    zero_state = jnp.zeros((S5_TILES, bp, 2 * S5_TILE_STATE), F32)
    tables = {tc: _s5_tables(*s5_args, tc) for tc in {S5_CHUNK, ls}}
    y_p, f_p = _s5(u, 0, tables[S5_CHUNK], zero_state,
                   tc=S5_CHUNK, n_chunks=lp // S5_CHUNK, nb=bp // 2, n_splits=2)
    y_s, f_s = _s5(u, tp, tables[ls], _s5_state_to_tiles(state_ssm_re[0], state_ssm_im[0]),
                   tc=ls, n_chunks=1, nb=bs, n_splits=1)
    re_p, im_p = _s5_state_from_tiles(f_p)
    re_s, im_s = _s5_state_from_tiles(f_s)

    bias_row = jnp.pad(jnp.concatenate([ml_b_ig[0], ml_b_fg[0]]), (0, GATE_PAD - 2 * ML_HEADS)).reshape(1, GATE_PAD)
    hm_p, c_p, n_p, m_p = _mlstm_prompt(qkv, o, gates, bias_row, bp, lp)
    m0 = jnp.broadcast_to(state_ml_m[0][:, :, None, None], (bs, ML_HEADS, 1, LANES))
    hm_s, c_s, n_s, m_s = _mlstm_sample(qkv, o, gates, bias_row, state_ml_c[0],
                                        state_ml_n[0].reshape(bs, ML_HEADS, 1, ML_DK), m0, tp, bs, ls)

    h1, xn, top_w_t, top_idx_t, rank_t, counts = _merge(xp, xs, (y_p, y_s), (hm_p, hm_s), gsm, ssm_w_glu[0], ssm_b_glu[0], w_ssm_up[0], w_ml_up[0],
                            w_out[0], g_ffn[0], w_router[0], b_router[0])

    pos, fill_lo, pad_end, n_rows, block_e, next_e, n_valid = _route(
        top_idx_t[:TOP_K], rank_t[:TOP_K], counts[:, 0].astype(jnp.int32), TM_MOE)
    x_rows = _dispatch(xn, pos.T, fill_lo, pad_end, n_rows)
    expert_w = lambda w: w.reshape(N_EXPERTS, D_MODEL, D_MODEL)
    y_rows = _moe_experts(x_rows, block_e, next_e, n_valid, expert_w(w_gate), b_gate[0], expert_w(w_up),
                          b_up[0], expert_w(w_down), b_down[0])
    expert_rows = y_rows[pos.reshape(-1)].reshape(TOP_K, t, D_MODEL)
    top_w_pad = top_w_t.T

    ple_w = (g_ple[0], w_ple_gate[0], w_ple_proj[0], g_final)
    y_prompt = _ple_final(h1, expert_rows, top_w_pad, p_prompt[0].reshape(tp, D_PLE), 0, *ple_w)
    y_sample = _ple_final(h1, expert_rows, top_w_pad, p_sample[0].reshape(ts, D_PLE), tp, *ple_w)

    return (y_prompt.reshape(bp, lp, D_MODEL), y_sample.reshape(bs, ls, D_MODEL),
            re_p, im_p, c_p[None], n_p.reshape(1, bp, ML_HEADS, ML_DK), m_p[:, :, 0, 0][None],
            re_s, im_s, c_s[None], n_s.reshape(1, bs, ML_HEADS, ML_DK), m_s[:, :, 0, 0][None])
```

```python
import functools

import jax
import jax.numpy as jnp
from jax import lax
from jax.experimental import pallas as pl
from jax.experimental.pallas import tpu as pltpu

F32 = jnp.float32
BF16 = jnp.bfloat16
HIGHEST = lax.Precision.HIGHEST

D_MODEL = 1024
D_SSM = 512
SSM_GROUP = 16
N_GROUPS = 32
SSM_STATE = 64
ML_HEADS = 4
ML_DK = 128
D_ML = 512
N_EXPERTS = 32
TOP_K = 4
SWIGLU_LIMIT = 7.0
SWIGLU_ALPHA = 1.702
D_PLE = 256
RMS_EPS = 1e-6

LANES = 128
GATE_PAD = LANES
S5_CHUNK = 8
ML_CHUNK_PROMPT = 256
TM_TOKENS = 512
TM_WIDE = 512
TM_MOE = 512
VMEM_LIMIT = 56 * 1024 * 1024


def _cparams(*sem):
    return pltpu.CompilerParams(dimension_semantics=sem, vmem_limit_bytes=VMEM_LIMIT)


def _rms(x, g):
    return x * lax.rsqrt(jnp.mean(x * x, axis=-1, keepdims=True) + RMS_EPS) * g


def _dot(a, b):
    return jnp.dot(a, b, preferred_element_type=F32)


def _dot_hi(a, b):
    return jnp.dot(a, b, preferred_element_type=F32, precision=HIGHEST)


def _full(shape):
    n = len(shape)
    return pl.BlockSpec(shape, lambda *_: (0,) * n)


def _inproj_kernel(xp_ref, xs_ref, g_ref, wu_ref, wqkv_ref, wo_ref, wgt_ref, wgsm_ref,
                   u_ref, qkv_ref, o_ref, gt_ref, gsm_ref, *, n_prompt_blocks):
    i = pl.program_id(0)
    x = jnp.where(i < n_prompt_blocks, xp_ref[...], xs_ref[...])
    hn = _rms(x, g_ref[...]).astype(BF16)
    u_ref[...] = _dot(hn, wu_ref[...])
    qkv = _dot(hn, wqkv_ref[...])
    col = lax.broadcasted_iota(jnp.int32, (1, 3 * D_ML), 1)
    k_scale = jnp.where((col >= D_ML) & (col < 2 * D_ML), ML_DK ** -0.5, 1.0).astype(F32)
    qkv_ref[...] = (qkv * k_scale).astype(BF16)
    o_ref[...] = _dot(hn, wo_ref[...])
    gt_ref[...] = _dot(hn, wgt_ref[...])
    gsm_ref[...] = _dot(hn, wgsm_ref[...])


def _two_source_specs(tm, width, n_prompt_blocks, n_sample_blocks):
    last_p, last_s = n_prompt_blocks - 1, n_sample_blocks - 1
    return (pl.BlockSpec((tm, width), lambda i: (jnp.minimum(i, last_p), 0)),
            pl.BlockSpec((tm, width), lambda i: (jnp.clip(i - n_prompt_blocks, 0, last_s), 0)))


def _inproj(xp, xs, g_mix, w_in):
    tm = TM_WIDE
    tp, ts = xp.shape[0], xs.shape[0]
    t = tp + ts
    npb = tp // tm
    w = w_in.astype(BF16)
    o0 = D_SSM
    wu = w[:, :o0]
    wqkv = w[:, o0:o0 + 3 * D_ML]
    wo = w[:, o0 + 3 * D_ML:o0 + 4 * D_ML]
    g0 = o0 + 4 * D_ML
    wgt = jnp.pad(w[:, g0:g0 + 2 * ML_HEADS], ((0, 0), (0, GATE_PAD - 2 * ML_HEADS)))
    wgsm = w[:, g0 + 2 * ML_HEADS:]
    xp_spec, xs_spec = _two_source_specs(tm, D_MODEL, npb, ts // tm)
    outs =(jax.ShapeDtypeStruct((t, D_SSM), F32), jax.ShapeDtypeStruct((t, 3 * D_ML), BF16),
            jax.ShapeDtypeStruct((t, D_ML), F32), jax.ShapeDtypeStruct((t, GATE_PAD), F32),
            jax.ShapeDtypeStruct((t, 2 * D_MODEL), F32))
    row = lambda width: pl.BlockSpec((tm, width), lambda i: (i, 0))
    return pl.pallas_call(
        functools.partial(_inproj_kernel, n_prompt_blocks=npb),
        grid=(t // tm,),
        in_specs=[xp_spec, xs_spec, _full((1, D_MODEL)), _full(wu.shape), _full(wqkv.shape),
                  _full(wo.shape), _full(wgt.shape), _full(wgsm.shape)],
        out_specs=[row(D_SSM), row(3 * D_ML), row(D_ML), row(GATE_PAD), row(2 * D_MODEL)],
        out_shape=outs,
        compiler_params=_cparams("arbitrary"),
        name="inproj",
    )(xp, xs, g_mix.reshape(1, D_MODEL), wu, wqkv, wo, wgt, wgsm)


S5_TILES = D_SSM // LANES
S5_TILE_GROUPS = LANES // SSM_GROUP
S5_TILE_STATE = S5_TILE_GROUPS * SSM_STATE


def _block_diag_tiles(x):
    gt = S5_TILE_GROUPS
    n, _, r, c = x.shape
    x5 = x.reshape(n, S5_TILES, gt, r, c)
    eye = jnp.eye(gt, dtype=x.dtype)
    return (x5[:, :, :, :, None, :] * eye[None, None, :, None, :, None]).reshape(n, S5_TILES, gt * r, gt * c)


def _s5_tables(a_re, a_im, log_dt, b_re, b_im, c_re, c_im, d_skip, tc):
    ein = functools.partial(jnp.einsum, precision=HIGHEST)
    dt = jnp.exp(log_dt)[:, None]
    mag = jnp.exp(a_re * dt)
    abar_r, abar_i = mag * jnp.cos(a_im * dt), mag * jnp.sin(a_im * dt)
    den = a_re * a_re + a_im * a_im
    nr, ni = abar_r - 1.0, abar_i
    coef_r = (nr * a_re + ni * a_im) / den
    coef_i = (ni * a_re - nr * a_im) / den
    bbar_r = coef_r[..., None] * b_re - coef_i[..., None] * b_im
    bbar_i = coef_r[..., None] * b_im + coef_i[..., None] * b_re

    def abar_pow(j):
        jj = j[..., None, None]
        mag_j = jnp.where(jj >= 0, jnp.exp(jj * (a_re * dt)), 0.0)
        return mag_j * jnp.cos(jj * (a_im * dt)), mag_j * jnp.sin(jj * (a_im * dt))

    half = tc // 2
    steps = jnp.arange(tc, dtype=F32)
    at_r, at_i = abar_pow(jnp.full((), tc, F32))
    lags = (2.0 * jnp.arange(half, dtype=F32)[:, None, None]
            + jnp.array([[0.0, 1.0], [-1.0, 0.0]], F32)[None])
    lag_r, lag_i = abar_pow(lags.reshape(-1))
    ab_r = lag_r[..., None] * bbar_r - lag_i[..., None] * bbar_i
    ab_i = lag_r[..., None] * bbar_i + lag_i[..., None] * bbar_r
    kern = ein('ghp,jgpk->jgkh', c_re, ab_r) - ein('ghp,jgpk->jgkh', c_im, ab_i)
    bd_lag = _block_diag_tiles(kern.astype(BF16)).reshape(half, 2, 2, S5_TILES, LANES, LANES)
    toe = jnp.transpose(bd_lag, (3, 0, 1, 4, 2, 5)).reshape(S5_TILES, half, 2 * LANES, 2 * LANES)
    rev_r, rev_i = abar_pow(tc - 1.0 - steps)
    s_r = jnp.transpose(rev_r[..., None] * bbar_r - rev_i[..., None] * bbar_i, (0, 1, 3, 2))
    s_i = jnp.transpose(rev_r[..., None] * bbar_i + rev_i[..., None] * bbar_r, (0, 1, 3, 2))
    a1_r, a1_i = abar_pow(steps + 1.0)
    p_r = c_re[None] * a1_r[:, :, None, :] - c_im[None] * a1_i[:, :, None, :]
    p_i = -c_re[None] * a1_i[:, :, None, :] - c_im[None] * a1_r[:, :, None, :]

    def compact(x):
        x = x.astype(BF16).reshape(half, 2, S5_TILES, S5_TILE_GROUPS, SSM_GROUP, SSM_STATE)
        x = jnp.transpose(x, (2, 0, 1, 3, 4, 5)).reshape(S5_TILES, half, 2 * LANES, SSM_STATE)
        return jnp.concatenate([x, x], axis=3)

    s_tab = jnp.stack([compact(s_r), compact(s_i)], axis=2)
    p_tab = jnp.stack([compact(p_r), compact(p_i)], axis=2)
    a_tab =jnp.stack([at_r.reshape(S5_TILES, S5_TILE_STATE), at_i.reshape(S5_TILES, S5_TILE_STATE)], axis=1)
    d_tab = d_skip.reshape(S5_TILES, 1, LANES)
    return toe, s_tab, p_tab, a_tab, d_tab


def _s5_kernel(u_ref, t_ref, s_ref, p_ref, a_ref, d_ref, h0_ref, y_ref, f_ref, loc, xprev,
               *, tc, n_chunks, nb):
    r = n_chunks * nb
    half = tc // 2
    ns = S5_TILE_STATE
    step_rows = lambda t: pl.ds(t, r, stride=tc)
    v = [u_ref[step_rows(t), :] for t in range(tc)]
    vp = [jnp.concatenate([v[2 * a].astype(BF16), v[2 * a + 1].astype(BF16)], axis=1) for a in range(half)]

    row_group = (lax.broadcasted_iota(jnp.int32, (2 * LANES, ns), 0) >> 4) & (S5_TILE_GROUPS - 1)
    col_group = lax.broadcasted_iota(jnp.int32, (2 * LANES, ns), 1) >> 6
    own_group = jnp.where(row_group == col_group, 1.0, 0.0).astype(BF16)

    def block_diag(tab):
        reps = ns // LANES
        return jnp.concatenate([jnp.tile(tab[0], (1, reps)) * own_group, jnp.tile(tab[1], (1, reps)) * own_group],
                               axis=1)

    acc = _dot(vp[0], block_diag(s_ref[0]))
    for a in range(1, half):
        acc = acc + _dot(vp[a], block_diag(s_ref[a]))
    nt = ns // LANES
    lane_tile = lambda k: slice(k * LANES, (k + 1) * LANES)
    for k in range(2 * nt):
        loc[k] = acc[:, lane_tile(k)]
    abar = a_ref[...]

    def body(c, carry):
        rows = pl.ds(c, nb, stride=n_chunks) if n_chunks > 1 else pl.ds(0, nb)
        new = []
        for k in range(nt):
            xr, xi = carry[k], carry[nt + k]
            xprev[k, rows, :] = xr
            xprev[nt + k, rows, :] = xi
            ar, ai = abar[0:1, lane_tile(k)], abar[1:2, lane_tile(k)]
            new.append((ar * xr - ai * xi + loc[k, rows, :], ar * xi + ai * xr + loc[nt + k, rows, :]))
        return tuple(n[0] for n in new) + tuple(n[1] for n in new)

    x_end = lax.fori_loop(0, n_chunks, body, tuple(h0_ref[:, lane_tile(k)] for k in range(2 * nt)),
                          unroll=min(8, n_chunks))
    for k in range(2 * nt):
        f_ref[:, lane_tile(k)] = x_end[k]
    xp = jnp.concatenate([xprev[k] for k in range(2 * nt)], axis=1).astype(BF16)
    d = d_ref[...]
    nt_dims = (((1,), (1,)), ((), ()))
    for a2 in range(half):
        acc = lax.dot_general(xp, block_diag(p_ref[a2]), nt_dims, preferred_element_type=F32)
        for a in range(a2 + 1):
            acc = acc + _dot(vp[a], t_ref[a2 - a])
        for k in range(2):
            t = 2 * a2 + k
            y_ref[step_rows(t), :] = acc[:, k * LANES:(k + 1) * LANES] + v[t] * d


def _s5(u, row_block_offset, tables, h0, *, tc, n_chunks, nb, n_splits):
    toe, s_tab, p_tab, a_tab, d_tab = tables
    rows = nb * n_chunks * tc
    off = row_block_offset // rows
    half = tc // 2
    ns2 = 2 * S5_TILE_STATE
    tile = lambda *tail: pl.BlockSpec((None,) + tail, lambda j, s: (j,) + (0,) * len(tail))
    state = pl.BlockSpec((None, None, nb, ns2), lambda j, s: (j, s, 0, 0))
    y, f = pl.pallas_call(
        functools.partial(_s5_kernel, tc=tc, n_chunks=n_chunks, nb=nb),
        grid=(S5_TILES, n_splits),
        in_specs=[pl.BlockSpec((rows, LANES), lambda j, s: (off + s, j)),
                  tile(half, 2 * LANES, 2 * LANES), tile(half, 2, 2 * LANES, LANES), tile(half, 2, 2 * LANES, LANES),
                  tile(2, S5_TILE_STATE), tile(1, LANES), state],
        out_specs=[pl.BlockSpec((rows, LANES), lambda j, s: (s, j)), state],
        out_shape=(jax.ShapeDtypeStruct((rows * n_splits, D_SSM), F32),
                   jax.ShapeDtypeStruct((S5_TILES, n_splits, nb, ns2), F32)),
        scratch_shapes=[pltpu.VMEM((ns2 // LANES, nb * n_chunks, LANES), F32)] * 2,
        compiler_params=_cparams("arbitrary", "arbitrary"),
        name=f"s5_c{n_chunks}",
    )(u, toe, s_tab, p_tab, a_tab, d_tab, h0.reshape(S5_TILES, n_splits, nb, ns2))
    return y, f.reshape(S5_TILES, n_splits * nb, ns2)


def _s5_state_to_tiles(s_re, s_im):
    b = s_re.shape[0]
    f = lambda s: s.reshape(b, S5_TILES, S5_TILE_STATE).transpose(1, 0, 2)
    return jnp.concatenate([f(s_re), f(s_im)], axis=2)


def _s5_state_from_tiles(f):
    b = f.shape[1]
    g = lambda s: s.transpose(1, 0, 2).reshape(1, b, N_GROUPS, SSM_STATE)
    return g(f[:, :, :S5_TILE_STATE]), g(f[:, :, S5_TILE_STATE:])


def _log_sigmoid(x):
    return jnp.minimum(x, 0.0) - jnp.log1p(jnp.exp(-jnp.abs(x)))


def _mlstm_gates(gates, bias_row, lc):
    g = gates + bias_row
    col = lax.broadcasted_iota(jnp.int32, (1, GATE_PAD), 1)
    gl = jnp.where(col >= ML_HEADS, _log_sigmoid(g), g)
    r = lax.broadcasted_iota(jnp.int32, (lc, lc), 0)
    c = lax.broadcasted_iota(jnp.int32, (lc, lc), 1)
    tril = (r >= c).astype(F32)
    bcols = _dot_hi(tril, gl)
    sel = (lax.broadcasted_iota(jnp.int32, (8, GATE_PAD), 0)
           == lax.broadcasted_iota(jnp.int32, (8, GATE_PAD), 1)).astype(F32)
    nt = (((1,), (1,)), ((), ()))
    grows = lax.dot_general(sel, gl, nt, precision=HIGHEST, preferred_element_type=F32)
    brows = lax.dot_general(sel, bcols, nt, precision=HIGHEST, preferred_element_type=F32)
    return gl, bcols, grows, brows, (r >= c)


def _mlstm_chunks(seqs, bias_row, lc):
    nt = (((1,), (1,)), ((), ()))
    tn = (((0,), (0,)), ((), ()))
    pairs = [(si, hd) for si in range(len(seqs)) for hd in range(ML_HEADS)]
    head = lambda x, part, hd: x[:, part * D_ML + hd * ML_DK:part * D_ML + (hd + 1) * ML_DK]
    q = {p: head(seqs[p[0]][0], 0, p[1]) for p in pairs}
    k = {p: head(seqs[p[0]][0], 1, p[1]) for p in pairs}
    v = {p: head(seqs[p[0]][0], 2, p[1]) for p in pairs}
    state = lambda p: seqs[p[0]][3](p[1])
    qk = {p: lax.dot_general(q[p], k[p], nt, preferred_element_type=F32) for p in pairs}
    qc = {p: _dot(q[p], state(p)[0].astype(BF16)) for p in pairs}
    tables = [_mlstm_gates(seq[2], bias_row, lc) for seq in seqs]
    w_in, w_out, mt, m_end, decay, kw = {}, {}, {}, {}, {}, {}
    for p in pairs:
        si, hd = p
        gl, bcols, grows, brows, causal = tables[si]
        f = ML_HEADS + hd
        ic, bc, ir, br = gl[:, hd:hd + 1], bcols[:, f:f + 1], grows[hd:hd + 1, :], brows[f:f + 1, :]
        m_state = state(p)[2]
        dmat = jnp.where(causal, bc - br + ir, -jnp.inf)
        inter = bc + m_state
        mt[p] = jnp.maximum(inter, jnp.max(dmat, axis=1, keepdims=True))
        w_in[p] = jnp.exp(dmat - mt[p])
        w_out[p] = jnp.exp(inter - mt[p])
        m_end[p] = mt[p][lc - 1:lc, :]
        b_last = bc[lc - 1:lc, :]
        decay[p] = jnp.exp(b_last + m_state - m_end[p])
        kw[p] = k[p].astype(F32) * jnp.exp(b_last - bc + ic - m_end[p])
    kv = {p: lax.dot_general(kw[p].astype(BF16), v[p], tn, preferred_element_type=F32) for p in pairs}
    s = {p: qk[p] * w_in[p] for p in pairs}
    sv = {p: _dot(s[p].astype(BF16), v[p]) for p in pairs}
    outs = [[] for _ in seqs]
    for p in pairs:
        si, hd = p
        c_state, n_state, _ = state(p)
        num = w_out[p] * qc[p] + sv[p]
        qn = (w_out[p] * jnp.sum(q[p].astype(F32) * n_state, axis=1, keepdims=True)
              + jnp.sum(s[p], axis=1, keepdims=True))
        h = num / jnp.maximum(jnp.abs(qn), jnp.exp(-mt[p]))
        seqs[si][4](hd, decay[p] * c_state + kv[p], decay[p] * n_state + jnp.sum(kw[p], axis=0, keepdims=True),
                    m_end[p])
        outs[si].append(jax.nn.sigmoid(head(seqs[si][1], 0, hd)) * h)
    return [jnp.concatenate(o, axis=1) for o in outs]


def _mlstm_prompt_kernel(qkv_ref, o_ref, gt_ref, bias_ref, h_ref, c_ref, n_ref, m_ref, *, lc):
    @pl.when(pl.program_id(1) == 0)
    def _():
        c_ref[...] = jnp.zeros_like(c_ref)
        n_ref[...] = jnp.zeros_like(n_ref)
        m_ref[...] = jnp.zeros_like(m_ref)

    def get_state(hd):
        return c_ref[hd], n_ref[hd], m_ref[hd][:, 0:1]

    def put_state(hd, c_new, n_new, m_new):
        c_ref[hd] = c_new
        n_ref[hd] = n_new
        m_ref[hd] = jnp.broadcast_to(m_new, (1, LANES))

    (h,) = _mlstm_chunks([(qkv_ref[...], o_ref[...], gt_ref[...], get_state, put_state)], bias_ref[...], lc)
    h_ref[...] = h.astype(BF16)


def _mlstm_prompt(qkv, o, gates, bias_row, bsz, seq):
    lc = ML_CHUNK_PROMPT
    nc = seq // lc
    row = lambda width: pl.BlockSpec((lc, width), lambda b, c: (b * nc + c, 0))
    st = lambda *tail: pl.BlockSpec((None, ML_HEADS) + tail, lambda b, c: (b, 0) + (0,) * len(tail))
    return pl.pallas_call(
        functools.partial(_mlstm_prompt_kernel, lc=lc),
        grid=(bsz, nc),
        in_specs=[row(3 * D_ML), row(D_ML), row(GATE_PAD), pl.BlockSpec((1, GATE_PAD), lambda b, c: (0, 0))],
        out_specs=[row(D_ML), st(ML_DK, ML_DK), st(1, ML_DK), st(1, LANES)],
        out_shape=(jax.ShapeDtypeStruct((bsz * seq, D_ML), BF16),
                   jax.ShapeDtypeStruct((bsz, ML_HEADS, ML_DK, ML_DK), F32),
                   jax.ShapeDtypeStruct((bsz, ML_HEADS, 1, ML_DK), F32),
                   jax.ShapeDtypeStruct((bsz, ML_HEADS, 1, LANES), F32)),
        compiler_params=_cparams("arbitrary", "arbitrary"),
        name="mlstm_prompt",
    )(qkv, o, gates, bias_row)


def _mlstm_sample_kernel(qkv_ref, o_ref, gt_ref, bias_ref, c0_ref, n0_ref, m0_ref,
                         h_ref, c_ref, n_ref, m_ref, qkv_s, *, lc, nb):
    qkv_s[...] = qkv_ref[...].astype(F32)

    def seq(b):
        rows = slice(b * lc, (b + 1) * lc)

        def get_state(hd):
            return c0_ref[b, hd], n0_ref[b, hd], m0_ref[b, hd][:, 0:1]

        def put_state(hd, c_new, n_new, m_new):
            c_ref[b, hd] = c_new
            n_ref[b, hd] = n_new
            m_ref[b, hd] = jnp.broadcast_to(m_new, (1, LANES))

        return (qkv_s[rows, :].astype(BF16), o_ref[rows, :], gt_ref[rows, :], get_state, put_state)

    hs = _mlstm_chunks([seq(b) for b in range(nb)], bias_ref[...], lc)
    for b, h in enumerate(hs):
        h_ref[b * lc:(b + 1) * lc, :] = h


def _mlstm_sample(qkv, o, gates, bias_row, c0, n0, m0, row_block_offset, bsz, seq):
    nb = 8
    lc = seq
    rows = nb * lc
    off = row_block_offset // rows
    row = lambda width: pl.BlockSpec((rows, width), lambda i: (off + i, 0))
    st = lambda *tail: pl.BlockSpec((nb, ML_HEADS) + tail, lambda i: (i, 0) + (0,) * len(tail))
    return pl.pallas_call(
        functools.partial(_mlstm_sample_kernel, lc=lc, nb=nb),
        grid=(bsz // nb,),
        in_specs=[row(3 * D_ML), row(D_ML), row(GATE_PAD), pl.BlockSpec((1, GATE_PAD), lambda i: (0, 0)),
                  st(ML_DK, ML_DK), st(1, ML_DK), st(1, LANES)],
        out_specs=[pl.BlockSpec((rows, D_ML), lambda i: (i, 0)), st(ML_DK, ML_DK), st(1, ML_DK), st(1, LANES)],
        out_shape=(jax.ShapeDtypeStruct((bsz * seq, D_ML), F32),
                   jax.ShapeDtypeStruct((bsz, ML_HEADS, ML_DK, ML_DK), F32),
                   jax.ShapeDtypeStruct((bsz, ML_HEADS, 1, ML_DK), F32),
                   jax.ShapeDtypeStruct((bsz, ML_HEADS, 1, LANES), F32)),
        scratch_shapes=[pltpu.VMEM((rows, 3 * D_ML), F32)],
        compiler_params=_cparams("arbitrary"),
        name="mlstm_sample",
    )(qkv, o, gates, bias_row, c0, n0, m0)


def _merge_kernel(xp_ref, xs_ref, yp_ref, ys_ref, mp_ref, ms_ref, gsm_ref, wglu_ref, bglu_ref, wsu_ref, wmu_ref,
                  wout_ref, gffn_ref, wrh_ref, wrl_ref, br_ref, h1_ref, xn_ref, tw_ref, ti_ref, rk_ref, cnt_ref,
                  seen, logits_s, *, n_prompt_blocks):
    i = pl.program_id(0)

    @pl.when(i == 0)
    def _():
        seen[...] = jnp.zeros_like(seen)
        logits_s[...] = jnp.zeros_like(logits_s)

    _route_tile(logits_s[...], tw_ref, ti_ref, rk_ref, cnt_ref, seen, i)

    is_prompt = i < n_prompt_blocks
    x = jnp.where(is_prompt, xp_ref[...], xs_ref[...])
    y = jax.nn.gelu(jnp.where(is_prompt, yp_ref[...], ys_ref[...]))
    ym = jnp.where(is_prompt, mp_ref[...], ms_ref[...].astype(BF16))
    ys = y * jax.nn.sigmoid(_dot(y.astype(BF16), wglu_ref[...]) + bglu_ref[...])
    gsm = gsm_ref[...]
    merged = (jax.nn.sigmoid(gsm[:, :D_MODEL]) * _dot(ys.astype(BF16), wsu_ref[...])
              + jax.nn.sigmoid(gsm[:, D_MODEL:]) * _dot(ym, wmu_ref[...]))
    h1 = x + _dot(merged.astype(BF16), wout_ref[...])
    h1_ref[...] = h1
    xn = _rms(h1, gffn_ref[...])
    xn_ref[...] = xn
    xn_hi = xn.astype(BF16)
    xn_lo = (xn - xn_hi.astype(F32)).astype(BF16)
    nt = (((1,), (1,)), ((), ()))
    dot_nt = lambda a, b: lax.dot_general(a, b, nt, preferred_element_type=F32)
    logits_s[...] = (dot_nt(wrh_ref[...], xn_hi) + dot_nt(wrh_ref[...], xn_lo) + dot_nt(wrl_ref[...], xn_hi)
                     + br_ref[...])


def _route_tile(logits_t, tw_ref, ti_ref, rk_ref, cnt_ref, seen, step):
    ne, tm = logits_t.shape
    seen_in = jnp.where(step <= 1, 0.0, seen[:, 0:1])
    row =lax.broadcasted_iota(jnp.int32, (ne, tm), 0)
    work = logits_t
    vals, idxs, hits = [], [], []
    for _ in range(TOP_K):
        m = jnp.max(work, axis=0, keepdims=True)
        idx = jnp.min(jnp.where(work == m, row, ne), axis=0, keepdims=True)
        hit = row == idx
        vals.append(m)
        idxs.append(idx)
        hits.append(hit)
        work = jnp.where(hit, -jnp.inf, work)
    ex = [jnp.exp(v - vals[0]) for v in vals]
    den = ex[0] + ex[1] + ex[2] + ex[3]
    multi = sum(jnp.where(h, 1.0, 0.0) for h in hits)
    r = lax.broadcasted_iota(jnp.int32, (tm, tm), 0)
    c = lax.broadcasted_iota(jnp.int32, (tm, tm), 1)
    strict_upper = jnp.where(r < c, 1.0, 0.0).astype(BF16)
    earlier = _dot(multi.astype(BF16), strict_upper) + seen_in
    ranks = [jnp.sum(jnp.where(h, earlier, 0.0), axis=0, keepdims=True) for h in hits]
    pad = 8 - TOP_K
    tw_ref[...] = jnp.concatenate([e / den for e in ex] + [jnp.zeros((pad, tm), F32)], axis=0)
    ti_ref[...] = jnp.concatenate(idxs + [jnp.zeros((pad, tm), jnp.int32)], axis=0)
    rk_ref[...] = jnp.concatenate([rk.astype(jnp.int32) for rk in ranks] + [jnp.zeros((pad, tm), jnp.int32)],
                                  axis=0)
    seen[...] = jnp.broadcast_to(seen_in + jnp.sum(multi, axis=1, keepdims=True), seen.shape)
    cnt_ref[...] = seen[...]


def _merge(xp, xs, y_pre, ym, gsm, w_glu, b_glu, w_ssm_up, w_ml_up, w_out, g_ffn, w_router, b_router):
    tm = TM_WIDE
    t = gsm.shape[0]
    npb = xp.shape[0] // tm
    n_tiles = t // tm
    nsb = n_tiles - npb
    xp_spec, xs_spec = _two_source_specs(tm, D_MODEL, npb, nsb)
    yp_spec, ys_spec = _two_source_specs(tm, D_SSM, npb, nsb)
    mp_spec, ms_spec = _two_source_specs(tm, D_ML, npb, nsb)
    row = lambda width: pl.BlockSpec((tm, width), lambda i: (jnp.minimum(i, n_tiles - 1), 0))
    wr = w_router.T
    wr_hi = wr.astype(BF16)
    wr_lo = (wr - wr_hi.astype(F32)).astype(BF16)
    br = b_router.reshape(N_EXPERTS, 1)
    tok = lambda: pl.BlockSpec((8, tm), lambda i: (0, jnp.maximum(i - 1, 0)))
    return pl.pallas_call(
        functools.partial(_merge_kernel, n_prompt_blocks=npb),
        grid=(n_tiles + 1,),
        in_specs=[xp_spec, xs_spec, yp_spec, ys_spec, mp_spec, ms_spec, row(2 * D_MODEL), _full((D_SSM, D_SSM)),
                  _full((1, D_SSM)), _full((D_SSM, D_MODEL)), _full((D_ML, D_MODEL)),
                  _full((D_MODEL, D_MODEL)), _full((1, D_MODEL)), _full((N_EXPERTS, D_MODEL)),
                  _full((N_EXPERTS, D_MODEL)), _full((N_EXPERTS, 1))],
        out_specs=[row(D_MODEL), row(D_MODEL), tok(), tok(), tok(), _full((N_EXPERTS, LANES))],
        out_shape=(jax.ShapeDtypeStruct((t, D_MODEL), F32), jax.ShapeDtypeStruct((t, D_MODEL), F32),
                   jax.ShapeDtypeStruct((8, t), F32), jax.ShapeDtypeStruct((8, t), jnp.int32),
                   jax.ShapeDtypeStruct((8, t), jnp.int32), jax.ShapeDtypeStruct((N_EXPERTS, LANES), F32)),
        scratch_shapes=[pltpu.VMEM((N_EXPERTS, LANES), F32), pltpu.VMEM((N_EXPERTS, tm), F32)],
        compiler_params=_cparams("arbitrary"),
        name="merge",
    )(xp, xs, *y_pre, *ym, gsm, w_glu.astype(BF16), b_glu.reshape(1, D_SSM), w_ssm_up.astype(BF16),
      w_ml_up.astype(BF16), w_out.astype(BF16), g_ffn.reshape(1, D_MODEL), wr_hi, wr_lo, br)


def _route(top_idx, rank, counts, tm):
    n_assign = top_idx.shape[0] * top_idx.shape[1]
    padded = (counts + tm - 1) // tm * tm
    pad_end = jnp.cumsum(padded)
    pad_start = pad_end - padded
    onehot = top_idx[:, :, None] == jnp.arange(N_EXPERTS, dtype=jnp.int32)[None, None, :]
    pos = jnp.sum(jnp.where(onehot, pad_start[None, None, :], 0), axis=2) + rank
    n_blocks = -(-(n_assign + N_EXPERTS * (tm - 1)) // tm)
    block_row0 = jnp.arange(n_blocks, dtype=jnp.int32) * tm
    block_e = jnp.minimum(jnp.sum((pad_end[None, :] <= block_row0[:, None]).astype(jnp.int32), axis=1),
                          N_EXPERTS - 1)
    n_valid = (pad_end[-1] // tm).astype(jnp.int32).reshape(1)
    ids = jnp.arange(N_EXPERTS, dtype=jnp.int32)
    later = (ids[None, :] > ids[:, None]) & (counts[None, :] > 0)
    next_e = jnp.min(jnp.where(later, ids[None, :], N_EXPERTS), axis=1)
    next_e = jnp.where(next_e == N_EXPERTS, -1, next_e)[block_e]
    fill_lo = pad_start + counts
    return pos, fill_lo, pad_end, n_blocks * tm, block_e, next_e, n_valid


def _dispatch_copy(x_ref, rows_hbm, sem, r, dst):
    return pltpu.make_async_copy(x_ref.at[pl.ds(r, 1), :], rows_hbm.at[pl.ds(dst, 1), :], sem)


def _dispatch_kernel(lo_ref, hi_ref, pos_ref, x_ref, rows_hbm, zero_s, sem, zero_sem):
    tm = x_ref.shape[0]

    @pl.when(pl.program_id(0) == 0)
    def _():
        zero_s[...] = jnp.zeros_like(zero_s)

        def tail(do):
            def body(b, c):
                rows = pl.ds(pl.multiple_of(b * tm, tm), tm)
                do(pltpu.make_async_copy(zero_s, rows_hbm.at[rows, :], zero_sem))
                return c
            lax.fori_loop(hi_ref[N_EXPERTS - 1] // tm, rows_hbm.shape[0] // tm, body, 0)

        tail(lambda c: c.start())
        tail(lambda c: c.wait())

        def fill(e, carry):
            lo, hi = lo_ref[e], hi_ref[e]
            aligned = jnp.minimum(jnp.bitwise_and(lo + 7, -8), hi)

            def each(do):
                def body(r, c):
                    do(_dispatch_copy(zero_s, rows_hbm, zero_sem, 0, r))
                    return c
                lax.fori_loop(lo, aligned, body, 0)
                left = hi - aligned
                cur = aligned
                size = TM_MOE // 2
                while size >= 8:
                    take = jnp.bitwise_and(left, size) != 0

                    @pl.when(take)
                    def _(cur=cur, size=size):
                        rows = pl.ds(pl.multiple_of(cur, 8), size)
                        do(pltpu.make_async_copy(zero_s.at[pl.ds(0, size), :], rows_hbm.at[rows, :], zero_sem))

                    cur = cur + jnp.where(take, size, 0)
                    size //= 2

            each(lambda c: c.start())
            each(lambda c: c.wait())
            return carry

        lax.fori_loop(0, N_EXPERTS, fill, 0)

    def each(do):
        for r in range(tm):
            for k in range(TOP_K):
                do(_dispatch_copy(x_ref, rows_hbm, sem, r, pos_ref[r * TOP_K + k]), k)

    each(lambda c, k: c.start(priority=k % 2))
    each(lambda c, k: c.wait())


def _dispatch(xn, pos, fill_lo, pad_end, n_rows):
    tm = TM_TOKENS
    t = xn.shape[0]
    grid_spec = pltpu.PrefetchScalarGridSpec(
        num_scalar_prefetch=2,
        grid=(t // tm,),
        in_specs=[pl.BlockSpec((tm * TOP_K,), lambda i, lo, hi: (i,), memory_space=pltpu.SMEM),
                  pl.BlockSpec((tm, D_MODEL), lambda i, lo, hi: (i, 0))],
        out_specs=pl.BlockSpec(memory_space=pl.ANY),
        scratch_shapes=[pltpu.VMEM((tm, D_MODEL), F32), pltpu.SemaphoreType.DMA(()), pltpu.SemaphoreType.DMA(())],
    )
    return pl.pallas_call(
        _dispatch_kernel,
        grid_spec=grid_spec,
        out_shape=jax.ShapeDtypeStruct((n_rows, D_MODEL), F32),
        compiler_params=_cparams("arbitrary"),
        name="moe_dispatch",
    )(fill_lo, pad_end, pos.reshape(-1), xn)


def _moe_weight_copies(e, w_hbm, wbuf, sems):
    return [pltpu.make_async_copy(w.at[e], wbuf.at[k], sems.at[k]) for k, w in enumerate(w_hbm)]


def _moe_kernel(be_ref, ne_ref, nv_ref, x_ref, wg_hbm, bg_ref, wu_hbm, bu_ref, wd_hbm, bd_ref, y_ref,
                w_bf, wbuf, sems):
    i = pl.program_id(0)
    e = be_ref[i]
    prev = be_ref[jnp.maximum(i - 1, 0)]
    valid = i < nv_ref[0]
    first = jnp.logical_or(i == 0, e != prev)
    copies = functools.partial(_moe_weight_copies, w_hbm=(wg_hbm, wu_hbm, wd_hbm), wbuf=wbuf, sems=sems)

    @pl.when(i == 0)
    def _():
        for c in copies(e):
            c.start()

    @pl.when(jnp.logical_and(valid, first))
    def _():
        nxt = ne_ref[i]
        for k, c in enumerate(copies(e)):
            c.wait()
            w_bf[k] = wbuf[k].astype(BF16)

        @pl.when(nxt >= 0)
        def _():
            for c in copies(nxt):
                c.start(priority=1)

    @pl.when(valid)
    def _():
        x = x_ref[...].astype(BF16)
        g =jnp.minimum(_dot(x, w_bf[0]) + bg_ref[...], SWIGLU_LIMIT)
        u = jnp.clip(_dot(x, w_bf[1]) + bu_ref[...], -SWIGLU_LIMIT, SWIGLU_LIMIT)
        a = g * jax.nn.sigmoid(SWIGLU_ALPHA * g) * (u + 1.0)
        y_ref[...] = _dot(a.astype(BF16), w_bf[2]) + bd_ref[...]

    @pl.when(jnp.logical_not(valid))
    def _():
        y_ref[...] = jnp.zeros_like(y_ref)


def _moe_experts(x_rows, block_e, next_e, n_valid, w_gate, b_gate, w_up, b_up, w_down, b_down):
    tm = TM_MOE
    n_rows = x_rows.shape[0]
    wspec = pl.BlockSpec(memory_space=pl.ANY)
    bspec = pl.BlockSpec((None, 1, D_MODEL), lambda i, be, ne, nv: (be[i], 0, 0))
    rows = pl.BlockSpec((tm, D_MODEL), lambda i, be, ne, nv: (i, 0))
    grid_spec = pltpu.PrefetchScalarGridSpec(
        num_scalar_prefetch=3,
        grid=(n_rows // tm,),
        in_specs=[rows, wspec, bspec, wspec, bspec, wspec, bspec],
        out_specs=rows,
        scratch_shapes=[pltpu.VMEM((3, D_MODEL, D_MODEL), BF16), pltpu.VMEM((3, D_MODEL, D_MODEL), F32),
                        pltpu.SemaphoreType.DMA((3,))],
    )
    b3 = lambda b: b.reshape(N_EXPERTS, 1, D_MODEL)
    return pl.pallas_call(
        _moe_kernel,
        grid_spec=grid_spec,
        out_shape=jax.ShapeDtypeStruct((n_rows, D_MODEL), F32),
        compiler_params=_cparams("arbitrary"),
        name="moe_experts",
    )(block_e, next_e, n_valid, x_rows, w_gate, b3(b_gate), w_up, b3(b_up), w_down, b3(b_down))


def _ple_kernel(h1_ref, rows_ref, tw_ref, p_ref, gple_ref, wg_ref, wp_ref, gfin_ref, y_ref):
    tw = tw_ref[...]
    h2 = h1_ref[...]
    for k in range(TOP_K):
        h2 = h2 + rows_ref[k] * tw[:, k:k + 1]
    gate =jax.nn.sigmoid(_dot(_rms(h2, gple_ref[...]).astype(BF16), wg_ref[...]))
    h3 = h2 + gate * _dot(p_ref[...].astype(BF16), wp_ref[...])
    y_ref[...] = _rms(h3, gfin_ref[...])


def _ple_final(h1, expert_rows, top_w, p, row_block_offset, g_ple, w_ple_gate, w_ple_proj, g_final):
    tm = TM_WIDE
    n = p.shape[0]
    off = row_block_offset // tm
    src = lambda width: pl.BlockSpec((tm, width), lambda i: (off + i, 0))
    loc = lambda width: pl.BlockSpec((tm, width), lambda i: (i, 0))
    return pl.pallas_call(
        _ple_kernel,
        grid=(n // tm,),
        in_specs=[src(D_MODEL), pl.BlockSpec((TOP_K, tm, D_MODEL), lambda i: (0, off + i, 0)), src(top_w.shape[1]),
                  loc(D_PLE), _full((1, D_MODEL)), _full((D_MODEL, D_MODEL)), _full((D_PLE, D_MODEL)),
                  _full((1, D_MODEL))],
        out_specs=loc(D_MODEL),
        out_shape=jax.ShapeDtypeStruct((n, D_MODEL), F32),
        compiler_params=_cparams("arbitrary"),
        name="ple_final",
    )(h1, expert_rows, top_w, p, g_ple.reshape(1, D_MODEL), w_ple_gate.astype(BF16),
      w_ple_proj.astype(BF16), g_final.reshape(1, D_MODEL))


def kernel(x_prompt, x_sample, p_prompt, p_sample, state_ssm_re, state_ssm_im, state_ml_c, state_ml_n, state_ml_m, g_mix, w_in, ssm_a_re, ssm_a_im, ssm_log_dt, ssm_b_re, ssm_b_im, ssm_c_re, ssm_c_im, ssm_d, ssm_w_glu, ssm_b_glu, ml_b_ig, ml_b_fg, w_ssm_up, w_ml_up, w_out, g_ffn, w_router, b_router, w_gate, b_gate, w_up, b_up, w_down, b_down, g_ple, w_ple_gate, w_ple_proj, g_final):
    assert g_mix.shape[0] == 1, "single-layer trunk"
    bp, lp, _ = x_prompt.shape
    bs, ls, _ = x_sample.shape
    tp, ts = bp * lp, bs * ls
    t = tp + ts
    xp = x_prompt.reshape(tp, D_MODEL)
    xs = x_sample.reshape(ts, D_MODEL)

    u, qkv, o, gates, gsm = _inproj(xp, xs, g_mix[0], w_in[0])

    s5_args = (ssm_a_re[0], ssm_a_im[0], ssm_log_dt[0], ssm_b_re[0], ssm_b_im[0], ssm_c_re[0],
               ssm_c_im[0], ssm_d[0])
    zero_state = jnp.zeros((S5_TILES, bp, 2 * S5_TILE_STATE), F32)
    tables = {tc: _s5_tables(*s5_args, tc) for tc in {S5_CHUNK, ls}}
    y_p, f_p = _s5(u, 0, tables[S5_CHUNK], zero_state,
                   tc=S5_CHUNK, n_chunks=lp // S5_CHUNK, nb=bp // 2, n_splits=2)
    y_s, f_s = _s5(u, tp, tables[ls], _s5_state_to_tiles(state_ssm_re[0], state_ssm_im[0]),
                   tc=ls, n_chunks=1, nb=bs, n_splits=1)
    re_p, im_p = _s5_state_from_tiles(f_p)
    re_s, im_s = _s5_state_from_tiles(f_s)

    bias_row = jnp.pad(jnp.concatenate([ml_b_ig[0], ml_b_fg[0]]), (0, GATE_PAD - 2 * ML_HEADS)).reshape(1, GATE_PAD)
    hm_p, c_p, n_p, m_p = _mlstm_prompt(qkv, o, gates, bias_row, bp, lp)
    m0 = jnp.broadcast_to(state_ml_m[0][:, :, None, None], (bs, ML_HEADS, 1, LANES))
    hm_s, c_s, n_s, m_s = _mlstm_sample(qkv, o, gates, bias_row, state_ml_c[0],
                                        state_ml_n[0].reshape(bs, ML_HEADS, 1, ML_DK), m0, tp, bs, ls)

    h1, xn, top_w_t, top_idx_t, rank_t, counts = _merge(xp, xs, (y_p, y_s), (hm_p, hm_s), gsm, ssm_w_glu[0], ssm_b_glu[0], w_ssm_up[0], w_ml_up[0],
                            w_out[0], g_ffn[0], w_router[0], b_router[0])

    pos, fill_lo, pad_end, n_rows, block_e, next_e, n_valid = _route(
        top_idx_t[:TOP_K], rank_t[:TOP_K], counts[:, 0].astype(jnp.int32), TM_MOE)
    x_rows = _dispatch(xn, pos.T, fill_lo, pad_end, n_rows)
    expert_w = lambda w: w.reshape(N_EXPERTS, D_MODEL, D_MODEL)
    y_rows = _moe_experts(x_rows, block_e, next_e, n_valid, expert_w(w_gate), b_gate[0], expert_w(w_up),
                          b_up[0], expert_w(w_down), b_down[0])
    expert_rows = y_rows[pos.reshape(-1)].reshape(TOP_K, t, D_MODEL)
    top_w_pad = top_w_t.T

    ple_w = (g_ple[0], w_ple_gate[0], w_ple_proj[0], g_final)
    y_prompt = _ple_final(h1, expert_rows, top_w_pad, p_prompt[0].reshape(tp, D_PLE), 0, *ple_w)
    y_sample = _ple_final(h1, expert_rows, top_w_pad, p_sample[0].reshape(ts, D_PLE), tp, *ple_w)

    return (y_prompt.reshape(bp, lp, D_MODEL), y_sample.reshape(bs, ls, D_MODEL),
            re_p, im_p, c_p[None], n_p.reshape(1, bp, ML_HEADS, ML_DK), m_p[:, :, 0, 0][None],
            re_s, im_s, c_s[None], n_s.reshape(1, bs, ML_HEADS, ML_DK), m_s[:, :, 0, 0][None])
```
